```python
import jax, jax.numpy as jnp
from jax import lax
import numpy as np

D_MODEL = 1024
BATCH = 8
SEQ = 8192
DEPTH = 4

CHUNK = 64
N_MIXERS = 2
CONV_WIDTH = 31
RET_HEADS = 4
RET_QK_DIM = D_MODEL // RET_HEADS
RET_V_DIM = 2 * D_MODEL // RET_HEADS
RET_QK_TOTAL = RET_HEADS * RET_QK_DIM
RET_V_TOTAL = RET_HEADS * RET_V_DIM
RET_IN_WIDTH = 2 * RET_QK_TOTAL + 2 * RET_V_TOTAL
D_FF = 4 * D_MODEL
ROPE_BASE = 10000.0
EPS = 1e-6
N_CONV_LAYERS = (DEPTH + 1) // 2
N_RET_LAYERS = DEPTH // 2

kernel_name = "hybrid_conformer_retention_adaln_trunk"


def rmsnorm(x, g):
    xf = x.astype(jnp.float32)
    y = xf * lax.rsqrt(jnp.mean(xf * xf, axis=-1, keepdims=True) + EPS)
    return (y * g.astype(jnp.float32)).astype(x.dtype)


def modulate(h, shift, scale):
    return h * (1.0 + scale[:, None, :]) + shift[:, None, :]


def conformer_conv(h, w_pw1, b_pw1, w_dw, b_dw, ln_g, ln_b, w_pw2, b_pw2):
    u = h @ w_pw1 + b_pw1
    a, g = jnp.split(u, 2, axis=-1)
    u = a * jax.nn.sigmoid(g)
    u = lax.conv_general_dilated(
        u, w_dw[:, None, :], window_strides=(1,), padding=[(CONV_WIDTH - 1, 0)],
        dimension_numbers=('NWC', 'WIO', 'NWC'), feature_group_count=D_MODEL) + b_dw
    uf = u.astype(jnp.float32)
    mu = jnp.mean(uf, axis=-1, keepdims=True)
    var = jnp.mean(jnp.square(uf - mu), axis=-1, keepdims=True)
    u = ((uf - mu) * lax.rsqrt(var + EPS) * ln_g + ln_b).astype(h.dtype)
    u = jax.nn.silu(u)
    return u @ w_pw2 + b_pw2


def rope_tables(seq):
    pos = jnp.arange(seq, dtype=jnp.float32)
    inv = ROPE_BASE ** (-jnp.arange(0, RET_QK_DIM, 2, dtype=jnp.float32) / RET_QK_DIM)
    ang = pos[:, None] * inv[None, :]
    return jnp.cos(ang), jnp.sin(ang)


def apply_rope(x, cos, sin):
    half = RET_QK_DIM // 2
    x1, x2 = x[..., :half], x[..., half:]
    c = cos[None, :, None, :].astype(x.dtype)
    s = sin[None, :, None, :].astype(x.dtype)
    return jnp.concatenate([x1 * c - x2 * s, x2 * c + x1 * s], axis=-1)


def retention(h, w_in, gn_g, gn_b, w_out, cos, sin, log_gamma):
    b, s, _ = h.shape
    nc = s // CHUNK
    proj = h @ w_in
    q, k, v, gate = jnp.split(
        proj, [RET_QK_TOTAL, 2 * RET_QK_TOTAL, 2 * RET_QK_TOTAL + RET_V_TOTAL], axis=-1)
    q = apply_rope(q.reshape(b, s, RET_HEADS, RET_QK_DIM), cos, sin)
    k = apply_rope(k.reshape(b, s, RET_HEADS, RET_QK_DIM), cos, sin) * (RET_QK_DIM ** -0.5)
    v = v.reshape(b, s, RET_HEADS, RET_V_DIM)

    def to_chunks(t):
        return t.reshape(b, nc, CHUNK, RET_HEADS, t.shape[-1]).transpose(0, 1, 3, 2, 4)

    qc, kc, vc = to_chunks(q), to_chunks(k), to_chunks(v)
    idx = jnp.arange(CHUNK, dtype=jnp.float32)
    d_intra = jnp.exp(log_gamma[:, None, None] * jnp.abs(idx[:, None] - idx[None, :]))
    scores = jnp.einsum('bnhcd,bnhed->bnhce', qc, kc) * d_intra.astype(qc.dtype)
    intra = jnp.einsum('bnhce,bnhef->bnhcf', scores, vc)

    xi = jnp.exp(log_gamma[:, None] * (idx + 1.0))
    zeta = jnp.exp(log_gamma[:, None] * (CHUNK - 1.0 - idx))
    chunk_decay = jnp.exp(log_gamma * CHUNK)

    def step(state, inp):
        qj, kj, vj = inp
        cross = jnp.einsum('bhcd,bhdf->bhcf', qj * xi[..., None], state)
        state = state * chunk_decay[:, None, None] + jnp.einsum(
            'bhcd,bhcf->bhdf', kj * zeta[..., None], vj)
        return state, cross

    state0 = jnp.zeros((b, RET_HEADS, RET_QK_DIM, RET_V_DIM), jnp.float32)
    xs = (qc.transpose(1, 0, 2, 3, 4), kc.transpose(1, 0, 2, 3, 4), vc.transpose(1, 0, 2, 3, 4))
    _, cross = lax.scan(step, state0, xs)
    y = intra + cross.transpose(1, 0, 2, 3, 4).astype(intra.dtype)
    y = y.transpose(0, 1, 3, 2, 4).reshape(b, s, RET_HEADS, RET_V_DIM)
    yf = y.astype(jnp.float32)
    mu = jnp.mean(yf, axis=-1, keepdims=True)
    var = jnp.mean(jnp.square(yf - mu), axis=-1, keepdims=True)
    y = ((yf - mu) * lax.rsqrt(var + EPS) * gn_g + gn_b).astype(h.dtype)
    y = jax.nn.silu(gate) * y.reshape(b, s, RET_V_TOTAL)
    return y @ w_out


def _fwd_setup_inputs(seed: int = 0) -> dict:
    key = jax.random.key(seed)
    ks = jax.random.split(key, 24)
    f32 = jnp.float32
    D = D_MODEL

    def nrm(k, shape, std):
        return jax.random.normal(k, shape, f32) * std

    return {
        "x": nrm(ks[0], (BATCH, SEQ, D), 1.0),
        "c": nrm(ks[1], (BATCH, D), 1.0),
        "ada_w": nrm(ks[2], (DEPTH, D, 6 * D), 0.5 * D ** -0.5),
        "ada_b": nrm(ks[3], (DEPTH, 6 * D), 0.02),
        "norm_mix_g": 1.0 + nrm(ks[4], (DEPTH, D), 0.02),
        "norm_mlp_g": 1.0 + nrm(ks[5], (DEPTH, D), 0.02),
        "conv_w_pw1": nrm(ks[6], (N_CONV_LAYERS, D, 2 * D), D ** -0.5),
        "conv_b_pw1": nrm(ks[7], (N_CONV_LAYERS, 2 * D), 0.02),
        "conv_w_dw": nrm(ks[8], (N_CONV_LAYERS, CONV_WIDTH, D), CONV_WIDTH ** -0.5),
        "conv_b_dw": nrm(ks[9], (N_CONV_LAYERS, D), 0.02),
        "conv_ln_g": 1.0 + nrm(ks[10], (N_CONV_LAYERS, D), 0.02),
        "conv_ln_b": nrm(ks[11], (N_CONV_LAYERS, D), 0.02),
        "conv_w_pw2": nrm(ks[12], (N_CONV_LAYERS, D, D), D ** -0.5),
        "conv_b_pw2": nrm(ks[13], (N_CONV_LAYERS, D), 0.02),
        "ret_w_in": nrm(ks[14], (N_RET_LAYERS, D, RET_IN_WIDTH), D ** -0.5),
        "ret_gn_g": 1.0 + nrm(ks[15], (N_RET_LAYERS, RET_HEADS, RET_V_DIM), 0.02),
        "ret_gn_b": nrm(ks[16], (N_RET_LAYERS, RET_HEADS, RET_V_DIM), 0.02),
        "ret_w_out": nrm(ks[17], (N_RET_LAYERS, RET_V_TOTAL, D), RET_V_TOTAL ** -0.5),
        "mlp_w1": nrm(ks[18], (DEPTH, D, D_FF), D ** -0.5),
        "mlp_w2": nrm(ks[19], (DEPTH, D_FF, D), D_FF ** -0.5),
        "final_norm_g": 1.0 + nrm(ks[20], (D,), 0.02),
    }


def _fwd_reference(x, c, ada_w, ada_b, norm_mix_g, norm_mlp_g, conv_w_pw1, conv_b_pw1, conv_w_dw,
              conv_b_dw, conv_ln_g, conv_ln_b, conv_w_pw2, conv_b_pw2, ret_w_in, ret_gn_g,
              ret_gn_b, ret_w_out, mlp_w1, mlp_w2, final_norm_g):
    seq = x.shape[1]
    cos, sin = rope_tables(seq)
    log_gamma = jnp.log(1.0 - 2.0 ** (-5.0 - jnp.arange(RET_HEADS, dtype=jnp.float32)))
    cond = jax.nn.silu(c)
    for i in range(DEPTH):
        mod = cond @ ada_w[i] + ada_b[i]
        sh1, sc1, g1, sh2, sc2, g2 = jnp.split(mod, 6, axis=-1)
        h = modulate(rmsnorm(x, norm_mix_g[i]), sh1, sc1)
        j = i // N_MIXERS
        if i % N_MIXERS == 0:
            y = conformer_conv(h, conv_w_pw1[j], conv_b_pw1[j], conv_w_dw[j], conv_b_dw[j],
                               conv_ln_g[j], conv_ln_b[j], conv_w_pw2[j], conv_b_pw2[j])
        else:
            y = retention(h, ret_w_in[j], ret_gn_g[j], ret_gn_b[j], ret_w_out[j],
                          cos, sin, log_gamma)
        x = x + g1[:, None, :] * y
        h = modulate(rmsnorm(x, norm_mlp_g[i]), sh2, sc2)
        x = x + g2[:, None, :] * (jnp.square(jax.nn.relu(h @ mlp_w1[i])) @ mlp_w2[i])
    return rmsnorm(x, final_norm_g)


import jax as _jax
import jax.numpy as _jnp

TWIN_FORMAT = 'train_step'
FWD_PARAMS = ['x', 'c', 'ada_w', 'ada_b', 'norm_mix_g', 'norm_mlp_g', 'conv_w_pw1', 'conv_b_pw1', 'conv_w_dw', 'conv_b_dw', 'conv_ln_g', 'conv_ln_b', 'conv_w_pw2', 'conv_b_pw2', 'ret_w_in', 'ret_gn_g', 'ret_gn_b', 'ret_w_out', 'mlp_w1', 'mlp_w2', 'final_norm_g']
TWIN_WEIGHTS = ['ada_w', 'ada_b', 'norm_mix_g', 'norm_mlp_g', 'conv_w_pw1', 'conv_b_pw1', 'conv_w_dw', 'conv_b_dw', 'conv_ln_g', 'conv_ln_b', 'conv_w_pw2', 'conv_b_pw2', 'ret_w_in', 'ret_gn_g', 'ret_gn_b', 'ret_w_out', 'mlp_w1', 'mlp_w2', 'final_norm_g']
TWIN_DIFF_INPUT = 'x'
TWIN_INPUTS = ['x', 'c', 'ada_w', 'ada_b', 'norm_mix_g', 'norm_mlp_g', 'conv_w_pw1', 'conv_b_pw1', 'conv_w_dw', 'conv_b_dw', 'conv_ln_g', 'conv_ln_b', 'conv_w_pw2', 'conv_b_pw2', 'ret_w_in', 'ret_gn_g', 'ret_gn_b', 'ret_w_out', 'mlp_w1', 'mlp_w2', 'final_norm_g', 'loss_target', 'm_ada_w', 'm_ada_b', 'm_norm_mix_g', 'm_norm_mlp_g', 'm_conv_w_pw1', 'm_conv_b_pw1', 'm_conv_w_dw', 'm_conv_b_dw', 'm_conv_ln_g', 'm_conv_ln_b', 'm_conv_w_pw2', 'm_conv_b_pw2', 'm_ret_w_in', 'm_ret_gn_g', 'm_ret_gn_b', 'm_ret_w_out', 'm_mlp_w1', 'm_mlp_w2', 'm_final_norm_g', 'v_ada_w', 'v_ada_b', 'v_norm_mix_g', 'v_norm_mlp_g', 'v_conv_w_pw1', 'v_conv_b_pw1', 'v_conv_w_dw', 'v_conv_b_dw', 'v_conv_ln_g', 'v_conv_ln_b', 'v_conv_w_pw2', 'v_conv_b_pw2', 'v_ret_w_in', 'v_ret_gn_g', 'v_ret_gn_b', 'v_ret_w_out', 'v_mlp_w1', 'v_mlp_w2', 'v_final_norm_g']
TWIN_OUTPUTS = ['loss', 'grad_x', 'grad_ada_w', 'grad_ada_b', 'grad_norm_mix_g', 'grad_norm_mlp_g', 'grad_conv_w_pw1', 'grad_conv_b_pw1', 'grad_conv_w_dw', 'grad_conv_b_dw', 'grad_conv_ln_g', 'grad_conv_ln_b', 'grad_conv_w_pw2', 'grad_conv_b_pw2', 'grad_ret_w_in', 'grad_ret_gn_g', 'grad_ret_gn_b', 'grad_ret_w_out', 'grad_mlp_w1', 'grad_mlp_w2', 'grad_final_norm_g', 'delta_ada_w', 'delta_ada_b', 'delta_norm_mix_g', 'delta_norm_mlp_g', 'delta_conv_w_pw1', 'delta_conv_b_pw1', 'delta_conv_w_dw', 'delta_conv_b_dw', 'delta_conv_ln_g', 'delta_conv_ln_b', 'delta_conv_w_pw2', 'delta_conv_b_pw2', 'delta_ret_w_in', 'delta_ret_gn_g', 'delta_ret_gn_b', 'delta_ret_w_out', 'delta_mlp_w1', 'delta_mlp_w2', 'delta_final_norm_g', 'new_m_ada_w', 'new_m_ada_b', 'new_m_norm_mix_g', 'new_m_norm_mlp_g', 'new_m_conv_w_pw1', 'new_m_conv_b_pw1', 'new_m_conv_w_dw', 'new_m_conv_b_dw', 'new_m_conv_ln_g', 'new_m_conv_ln_b', 'new_m_conv_w_pw2', 'new_m_conv_b_pw2', 'new_m_ret_w_in', 'new_m_ret_gn_g', 'new_m_ret_gn_b', 'new_m_ret_w_out', 'new_m_mlp_w1', 'new_m_mlp_w2', 'new_m_final_norm_g', 'new_v_ada_w', 'new_v_ada_b', 'new_v_norm_mix_g', 'new_v_norm_mlp_g', 'new_v_conv_w_pw1', 'new_v_conv_b_pw1', 'new_v_conv_w_dw', 'new_v_conv_b_dw', 'new_v_conv_ln_g', 'new_v_conv_ln_b', 'new_v_conv_w_pw2', 'new_v_conv_b_pw2', 'new_v_ret_w_in', 'new_v_ret_gn_g', 'new_v_ret_gn_b', 'new_v_ret_w_out', 'new_v_mlp_w1', 'new_v_mlp_w2', 'new_v_final_norm_g']
TWIN_LEAF_KINDS = {'loss': 'loss', 'grad_x': 'grad_x', 'grad_ada_w': 'grad_w', 'grad_ada_b': 'grad_w', 'grad_norm_mix_g': 'grad_w', 'grad_norm_mlp_g': 'grad_w', 'grad_conv_w_pw1': 'grad_w', 'grad_conv_b_pw1': 'grad_w', 'grad_conv_w_dw': 'grad_w', 'grad_conv_b_dw': 'grad_w', 'grad_conv_ln_g': 'grad_w', 'grad_conv_ln_b': 'grad_w', 'grad_conv_w_pw2': 'grad_w', 'grad_conv_b_pw2': 'grad_w', 'grad_ret_w_in': 'grad_w', 'grad_ret_gn_g': 'grad_w', 'grad_ret_gn_b': 'grad_w', 'grad_ret_w_out': 'grad_w', 'grad_mlp_w1': 'grad_w', 'grad_mlp_w2': 'grad_w', 'grad_final_norm_g': 'grad_w', 'delta_ada_w': 'delta_w', 'delta_ada_b': 'delta_w', 'delta_norm_mix_g': 'delta_w', 'delta_norm_mlp_g': 'delta_w', 'delta_conv_w_pw1': 'delta_w', 'delta_conv_b_pw1': 'delta_w', 'delta_conv_w_dw': 'delta_w', 'delta_conv_b_dw': 'delta_w', 'delta_conv_ln_g': 'delta_w', 'delta_conv_ln_b': 'delta_w', 'delta_conv_w_pw2': 'delta_w', 'delta_conv_b_pw2': 'delta_w', 'delta_ret_w_in': 'delta_w', 'delta_ret_gn_g': 'delta_w', 'delta_ret_gn_b': 'delta_w', 'delta_ret_w_out': 'delta_w', 'delta_mlp_w1': 'delta_w', 'delta_mlp_w2': 'delta_w', 'delta_final_norm_g': 'delta_w', 'new_m_ada_w': 'new_m', 'new_m_ada_b': 'new_m', 'new_m_norm_mix_g': 'new_m', 'new_m_norm_mlp_g': 'new_m', 'new_m_conv_w_pw1': 'new_m', 'new_m_conv_b_pw1': 'new_m', 'new_m_conv_w_dw': 'new_m', 'new_m_conv_b_dw': 'new_m', 'new_m_conv_ln_g': 'new_m', 'new_m_conv_ln_b': 'new_m', 'new_m_conv_w_pw2': 'new_m', 'new_m_conv_b_pw2': 'new_m', 'new_m_ret_w_in': 'new_m', 'new_m_ret_gn_g': 'new_m', 'new_m_ret_gn_b': 'new_m', 'new_m_ret_w_out': 'new_m', 'new_m_mlp_w1': 'new_m', 'new_m_mlp_w2': 'new_m', 'new_m_final_norm_g': 'new_m', 'new_v_ada_w': 'new_v', 'new_v_ada_b': 'new_v', 'new_v_norm_mix_g': 'new_v', 'new_v_norm_mlp_g': 'new_v', 'new_v_conv_w_pw1': 'new_v', 'new_v_conv_b_pw1': 'new_v', 'new_v_conv_w_dw': 'new_v', 'new_v_conv_b_dw': 'new_v', 'new_v_conv_ln_g': 'new_v', 'new_v_conv_ln_b': 'new_v', 'new_v_conv_w_pw2': 'new_v', 'new_v_conv_b_pw2': 'new_v', 'new_v_ret_w_in': 'new_v', 'new_v_ret_gn_g': 'new_v', 'new_v_ret_gn_b': 'new_v', 'new_v_ret_w_out': 'new_v', 'new_v_mlp_w1': 'new_v', 'new_v_mlp_w2': 'new_v', 'new_v_final_norm_g': 'new_v'}


def _forward(args):
    return _fwd_reference(*[args[k] for k in FWD_PARAMS])


def _output_shape():
    def fwd():
        inp = _fwd_setup_inputs(0)
        return _fwd_reference(*[inp[k] for k in FWD_PARAMS])
    out = _jax.eval_shape(fwd)
    return out.shape, out.dtype

N_MICROBATCH = 1
ADAM_LR = 0.001
ADAM_B1 = 0.9
ADAM_B2 = 0.999
ADAM_EPS = 1e-08
ADAM_WD = 0.01
ADAM_STEP = 10
PER_EXAMPLE_BATCH_AXIS = {'x': 0, 'c': 0, 'loss_target': 0}
SHARED_INPUTS = []
_WEIGHT_DTYPES = {'ada_w': _jnp.float32, 'ada_b': _jnp.float32, 'norm_mix_g': _jnp.float32, 'norm_mlp_g': _jnp.float32, 'conv_w_pw1': _jnp.float32, 'conv_b_pw1': _jnp.float32, 'conv_w_dw': _jnp.float32, 'conv_b_dw': _jnp.float32, 'conv_ln_g': _jnp.float32, 'conv_ln_b': _jnp.float32, 'conv_w_pw2': _jnp.float32, 'conv_b_pw2': _jnp.float32, 'ret_w_in': _jnp.float32, 'ret_gn_g': _jnp.float32, 'ret_gn_b': _jnp.float32, 'ret_w_out': _jnp.float32, 'mlp_w1': _jnp.float32, 'mlp_w2': _jnp.float32, 'final_norm_g': _jnp.float32}
MOMENT_SCALE = {'ada_w': 1.104689e-01, 'ada_b': 1.959174e-01, 'norm_mix_g': 7.031455e-02, 'norm_mlp_g': 1.035836e-01, 'conv_w_pw1': 3.667562e-02, 'conv_b_pw1': 4.387953e-02, 'conv_w_dw': 4.828725e-02, 'conv_b_dw': 9.495977e-02, 'conv_ln_g': 6.295374e-02, 'conv_ln_b': 6.039193e-02, 'conv_w_pw2': 4.838338e-02, 'conv_b_pw2': 1.040040e-01, 'ret_w_in': 3.931982e-02, 'ret_gn_g': 3.101322e-02, 'ret_gn_b': 3.160655e-02, 'ret_w_out': 4.327574e-02, 'mlp_w1': 5.540169e-02, 'mlp_w2': 1.028239e-01, 'final_norm_g': 6.457395e+01}


def _to_microbatches(a, axis):
    t = _jnp.moveaxis(a, axis, 0)
    t = t.reshape((N_MICROBATCH, t.shape[0] // N_MICROBATCH) + t.shape[1:])
    return _jnp.moveaxis(t, 1, axis + 1)


def setup_inputs(seed: int = 0) -> dict:
    inp = _fwd_setup_inputs(seed)
    key = _jax.random.fold_in(_jax.random.key(seed), 7919)
    shape, _ = _output_shape()
    out = dict(inp)
    out["loss_target"] = _jax.random.normal(_jax.random.fold_in(key, 0), shape, _jnp.float32)
    for i, name in enumerate(TWIN_WEIGHTS):
        w = inp[name].astype(_jnp.float32)
        if MOMENT_SCALE is None:
            s = _jnp.sqrt(_jnp.mean(_jnp.square(w)) + 1e-30)
        else:
            s = MOMENT_SCALE[name]
        km, kv = _jax.random.split(_jax.random.fold_in(key, i + 1))
        out[name] = w
        out["m_" + name] = s * _jax.random.normal(km, w.shape, _jnp.float32)
        out["v_" + name] = (s * s) * _jax.random.uniform(kv, w.shape, _jnp.float32, 0.5, 1.5)
    if N_MICROBATCH > 1:
        for name, axis in PER_EXAMPLE_BATCH_AXIS.items():
            out[name] = _to_microbatches(out[name], axis)
    return {'x': out['x'], 'c': out['c'], 'ada_w': out['ada_w'], 'ada_b': out['ada_b'], 'norm_mix_g': out['norm_mix_g'], 'norm_mlp_g': out['norm_mlp_g'], 'conv_w_pw1': out['conv_w_pw1'], 'conv_b_pw1': out['conv_b_pw1'], 'conv_w_dw': out['conv_w_dw'], 'conv_b_dw': out['conv_b_dw'], 'conv_ln_g': out['conv_ln_g'], 'conv_ln_b': out['conv_ln_b'], 'conv_w_pw2': out['conv_w_pw2'], 'conv_b_pw2': out['conv_b_pw2'], 'ret_w_in': out['ret_w_in'], 'ret_gn_g': out['ret_gn_g'], 'ret_gn_b': out['ret_gn_b'], 'ret_w_out': out['ret_w_out'], 'mlp_w1': out['mlp_w1'], 'mlp_w2': out['mlp_w2'], 'final_norm_g': out['final_norm_g'], 'loss_target': out['loss_target'], 'm_ada_w': out['m_ada_w'], 'm_ada_b': out['m_ada_b'], 'm_norm_mix_g': out['m_norm_mix_g'], 'm_norm_mlp_g': out['m_norm_mlp_g'], 'm_conv_w_pw1': out['m_conv_w_pw1'], 'm_conv_b_pw1': out['m_conv_b_pw1'], 'm_conv_w_dw': out['m_conv_w_dw'], 'm_conv_b_dw': out['m_conv_b_dw'], 'm_conv_ln_g': out['m_conv_ln_g'], 'm_conv_ln_b': out['m_conv_ln_b'], 'm_conv_w_pw2': out['m_conv_w_pw2'], 'm_conv_b_pw2': out['m_conv_b_pw2'], 'm_ret_w_in': out['m_ret_w_in'], 'm_ret_gn_g': out['m_ret_gn_g'], 'm_ret_gn_b': out['m_ret_gn_b'], 'm_ret_w_out': out['m_ret_w_out'], 'm_mlp_w1': out['m_mlp_w1'], 'm_mlp_w2': out['m_mlp_w2'], 'm_final_norm_g': out['m_final_norm_g'], 'v_ada_w': out['v_ada_w'], 'v_ada_b': out['v_ada_b'], 'v_norm_mix_g': out['v_norm_mix_g'], 'v_norm_mlp_g': out['v_norm_mlp_g'], 'v_conv_w_pw1': out['v_conv_w_pw1'], 'v_conv_b_pw1': out['v_conv_b_pw1'], 'v_conv_w_dw': out['v_conv_w_dw'], 'v_conv_b_dw': out['v_conv_b_dw'], 'v_conv_ln_g': out['v_conv_ln_g'], 'v_conv_ln_b': out['v_conv_ln_b'], 'v_conv_w_pw2': out['v_conv_w_pw2'], 'v_conv_b_pw2': out['v_conv_b_pw2'], 'v_ret_w_in': out['v_ret_w_in'], 'v_ret_gn_g': out['v_ret_gn_g'], 'v_ret_gn_b': out['v_ret_gn_b'], 'v_ret_w_out': out['v_ret_w_out'], 'v_mlp_w1': out['v_mlp_w1'], 'v_mlp_w2': out['v_mlp_w2'], 'v_final_norm_g': out['v_final_norm_g']}


def _loss(weights, diff, rest, loss_target):
    with _jax.named_scope("forward"):
        args = {**rest, TWIN_DIFF_INPUT: diff, **{k: w.astype(_WEIGHT_DTYPES[k]) for k, w in weights.items()}}
        y = _forward(args)
    with _jax.named_scope("loss_head"):
        err = _jnp.square(y.astype(_jnp.float32) - loss_target)
        return 0.5 * _jnp.sum(_jnp.mean(err, axis=-1)) if err.ndim else 0.5 * err


def _adamw(w, g, m, v):
    m = ADAM_B1 * m + (1.0 - ADAM_B1) * g
    v = ADAM_B2 * v + (1.0 - ADAM_B2) * _jnp.square(g)
    m_hat = m / (1.0 - ADAM_B1 ** ADAM_STEP)
    v_hat = v / (1.0 - ADAM_B2 ** ADAM_STEP)
    delta = -ADAM_LR * (m_hat / (_jnp.sqrt(v_hat) + ADAM_EPS) + ADAM_WD * w)
    return delta, m, v


def reference(x, c, ada_w, ada_b, norm_mix_g, norm_mlp_g, conv_w_pw1, conv_b_pw1, conv_w_dw, conv_b_dw, conv_ln_g, conv_ln_b, conv_w_pw2, conv_b_pw2, ret_w_in, ret_gn_g, ret_gn_b, ret_w_out, mlp_w1, mlp_w2, final_norm_g, loss_target, m_ada_w, m_ada_b, m_norm_mix_g, m_norm_mlp_g, m_conv_w_pw1, m_conv_b_pw1, m_conv_w_dw, m_conv_b_dw, m_conv_ln_g, m_conv_ln_b, m_conv_w_pw2, m_conv_b_pw2, m_ret_w_in, m_ret_gn_g, m_ret_gn_b, m_ret_w_out, m_mlp_w1, m_mlp_w2, m_final_norm_g, v_ada_w, v_ada_b, v_norm_mix_g, v_norm_mlp_g, v_conv_w_pw1, v_conv_b_pw1, v_conv_w_dw, v_conv_b_dw, v_conv_ln_g, v_conv_ln_b, v_conv_w_pw2, v_conv_b_pw2, v_ret_w_in, v_ret_gn_g, v_ret_gn_b, v_ret_w_out, v_mlp_w1, v_mlp_w2, v_final_norm_g):
    given = dict(x=x, c=c, ada_w=ada_w, ada_b=ada_b, norm_mix_g=norm_mix_g, norm_mlp_g=norm_mlp_g, conv_w_pw1=conv_w_pw1, conv_b_pw1=conv_b_pw1, conv_w_dw=conv_w_dw, conv_b_dw=conv_b_dw, conv_ln_g=conv_ln_g, conv_ln_b=conv_ln_b, conv_w_pw2=conv_w_pw2, conv_b_pw2=conv_b_pw2, ret_w_in=ret_w_in, ret_gn_g=ret_gn_g, ret_gn_b=ret_gn_b, ret_w_out=ret_w_out, mlp_w1=mlp_w1, mlp_w2=mlp_w2, final_norm_g=final_norm_g, loss_target=loss_target, m_ada_w=m_ada_w, m_ada_b=m_ada_b, m_norm_mix_g=m_norm_mix_g, m_norm_mlp_g=m_norm_mlp_g, m_conv_w_pw1=m_conv_w_pw1, m_conv_b_pw1=m_conv_b_pw1, m_conv_w_dw=m_conv_w_dw, m_conv_b_dw=m_conv_b_dw, m_conv_ln_g=m_conv_ln_g, m_conv_ln_b=m_conv_ln_b, m_conv_w_pw2=m_conv_w_pw2, m_conv_b_pw2=m_conv_b_pw2, m_ret_w_in=m_ret_w_in, m_ret_gn_g=m_ret_gn_g, m_ret_gn_b=m_ret_gn_b, m_ret_w_out=m_ret_w_out, m_mlp_w1=m_mlp_w1, m_mlp_w2=m_mlp_w2, m_final_norm_g=m_final_norm_g, v_ada_w=v_ada_w, v_ada_b=v_ada_b, v_norm_mix_g=v_norm_mix_g, v_norm_mlp_g=v_norm_mlp_g, v_conv_w_pw1=v_conv_w_pw1, v_conv_b_pw1=v_conv_b_pw1, v_conv_w_dw=v_conv_w_dw, v_conv_b_dw=v_conv_b_dw, v_conv_ln_g=v_conv_ln_g, v_conv_ln_b=v_conv_ln_b, v_conv_w_pw2=v_conv_w_pw2, v_conv_b_pw2=v_conv_b_pw2, v_ret_w_in=v_ret_w_in, v_ret_gn_g=v_ret_gn_g, v_ret_gn_b=v_ret_gn_b, v_ret_w_out=v_ret_w_out, v_mlp_w1=v_mlp_w1, v_mlp_w2=v_mlp_w2, v_final_norm_g=v_final_norm_g)
    weights = {n: given[n] for n in TWIN_WEIGHTS}
    shared = {n: given[n] for n in SHARED_INPUTS}
    per_example = {n: given[n] for n in ['x', 'c']}
    grad_fn = _jax.value_and_grad(_loss, argnums=(0, 1))

    def one_microbatch(ex, loss_target):
        ex = dict(ex)
        diff = ex.pop(TWIN_DIFF_INPUT)
        return grad_fn(weights, diff, {**shared, **ex}, loss_target)

    if N_MICROBATCH == 1:
        loss, (grad_w, grad_x) = one_microbatch(per_example, given["loss_target"])
    else:
        def body(carry, xs):
            loss_sum, grad_sum = carry
            l_k, (gw_k, gx_k) = one_microbatch(xs[0], xs[1])
            with _jax.named_scope("update"):
                return (loss_sum + l_k, _jax.tree.map(_jnp.add, grad_sum, gw_k)), gx_k

        init = (_jnp.zeros((), _jnp.float32), _jax.tree.map(_jnp.zeros_like, weights))
        (loss, grad_w), grad_x = _jax.lax.scan(body, init, (per_example, given["loss_target"]))
    with _jax.named_scope("update"):
        delta_w, new_m, new_v = {}, {}, {}
        for n in TWIN_WEIGHTS:
            delta_w[n], new_m[n], new_v[n] = _adamw(weights[n], grad_w[n], given["m_" + n], given["v_" + n])
    return (loss, grad_x, *[grad_w[n] for n in TWIN_WEIGHTS], *[delta_w[n] for n in TWIN_WEIGHTS],
            *[new_m[n] for n in TWIN_WEIGHTS], *[new_v[n] for n in TWIN_WEIGHTS])
```

```python
import functools
import math

import jax
import jax.numpy as jnp
from jax import lax
from jax.experimental import pallas as pl
from jax.experimental.pallas import tpu as pltpu

F32 = jnp.float32
BF16 = jnp.bfloat16
MESH = pl.DeviceIdType.MESH

EPS = 1e-6
CHUNK = 64
CONV_WIDTH = 31
CONV_HALO = 32
RET_HEADS = 4
ROPE_BASE = 10000.0
ADAM_LR = 0.001
ADAM_B1 = 0.9
ADAM_B2 = 0.999
ADAM_EPS = 1e-08
ADAM_WD = 0.01
ADAM_STEP = 10
N_CHIPS = 4
N_DEV = 8
V7X_VMEM_LIMIT = 48 * 1024 * 1024
ANY = pl.BlockSpec(memory_space=pl.ANY)


def _cparams(sem=None):
    return pltpu.CompilerParams(dimension_semantics=sem, vmem_limit_bytes=V7X_VMEM_LIMIT)


def _sigmoid(x):
    return jax.nn.sigmoid(x)


def _silu(x):
    return x * _sigmoid(x)


_DIMS = {
    "nn": (((1,), (0,)), ((), ())),
    "nt": (((1,), (1,)), ((), ())),
    "tn": (((0,), (0,)), ((), ())),
}


def _mm(a, b, *, mode, name, out_dtype=F32, tm=1024, tn=1024, tk=1024, b3d=False, out3d=None,
        bias=None, epi=None, extra=None, a_silu=False):
    if mode == "tn":
        K, M = a.shape
    else:
        M, K = a.shape
    if b3d:
        P, R, Cs = b.shape
        bshape = (R, P * Cs)
    else:
        bshape = b.shape
    N = bshape[0] if mode == "nt" else bshape[1]
    assert (bshape[1] if mode == "nt" else bshape[0]) == K, (name, a.shape, b.shape)
    tm, tn, tk = min(tm, M), min(tn, N), min(tk, K)
    assert M % tm == 0 and N % tn == 0 and K % tk == 0, (name, M, N, K, tm, tn, tk)
    nk = K // tk

    if mode == "tn":
        a_spec = pl.BlockSpec((tk, tm), lambda i, j, k: (k, i))
    else:
        a_spec = pl.BlockSpec((tm, tk), lambda i, j, k: (i, k))
    if mode == "nt":
        if b3d:
            nb = Cs // tk
            assert Cs % tk == 0
            b_spec = pl.BlockSpec((None, tn, tk), lambda i, j, k: (k // nb, j, k % nb))
        else:
            b_spec = pl.BlockSpec((tn, tk), lambda i, j, k: (j, k))
    else:
        if b3d:
            nb = Cs // tn
            assert Cs % tn == 0
            b_spec = pl.BlockSpec((None, tk, tn), lambda i, j, k: (j // nb, k, j % nb))
        else:
            b_spec = pl.BlockSpec((tk, tn), lambda i, j, k: (k, j))
    in_specs = [a_spec, b_spec]
    args = [a, b]
    if bias is not None:
        in_specs.append(pl.BlockSpec((1, tn), lambda i, j, k: (0, j)))
        args.append(bias)
    if extra is not None:
        in_specs.append(pl.BlockSpec((tm, tn), lambda i, j, k: (i, j)))
        args.append(extra)

    if out3d is not None:
        P_o, Cs_o = out3d
        assert P_o * Cs_o == N and Cs_o % tn == 0
        nbo = Cs_o // tn
        o_spec = pl.BlockSpec((None, tm, tn), lambda i, j, k: (j // nbo, i, j % nbo))
        o_shape = (P_o, M, Cs_o)
    else:
        o_spec = pl.BlockSpec((tm, tn), lambda i, j, k: (i, j))
        o_shape = (M, N)
    if epi == "relu2":
        out_shape = [jax.ShapeDtypeStruct(o_shape, BF16), jax.ShapeDtypeStruct(o_shape, BF16)]
        out_specs = [o_spec, o_spec]
    else:
        out_shape = jax.ShapeDtypeStruct(o_shape, out_dtype)
        out_specs = o_spec
    n_out = 2 if epi == "relu2" else 1
    dims = _DIMS[mode]
    has_bias, has_extra = bias is not None, extra is not None

    def body(*refs):
        a_ref, b_ref = refs[0], refs[1]
        pos = 2
        bias_ref = extra_ref = None
        if has_bias:
            bias_ref = refs[pos]
            pos += 1
        if has_extra:
            extra_ref = refs[pos]
            pos += 1
        outs = refs[pos:pos + n_out]
        acc_ref = refs[pos + n_out] if nk > 1 else None

        def partial():
            av = a_ref[...]
            if a_silu:
                av = _silu(av)
            return lax.dot_general(av, b_ref[...], dims, preferred_element_type=F32)

        def finish(r):
            if has_bias:
                r = r + bias_ref[...]
            if epi == "relu2":
                rr = jnp.maximum(r, 0.0)
                outs[0][...] = rr.astype(BF16)
                outs[1][...] = (rr * rr).astype(BF16)
            elif epi == "mul2":
                outs[0][...] = (r * 2.0 * extra_ref[...].astype(F32)).astype(outs[0].dtype)
            else:
                outs[0][...] = r.astype(outs[0].dtype)

        if nk == 1:
            finish(partial())
        else:
            k = pl.program_id(2)

            @pl.when(k == 0)
            def _():
                acc_ref[...] = jnp.zeros_like(acc_ref)

            acc_ref[...] += partial()

            @pl.when(k == nk - 1)
            def _():
                finish(acc_ref[...])

    return pl.pallas_call(
        body, name=name, grid=(M // tm, N // tn, nk), in_specs=in_specs, out_specs=out_specs,
        out_shape=out_shape,
        scratch_shapes=[pltpu.VMEM((tm, tn), F32)] if nk > 1 else [],
        compiler_params=_cparams(("parallel", "parallel", "arbitrary")),
    )(*args)


def _modnorm(x, gain, shift, scale):
    y = x * lax.rsqrt(jnp.mean(x * x, axis=-1, keepdims=True) + EPS)
    return (y * gain) * (1.0 + scale) + shift


def _row_fwd(xprev, y, vec, *, name, ts=512):
    S, D = xprev.shape
    ts = min(ts, S)
    has_res = y is not None
    row = pl.BlockSpec((ts, D), lambda i: (i, 0))
    vspec = pl.BlockSpec((8, D), lambda i: (0, 0))

    def body(*refs):
        if has_res:
            xp_ref, y_ref, v_ref, x_ref, h_ref = refs
            x = xp_ref[...] + v_ref[0:1, :] * y_ref[...]
            x_ref[...] = x
        else:
            xp_ref, v_ref, h_ref = refs
            x = xp_ref[...]
        h_ref[...] = _modnorm(x, v_ref[1:2, :], v_ref[2:3, :], v_ref[3:4, :]).astype(BF16)

    if has_res:
        return pl.pallas_call(
            body, name=name, grid=(S // ts,), in_specs=[row, row, vspec], out_specs=[row, row],
            out_shape=[jax.ShapeDtypeStruct((S, D), F32), jax.ShapeDtypeStruct((S, D), BF16)],
            compiler_params=_cparams(("parallel",)),
        )(xprev, y, vec)
    h = pl.pallas_call(
        body, name=name, grid=(S // ts,), in_specs=[row, vspec], out_specs=row,
        out_shape=jax.ShapeDtypeStruct((S, D), BF16),
        compiler_params=_cparams(("parallel",)),
    )(xprev, vec)
    return xprev, h


def _row_bwd(xin, dh, dxout, yprev, vec, *, name, ts=512):
    S, D = xin.shape
    ts = min(ts, S)
    row = pl.BlockSpec((ts, D), lambda i: (i, 0))
    vspec = pl.BlockSpec((8, D), lambda i: (0, 0))

    def body(x_ref, dh_ref, dx_ref, y_ref, v_ref, dxin_ref, dy_ref, part_ref):
        @pl.when(pl.program_id(0) == 0)
        def _():
            part_ref[...] = jnp.zeros_like(part_ref)

        gate = v_ref[0:1, :]
        _, vjp = jax.vjp(_modnorm, x_ref[...], v_ref[1:2, :], v_ref[2:3, :], v_ref[3:4, :])
        dxn, dgain, dshift, dscale = vjp(dh_ref[...])
        dxin = dx_ref[...] + dxn
        dxin_ref[...] = dxin
        dy = dxin * gate
        dy_ref[...] = dy.astype(BF16)
        part_ref[0:1, :] += jnp.sum(dxin * y_ref[...], axis=0, keepdims=True)
        part_ref[1:2, :] += dgain
        part_ref[2:3, :] += dshift
        part_ref[3:4, :] += dscale
        part_ref[4:5, :] += jnp.sum(dy, axis=0, keepdims=True)

    return pl.pallas_call(
        body, name=name, grid=(S // ts,), in_specs=[row, row, row, row, vspec],
        out_specs=[row, row, vspec],
        out_shape=[jax.ShapeDtypeStruct((S, D), F32), jax.ShapeDtypeStruct((S, D), BF16),
                   jax.ShapeDtypeStruct((8, D), F32)],
        compiler_params=_cparams(("arbitrary",)),
    )(xin, dh, dxout, yprev, vec)


def _final(xprev, y, target, vec, *, name, ts=512):
    S, D = xprev.shape
    ts = min(ts, S)
    row = pl.BlockSpec((ts, D), lambda i: (i, 0))
    vspec = pl.BlockSpec((8, D), lambda i: (0, 0))

    def norm(x, gain):
        return x * lax.rsqrt(jnp.mean(x * x, axis=-1, keepdims=True) + EPS) * gain

    def body(xp_ref, y_ref, t_ref, v_ref, dx_ref, dy_ref, part_ref):
        @pl.when(pl.program_id(0) == 0)
        def _():
            part_ref[...] = jnp.zeros_like(part_ref)

        gate = v_ref[0:1, :]
        yv = y_ref[...]
        x = xp_ref[...] + gate * yv
        out, vjp = jax.vjp(norm, x, v_ref[1:2, :])
        err = out - t_ref[...]
        dx, dgain = vjp(err * (1.0 / D))
        dx_ref[...] = dx
        dy_ref[...] = (dx * gate).astype(BF16)
        part_ref[0:1, :] += jnp.sum(dx * yv, axis=0, keepdims=True)
        part_ref[1:2, :] += dgain
        part_ref[2:3, :] += jnp.sum(err * err, axis=0, keepdims=True) * (0.5 / D)

    return pl.pallas_call(
        body, name=name, grid=(S // ts,), in_specs=[row, row, row, vspec], out_specs=[row, row, vspec],
        out_shape=[jax.ShapeDtypeStruct((S, D), F32), jax.ShapeDtypeStruct((S, D), BF16),
                   jax.ShapeDtypeStruct((8, D), F32)],
        compiler_params=_cparams(("arbitrary",)),
    )(xprev, y, target, vec)


def _ln_silu(cv, g, b):
    mu = jnp.mean(cv, axis=-1, keepdims=True)
    var = jnp.mean(jnp.square(cv - mu), axis=-1, keepdims=True)
    u = (cv - mu) * lax.rsqrt(var + EPS) * g + b
    return _silu(u)


def _conv_fwd(u, wdw, vec, *, name, ts=256):
    S, D2 = u.shape
    D = D2 // 2
    ts = min(ts, S)
    H = CONV_HALO
    row = pl.BlockSpec((ts, D), lambda i: (i, 0))

    def body(u_ref, w_ref, v_ref, vo_ref, cv_ref, z_ref, ext):
        @pl.when(pl.program_id(0) == 0)
        def _():
            ext[0:H, :] = jnp.zeros((H, D), F32)

        uu = u_ref[...]
        v = uu[:, :D] * _sigmoid(uu[:, D:])
        vo_ref[...] = v
        ext[H:H + ts, :] = v
        acc = jnp.zeros((ts, D), F32)
        for t in range(CONV_WIDTH):
            acc = acc + ext[pl.ds(H - (CONV_WIDTH - 1) + t, ts), :] * w_ref[pl.ds(t, 1), :]
        cv = acc + v_ref[0:1, :]
        cv_ref[...] = cv
        z_ref[...] = _ln_silu(cv, v_ref[1:2, :], v_ref[2:3, :]).astype(BF16)
        ext[0:H, :] = ext[ts:ts + H, :]

    return pl.pallas_call(
        body, name=name, grid=(S // ts,),
        in_specs=[pl.BlockSpec((ts, D2), lambda i: (i, 0)), pl.BlockSpec((H, D), lambda i: (0, 0)),
                  pl.BlockSpec((8, D), lambda i: (0, 0))],
        out_specs=[row, row, row],
        out_shape=[jax.ShapeDtypeStruct((S, D), F32), jax.ShapeDtypeStruct((S, D), F32),
                   jax.ShapeDtypeStruct((S, D), BF16)],
        scratch_shapes=[pltpu.VMEM((ts + H, D), F32)],
        compiler_params=_cparams(("arbitrary",)),
    )(u, wdw, vec)


def _conv_bwd(dz, cv, v, u, wdw, vec, *, name, ts=256):
    S, D = cv.shape
    ts = min(ts, S)
    H = CONV_HALO
    nt = S // ts
    per = ts // H
    rev = lambda i: (nt - 1 - i, 0)
    row = pl.BlockSpec((ts, D), rev)

    def body(dz_ref, cv_ref, v_ref, vh_ref, u_ref, w_ref, vec_ref, du_ref, dw_ref, part_ref, dbu_ref,
             dext, vext):
        i = pl.program_id(0)

        @pl.when(i == 0)
        def _():
            dext[ts:ts + H, :] = jnp.zeros((H, D), F32)
            dw_ref[...] = jnp.zeros_like(dw_ref)
            part_ref[...] = jnp.zeros_like(part_ref)
            dbu_ref[...] = jnp.zeros_like(dbu_ref)

        _, vjp = jax.vjp(_ln_silu, cv_ref[...], vec_ref[1:2, :], vec_ref[2:3, :])
        dcv, dg, db = vjp(dz_ref[...])
        part_ref[0:1, :] += jnp.sum(dcv, axis=0, keepdims=True)
        part_ref[1:2, :] += dg
        part_ref[2:3, :] += db
        dext[0:ts, :] = dcv
        vext[0:H, :] = vh_ref[...] * jnp.where(i == nt - 1, 0.0, 1.0)
        vext[H:H + ts, :] = v_ref[...]
        dv = jnp.zeros((ts, D), F32)
        for t in range(CONV_WIDTH):
            off = H - (CONV_WIDTH - 1) + t
            dw_ref[pl.ds(t, 1), :] += jnp.sum(dcv * vext[pl.ds(off, ts), :], axis=0, keepdims=True)
            dv = dv + dext[pl.ds(CONV_WIDTH - 1 - t, ts), :] * w_ref[pl.ds(t, 1), :]
        uu = u_ref[...]
        a, g = uu[:, :D], uu[:, D:]
        sg = _sigmoid(g)
        da = dv * sg
        dg_ = dv * a * sg * (1.0 - sg)
        du = jnp.concatenate([da, dg_], axis=-1)
        du_ref[...] = du.astype(BF16)
        dbu_ref[0:1, :] += jnp.sum(du, axis=0, keepdims=True)
        dext[ts:ts + H, :] = dext[0:H, :]

    return pl.pallas_call(
        body, name=name, grid=(nt,),
        in_specs=[row, row, row,
                  pl.BlockSpec((H, D), lambda i: (jnp.maximum((nt - 1 - i) * per - 1, 0), 0)),
                  pl.BlockSpec((ts, 2 * D), rev), pl.BlockSpec((H, D), lambda i: (0, 0)),
                  pl.BlockSpec((8, D), lambda i: (0, 0))],
        out_specs=[pl.BlockSpec((ts, 2 * D), rev), pl.BlockSpec((H, D), lambda i: (0, 0)),
                   pl.BlockSpec((8, D), lambda i: (0, 0)), pl.BlockSpec((8, 2 * D), lambda i: (0, 0))],
        out_shape=[jax.ShapeDtypeStruct((S, 2 * D), BF16), jax.ShapeDtypeStruct((H, D), F32),
                   jax.ShapeDtypeStruct((8, D), F32), jax.ShapeDtypeStruct((8, 2 * D), F32)],
        scratch_shapes=[pltpu.VMEM((ts + H, D), F32), pltpu.VMEM((ts + H, D), F32)],
        compiler_params=_cparams(("arbitrary",)),
    )(dz, cv, v, v, u, wdw, vec)


def _rope(x, c, s, half):
    x1, x2 = x[:, :half], x[:, half:]
    return jnp.concatenate([x1 * c - x2 * s, x2 * c + x1 * s], axis=-1)


def _rope_t(d, c, s, half):
    d1, d2 = d[:, :half], d[:, half:]
    return jnp.concatenate([d1 * c + d2 * s, d2 * c - d1 * s], axis=-1)


def _gn_gate(y, gate, g, b):
    mu = jnp.mean(y, axis=-1, keepdims=True)
    var = jnp.mean(jnp.square(y - mu), axis=-1, keepdims=True)
    return _silu(gate) * ((y - mu) * lax.rsqrt(var + EPS) * g + b)


def _dot(a, b, mode="nn"):
    return lax.dot_general(a, b, _DIMS[mode], preferred_element_type=F32)


def _ret_tables(H):
    lg = jnp.log(1.0 - 2.0 ** (-5.0 - jnp.arange(H, dtype=F32)))
    idx = jnp.arange(CHUNK, dtype=F32)
    dmat = jnp.exp(lg[:, None, None] * jnp.abs(idx[:, None] - idx[None, :]))
    xi = jnp.exp(lg[:, None] * (idx + 1.0))[..., None]
    zeta = jnp.exp(lg[:, None] * (CHUNK - 1.0 - idx))[..., None]
    dec = jnp.exp(lg * CHUNK)[:, None, None]
    return dmat, xi, zeta, dec


def _ret_specs(R, dk, dv, half, order):
    H = RET_HEADS
    C = CHUNK
    nq = H
    return dict(
        q=pl.BlockSpec((R, dk), lambda h, n: (order(n), h)),
        k=pl.BlockSpec((R, dk), lambda h, n: (order(n), nq + h)),
        v=pl.BlockSpec((R, dv), lambda h, n: (order(n), nq + h)),
        gate=pl.BlockSpec((R, dv), lambda h, n: (order(n), 2 * nq + h)),
        rope=pl.BlockSpec((R, half), lambda h, n: (order(n), 0)),
        dmat=pl.BlockSpec((None, C, C), lambda h, n: (h, 0, 0)),
        col=pl.BlockSpec((None, C, 1), lambda h, n: (h, 0, 0)),
        one=pl.BlockSpec((None, 1, 1), lambda h, n: (h, 0, 0)),
        gn=pl.BlockSpec((None, 1, dv), lambda h, n: (h, 0, 0)),
        yv=pl.BlockSpec((R, dv), lambda h, n: (order(n), h)),
        yk=pl.BlockSpec((R, dk), lambda h, n: (order(n), h)),
    )


def _ret_fwd(proj, cos, sin, tables, gn_g, gn_b, *, name, cps=4):
    S = proj.shape[0]
    D = proj.shape[1] // 6
    H, C = RET_HEADS, CHUNK
    dk, dv, half = D // H, 2 * D // H, D // H // 2
    nc = S // C
    cps = min(cps, nc)
    R = cps * C
    scale = dk ** -0.5
    sp = _ret_specs(R, dk, dv, half, lambda n: n)
    dmat, xi, zeta, dec = tables

    def body(q_ref, k_ref, v_ref, g_ref, cos_ref, sin_ref, dm_ref, xi_ref, ze_ref, dec_ref, gg_ref, gb_ref,
             y_ref, y2_ref, st_ref, state):
        @pl.when(pl.program_id(1) == 0)
        def _():
            state[...] = jnp.zeros_like(state)

        dm, xv, zv, dc = dm_ref[...], xi_ref[...], ze_ref[...], dec_ref[...]
        for j in range(cps):
            rows = pl.ds(j * C, C)
            cs, sn = cos_ref[rows, :], sin_ref[rows, :]
            qr = _rope(q_ref[rows, :].astype(F32), cs, sn, half)
            kr = _rope(k_ref[rows, :].astype(F32), cs, sn, half) * scale
            vb = v_ref[rows, :]
            p = (_dot(qr.astype(BF16), kr.astype(BF16), "nt") * dm).astype(BF16)
            st = state[...]
            stb = st.astype(BF16)
            st_ref[j] = stb
            y = _dot(p, vb) + _dot((qr * xv).astype(BF16), stb)
            state[...] = st * dc + _dot((kr * zv).astype(BF16), vb, "tn")
            y_ref[rows, :] = y
            y2_ref[rows, :] = _gn_gate(y, g_ref[rows, :].astype(F32), gg_ref[...], gb_ref[...]).astype(BF16)

    return pl.pallas_call(
        body, name=name, grid=(H, nc // cps),
        in_specs=[sp["q"], sp["k"], sp["v"], sp["gate"], sp["rope"], sp["rope"], sp["dmat"], sp["col"],
                  sp["col"], sp["one"], sp["gn"], sp["gn"]],
        out_specs=[sp["yv"], sp["yv"], pl.BlockSpec((None, cps, dk, dv), lambda h, n: (h, n, 0, 0))],
        out_shape=[jax.ShapeDtypeStruct((S, 2 * D), F32), jax.ShapeDtypeStruct((S, 2 * D), BF16),
                   jax.ShapeDtypeStruct((H, nc, dk, dv), BF16)],
        scratch_shapes=[pltpu.VMEM((dk, dv), F32)],
        compiler_params=_cparams(("arbitrary", "arbitrary")),
    )(proj, proj, proj, proj, cos, sin, dmat, xi, zeta, dec, gn_g, gn_b)


def _ret_bwd(proj, cos, sin, tables, gn_g, gn_b, y, dy2, states, *, name, cps=4):
    S = proj.shape[0]
    D = proj.shape[1] // 6
    H, C = RET_HEADS, CHUNK
    dk, dv, half = D // H, 2 * D // H, D // H // 2
    nc = S // C
    cps = min(cps, nc)
    ns = nc // cps
    R = cps * C
    scale = dk ** -0.5
    order = lambda n: ns - 1 - n
    sp = _ret_specs(R, dk, dv, half, order)
    dmat, xi, zeta, dec = tables

    def body(q_ref, k_ref, v_ref, g_ref, cos_ref, sin_ref, dm_ref, xi_ref, ze_ref, dec_ref, gg_ref, gb_ref,
             y_ref, dy2_ref, st_ref, dq_ref, dk_ref, dv_ref, dg_ref, dgg_ref, dgb_ref, gst):
        @pl.when(pl.program_id(1) == 0)
        def _():
            gst[...] = jnp.zeros_like(gst)
            dgg_ref[...] = jnp.zeros_like(dgg_ref)
            dgb_ref[...] = jnp.zeros_like(dgb_ref)

        dm, xv, zv, dc = dm_ref[...], xi_ref[...], ze_ref[...], dec_ref[...]
        for j in reversed(range(cps)):
            rows = pl.ds(j * C, C)
            cs, sn = cos_ref[rows, :], sin_ref[rows, :]
            _, vjp = jax.vjp(_gn_gate, y_ref[rows, :], g_ref[rows, :].astype(F32), gg_ref[...], gb_ref[...])
            dy, dgate, dgg, dgb = vjp(dy2_ref[rows, :])
            dgg_ref[...] += dgg
            dgb_ref[...] += dgb
            dg_ref[rows, :] = dgate.astype(BF16)
            dyb = dy.astype(BF16)
            qr = _rope(q_ref[rows, :].astype(F32), cs, sn, half)
            kr = _rope(k_ref[rows, :].astype(F32), cs, sn, half) * scale
            qb, kb, vb = qr.astype(BF16), kr.astype(BF16), v_ref[rows, :]
            p = (_dot(qb, kb, "nt") * dm).astype(BF16)
            g = gst[...]
            gb16 = g.astype(BF16)
            sprev = st_ref[j]
            dvv = _dot(p, dyb, "tn") + _dot((kr * zv).astype(BF16), gb16)
            dpb = (_dot(dyb, vb, "nt") * dm).astype(BF16)
            dqr = _dot(dpb, kb) + _dot(dyb, sprev, "nt") * xv
            dkr = _dot(dpb, qb, "tn") + _dot(vb, gb16, "nt") * zv
            gst[...] = g * dc + _dot((qr * xv).astype(BF16), dyb, "tn")
            dq_ref[rows, :] = _rope_t(dqr, cs, sn, half).astype(BF16)
            dk_ref[rows, :] = _rope_t(dkr * scale, cs, sn, half).astype(BF16)
            dv_ref[rows, :] = dvv.astype(BF16)

    return pl.pallas_call(
        body, name=name, grid=(H, ns),
        in_specs=[sp["q"], sp["k"], sp["v"], sp["gate"], sp["rope"], sp["rope"], sp["dmat"], sp["col"],
                  sp["col"], sp["one"], sp["gn"], sp["gn"], sp["yv"], sp["yv"],
                  pl.BlockSpec((None, cps, dk, dv), lambda h, n: (h, order(n), 0, 0))],
        out_specs=[sp["yk"], sp["yk"], sp["yv"], sp["yv"], sp["gn"], sp["gn"]],
        out_shape=[jax.ShapeDtypeStruct((S, D), BF16), jax.ShapeDtypeStruct((S, D), BF16),
                   jax.ShapeDtypeStruct((S, 2 * D), BF16), jax.ShapeDtypeStruct((S, 2 * D), BF16),
                   jax.ShapeDtypeStruct((H, 1, dv), F32), jax.ShapeDtypeStruct((H, 1, dv), F32)],
        scratch_shapes=[pltpu.VMEM((dk, dv), F32)],
        compiler_params=_cparams(("arbitrary", "arbitrary")),
    )(proj, proj, proj, proj, cos, sin, dmat, xi, zeta, dec, gn_g, gn_b, y, dy2, states)


def _rows_tile(rows, cols, n_arrays):
    cap = max(8, V7X_VMEM_LIMIT // 3 // (n_arrays * 2 * 4 * cols))
    t = rows
    while t > cap and t % 2 == 0:
        t //= 2
    return t


def _add2(a, b, *, name):
    shape = a.shape
    a2, b2 = a.reshape(-1, shape[-1]), b.reshape(-1, shape[-1])
    R, Cc = a2.shape
    tr = _rows_tile(R, Cc, 3)
    spec = pl.BlockSpec((tr, Cc), lambda i: (i, 0))

    def body(a_ref, b_ref, o_ref):
        o_ref[...] = a_ref[...] + b_ref[...]

    out = pl.pallas_call(
        body, name=name, grid=(R // tr,), in_specs=[spec, spec], out_specs=spec,
        out_shape=jax.ShapeDtypeStruct((R, Cc), F32), compiler_params=_cparams(("parallel",)),
    )(a2, b2)
    return out.reshape(shape)


def _sum_slots(x, *, name):
    NS, R, Cc = x.shape
    tr = _rows_tile(R, Cc, NS + 1)

    def body(x_ref, o_ref):
        acc = x_ref[0]
        for s in range(1, NS):
            acc = acc + x_ref[s]
        o_ref[...] = acc

    return pl.pallas_call(
        body, name=name, grid=(R // tr,), in_specs=[pl.BlockSpec((NS, tr, Cc), lambda i: (0, i, 0))],
        out_specs=pl.BlockSpec((tr, Cc), lambda i: (i, 0)),
        out_shape=jax.ShapeDtypeStruct((R, Cc), F32), compiler_params=_cparams(("parallel",)),
    )(x)


def _adamw(gslots, w, m, v, *, name):
    L, NS, R, Cc = gslots.shape
    tr = _rows_tile(R, Cc, NS + 7)
    gspec = pl.BlockSpec((None, NS, tr, Cc), lambda l, i: (l, 0, i, 0))
    spec = pl.BlockSpec((None, tr, Cc), lambda l, i: (l, i, 0))
    c1 = 1.0 - ADAM_B1 ** ADAM_STEP
    c2 = 1.0 - ADAM_B2 ** ADAM_STEP

    def body(g_ref, w_ref, m_ref, v_ref, go_ref, d_ref, mo_ref, vo_ref):
        g = g_ref[0]
        for s in range(1, NS):
            g = g + g_ref[s]
        mn = ADAM_B1 * m_ref[...] + (1.0 - ADAM_B1) * g
        vn = ADAM_B2 * v_ref[...] + (1.0 - ADAM_B2) * jnp.square(g)
        m_hat = mn / c1
        v_hat = vn / c2
        go_ref[...] = g
        d_ref[...] = -ADAM_LR * (m_hat / (jnp.sqrt(v_hat) + ADAM_EPS) + ADAM_WD * w_ref[...])
        mo_ref[...] = mn
        vo_ref[...] = vn

    sd = jax.ShapeDtypeStruct((L, R, Cc), F32)
    return pl.pallas_call(
        body, name=name, grid=(L, R // tr), in_specs=[gspec, spec, spec, spec],
        out_specs=[spec, spec, spec, spec], out_shape=[sd, sd, sd, sd],
        compiler_params=_cparams(("parallel", "parallel")),
    )(gslots, w, m, v)


def _me():
    return lax.axis_index("x"), lax.axis_index("y"), lax.axis_index("c")


def _flip(v, bit):
    return 1 - v if bit else v


def _allgather8(x, *, name):
    def body(x_ref, out_ref, send_sems, recv_sems, loc_sem):
        mx, my, mc = _me()
        me = 4 * mx + 2 * my + mc
        loc = pltpu.make_async_copy(x_ref, out_ref.at[me], loc_sem)
        loc.start()
        sends, recvs = [], []
        for k in range(1, N_DEV):
            px, py, pc = _flip(mx, k & 4), _flip(my, k & 2), _flip(mc, k & 1)
            sends.append(pltpu.make_async_remote_copy(
                src_ref=x_ref, dst_ref=out_ref.at[me], send_sem=send_sems.at[k - 1],
                recv_sem=recv_sems.at[k - 1], device_id=(px, py, pc), device_id_type=MESH))
            recvs.append(pltpu.make_async_remote_copy(
                src_ref=x_ref, dst_ref=out_ref.at[4 * px + 2 * py + pc], send_sem=send_sems.at[k - 1],
                recv_sem=recv_sems.at[k - 1], device_id=(px, py, pc), device_id_type=MESH))
        for cp in sends:
            cp.start()
        for cp in recvs:
            cp.wait_recv()
        for cp in sends:
            cp.wait_send()
        loc.wait()

    return pl.pallas_call(
        body, name=name, in_specs=[ANY], out_specs=ANY,
        out_shape=jax.ShapeDtypeStruct((N_DEV,) + x.shape, x.dtype),
        scratch_shapes=[pltpu.SemaphoreType.DMA((N_DEV - 1,)), pltpu.SemaphoreType.DMA((N_DEV - 1,)),
                        pltpu.SemaphoreType.DMA],
    )(x)


def _gather_chips(arrays, *, name):
    n = len(arrays)

    def body(*refs):
        ins, outs = refs[:n], refs[n:2 * n]
        send_sems, recv_sems, loc_sems = refs[2 * n:]
        mx, my, mc = _me()
        me = 2 * mx + my
        locs, sends, recvs = [], [], []
        for a in range(n):
            locs.append(pltpu.make_async_copy(ins[a], outs[a].at[me], loc_sems.at[a]))
            for k in range(1, N_CHIPS):
                px, py = _flip(mx, k & 2), _flip(my, k & 1)
                sems = dict(send_sem=send_sems.at[a, k - 1], recv_sem=recv_sems.at[a, k - 1],
                            device_id=(px, py, mc), device_id_type=MESH)
                sends.append(pltpu.make_async_remote_copy(src_ref=ins[a], dst_ref=outs[a].at[me], **sems))
                recvs.append(pltpu.make_async_remote_copy(src_ref=ins[a], dst_ref=outs[a].at[2 * px + py], **sems))
        for cp in locs + sends:
            cp.start()
        for cp in recvs:
            cp.wait_recv()
        for cp in sends:
            cp.wait_send()
        for cp in locs:
            cp.wait()

    return pl.pallas_call(
        body, name=name, in_specs=[ANY] * n, out_specs=[ANY] * n,
        out_shape=[jax.ShapeDtypeStruct((N_CHIPS,) + a.shape, a.dtype) for a in arrays],
        scratch_shapes=[pltpu.SemaphoreType.DMA((n, N_CHIPS - 1)), pltpu.SemaphoreType.DMA((n, N_CHIPS - 1)),
                        pltpu.SemaphoreType.DMA((n,))],
    )(*arrays)


def _swap_sibling(arrays, *, name):
    n = len(arrays)

    def body(*refs):
        ins, outs = refs[:n], refs[n:2 * n]
        send_sems, recv_sems = refs[2 * n:]
        mx, my, mc = _me()
        cps = [pltpu.make_async_remote_copy(
            src_ref=ins[a], dst_ref=outs[a], send_sem=send_sems.at[a], recv_sem=recv_sems.at[a],
            device_id=(mx, my, 1 - mc), device_id_type=MESH) for a in range(n)]
        for cp in cps:
            cp.start()
        for cp in cps:
            cp.wait_recv()
        for cp in cps:
            cp.wait_send()

    return pl.pallas_call(
        body, name=name, in_specs=[ANY] * n, out_specs=[ANY] * n,
        out_shape=[jax.ShapeDtypeStruct(a.shape, a.dtype) for a in arrays],
        scratch_shapes=[pltpu.SemaphoreType.DMA((n,)), pltpu.SemaphoreType.DMA((n,))],
    )(*arrays)


def _scatter_chips(groups, *, name):
    flat = [(gi, l, a) for gi, g in enumerate(groups) for l, a in enumerate(g)]
    n = len(flat)
    ng = len(groups)

    def body(*refs):
        ins, outs = refs[:n], refs[n:n + ng]
        send_sems, recv_sems, loc_sems = refs[n + ng:]
        mx, my, mc = _me()
        me = 2 * mx + my
        locs, sends, recvs = [], [], []
        for a, (gi, l, _) in enumerate(flat):
            locs.append(pltpu.make_async_copy(ins[a].at[me], outs[gi].at[l, me], loc_sems.at[a]))
            for k in range(1, N_CHIPS):
                px, py = _flip(mx, k & 2), _flip(my, k & 1)
                peer = 2 * px + py
                sems = dict(send_sem=send_sems.at[a, k - 1], recv_sem=recv_sems.at[a, k - 1],
                            device_id=(px, py, mc), device_id_type=MESH)
                sends.append(pltpu.make_async_remote_copy(
                    src_ref=ins[a].at[peer], dst_ref=outs[gi].at[l, me], **sems))
                recvs.append(pltpu.make_async_remote_copy(
                    src_ref=ins[a].at[peer], dst_ref=outs[gi].at[l, peer], **sems))
        for cp in locs + sends:
            cp.start()
        for cp in recvs:
            cp.wait_recv()
        for cp in sends:
            cp.wait_send()
        for cp in locs:
            cp.wait()

    return pl.pallas_call(
        body, name=name, in_specs=[ANY] * n, out_specs=[ANY] * ng,
        out_shape=[jax.ShapeDtypeStruct((len(g),) + g[0].shape, g[0].dtype) for g in groups],
        scratch_shapes=[pltpu.SemaphoreType.DMA((n, N_CHIPS - 1)), pltpu.SemaphoreType.DMA((n, N_CHIPS - 1)),
                        pltpu.SemaphoreType.DMA((n,))],
    )(*[a for _, _, a in flat])


BIG = ("conv_w_pw1", "conv_w_pw2", "ret_w_in", "ret_w_out", "mlp_w1", "mlp_w2")
COLS = ("conv_w_pw1", "ret_w_in", "mlp_w1")
SMALL = ("ada_b", "norm_mix_g", "norm_mlp_g", "conv_b_pw1", "conv_w_dw", "conv_b_dw", "conv_ln_g", "conv_ln_b",
         "conv_b_pw2", "ret_gn_g", "ret_gn_b", "final_norm_g")
SMALL_SHARDED = ("conv_w_dw", "ret_gn_g", "ret_gn_b")
WEIGHTS = ("ada_w", "ada_b", "norm_mix_g", "norm_mlp_g", "conv_w_pw1", "conv_b_pw1", "conv_w_dw", "conv_b_dw",
           "conv_ln_g", "conv_ln_b", "conv_w_pw2", "conv_b_pw2", "ret_w_in", "ret_gn_g", "ret_gn_b", "ret_w_out",
           "mlp_w1", "mlp_w2", "final_norm_g")


def _vec8(rows, D):
    rows = [r.reshape(1, D).astype(F32) for r in rows]
    return jnp.concatenate(rows + [jnp.zeros((8 - len(rows), D), F32)], axis=0)


def _unshard_last(g):
    nd = g.ndim
    t = jnp.transpose(g, tuple(range(1, nd - 1)) + (0, nd - 1))
    return t.reshape(t.shape[:-2] + (t.shape[-2] * t.shape[-1],))


def _pack(parts):
    flat = jnp.concatenate([p.reshape(-1).astype(F32) for p in parts])
    pad = (-flat.shape[0]) % 1024
    return jnp.concatenate([flat, jnp.zeros((pad,), F32)]).reshape(-1, 128)


def _unpack(packed, shapes):
    flat = packed.reshape(-1)
    out, pos = [], 0
    for s in shapes:
        n = math.prod(s)
        out.append(flat[pos:pos + n].reshape(s))
        pos += n
    return out


def kernel(x, c, ada_w, ada_b, norm_mix_g, norm_mlp_g, conv_w_pw1, conv_b_pw1, conv_w_dw, conv_b_dw, conv_ln_g, conv_ln_b, conv_w_pw2, conv_b_pw2, ret_w_in, ret_gn_g, ret_gn_b, ret_w_out, mlp_w1, mlp_w2, final_norm_g, loss_target, m_ada_w, m_ada_b, m_norm_mix_g, m_norm_mlp_g, m_conv_w_pw1, m_conv_b_pw1, m_conv_w_dw, m_conv_b_dw, m_conv_ln_g, m_conv_ln_b, m_conv_w_pw2, m_conv_b_pw2, m_ret_w_in, m_ret_gn_g, m_ret_gn_b, m_ret_w_out, m_mlp_w1, m_mlp_w2, m_final_norm_g, v_ada_w, v_ada_b, v_norm_mix_g, v_norm_mlp_g, v_conv_w_pw1, v_conv_b_pw1, v_conv_w_dw, v_conv_b_dw, v_conv_ln_g, v_conv_ln_b, v_conv_w_pw2, v_conv_b_pw2, v_ret_w_in, v_ret_gn_g, v_ret_gn_b, v_ret_w_out, v_mlp_w1, v_mlp_w2, v_final_norm_g):
    W = dict(ada_w=ada_w, ada_b=ada_b, norm_mix_g=norm_mix_g, norm_mlp_g=norm_mlp_g, conv_w_pw1=conv_w_pw1,
             conv_b_pw1=conv_b_pw1, conv_w_dw=conv_w_dw, conv_b_dw=conv_b_dw, conv_ln_g=conv_ln_g,
             conv_ln_b=conv_ln_b, conv_w_pw2=conv_w_pw2, conv_b_pw2=conv_b_pw2, ret_w_in=ret_w_in,
             ret_gn_g=ret_gn_g, ret_gn_b=ret_gn_b, ret_w_out=ret_w_out, mlp_w1=mlp_w1, mlp_w2=mlp_w2,
             final_norm_g=final_norm_g)
    Mo = dict(ada_w=m_ada_w, ada_b=m_ada_b, norm_mix_g=m_norm_mix_g, norm_mlp_g=m_norm_mlp_g,
              conv_w_pw1=m_conv_w_pw1, conv_b_pw1=m_conv_b_pw1, conv_w_dw=m_conv_w_dw, conv_b_dw=m_conv_b_dw,
              conv_ln_g=m_conv_ln_g, conv_ln_b=m_conv_ln_b, conv_w_pw2=m_conv_w_pw2, conv_b_pw2=m_conv_b_pw2,
              ret_w_in=m_ret_w_in, ret_gn_g=m_ret_gn_g, ret_gn_b=m_ret_gn_b, ret_w_out=m_ret_w_out,
              mlp_w1=m_mlp_w1, mlp_w2=m_mlp_w2, final_norm_g=m_final_norm_g)
    Vo = dict(ada_w=v_ada_w, ada_b=v_ada_b, norm_mix_g=v_norm_mix_g, norm_mlp_g=v_norm_mlp_g,
              conv_w_pw1=v_conv_w_pw1, conv_b_pw1=v_conv_b_pw1, conv_w_dw=v_conv_w_dw, conv_b_dw=v_conv_b_dw,
              conv_ln_g=v_conv_ln_g, conv_ln_b=v_conv_ln_b, conv_w_pw2=v_conv_w_pw2, conv_b_pw2=v_conv_b_pw2,
              ret_w_in=v_ret_w_in, ret_gn_g=v_ret_gn_g, ret_gn_b=v_ret_gn_b, ret_w_out=v_ret_w_out,
              mlp_w1=v_mlp_w1, mlp_w2=v_mlp_w2, final_norm_g=v_final_norm_g)

    S, D = x.shape[1], x.shape[2]
    depth = ada_w.shape[0]
    H = RET_HEADS
    dv = 2 * D // H
    xs = x.reshape(S, D)
    target = loss_target.reshape(S, D)
    mx, my, mc = _me()
    chip = 2 * mx + my
    dev = 4 * mx + 2 * my + mc

    send = [W[nm][l].astype(BF16) for nm in BIG for l in range(W[nm].shape[0])]
    send += [W[nm] for nm in SMALL_SHARDED]
    got = _gather_chips(send, name="gather_weights")
    Wg, pos = {}, 0
    for nm in BIG:
        L = W[nm].shape[0]
        Wg[nm] = got[pos:pos + L]
        pos += L
    full_small = {nm: _unshard_last(got[pos + i]) for i, nm in enumerate(SMALL_SHARDED)}

    def wfull(nm, l):
        g = Wg[nm][l]
        return g.reshape(g.shape[0] * g.shape[1], g.shape[2])

    c_all = _allgather8(c.reshape(8, D // 8), name="gather_c").reshape(N_DEV, D)
    cs_ada = ada_w.shape[2]
    bias_sh = lax.dynamic_slice_in_dim(ada_b.reshape(depth, N_CHIPS, cs_ada), chip, 1, axis=1)
    mod_sh = _mm(c_all, ada_w, mode="nn", name="ada_fwd", b3d=True, tn=cs_ada, a_silu=True,
                 bias=bias_sh.reshape(1, depth * cs_ada))
    mod_all = _allgather8(mod_sh, name="gather_mod")[0::2]
    mod_me = lax.dynamic_slice_in_dim(mod_all, dev, 1, axis=1).reshape(N_CHIPS, depth, cs_ada)
    mod = jnp.transpose(mod_me, (1, 0, 2)).reshape(depth, 6, D)

    pos_ids = jnp.arange(S, dtype=F32)
    dk = D // H
    inv = ROPE_BASE ** (-jnp.arange(0, dk, 2, dtype=F32) / dk)
    ang = pos_ids[:, None] * inv[None, :]
    cos_t, sin_t = jnp.cos(ang), jnp.sin(ang)
    tables = _ret_tables(H)
    gn_g_full = full_small["ret_gn_g"].reshape(-1, H, 1, dv)
    gn_b_full = full_small["ret_gn_b"].reshape(-1, H, 1, dv)
    wdw_full = full_small["conv_w_dw"]

    def wdw_pad(j):
        return jnp.concatenate([wdw_full[j], jnp.zeros((CONV_HALO - CONV_WIDTH, D), F32)], axis=0)

    saved = []
    xa, y_prev, gate_prev = xs, None, None
    for l in range(depth):
        j = l // 2
        sv = {}
        vec_a = _vec8([gate_prev if gate_prev is not None else jnp.zeros((D,), F32), norm_mix_g[l], mod[l, 0],
                       mod[l, 1]], D)
        xa, h = _row_fwd(xa, y_prev, vec_a, name="row_fwd" if y_prev is not None else "row_fwd_first")
        sv.update(xa=xa, h=h, vec_a=vec_a)
        if l % 2 == 0:
            u = _mm(h, Wg["conv_w_pw1"][j], mode="nn", name="pw1_fwd", b3d=True,
                    tn=Wg["conv_w_pw1"][j].shape[2], bias=conv_b_pw1[j].reshape(1, -1))
            cvec = _vec8([conv_b_dw[j], conv_ln_g[j], conv_ln_b[j]], D)
            v_glu, cv, z = _conv_fwd(u, wdw_pad(j), cvec, name="conv_fwd")
            ymix = _mm(z, wfull("conv_w_pw2", j), mode="nn", name="pw2_fwd", bias=conv_b_pw2[j].reshape(1, -1))
            sv.update(u=u, v_glu=v_glu, cv=cv, z=z, cvec=cvec)
        else:
            proj = _mm(h, Wg["ret_w_in"][j], mode="nn", name="win_fwd", b3d=True, out_dtype=BF16,
                       tn=Wg["ret_w_in"][j].shape[2])
            yr, y2, states = _ret_fwd(proj, cos_t, sin_t, tables, gn_g_full[j], gn_b_full[j], name="ret_fwd")
            ymix = _mm(y2, wfull("ret_w_out", j), mode="nn", name="wout_fwd")
            sv.update(proj=proj, yr=yr, y2=y2, states=states)
        vec_b = _vec8([mod[l, 2], norm_mlp_g[l], mod[l, 3], mod[l, 4]], D)
        xb, h2 = _row_fwd(xa, ymix, vec_b, name="row_fwd")
        ra, p = _mm(h2, Wg["mlp_w1"][l], mode="nn", name="w1_fwd", b3d=True, tn=Wg["mlp_w1"][l].shape[2],
                    epi="relu2")
        mo = _mm(p, wfull("mlp_w2", l), mode="nn", name="w2_fwd")
        sv.update(ymix=ymix, xb=xb, h2=h2, ra=ra, p=p, mo=mo, vec_b=vec_b)
        saved.append(sv)
        xa, y_prev, gate_prev = xb, mo, mod[l, 5]

    fvec = _vec8([gate_prev, final_norm_g], D)
    dx, dyb, fpart = _final(xa, y_prev, target, fvec, name="final")
    loss = lax.psum(jnp.sum(fpart[2]), ("x", "y", "c"))
    G = {nm: [None] * W[nm].shape[0] for nm in BIG}
    dmod = [[None] * 6 for _ in range(depth)]
    dmod[depth - 1][5] = fpart[0]
    sg = dict(norm_mix_g=[None] * depth, norm_mlp_g=[None] * depth, final_norm_g=fpart[1])
    n_conv, n_ret = conv_w_pw1.shape[0], ret_w_in.shape[0]
    for nm in ("conv_b_pw1", "conv_w_dw", "conv_b_dw", "conv_ln_g", "conv_ln_b", "conv_b_pw2"):
        sg[nm] = [None] * n_conv
    for nm in ("ret_gn_g", "ret_gn_b"):
        sg[nm] = [None] * n_ret

    for l in reversed(range(depth)):
        j = l // 2
        sv = saved[l]
        w1, w2 = Wg["mlp_w1"][l], wfull("mlp_w2", l)
        cs1 = w1.shape[2]
        da = _mm(dyb, w2, mode="nt", name="w2_dx", out_dtype=BF16, epi="mul2", extra=sv["ra"])
        gw2 = _mm(sv["p"], dyb, mode="tn", name="w2_dw")
        G["mlp_w2"][l] = gw2.reshape(N_CHIPS, gw2.shape[0] // N_CHIPS, gw2.shape[1])
        G["mlp_w1"][l] = _mm(sv["h2"], da, mode="tn", name="w1_dw", out3d=(N_CHIPS, cs1), tn=cs1)
        dh2 = _mm(da, w1, mode="nt", name="w1_dx", b3d=True, tk=cs1)
        dx, dyb, part = _row_bwd(sv["xb"], dh2, dx, sv["ymix"], sv["vec_b"], name="row_bwd")
        dmod[l][2], sg["norm_mlp_g"][l], dmod[l][3], dmod[l][4] = part[0], part[1], part[2], part[3]
        if l % 2 == 0:
            sg["conv_b_pw2"][j] = part[4]
            wp1, wp2 = Wg["conv_w_pw1"][j], wfull("conv_w_pw2", j)
            csp = wp1.shape[2]
            dz = _mm(dyb, wp2, mode="nt", name="pw2_dx")
            gp2 = _mm(sv["z"], dyb, mode="tn", name="pw2_dw")
            G["conv_w_pw2"][j] = gp2.reshape(N_CHIPS, gp2.shape[0] // N_CHIPS, gp2.shape[1])
            du, dwdw, cpart, dbu = _conv_bwd(dz, sv["cv"], sv["v_glu"], sv["u"], wdw_pad(j), sv["cvec"],
                                             name="conv_bwd")
            sg["conv_w_dw"][j] = dwdw[:CONV_WIDTH]
            sg["conv_b_dw"][j], sg["conv_ln_g"][j], sg["conv_ln_b"][j] = cpart[0], cpart[1], cpart[2]
            sg["conv_b_pw1"][j] = dbu[0]
            G["conv_w_pw1"][j] = _mm(sv["h"], du, mode="tn", name="pw1_dw", out3d=(N_CHIPS, csp), tn=csp)
            dh = _mm(du, wp1, mode="nt", name="pw1_dx", b3d=True, tk=csp)
        else:
            wi, wo = Wg["ret_w_in"][j], wfull("ret_w_out", j)
            csi = wi.shape[2]
            dy2 = _mm(dyb, wo, mode="nt", name="wout_dx")
            gwo = _mm(sv["y2"], dyb, mode="tn", name="wout_dw")
            G["ret_w_out"][j] = gwo.reshape(N_CHIPS, gwo.shape[0] // N_CHIPS, gwo.shape[1])
            dq, dkk, dvv, dgt, dgg, dgb = _ret_bwd(sv["proj"], cos_t, sin_t, tables, gn_g_full[j], gn_b_full[j],
                                                   sv["yr"], dy2, sv["states"], name="ret_bwd")
            sg["ret_gn_g"][j], sg["ret_gn_b"][j] = dgg.reshape(H, dv), dgb.reshape(H, dv)
            dproj = jnp.concatenate([dq, dkk, dvv, dgt], axis=1)
            G["ret_w_in"][j] = _mm(sv["h"], dproj, mode="tn", name="win_dw", out3d=(N_CHIPS, csi), tn=csi)
            dh = _mm(dproj, wi, mode="nt", name="win_dx", b3d=True, tk=csi)
        yp = saved[l - 1]["mo"] if l > 0 else dh
        dx, dyb, part = _row_bwd(sv["xa"], dh, dx, yp, sv["vec_a"], name="row_bwd")
        sg["norm_mix_g"][l], dmod[l][0], dmod[l][1] = part[1], part[2], part[3]
        if l > 0:
            dmod[l - 1][5] = part[0]
    grad_x = dx.reshape(x.shape)

    dmod_me = jnp.stack([jnp.stack(r) for r in dmod]).reshape(depth, 6 * D)
    sgrads = dict(ada_b=dmod_me)
    for nm in SMALL[1:]:
        sgrads[nm] = sg[nm] if nm == "final_norm_g" else jnp.stack(sg[nm])
    full_shapes = [sgrads[nm].shape for nm in SMALL]
    packed_all = _allgather8(_pack([sgrads[nm] for nm in SMALL]), name="gather_small_grads")
    sums = _unpack(_sum_slots(packed_all, name="sum_small_grads"), full_shapes)
    gsm = {}
    for nm, g in zip(SMALL, sums):
        if nm in SMALL_SHARDED:
            n = g.shape[-1] // N_CHIPS
            g = lax.dynamic_slice_in_dim(g.reshape(g.shape[:-1] + (N_CHIPS, n)), chip, 1, axis=g.ndim - 1)
            g = g.reshape(g.shape[:-2] + (n,))
        gsm[nm] = g.reshape(W[nm].shape)
    pw, pm, pv, pg = (_pack([t[nm] for nm in SMALL]) for t in (W, Mo, Vo, gsm))
    e4 = lambda a: a.reshape((1, 1) + a.shape)
    e3 = lambda a: a.reshape((1,) + a.shape)
    sres = _adamw(e4(pg), e3(pw), e3(pm), e3(pv), name="adamw_small")
    shard_shapes = [W[nm].shape for nm in SMALL]
    small_out = [dict(zip(SMALL, _unpack(r, shard_shapes))) for r in sres]

    n_mod_rows = depth * 6 * D // 128
    dmod_all = packed_all[:, :n_mod_rows].reshape(N_DEV, depth, N_CHIPS, cs_ada)
    dmod_cols = lax.dynamic_slice_in_dim(dmod_all, chip, 1, axis=2).reshape(N_DEV, depth * cs_ada)
    kpad = 128 - N_DEV
    dmod_pad = jnp.concatenate([dmod_cols, jnp.zeros((kpad, depth * cs_ada), F32)], axis=0)
    ct_pad = jnp.concatenate([c_all.T, jnp.zeros((D, kpad), F32)], axis=1)
    g_ada = _mm(ct_pad, dmod_pad, mode="nn", name="ada_dw", a_silu=True, out3d=(depth, cs_ada), tn=cs_ada)
    ada_out = _adamw(g_ada.reshape(depth, 1, D, cs_ada), ada_w, m_ada_w, v_ada_w, name="adamw_ada")

    flat_g = [G[nm][l] for nm in BIG for l in range(len(G[nm]))]
    sib = _swap_sibling(flat_g, name="swap_grads")
    chip_sum = [_add2(a, b, name="add_grads") for a, b in zip(flat_g, sib)]
    groups, pos = [], 0
    for nm in BIG:
        L = len(G[nm])
        groups.append(chip_sum[pos:pos + L])
        pos += L
    slots = _scatter_chips(groups, name="scatter_grads")
    big_out = {nm: _adamw(sl, W[nm], Mo[nm], Vo[nm], name="adamw_big") for nm, sl in zip(BIG, slots)}

    def res(nm, i):
        if nm == "ada_w":
            return ada_out[i]
        if nm in big_out:
            return big_out[nm][i]
        return small_out[i][nm]

    return (loss, grad_x, *[res(nm, 0) for nm in WEIGHTS], *[res(nm, 1) for nm in WEIGHTS],
            *[res(nm, 2) for nm in WEIGHTS], *[res(nm, 3) for nm in WEIGHTS])
```

```python
import functools
import math

import jax
import jax.numpy as jnp
from jax import lax
from jax.experimental import pallas as pl
from jax.experimental.pallas import tpu as pltpu

F32 = jnp.float32
BF16 = jnp.bfloat16
MESH = pl.DeviceIdType.MESH

EPS = 1e-6
CHUNK = 64
CONV_WIDTH = 31
CONV_HALO = 32
RET_HEADS = 4
ROPE_BASE = 10000.0
ADAM_LR = 0.001
ADAM_B1 = 0.9
ADAM_B2 = 0.999
ADAM_EPS = 1e-08
ADAM_WD = 0.01
ADAM_STEP = 10
N_CHIPS = 4
N_DEV = 8
V7X_VMEM_LIMIT = 48 * 1024 * 1024
ANY = pl.BlockSpec(memory_space=pl.ANY)


def _cparams(sem=None):
    return pltpu.CompilerParams(dimension_semantics=sem, vmem_limit_bytes=V7X_VMEM_LIMIT)


def _sigmoid(x):
    return jax.nn.sigmoid(x)


def _silu(x):
    return x * _sigmoid(x)


_DIMS = {
    "nn": (((1,), (0,)), ((), ())),
    "nt": (((1,), (1,)), ((), ())),
    "tn": (((0,), (0,)), ((), ())),
}


def _mm(a, b, *, mode, name, out_dtype=F32, tm=1024, tn=1024, tk=1024, b3d=False, out3d=None,
        bias=None, epi=None, extra=None, a_silu=False):
    if mode == "tn":
        K, M = a.shape
    else:
        M, K = a.shape
    if b3d:
        P, R, Cs = b.shape
        bshape = (R, P * Cs)
    else:
        bshape = b.shape
    N = bshape[0] if mode == "nt" else bshape[1]
    assert (bshape[1] if mode == "nt" else bshape[0]) == K, (name, a.shape, b.shape)
    tm, tn, tk = min(tm, M), min(tn, N), min(tk, K)
    assert M % tm == 0 and N % tn == 0 and K % tk == 0, (name, M, N, K, tm, tn, tk)
    nk = K // tk

    if mode == "tn":
        a_spec = pl.BlockSpec((tk, tm), lambda i, j, k: (k, i))
    else:
        a_spec = pl.BlockSpec((tm, tk), lambda i, j, k: (i, k))
    if mode == "nt":
        if b3d:
            nb = Cs // tk
            assert Cs % tk == 0
            b_spec = pl.BlockSpec((None, tn, tk), lambda i, j, k: (k // nb, j, k % nb))
        else:
            b_spec = pl.BlockSpec((tn, tk), lambda i, j, k: (j, k))
    else:
        if b3d:
            nb = Cs // tn
            assert Cs % tn == 0
            b_spec = pl.BlockSpec((None, tk, tn), lambda i, j, k: (j // nb, k, j % nb))
        else:
            b_spec = pl.BlockSpec((tk, tn), lambda i, j, k: (k, j))
    in_specs = [a_spec, b_spec]
    args = [a, b]
    if bias is not None:
        in_specs.append(pl.BlockSpec((1, tn), lambda i, j, k: (0, j)))
        args.append(bias)
    if extra is not None:
        in_specs.append(pl.BlockSpec((tm, tn), lambda i, j, k: (i, j)))
        args.append(extra)

    if out3d is not None:
        P_o, Cs_o = out3d
        assert P_o * Cs_o == N and Cs_o % tn == 0
        nbo = Cs_o // tn
        o_spec = pl.BlockSpec((None, tm, tn), lambda i, j, k: (j // nbo, i, j % nbo))
        o_shape = (P_o, M, Cs_o)
    else:
        o_spec = pl.BlockSpec((tm, tn), lambda i, j, k: (i, j))
        o_shape = (M, N)
    if epi == "relu2":
        out_shape = [jax.ShapeDtypeStruct(o_shape, BF16), jax.ShapeDtypeStruct(o_shape, BF16)]
        out_specs = [o_spec, o_spec]
    else:
        out_shape = jax.ShapeDtypeStruct(o_shape, out_dtype)
        out_specs = o_spec
    n_out = 2 if epi == "relu2" else 1
    dims = _DIMS[mode]
    has_bias, has_extra = bias is not None, extra is not None

    def body(*refs):
        a_ref, b_ref = refs[0], refs[1]
        pos = 2
        bias_ref = extra_ref = None
        if has_bias:
            bias_ref = refs[pos]
            pos += 1
        if has_extra:
            extra_ref = refs[pos]
            pos += 1
        outs = refs[pos:pos + n_out]
        acc_ref = refs[pos + n_out] if nk > 1 else None

        def partial():
            av = a_ref[...]
            if a_silu:
                av = _silu(av)
            return lax.dot_general(av, b_ref[...], dims, preferred_element_type=F32)

        def finish(r):
            if has_bias:
                r = r + bias_ref[...]
            if epi == "relu2":
                rr = jnp.maximum(r, 0.0)
                outs[0][...] = rr.astype(BF16)
                outs[1][...] = (rr * rr).astype(BF16)
            elif epi == "mul2":
                outs[0][...] = (r * 2.0 * extra_ref[...].astype(F32)).astype(outs[0].dtype)
            else:
                outs[0][...] = r.astype(outs[0].dtype)

        if nk == 1:
            finish(partial())
        else:
            k = pl.program_id(2)

            @pl.when(k == 0)
            def _():
                acc_ref[...] = jnp.zeros_like(acc_ref)

            acc_ref[...] += partial()

            @pl.when(k == nk - 1)
            def _():
                finish(acc_ref[...])

    return pl.pallas_call(
        body, name=name, grid=(M // tm, N // tn, nk), in_specs=in_specs, out_specs=out_specs,
        out_shape=out_shape,
        scratch_shapes=[pltpu.VMEM((tm, tn), F32)] if nk > 1 else [],
        compiler_params=_cparams(("parallel", "parallel", "arbitrary")),
    )(*args)


def _modnorm(x, gain, shift, scale):
    y = x * lax.rsqrt(jnp.mean(x * x, axis=-1, keepdims=True) + EPS)
    return (y * gain) * (1.0 + scale) + shift


def _row_fwd(xprev, y, vec, *, name, ts=512):
    S, D = xprev.shape
    ts = min(ts, S)
    has_res = y is not None
    row = pl.BlockSpec((ts, D), lambda i: (i, 0))
    vspec = pl.BlockSpec((8, D), lambda i: (0, 0))

    def body(*refs):
        if has_res:
            xp_ref, y_ref, v_ref, x_ref, h_ref = refs
            x = xp_ref[...] + v_ref[0:1, :] * y_ref[...]
            x_ref[...] = x
        else:
            xp_ref, v_ref, h_ref = refs
            x = xp_ref[...]
        h_ref[...] = _modnorm(x, v_ref[1:2, :], v_ref[2:3, :], v_ref[3:4, :]).astype(BF16)

    if has_res:
        return pl.pallas_call(
            body, name=name, grid=(S // ts,), in_specs=[row, row, vspec], out_specs=[row, row],
            out_shape=[jax.ShapeDtypeStruct((S, D), F32), jax.ShapeDtypeStruct((S, D), BF16)],
            compiler_params=_cparams(("parallel",)),
        )(xprev, y, vec)
    h = pl.pallas_call(
        body, name=name, grid=(S // ts,), in_specs=[row, vspec], out_specs=row,
        out_shape=jax.ShapeDtypeStruct((S, D), BF16),
        compiler_params=_cparams(("parallel",)),
    )(xprev, vec)
    return xprev, h


def _row_bwd(xin, dh, dxout, yprev, vec, *, name, ts=512):
    S, D = xin.shape
    ts = min(ts, S)
    row = pl.BlockSpec((ts, D), lambda i: (i, 0))
    vspec = pl.BlockSpec((8, D), lambda i: (0, 0))

    def body(x_ref, dh_ref, dx_ref, y_ref, v_ref, dxin_ref, dy_ref, part_ref):
        @pl.when(pl.program_id(0) == 0)
        def _():
            part_ref[...] = jnp.zeros_like(part_ref)

        gate = v_ref[0:1, :]
        _, vjp = jax.vjp(_modnorm, x_ref[...], v_ref[1:2, :], v_ref[2:3, :], v_ref[3:4, :])
        dxn, dgain, dshift, dscale = vjp(dh_ref[...])
        dxin = dx_ref[...] + dxn
        dxin_ref[...] = dxin
        dy = dxin * gate
        dy_ref[...] = dy.astype(BF16)
        part_ref[0:1, :] += jnp.sum(dxin * y_ref[...], axis=0, keepdims=True)
        part_ref[1:2, :] += dgain
        part_ref[2:3, :] += dshift
        part_ref[3:4, :] += dscale
        part_ref[4:5, :] += jnp.sum(dy, axis=0, keepdims=True)

    return pl.pallas_call(
        body, name=name, grid=(S // ts,), in_specs=[row, row, row, row, vspec],
        out_specs=[row, row, vspec],
        out_shape=[jax.ShapeDtypeStruct((S, D), F32), jax.ShapeDtypeStruct((S, D), BF16),
                   jax.ShapeDtypeStruct((8, D), F32)],
        compiler_params=_cparams(("arbitrary",)),
    )(xin, dh, dxout, yprev, vec)


def _final(xprev, y, target, vec, *, name, ts=512):
    S, D = xprev.shape
    ts = min(ts, S)
    row = pl.BlockSpec((ts, D), lambda i: (i, 0))
    vspec = pl.BlockSpec((8, D), lambda i: (0, 0))

    def norm(x, gain):
        return x * lax.rsqrt(jnp.mean(x * x, axis=-1, keepdims=True) + EPS) * gain

    def body(xp_ref, y_ref, t_ref, v_ref, dx_ref, dy_ref, part_ref):
        @pl.when(pl.program_id(0) == 0)
        def _():
            part_ref[...] = jnp.zeros_like(part_ref)

        gate = v_ref[0:1, :]
        yv = y_ref[...]
        x = xp_ref[...] + gate * yv
        out, vjp = jax.vjp(norm, x, v_ref[1:2, :])
        err = out - t_ref[...]
        dx, dgain = vjp(err * (1.0 / D))
        dx_ref[...] = dx
        dy_ref[...] = (dx * gate).astype(BF16)
        part_ref[0:1, :] += jnp.sum(dx * yv, axis=0, keepdims=True)
        part_ref[1:2, :] += dgain
        part_ref[2:3, :] += jnp.sum(err * err, axis=0, keepdims=True) * (0.5 / D)

    return pl.pallas_call(
        body, name=name, grid=(S // ts,), in_specs=[row, row, row, vspec], out_specs=[row, row, vspec],
        out_shape=[jax.ShapeDtypeStruct((S, D), F32), jax.ShapeDtypeStruct((S, D), BF16),
                   jax.ShapeDtypeStruct((8, D), F32)],
        compiler_params=_cparams(("arbitrary",)),
    )(xprev, y, target, vec)


def _ln_silu(cv, g, b):
    mu = jnp.mean(cv, axis=-1, keepdims=True)
    var = jnp.mean(jnp.square(cv - mu), axis=-1, keepdims=True)
    u = (cv - mu) * lax.rsqrt(var + EPS) * g + b
    return _silu(u)


def _conv_fwd(u, wdw, vec, *, name, ts=256):
    S, D2 = u.shape
    D = D2 // 2
    ts = min(ts, S)
    H = CONV_HALO
    row = pl.BlockSpec((ts, D), lambda i: (i, 0))

    def body(u_ref, w_ref, v_ref, vo_ref, cv_ref, z_ref, ext):
        @pl.when(pl.program_id(0) == 0)
        def _():
            ext[0:H, :] = jnp.zeros((H, D), F32)

        uu = u_ref[...]
        v = uu[:, :D] * _sigmoid(uu[:, D:])
        vo_ref[...] = v
        ext[H:H + ts, :] = v
        acc = jnp.zeros((ts, D), F32)
        for t in range(CONV_WIDTH):
            acc = acc + ext[pl.ds(H - (CONV_WIDTH - 1) + t, ts), :] * w_ref[pl.ds(t, 1), :]
        cv = acc + v_ref[0:1, :]
        cv_ref[...] = cv
        z_ref[...] = _ln_silu(cv, v_ref[1:2, :], v_ref[2:3, :]).astype(BF16)
        ext[0:H, :] = ext[ts:ts + H, :]

    return pl.pallas_call(
        body, name=name, grid=(S // ts,),
        in_specs=[pl.BlockSpec((ts, D2), lambda i: (i, 0)), pl.BlockSpec((H, D), lambda i: (0, 0)),
                  pl.BlockSpec((8, D), lambda i: (0, 0))],
        out_specs=[row, row, row],
        out_shape=[jax.ShapeDtypeStruct((S, D), F32), jax.ShapeDtypeStruct((S, D), F32),
                   jax.ShapeDtypeStruct((S, D), BF16)],
        scratch_shapes=[pltpu.VMEM((ts + H, D), F32)],
        compiler_params=_cparams(("arbitrary",)),
    )(u, wdw, vec)


def _conv_bwd(dz, cv, v, u, wdw, vec, *, name, ts=256):
    S, D = cv.shape
    ts = min(ts, S)
    H = CONV_HALO
    nt = S // ts
    per = ts // H
    rev = lambda i: (nt - 1 - i, 0)
    row = pl.BlockSpec((ts, D), rev)

    def body(dz_ref, cv_ref, v_ref, vh_ref, u_ref, w_ref, vec_ref, du_ref, dw_ref, part_ref, dbu_ref,
             dext, vext):
        i = pl.program_id(0)

        @pl.when(i == 0)
        def _():
            dext[ts:ts + H, :] = jnp.zeros((H, D), F32)
            dw_ref[...] = jnp.zeros_like(dw_ref)
            part_ref[...] = jnp.zeros_like(part_ref)
            dbu_ref[...] = jnp.zeros_like(dbu_ref)

        _, vjp = jax.vjp(_ln_silu, cv_ref[...], vec_ref[1:2, :], vec_ref[2:3, :])
        dcv, dg, db = vjp(dz_ref[...])
        part_ref[0:1, :] += jnp.sum(dcv, axis=0, keepdims=True)
        part_ref[1:2, :] += dg
        part_ref[2:3, :] += db
        dext[0:ts, :] = dcv
        vext[0:H, :] = vh_ref[...] * jnp.where(i == nt - 1, 0.0, 1.0)
        vext[H:H + ts, :] = v_ref[...]
        dv = jnp.zeros((ts, D), F32)
        for t in range(CONV_WIDTH):
            off = H - (CONV_WIDTH - 1) + t
            dw_ref[pl.ds(t, 1), :] += jnp.sum(dcv * vext[pl.ds(off, ts), :], axis=0, keepdims=True)
            dv = dv + dext[pl.ds(CONV_WIDTH - 1 - t, ts), :] * w_ref[pl.ds(t, 1), :]
        uu = u_ref[...]
        a, g = uu[:, :D], uu[:, D:]
        sg = _sigmoid(g)
        da = dv * sg
        dg_ = dv * a * sg * (1.0 - sg)
        du = jnp.concatenate([da, dg_], axis=-1)
        du_ref[...] = du.astype(BF16)
        dbu_ref[0:1, :] += jnp.sum(du, axis=0, keepdims=True)
        dext[ts:ts + H, :] = dext[0:H, :]

    return pl.pallas_call(
        body, name=name, grid=(nt,),
        in_specs=[row, row, row,
                  pl.BlockSpec((H, D), lambda i: (jnp.maximum((nt - 1 - i) * per - 1, 0), 0)),
                  pl.BlockSpec((ts, 2 * D), rev), pl.BlockSpec((H, D), lambda i: (0, 0)),
                  pl.BlockSpec((8, D), lambda i: (0, 0))],
        out_specs=[pl.BlockSpec((ts, 2 * D), rev), pl.BlockSpec((H, D), lambda i: (0, 0)),
                   pl.BlockSpec((8, D), lambda i: (0, 0)), pl.BlockSpec((8, 2 * D), lambda i: (0, 0))],
        out_shape=[jax.ShapeDtypeStruct((S, 2 * D), BF16), jax.ShapeDtypeStruct((H, D), F32),
                   jax.ShapeDtypeStruct((8, D), F32), jax.ShapeDtypeStruct((8, 2 * D), F32)],
        scratch_shapes=[pltpu.VMEM((ts + H, D), F32), pltpu.VMEM((ts + H, D), F32)],
        compiler_params=_cparams(("arbitrary",)),
    )(dz, cv, v, v, u, wdw, vec)


def _rope(x, c, s, half):
    x1, x2 = x[:, :half], x[:, half:]
    return jnp.concatenate([x1 * c - x2 * s, x2 * c + x1 * s], axis=-1)


def _rope_t(d, c, s, half):
    d1, d2 = d[:, :half], d[:, half:]
    return jnp.concatenate([d1 * c + d2 * s, d2 * c - d1 * s], axis=-1)


def _gn_gate(y, gate, g, b):
    mu = jnp.mean(y, axis=-1, keepdims=True)
    var = jnp.mean(jnp.square(y - mu), axis=-1, keepdims=True)
    return _silu(gate) * ((y - mu) * lax.rsqrt(var + EPS) * g + b)


def _dot(a, b, mode="nn"):
    return lax.dot_general(a, b, _DIMS[mode], preferred_element_type=F32)


def _ret_tables(H):
    lg = jnp.log(1.0 - 2.0 ** (-5.0 - jnp.arange(H, dtype=F32)))
    idx = jnp.arange(CHUNK, dtype=F32)
    dmat = jnp.exp(lg[:, None, None] * jnp.abs(idx[:, None] - idx[None, :]))
    xi = jnp.exp(lg[:, None] * (idx + 1.0))[..., None]
    zeta = jnp.exp(lg[:, None] * (CHUNK - 1.0 - idx))[..., None]
    dec = jnp.exp(lg * CHUNK)[:, None, None]
    return dmat, xi, zeta, dec


def _ret_specs(R, dk, dv, half, order):
    H = RET_HEADS
    C = CHUNK
    nq = H
    return dict(
        q=pl.BlockSpec((R, dk), lambda h, n: (order(n), h)),
        k=pl.BlockSpec((R, dk), lambda h, n: (order(n), nq + h)),
        v=pl.BlockSpec((R, dv), lambda h, n: (order(n), nq + h)),
        gate=pl.BlockSpec((R, dv), lambda h, n: (order(n), 2 * nq + h)),
        rope=pl.BlockSpec((R, half), lambda h, n: (order(n), 0)),
        dmat=pl.BlockSpec((None, C, C), lambda h, n: (h, 0, 0)),
        col=pl.BlockSpec((None, C, 1), lambda h, n: (h, 0, 0)),
        one=pl.BlockSpec((None, 1, 1), lambda h, n: (h, 0, 0)),
        gn=pl.BlockSpec((None, 1, dv), lambda h, n: (h, 0, 0)),
        yv=pl.BlockSpec((R, dv), lambda h, n: (order(n), h)),
        yk=pl.BlockSpec((R, dk), lambda h, n: (order(n), h)),
    )


def _ret_fwd(proj, cos, sin, tables, gn_g, gn_b, *, name, cps=4):
    S = proj.shape[0]
    D = proj.shape[1] // 6
    H, C = RET_HEADS, CHUNK
    dk, dv, half = D // H, 2 * D // H, D // H // 2
    nc = S // C
    cps = min(cps, nc)
    R = cps * C
    scale = dk ** -0.5
    sp = _ret_specs(R, dk, dv, half, lambda n: n)
    dmat, xi, zeta, dec = tables

    def body(q_ref, k_ref, v_ref, g_ref, cos_ref, sin_ref, dm_ref, xi_ref, ze_ref, dec_ref, gg_ref, gb_ref,
             y_ref, y2_ref, st_ref, state):
        @pl.when(pl.program_id(1) == 0)
        def _():
            state[...] = jnp.zeros_like(state)

        dm, xv, zv, dc = dm_ref[...], xi_ref[...], ze_ref[...], dec_ref[...]
        for j in range(cps):
            rows = pl.ds(j * C, C)
            cs, sn = cos_ref[rows, :], sin_ref[rows, :]
            qr = _rope(q_ref[rows, :].astype(F32), cs, sn, half)
            kr = _rope(k_ref[rows, :].astype(F32), cs, sn, half) * scale
            vb = v_ref[rows, :]
            p = (_dot(qr.astype(BF16), kr.astype(BF16), "nt") * dm).astype(BF16)
            st = state[...]
            stb = st.astype(BF16)
            st_ref[j] = stb
            y = _dot(p, vb) + _dot((qr * xv).astype(BF16), stb)
            state[...] = st * dc + _dot((kr * zv).astype(BF16), vb, "tn")
            y_ref[rows, :] = y
            y2_ref[rows, :] = _gn_gate(y, g_ref[rows, :].astype(F32), gg_ref[...], gb_ref[...]).astype(BF16)

    return pl.pallas_call(
        body, name=name, grid=(H, nc // cps),
        in_specs=[sp["q"], sp["k"], sp["v"], sp["gate"], sp["rope"], sp["rope"], sp["dmat"], sp["col"],
                  sp["col"], sp["one"], sp["gn"], sp["gn"]],
        out_specs=[sp["yv"], sp["yv"], pl.BlockSpec((None, cps, dk, dv), lambda h, n: (h, n, 0, 0))],
        out_shape=[jax.ShapeDtypeStruct((S, 2 * D), F32), jax.ShapeDtypeStruct((S, 2 * D), BF16),
                   jax.ShapeDtypeStruct((H, nc, dk, dv), BF16)],
        scratch_shapes=[pltpu.VMEM((dk, dv), F32)],
        compiler_params=_cparams(("arbitrary", "arbitrary")),
    )(proj, proj, proj, proj, cos, sin, dmat, xi, zeta, dec, gn_g, gn_b)


def _ret_bwd(proj, cos, sin, tables, gn_g, gn_b, y, dy2, states, *, name, cps=4):
    S = proj.shape[0]
    D = proj.shape[1] // 6
    H, C = RET_HEADS, CHUNK
    dk, dv, half = D // H, 2 * D // H, D // H // 2
    nc = S // C
    cps = min(cps, nc)
    ns = nc // cps
    R = cps * C
    scale = dk ** -0.5
    order = lambda n: ns - 1 - n
    sp = _ret_specs(R, dk, dv, half, order)
    dmat, xi, zeta, dec = tables

    def body(q_ref, k_ref, v_ref, g_ref, cos_ref, sin_ref, dm_ref, xi_ref, ze_ref, dec_ref, gg_ref, gb_ref,
             y_ref, dy2_ref, st_ref, dq_ref, dk_ref, dv_ref, dg_ref, dgg_ref, dgb_ref, gst):
        @pl.when(pl.program_id(1) == 0)
        def _():
            gst[...] = jnp.zeros_like(gst)
            dgg_ref[...] = jnp.zeros_like(dgg_ref)
            dgb_ref[...] = jnp.zeros_like(dgb_ref)

        dm, xv, zv, dc = dm_ref[...], xi_ref[...], ze_ref[...], dec_ref[...]
        for j in reversed(range(cps)):
            rows = pl.ds(j * C, C)
            cs, sn = cos_ref[rows, :], sin_ref[rows, :]
            _, vjp = jax.vjp(_gn_gate, y_ref[rows, :], g_ref[rows, :].astype(F32), gg_ref[...], gb_ref[...])
            dy, dgate, dgg, dgb = vjp(dy2_ref[rows, :])
            dgg_ref[...] += dgg
            dgb_ref[...] += dgb
            dg_ref[rows, :] = dgate.astype(BF16)
            dyb = dy.astype(BF16)
            qr = _rope(q_ref[rows, :].astype(F32), cs, sn, half)
            kr = _rope(k_ref[rows, :].astype(F32), cs, sn, half) * scale
            qb, kb, vb = qr.astype(BF16), kr.astype(BF16), v_ref[rows, :]
            p = (_dot(qb, kb, "nt") * dm).astype(BF16)
            g = gst[...]
            gb16 = g.astype(BF16)
            sprev = st_ref[j]
            dvv = _dot(p, dyb, "tn") + _dot((kr * zv).astype(BF16), gb16)
            dpb = (_dot(dyb, vb, "nt") * dm).astype(BF16)
            dqr = _dot(dpb, kb) + _dot(dyb, sprev, "nt") * xv
            dkr = _dot(dpb, qb, "tn") + _dot(vb, gb16, "nt") * zv
            gst[...] = g * dc + _dot((qr * xv).astype(BF16), dyb, "tn")
            dq_ref[rows, :] = _rope_t(dqr, cs, sn, half).astype(BF16)
            dk_ref[rows, :] = _rope_t(dkr * scale, cs, sn, half).astype(BF16)
            dv_ref[rows, :] = dvv.astype(BF16)

    return pl.pallas_call(
        body, name=name, grid=(H, ns),
        in_specs=[sp["q"], sp["k"], sp["v"], sp["gate"], sp["rope"], sp["rope"], sp["dmat"], sp["col"],
                  sp["col"], sp["one"], sp["gn"], sp["gn"], sp["yv"], sp["yv"],
                  pl.BlockSpec((None, cps, dk, dv), lambda h, n: (h, order(n), 0, 0))],
        out_specs=[sp["yk"], sp["yk"], sp["yv"], sp["yv"], sp["gn"], sp["gn"]],
        out_shape=[jax.ShapeDtypeStruct((S, D), BF16), jax.ShapeDtypeStruct((S, D), BF16),
                   jax.ShapeDtypeStruct((S, 2 * D), BF16), jax.ShapeDtypeStruct((S, 2 * D), BF16),
                   jax.ShapeDtypeStruct((H, 1, dv), F32), jax.ShapeDtypeStruct((H, 1, dv), F32)],
        scratch_shapes=[pltpu.VMEM((dk, dv), F32)],
        compiler_params=_cparams(("arbitrary", "arbitrary")),
    )(proj, proj, proj, proj, cos, sin, dmat, xi, zeta, dec, gn_g, gn_b, y, dy2, states)


def _rows_tile(rows, cols, n_arrays):
    cap = max(8, V7X_VMEM_LIMIT // 3 // (n_arrays * 2 * 4 * cols))
    t = rows
    while t > cap and t % 2 == 0:
        t //= 2
    return t


def _add_half(g, r, half_idx, *, name):
    P, R, Cc = g.shape
    hR = R // 2
    tr = _rows_tile(hR, Cc, 3)
    nb = hR // tr

    def body(h_ref, g_ref, r_ref, o_ref):
        o_ref[...] = (g_ref[...] + r_ref[...]).astype(BF16)

    return pl.pallas_call(
        body, name=name,
        grid_spec=pltpu.PrefetchScalarGridSpec(
            num_scalar_prefetch=1, grid=(P, nb),
            in_specs=[pl.BlockSpec((None, tr, Cc), lambda s, i, h: (s, h[0] * nb + i, 0)),
                      pl.BlockSpec((None, tr, Cc), lambda s, i, h: (s, i, 0))],
            out_specs=pl.BlockSpec((None, tr, Cc), lambda s, i, h: (s, i, 0))),
        out_shape=jax.ShapeDtypeStruct((P, hR, Cc), BF16), compiler_params=_cparams(("parallel", "parallel")),
    )(half_idx, g, r)


def _sum_slots(x, *, name):
    L, NS, R, Cc = x.shape
    tr = _rows_tile(R, Cc, NS + 1)

    def body(x_ref, o_ref):
        acc = x_ref[0].astype(F32)
        for s in range(1, NS):
            acc = acc + x_ref[s].astype(F32)
        o_ref[...] = acc

    return pl.pallas_call(
        body, name=name, grid=(L, R // tr),
        in_specs=[pl.BlockSpec((None, NS, tr, Cc), lambda l, i: (l, 0, i, 0))],
        out_specs=pl.BlockSpec((None, tr, Cc), lambda l, i: (l, i, 0)),
        out_shape=jax.ShapeDtypeStruct((L, R, Cc), F32), compiler_params=_cparams(("parallel", "parallel")),
    )(x)


def _adamw(gslots, w, m, v, *, name):
    L, NS, R, Cc = gslots.shape
    tr = _rows_tile(R, Cc, NS + 7)
    gspec = pl.BlockSpec((None, NS, tr, Cc), lambda l, i: (l, 0, i, 0))
    spec = pl.BlockSpec((None, tr, Cc), lambda l, i: (l, i, 0))
    c1 = 1.0 - ADAM_B1 ** ADAM_STEP
    c2 = 1.0 - ADAM_B2 ** ADAM_STEP

    def body(g_ref, w_ref, m_ref, v_ref, go_ref, d_ref, mo_ref, vo_ref):
        g = g_ref[0]
        for s in range(1, NS):
            g = g + g_ref[s]
        mn = ADAM_B1 * m_ref[...] + (1.0 - ADAM_B1) * g
        vn = ADAM_B2 * v_ref[...] + (1.0 - ADAM_B2) * jnp.square(g)
        m_hat = mn / c1
        v_hat = vn / c2
        go_ref[...] = g
        d_ref[...] = -ADAM_LR * (m_hat / (jnp.sqrt(v_hat) + ADAM_EPS) + ADAM_WD * w_ref[...])
        mo_ref[...] = mn
        vo_ref[...] = vn

    sd = jax.ShapeDtypeStruct((L, R, Cc), F32)
    return pl.pallas_call(
        body, name=name, grid=(L, R // tr), in_specs=[gspec, spec, spec, spec],
        out_specs=[spec, spec, spec, spec], out_shape=[sd, sd, sd, sd],
        compiler_params=_cparams(("parallel", "parallel")),
    )(gslots, w, m, v)


def _me():
    return lax.axis_index("x"), lax.axis_index("y"), lax.axis_index("c")


def _flip(v, bit):
    return 1 - v if bit else v


def _allgather8(x, *, name):
    def body(x_ref, out_ref, send_sems, recv_sems, loc_sem):
        mx, my, mc = _me()
        me = 4 * mx + 2 * my + mc
        loc = pltpu.make_async_copy(x_ref, out_ref.at[me], loc_sem)
        loc.start()
        sends, recvs = [], []
        for k in range(1, N_DEV):
            px, py, pc = _flip(mx, k & 4), _flip(my, k & 2), _flip(mc, k & 1)
            sends.append(pltpu.make_async_remote_copy(
                src_ref=x_ref, dst_ref=out_ref.at[me], send_sem=send_sems.at[k - 1],
                recv_sem=recv_sems.at[k - 1], device_id=(px, py, pc), device_id_type=MESH))
            recvs.append(pltpu.make_async_remote_copy(
                src_ref=x_ref, dst_ref=out_ref.at[4 * px + 2 * py + pc], send_sem=send_sems.at[k - 1],
                recv_sem=recv_sems.at[k - 1], device_id=(px, py, pc), device_id_type=MESH))
        for cp in sends:
            cp.start()
        for cp in recvs:
            cp.wait_recv()
        for cp in sends:
            cp.wait_send()
        loc.wait()

    return pl.pallas_call(
        body, name=name, in_specs=[ANY], out_specs=ANY,
        out_shape=jax.ShapeDtypeStruct((N_DEV,) + x.shape, x.dtype),
        scratch_shapes=[pltpu.SemaphoreType.DMA((N_DEV - 1,)), pltpu.SemaphoreType.DMA((N_DEV - 1,)),
                        pltpu.SemaphoreType.DMA],
    )(x)


def _gather_chips(arrays, *, name):
    n = len(arrays)

    def body(*refs):
        ins, outs = refs[:n], refs[n:2 * n]
        ici_send, ici_recv, d2d_send, d2d_recv, loc_sems = refs[2 * n:]
        mx, my, mc = _me()
        me = 2 * mx + my
        sib = (mx, my, 1 - mc)
        locs, sends, lands, passes, gifts = [], [], [], [], []
        for a in range(n):
            h = arrays[a].shape[0] // 2
            mine, other = pl.ds(mc * h, h), pl.ds((1 - mc) * h, h)
            locs.append(pltpu.make_async_copy(ins[a], outs[a].at[me], loc_sems.at[a]))
            for k in range(1, N_CHIPS):
                px, py = _flip(mx, k & 2), _flip(my, k & 1)
                peer = 2 * px + py
                ici = dict(send_sem=ici_send.at[a, k - 1], recv_sem=ici_recv.at[a, k - 1],
                           device_id=(px, py, mc), device_id_type=MESH)
                d2d = dict(send_sem=d2d_send.at[a, k - 1], recv_sem=d2d_recv.at[a, k - 1],
                           device_id=sib, device_id_type=MESH)
                sends.append(pltpu.make_async_remote_copy(
                    src_ref=ins[a].at[mine], dst_ref=outs[a].at[me, mine], **ici))
                lands.append(pltpu.make_async_remote_copy(
                    src_ref=ins[a].at[mine], dst_ref=outs[a].at[peer, mine], **ici))
                passes.append(pltpu.make_async_remote_copy(
                    src_ref=outs[a].at[peer, mine], dst_ref=outs[a].at[peer, mine], **d2d))
                gifts.append(pltpu.make_async_remote_copy(
                    src_ref=outs[a].at[peer, other], dst_ref=outs[a].at[peer, other], **d2d))
        for cp in locs + sends:
            cp.start()
        for land, fwd in zip(lands, passes):
            land.wait_recv()
            fwd.start()
        for cp in gifts:
            cp.wait_recv()
        for cp in sends + passes:
            cp.wait_send()
        for cp in locs:
            cp.wait()

    nsem = (n, N_CHIPS - 1)
    return pl.pallas_call(
        body, name=name, in_specs=[ANY] * n, out_specs=[ANY] * n,
        out_shape=[jax.ShapeDtypeStruct((N_CHIPS,) + a.shape, a.dtype) for a in arrays],
        scratch_shapes=[pltpu.SemaphoreType.DMA(nsem), pltpu.SemaphoreType.DMA(nsem), pltpu.SemaphoreType.DMA(nsem),
                        pltpu.SemaphoreType.DMA(nsem), pltpu.SemaphoreType.DMA((n,))],
    )(*arrays)


def _swap_half(arrays, *, name):
    n = len(arrays)

    def body(*refs):
        ins, outs = refs[:n], refs[n:2 * n]
        send_sems, recv_sems = refs[2 * n:]
        mx, my, mc = _me()
        cps = []
        for a in range(n):
            P, R, _ = arrays[a].shape
            cps.append(pltpu.make_async_remote_copy(
                src_ref=ins[a].at[pl.ds(0, P), pl.ds((1 - mc) * (R // 2), R // 2)], dst_ref=outs[a],
                send_sem=send_sems.at[a], recv_sem=recv_sems.at[a],
                device_id=(mx, my, 1 - mc), device_id_type=MESH))
        for cp in cps:
            cp.start()
        for cp in cps:
            cp.wait_recv()
        for cp in cps:
            cp.wait_send()

    return pl.pallas_call(
        body, name=name, in_specs=[ANY] * n, out_specs=[ANY] * n,
        out_shape=[jax.ShapeDtypeStruct((a.shape[0], a.shape[1] // 2, a.shape[2]), a.dtype) for a in arrays],
        scratch_shapes=[pltpu.SemaphoreType.DMA((n,)), pltpu.SemaphoreType.DMA((n,))],
    )(*arrays)


def _join_halves(arrays, *, name):
    n = len(arrays)

    def body(*refs):
        ins, outs = refs[:n], refs[n:2 * n]
        send_sems, recv_sems, loc_sems = refs[2 * n:]
        mx, my, mc = _me()
        locs, sends, lands = [], [], []
        for a in range(n):
            L, h, _ = arrays[a].shape
            mine = outs[a].at[pl.ds(0, L), pl.ds(mc * h, h)]
            other = outs[a].at[pl.ds(0, L), pl.ds((1 - mc) * h, h)]
            sems = dict(send_sem=send_sems.at[a], recv_sem=recv_sems.at[a], device_id=(mx, my, 1 - mc),
                        device_id_type=MESH)
            locs.append(pltpu.make_async_copy(ins[a], mine, loc_sems.at[a]))
            sends.append(pltpu.make_async_remote_copy(src_ref=ins[a], dst_ref=mine, **sems))
            lands.append(pltpu.make_async_remote_copy(src_ref=ins[a], dst_ref=other, **sems))
        for cp in locs + sends:
            cp.start()
        for cp in lands:
            cp.wait_recv()
        for cp in sends:
            cp.wait_send()
        for cp in locs:
            cp.wait()

    return pl.pallas_call(
        body, name=name, in_specs=[ANY] * n, out_specs=[ANY] * n,
        out_shape=[jax.ShapeDtypeStruct((a.shape[0], 2 * a.shape[1], a.shape[2]), a.dtype) for a in arrays],
        scratch_shapes=[pltpu.SemaphoreType.DMA((n,)), pltpu.SemaphoreType.DMA((n,)),
                        pltpu.SemaphoreType.DMA((n,))],
    )(*arrays)


def _scatter_chips(groups, *, name):
    flat = [(gi, l, a) for gi, g in enumerate(groups) for l, a in enumerate(g)]
    n = len(flat)
    ng = len(groups)

    def body(*refs):
        ins, outs = refs[:n], refs[n:n + ng]
        send_sems, recv_sems, loc_sems = refs[n + ng:]
        mx, my, mc = _me()
        me = 2 * mx + my
        locs, sends, recvs = [], [], []
        for a, (gi, l, _) in enumerate(flat):
            locs.append(pltpu.make_async_copy(ins[a].at[me], outs[gi].at[l, me], loc_sems.at[a]))
            for k in range(1, N_CHIPS):
                px, py = _flip(mx, k & 2), _flip(my, k & 1)
                peer = 2 * px + py
                sems = dict(send_sem=send_sems.at[a, k - 1], recv_sem=recv_sems.at[a, k - 1],
                            device_id=(px, py, mc), device_id_type=MESH)
                sends.append(pltpu.make_async_remote_copy(
                    src_ref=ins[a].at[peer], dst_ref=outs[gi].at[l, me], **sems))
                recvs.append(pltpu.make_async_remote_copy(
                    src_ref=ins[a].at[peer], dst_ref=outs[gi].at[l, peer], **sems))
        for cp in locs + sends:
            cp.start()
        for cp in recvs:
            cp.wait_recv()
        for cp in sends:
            cp.wait_send()
        for cp in locs:
            cp.wait()

    return pl.pallas_call(
        body, name=name, in_specs=[ANY] * n, out_specs=[ANY] * ng,
        out_shape=[jax.ShapeDtypeStruct((len(g),) + g[0].shape, g[0].dtype) for g in groups],
        scratch_shapes=[pltpu.SemaphoreType.DMA((n, N_CHIPS - 1)), pltpu.SemaphoreType.DMA((n, N_CHIPS - 1)),
                        pltpu.SemaphoreType.DMA((n,))],
    )(*[a for _, _, a in flat])


BIG = ("conv_w_pw1", "conv_w_pw2", "ret_w_in", "ret_w_out", "mlp_w1", "mlp_w2")
COLS = ("conv_w_pw1", "ret_w_in", "mlp_w1")
SMALL = ("ada_b", "norm_mix_g", "norm_mlp_g", "conv_b_pw1", "conv_w_dw", "conv_b_dw", "conv_ln_g", "conv_ln_b",
         "conv_b_pw2", "ret_gn_g", "ret_gn_b", "final_norm_g")
SMALL_SHARDED = ("conv_w_dw", "ret_gn_g", "ret_gn_b")
WEIGHTS = ("ada_w", "ada_b", "norm_mix_g", "norm_mlp_g", "conv_w_pw1", "conv_b_pw1", "conv_w_dw", "conv_b_dw",
           "conv_ln_g", "conv_ln_b", "conv_w_pw2", "conv_b_pw2", "ret_w_in", "ret_gn_g", "ret_gn_b", "ret_w_out",
           "mlp_w1", "mlp_w2", "final_norm_g")


def _vec8(rows, D):
    rows = [r.reshape(1, D).astype(F32) for r in rows]
    return jnp.concatenate(rows + [jnp.zeros((8 - len(rows), D), F32)], axis=0)


def _unshard_last(g):
    nd = g.ndim
    t = jnp.transpose(g, tuple(range(1, nd - 1)) + (0, nd - 1))
    return t.reshape(t.shape[:-2] + (t.shape[-2] * t.shape[-1],))


def _pack(parts):
    flat = jnp.concatenate([p.reshape(-1).astype(F32) for p in parts])
    pad = (-flat.shape[0]) % 1024
    return jnp.concatenate([flat, jnp.zeros((pad,), F32)]).reshape(-1, 128)


def _unpack(packed, shapes):
    flat = packed.reshape(-1)
    out, pos = [], 0
    for s in shapes:
        n = math.prod(s)
        out.append(flat[pos:pos + n].reshape(s))
        pos += n
    return out


def kernel(x, c, ada_w, ada_b, norm_mix_g, norm_mlp_g, conv_w_pw1, conv_b_pw1, conv_w_dw, conv_b_dw, conv_ln_g, conv_ln_b, conv_w_pw2, conv_b_pw2, ret_w_in, ret_gn_g, ret_gn_b, ret_w_out, mlp_w1, mlp_w2, final_norm_g, loss_target, m_ada_w, m_ada_b, m_norm_mix_g, m_norm_mlp_g, m_conv_w_pw1, m_conv_b_pw1, m_conv_w_dw, m_conv_b_dw, m_conv_ln_g, m_conv_ln_b, m_conv_w_pw2, m_conv_b_pw2, m_ret_w_in, m_ret_gn_g, m_ret_gn_b, m_ret_w_out, m_mlp_w1, m_mlp_w2, m_final_norm_g, v_ada_w, v_ada_b, v_norm_mix_g, v_norm_mlp_g, v_conv_w_pw1, v_conv_b_pw1, v_conv_w_dw, v_conv_b_dw, v_conv_ln_g, v_conv_ln_b, v_conv_w_pw2, v_conv_b_pw2, v_ret_w_in, v_ret_gn_g, v_ret_gn_b, v_ret_w_out, v_mlp_w1, v_mlp_w2, v_final_norm_g):
    W = dict(ada_w=ada_w, ada_b=ada_b, norm_mix_g=norm_mix_g, norm_mlp_g=norm_mlp_g, conv_w_pw1=conv_w_pw1,
             conv_b_pw1=conv_b_pw1, conv_w_dw=conv_w_dw, conv_b_dw=conv_b_dw, conv_ln_g=conv_ln_g,
             conv_ln_b=conv_ln_b, conv_w_pw2=conv_w_pw2, conv_b_pw2=conv_b_pw2, ret_w_in=ret_w_in,
             ret_gn_g=ret_gn_g, ret_gn_b=ret_gn_b, ret_w_out=ret_w_out, mlp_w1=mlp_w1, mlp_w2=mlp_w2,
             final_norm_g=final_norm_g)
    Mo = dict(ada_w=m_ada_w, ada_b=m_ada_b, norm_mix_g=m_norm_mix_g, norm_mlp_g=m_norm_mlp_g,
              conv_w_pw1=m_conv_w_pw1, conv_b_pw1=m_conv_b_pw1, conv_w_dw=m_conv_w_dw, conv_b_dw=m_conv_b_dw,
              conv_ln_g=m_conv_ln_g, conv_ln_b=m_conv_ln_b, conv_w_pw2=m_conv_w_pw2, conv_b_pw2=m_conv_b_pw2,
              ret_w_in=m_ret_w_in, ret_gn_g=m_ret_gn_g, ret_gn_b=m_ret_gn_b, ret_w_out=m_ret_w_out,
              mlp_w1=m_mlp_w1, mlp_w2=m_mlp_w2, final_norm_g=m_final_norm_g)
    Vo = dict(ada_w=v_ada_w, ada_b=v_ada_b, norm_mix_g=v_norm_mix_g, norm_mlp_g=v_norm_mlp_g,
              conv_w_pw1=v_conv_w_pw1, conv_b_pw1=v_conv_b_pw1, conv_w_dw=v_conv_w_dw, conv_b_dw=v_conv_b_dw,
              conv_ln_g=v_conv_ln_g, conv_ln_b=v_conv_ln_b, conv_w_pw2=v_conv_w_pw2, conv_b_pw2=v_conv_b_pw2,
              ret_w_in=v_ret_w_in, ret_gn_g=v_ret_gn_g, ret_gn_b=v_ret_gn_b, ret_w_out=v_ret_w_out,
              mlp_w1=v_mlp_w1, mlp_w2=v_mlp_w2, final_norm_g=v_final_norm_g)

    S, D = x.shape[1], x.shape[2]
    depth = ada_w.shape[0]
    H = RET_HEADS
    dv = 2 * D // H
    xs = x.reshape(S, D)
    target = loss_target.reshape(S, D)
    mx, my, mc = _me()
    chip = 2 * mx + my
    dev = 4 * mx + 2 * my + mc

    send = [W[nm][l].astype(BF16) for nm in BIG for l in range(W[nm].shape[0])]
    send += [W[nm] for nm in SMALL_SHARDED]
    got = _gather_chips(send, name="gather_weights")
    Wg, pos = {}, 0
    for nm in BIG:
        L = W[nm].shape[0]
        Wg[nm] = got[pos:pos + L]
        pos += L
    full_small = {nm: _unshard_last(got[pos + i]) for i, nm in enumerate(SMALL_SHARDED)}

    def wfull(nm, l):
        g = Wg[nm][l]
        return g.reshape(g.shape[0] * g.shape[1], g.shape[2])

    c_all = _allgather8(c.reshape(8, D // 8), name="gather_c").reshape(N_DEV, D)
    cs_ada = ada_w.shape[2]
    bias_sh = lax.dynamic_slice_in_dim(ada_b.reshape(depth, N_CHIPS, cs_ada), chip, 1, axis=1)
    mod_sh = _mm(c_all, ada_w, mode="nn", name="ada_fwd", b3d=True, tn=cs_ada, a_silu=True,
                 bias=bias_sh.reshape(1, depth * cs_ada))
    mod_all = _allgather8(mod_sh, name="gather_mod")[0::2]
    mod_me = lax.dynamic_slice_in_dim(mod_all, dev, 1, axis=1).reshape(N_CHIPS, depth, cs_ada)
    mod = jnp.transpose(mod_me, (1, 0, 2)).reshape(depth, 6, D)

    pos_ids = jnp.arange(S, dtype=F32)
    dk = D // H
    inv = ROPE_BASE ** (-jnp.arange(0, dk, 2, dtype=F32) / dk)
    ang = pos_ids[:, None] * inv[None, :]
    cos_t, sin_t = jnp.cos(ang), jnp.sin(ang)
    tables = _ret_tables(H)
    gn_g_full = full_small["ret_gn_g"].reshape(-1, H, 1, dv)
    gn_b_full = full_small["ret_gn_b"].reshape(-1, H, 1, dv)
    wdw_full = full_small["conv_w_dw"]

    def wdw_pad(j):
        return jnp.concatenate([wdw_full[j], jnp.zeros((CONV_HALO - CONV_WIDTH, D), F32)], axis=0)

    saved = []
    xa, y_prev, gate_prev = xs, None, None
    for l in range(depth):
        j = l // 2
        sv = {}
        vec_a = _vec8([gate_prev if gate_prev is not None else jnp.zeros((D,), F32), norm_mix_g[l], mod[l, 0],
                       mod[l, 1]], D)
        xa, h = _row_fwd(xa, y_prev, vec_a, name="row_fwd" if y_prev is not None else "row_fwd_first")
        sv.update(xa=xa, h=h, vec_a=vec_a)
        if l % 2 == 0:
            u = _mm(h, Wg["conv_w_pw1"][j], mode="nn", name="pw1_fwd", b3d=True,
                    tn=Wg["conv_w_pw1"][j].shape[2], bias=conv_b_pw1[j].reshape(1, -1))
            cvec = _vec8([conv_b_dw[j], conv_ln_g[j], conv_ln_b[j]], D)
            v_glu, cv, z = _conv_fwd(u, wdw_pad(j), cvec, name="conv_fwd")
            ymix = _mm(z, wfull("conv_w_pw2", j), mode="nn", name="pw2_fwd", bias=conv_b_pw2[j].reshape(1, -1))
            sv.update(u=u, v_glu=v_glu, cv=cv, z=z, cvec=cvec)
        else:
            proj = _mm(h, Wg["ret_w_in"][j], mode="nn", name="win_fwd", b3d=True, out_dtype=BF16,
                       tn=Wg["ret_w_in"][j].shape[2])
            yr, y2, states = _ret_fwd(proj, cos_t, sin_t, tables, gn_g_full[j], gn_b_full[j], name="ret_fwd")
            ymix = _mm(y2, wfull("ret_w_out", j), mode="nn", name="wout_fwd")
            sv.update(proj=proj, yr=yr, y2=y2, states=states)
        vec_b = _vec8([mod[l, 2], norm_mlp_g[l], mod[l, 3], mod[l, 4]], D)
        xb, h2 = _row_fwd(xa, ymix, vec_b, name="row_fwd")
        ra, p = _mm(h2, Wg["mlp_w1"][l], mode="nn", name="w1_fwd", b3d=True, tn=Wg["mlp_w1"][l].shape[2],
                    epi="relu2")
        mo = _mm(p, wfull("mlp_w2", l), mode="nn", name="w2_fwd")
        sv.update(ymix=ymix, xb=xb, h2=h2, ra=ra, p=p, mo=mo, vec_b=vec_b)
        saved.append(sv)
        xa, y_prev, gate_prev = xb, mo, mod[l, 5]

    fvec = _vec8([gate_prev, final_norm_g], D)
    dx, dyb, fpart = _final(xa, y_prev, target, fvec, name="final")
    loss = lax.psum(jnp.sum(fpart[2]), ("x", "y", "c"))
    G = {nm: [None] * W[nm].shape[0] for nm in BIG}
    dmod = [[None] * 6 for _ in range(depth)]
    dmod[depth - 1][5] = fpart[0]
    sg = dict(norm_mix_g=[None] * depth, norm_mlp_g=[None] * depth, final_norm_g=fpart[1])
    n_conv, n_ret = conv_w_pw1.shape[0], ret_w_in.shape[0]
    for nm in ("conv_b_pw1", "conv_w_dw", "conv_b_dw", "conv_ln_g", "conv_ln_b", "conv_b_pw2"):
        sg[nm] = [None] * n_conv
    for nm in ("ret_gn_g", "ret_gn_b"):
        sg[nm] = [None] * n_ret

    for l in reversed(range(depth)):
        j = l // 2
        sv = saved[l]
        w1, w2 = Wg["mlp_w1"][l], wfull("mlp_w2", l)
        cs1 = w1.shape[2]
        da = _mm(dyb, w2, mode="nt", name="w2_dx", out_dtype=BF16, epi="mul2", extra=sv["ra"])
        gw2 = _mm(sv["p"], dyb, mode="tn", name="w2_dw")
        G["mlp_w2"][l] = gw2.reshape(N_CHIPS, gw2.shape[0] // N_CHIPS, gw2.shape[1])
        G["mlp_w1"][l] = _mm(sv["h2"], da, mode="tn", name="w1_dw", out3d=(N_CHIPS, cs1), tn=cs1)
        dh2 = _mm(da, w1, mode="nt", name="w1_dx", b3d=True, tk=cs1)
        dx, dyb, part = _row_bwd(sv["xb"], dh2, dx, sv["ymix"], sv["vec_b"], name="row_bwd")
        dmod[l][2], sg["norm_mlp_g"][l], dmod[l][3], dmod[l][4] = part[0], part[1], part[2], part[3]
        if l % 2 == 0:
            sg["conv_b_pw2"][j] = part[4]
            wp1, wp2 = Wg["conv_w_pw1"][j], wfull("conv_w_pw2", j)
            csp = wp1.shape[2]
            dz = _mm(dyb, wp2, mode="nt", name="pw2_dx")
            gp2 = _mm(sv["z"], dyb, mode="tn", name="pw2_dw")
            G["conv_w_pw2"][j] = gp2.reshape(N_CHIPS, gp2.shape[0] // N_CHIPS, gp2.shape[1])
            du, dwdw, cpart, dbu = _conv_bwd(dz, sv["cv"], sv["v_glu"], sv["u"], wdw_pad(j), sv["cvec"],
                                             name="conv_bwd")
            sg["conv_w_dw"][j] = dwdw[:CONV_WIDTH]
            sg["conv_b_dw"][j], sg["conv_ln_g"][j], sg["conv_ln_b"][j] = cpart[0], cpart[1], cpart[2]
            sg["conv_b_pw1"][j] = dbu[0]
            G["conv_w_pw1"][j] = _mm(sv["h"], du, mode="tn", name="pw1_dw", out3d=(N_CHIPS, csp), tn=csp)
            dh = _mm(du, wp1, mode="nt", name="pw1_dx", b3d=True, tk=csp)
        else:
            wi, wo = Wg["ret_w_in"][j], wfull("ret_w_out", j)
            csi = wi.shape[2]
            dy2 = _mm(dyb, wo, mode="nt", name="wout_dx")
            gwo = _mm(sv["y2"], dyb, mode="tn", name="wout_dw")
            G["ret_w_out"][j] = gwo.reshape(N_CHIPS, gwo.shape[0] // N_CHIPS, gwo.shape[1])
            dq, dkk, dvv, dgt, dgg, dgb = _ret_bwd(sv["proj"], cos_t, sin_t, tables, gn_g_full[j], gn_b_full[j],
                                                   sv["yr"], dy2, sv["states"], name="ret_bwd")
            sg["ret_gn_g"][j], sg["ret_gn_b"][j] = dgg.reshape(H, dv), dgb.reshape(H, dv)
            dproj = jnp.concatenate([dq, dkk, dvv, dgt], axis=1)
            G["ret_w_in"][j] = _mm(sv["h"], dproj, mode="tn", name="win_dw", out3d=(N_CHIPS, csi), tn=csi)
            dh = _mm(dproj, wi, mode="nt", name="win_dx", b3d=True, tk=csi)
        yp = saved[l - 1]["mo"] if l > 0 else dh
        dx, dyb, part = _row_bwd(sv["xa"], dh, dx, yp, sv["vec_a"], name="row_bwd")
        sg["norm_mix_g"][l], dmod[l][0], dmod[l][1] = part[1], part[2], part[3]
        if l > 0:
            dmod[l - 1][5] = part[0]
    grad_x = dx.reshape(x.shape)

    dmod_me = jnp.stack([jnp.stack(r) for r in dmod]).reshape(depth, 6 * D)
    sgrads = dict(ada_b=dmod_me)
    for nm in SMALL[1:]:
        sgrads[nm] = sg[nm] if nm == "final_norm_g" else jnp.stack(sg[nm])
    full_shapes = [sgrads[nm].shape for nm in SMALL]
    packed_all = _allgather8(_pack([sgrads[nm] for nm in SMALL]), name="gather_small_grads")
    sums = _unpack(_sum_slots(packed_all[None], name="sum_small_grads"), full_shapes)
    gsm = {}
    for nm, g in zip(SMALL, sums):
        if nm in SMALL_SHARDED:
            n = g.shape[-1] // N_CHIPS
            g = lax.dynamic_slice_in_dim(g.reshape(g.shape[:-1] + (N_CHIPS, n)), chip, 1, axis=g.ndim - 1)
            g = g.reshape(g.shape[:-2] + (n,))
        gsm[nm] = g.reshape(W[nm].shape)
    pw, pm, pv, pg = (_pack([t[nm] for nm in SMALL]) for t in (W, Mo, Vo, gsm))
    e4 = lambda a: a.reshape((1, 1) + a.shape)
    e3 = lambda a: a.reshape((1,) + a.shape)
    sres = _adamw(e4(pg), e3(pw), e3(pm), e3(pv), name="adamw_small")
    shard_shapes = [W[nm].shape for nm in SMALL]
    small_out = [dict(zip(SMALL, _unpack(r, shard_shapes))) for r in sres]

    n_mod_rows = depth * 6 * D // 128
    dmod_all = packed_all[:, :n_mod_rows].reshape(N_DEV, depth, N_CHIPS, cs_ada)
    dmod_cols = lax.dynamic_slice_in_dim(dmod_all, chip, 1, axis=2).reshape(N_DEV, depth * cs_ada)
    kpad = 128 - N_DEV
    dmod_pad = jnp.concatenate([dmod_cols, jnp.zeros((kpad, depth * cs_ada), F32)], axis=0)
    ct_pad = jnp.concatenate([c_all.T, jnp.zeros((D, kpad), F32)], axis=1)
    g_ada = _mm(ct_pad, dmod_pad, mode="nn", name="ada_dw", a_silu=True, out3d=(depth, cs_ada), tn=cs_ada)
    ada_out = _adamw(g_ada.reshape(depth, 1, D, cs_ada), ada_w, m_ada_w, v_ada_w, name="adamw_ada")

    flat_g = [G[nm][l] for nm in BIG for l in range(len(G[nm]))]
    sib = _swap_half(flat_g, name="swap_grads")
    half_idx = mc.astype(jnp.int32).reshape(1)
    chip_sum = [_add_half(a, b, half_idx, name="add_grads") for a, b in zip(flat_g, sib)]
    groups, pos = [], 0
    for nm in BIG:
        L = len(G[nm])
        groups.append(chip_sum[pos:pos + L])
        pos += L
    slots = _scatter_chips(groups, name="scatter_grads")
    halves = [_sum_slots(sl, name="sum_grads") for sl in slots]
    totals = _join_halves(halves, name="join_grads")
    big_out = {nm: _adamw(t.reshape(t.shape[0], 1, t.shape[1], t.shape[2]), W[nm], Mo[nm], Vo[nm],
                          name="adamw_big") for nm, t in zip(BIG, totals)}

    def res(nm, i):
        if nm == "ada_w":
            return ada_out[i]
        if nm in big_out:
            return big_out[nm][i]
        return small_out[i][nm]

    return (loss, grad_x, *[res(nm, 0) for nm in WEIGHTS], *[res(nm, 1) for nm in WEIGHTS],
            *[res(nm, 2) for nm in WEIGHTS], *[res(nm, 3) for nm in WEIGHTS])
```

```python
import functools
import math

import jax
import jax.numpy as jnp
from jax import lax
from jax.experimental import pallas as pl
from jax.experimental.pallas import tpu as pltpu

F32 = jnp.float32
BF16 = jnp.bfloat16
MESH = pl.DeviceIdType.MESH

EPS = 1e-6
CHUNK = 64
CONV_WIDTH = 31
CONV_HALO = 32
SUBLANES = 8
LANES = 128
CONV_BLOCK_ROWS = 128
RET_HEADS = 4
ROPE_BASE = 10000.0
ADAM_LR = 0.001
ADAM_B1 = 0.9
ADAM_B2 = 0.999
ADAM_EPS = 1e-08
ADAM_WD = 0.01
ADAM_STEP = 10
N_CHIPS = 4
N_DEV = 8
V7X_VMEM_LIMIT = 48 * 1024 * 1024
ANY = pl.BlockSpec(memory_space=pl.ANY)


def _cparams(sem=None):
    return pltpu.CompilerParams(dimension_semantics=sem, vmem_limit_bytes=V7X_VMEM_LIMIT)


def _sigmoid(x):
    return jax.nn.sigmoid(x)


def _silu(x):
    return x * _sigmoid(x)


_DIMS = {
    "nn": (((1,), (0,)), ((), ())),
    "nt": (((1,), (1,)), ((), ())),
    "tn": (((0,), (0,)), ((), ())),
}


def _mm(a, b, *, mode, name, out_dtype=F32, tm=1024, tn=1024, tk=1024, b3d=False, out3d=None,
        bias=None, epi=None, extra=None, a_silu=False):
    if mode == "tn":
        K, M = a.shape
    else:
        M, K = a.shape
    if b3d:
        P, R, Cs = b.shape
        bshape = (R, P * Cs)
    else:
        bshape = b.shape
    N = bshape[0] if mode == "nt" else bshape[1]
    assert (bshape[1] if mode == "nt" else bshape[0]) == K, (name, a.shape, b.shape)
    tm, tn, tk = min(tm, M), min(tn, N), min(tk, K)
    assert M % tm == 0 and N % tn == 0 and K % tk == 0, (name, M, N, K, tm, tn, tk)
    nk = K // tk

    if mode == "tn":
        a_spec = pl.BlockSpec((tk, tm), lambda i, j, k: (k, i))
    else:
        a_spec = pl.BlockSpec((tm, tk), lambda i, j, k: (i, k))
    if mode == "nt":
        if b3d:
            nb = Cs // tk
            assert Cs % tk == 0
            b_spec = pl.BlockSpec((None, tn, tk), lambda i, j, k: (k // nb, j, k % nb))
        else:
            b_spec = pl.BlockSpec((tn, tk), lambda i, j, k: (j, k))
    else:
        if b3d:
            nb = Cs // tn
            assert Cs % tn == 0
            b_spec = pl.BlockSpec((None, tk, tn), lambda i, j, k: (j // nb, k, j % nb))
        else:
            b_spec = pl.BlockSpec((tk, tn), lambda i, j, k: (k, j))
    in_specs = [a_spec, b_spec]
    args = [a, b]
    if bias is not None:
        in_specs.append(pl.BlockSpec((1, tn), lambda i, j, k: (0, j)))
        args.append(bias)
    if extra is not None:
        in_specs.append(pl.BlockSpec((tm, tn), lambda i, j, k: (i, j)))
        args.append(extra)

    if out3d is not None:
        P_o, Cs_o = out3d
        assert P_o * Cs_o == N and Cs_o % tn == 0
        nbo = Cs_o // tn
        o_spec = pl.BlockSpec((None, tm, tn), lambda i, j, k: (j // nbo, i, j % nbo))
        o_shape = (P_o, M, Cs_o)
    else:
        o_spec = pl.BlockSpec((tm, tn), lambda i, j, k: (i, j))
        o_shape = (M, N)
    if epi == "relu2":
        out_shape = [jax.ShapeDtypeStruct(o_shape, BF16), jax.ShapeDtypeStruct(o_shape, BF16)]
        out_specs = [o_spec, o_spec]
    else:
        out_shape = jax.ShapeDtypeStruct(o_shape, out_dtype)
        out_specs = o_spec
    n_out = 2 if epi == "relu2" else 1
    dims = _DIMS[mode]
    has_bias, has_extra = bias is not None, extra is not None

    def body(*refs):
        a_ref, b_ref = refs[0], refs[1]
        pos = 2
        bias_ref = extra_ref = None
        if has_bias:
            bias_ref = refs[pos]
            pos += 1
        if has_extra:
            extra_ref = refs[pos]
            pos += 1
        outs = refs[pos:pos + n_out]
        acc_ref = refs[pos + n_out] if nk > 1 else None

        def partial():
            av = a_ref[...]
            if a_silu:
                av = _silu(av)
            return lax.dot_general(av, b_ref[...], dims, preferred_element_type=F32)

        def finish(r):
            if has_bias:
                r = r + bias_ref[...]
            if epi == "relu2":
                rr = jnp.maximum(r, 0.0)
                outs[0][...] = rr.astype(BF16)
                outs[1][...] = (rr * rr).astype(BF16)
            elif epi == "mul2":
                outs[0][...] = (r * 2.0 * extra_ref[...].astype(F32)).astype(outs[0].dtype)
            else:
                outs[0][...] = r.astype(outs[0].dtype)

        if nk == 1:
            finish(partial())
        else:
            k = pl.program_id(2)

            @pl.when(k == 0)
            def _():
                acc_ref[...] = jnp.zeros_like(acc_ref)

            acc_ref[...] += partial()

            @pl.when(k == nk - 1)
            def _():
                finish(acc_ref[...])

    return pl.pallas_call(
        body, name=name, grid=(M // tm, N // tn, nk), in_specs=in_specs, out_specs=out_specs,
        out_shape=out_shape,
        scratch_shapes=[pltpu.VMEM((tm, tn), F32)] if nk > 1 else [],
        compiler_params=_cparams(("parallel", "parallel", "arbitrary")),
    )(*args)


def _modnorm(x, gain, shift, scale):
    y = x * lax.rsqrt(jnp.mean(x * x, axis=-1, keepdims=True) + EPS)
    return (y * gain) * (1.0 + scale) + shift


def _row_fwd(xprev, y, vec, *, name, ts=512):
    S, D = xprev.shape
    ts = min(ts, S)
    has_res = y is not None
    row = pl.BlockSpec((ts, D), lambda i: (i, 0))
    vspec = pl.BlockSpec((8, D), lambda i: (0, 0))

    def body(*refs):
        if has_res:
            xp_ref, y_ref, v_ref, x_ref, h_ref = refs
            x = xp_ref[...] + v_ref[0:1, :] * y_ref[...]
            x_ref[...] = x
        else:
            xp_ref, v_ref, h_ref = refs
            x = xp_ref[...]
        h_ref[...] = _modnorm(x, v_ref[1:2, :], v_ref[2:3, :], v_ref[3:4, :]).astype(BF16)

    if has_res:
        return pl.pallas_call(
            body, name=name, grid=(S // ts,), in_specs=[row, row, vspec], out_specs=[row, row],
            out_shape=[jax.ShapeDtypeStruct((S, D), F32), jax.ShapeDtypeStruct((S, D), BF16)],
            compiler_params=_cparams(("parallel",)),
        )(xprev, y, vec)
    h = pl.pallas_call(
        body, name=name, grid=(S // ts,), in_specs=[row, vspec], out_specs=row,
        out_shape=jax.ShapeDtypeStruct((S, D), BF16),
        compiler_params=_cparams(("parallel",)),
    )(xprev, vec)
    return xprev, h


def _row_bwd(xin, dh, dxout, yprev, vec, *, name, ts=512):
    S, D = xin.shape
    ts = min(ts, S)
    row = pl.BlockSpec((ts, D), lambda i: (i, 0))
    vspec = pl.BlockSpec((8, D), lambda i: (0, 0))

    def body(x_ref, dh_ref, dx_ref, y_ref, v_ref, dxin_ref, dy_ref, part_ref):
        @pl.when(pl.program_id(0) == 0)
        def _():
            part_ref[...] = jnp.zeros_like(part_ref)

        gate = v_ref[0:1, :]
        _, vjp = jax.vjp(_modnorm, x_ref[...], v_ref[1:2, :], v_ref[2:3, :], v_ref[3:4, :])
        dxn, dgain, dshift, dscale = vjp(dh_ref[...])
        dxin = dx_ref[...] + dxn
        dxin_ref[...] = dxin
        dy = dxin * gate
        dy_ref[...] = dy.astype(BF16)
        part_ref[0:1, :] += jnp.sum(dxin * y_ref[...], axis=0, keepdims=True)
        part_ref[1:2, :] += dgain
        part_ref[2:3, :] += dshift
        part_ref[3:4, :] += dscale
        part_ref[4:5, :] += jnp.sum(dy, axis=0, keepdims=True)

    return pl.pallas_call(
        body, name=name, grid=(S // ts,), in_specs=[row, row, row, row, vspec],
        out_specs=[row, row, vspec],
        out_shape=[jax.ShapeDtypeStruct((S, D), F32), jax.ShapeDtypeStruct((S, D), BF16),
                   jax.ShapeDtypeStruct((8, D), F32)],
        compiler_params=_cparams(("arbitrary",)),
    )(xin, dh, dxout, yprev, vec)


def _final(xprev, y, target, vec, *, name, ts=512):
    S, D = xprev.shape
    ts = min(ts, S)
    row = pl.BlockSpec((ts, D), lambda i: (i, 0))
    vspec = pl.BlockSpec((8, D), lambda i: (0, 0))

    def norm(x, gain):
        return x * lax.rsqrt(jnp.mean(x * x, axis=-1, keepdims=True) + EPS) * gain

    def body(xp_ref, y_ref, t_ref, v_ref, dx_ref, dy_ref, part_ref):
        @pl.when(pl.program_id(0) == 0)
        def _():
            part_ref[...] = jnp.zeros_like(part_ref)

        gate = v_ref[0:1, :]
        yv = y_ref[...]
        x = xp_ref[...] + gate * yv
        out, vjp = jax.vjp(norm, x, v_ref[1:2, :])
        err = out - t_ref[...]
        dx, dgain = vjp(err * (1.0 / D))
        dx_ref[...] = dx
        dy_ref[...] = (dx * gate).astype(BF16)
        part_ref[0:1, :] += jnp.sum(dx * yv, axis=0, keepdims=True)
        part_ref[1:2, :] += dgain
        part_ref[2:3, :] += jnp.sum(err * err, axis=0, keepdims=True) * (0.5 / D)

    return pl.pallas_call(
        body, name=name, grid=(S // ts,), in_specs=[row, row, row, vspec], out_specs=[row, row, vspec],
        out_shape=[jax.ShapeDtypeStruct((S, D), F32), jax.ShapeDtypeStruct((S, D), BF16),
                   jax.ShapeDtypeStruct((8, D), F32)],
        compiler_params=_cparams(("arbitrary",)),
    )(xprev, y, target, vec)


def _ln_silu(cv, g, b):
    mu = jnp.mean(cv, axis=-1, keepdims=True)
    var = jnp.mean(jnp.square(cv - mu), axis=-1, keepdims=True)
    u = (cv - mu) * lax.rsqrt(var + EPS) * g + b
    return _silu(u)


def _shift_copies(ext, sh, n):
    for b in range(1, SUBLANES):
        sh[b - 1, 0:n, :] = ext[pl.ds(b, n), :]


def _shifted(ext, sh, off, r0, rows, cols):
    a, b = divmod(off, SUBLANES)
    if b == 0:
        return ext[pl.ds(SUBLANES * a + r0, rows), cols]
    return sh[b - 1, pl.ds(SUBLANES * a + r0, rows), cols]


def _conv_fwd(u, wdw, vec, *, name, ts=256):
    S, D2 = u.shape
    D = D2 // 2
    ts = min(ts, S)
    H = CONV_HALO
    row = pl.BlockSpec((ts, D), lambda i: (i, 0))

    rb_rows = min(CONV_BLOCK_ROWS, ts)

    def body(u_ref, w_ref, v_ref, vo_ref, cv_ref, z_ref, ext, sh):
        @pl.when(pl.program_id(0) == 0)
        def _():
            ext[0:H, :] = jnp.zeros((H, D), F32)

        uu = u_ref[...]
        v = uu[:, :D] * _sigmoid(uu[:, D:])
        vo_ref[...] = v
        ext[H:H + ts, :] = v
        _shift_copies(ext, sh, ts + H - 8)
        for r0 in range(0, ts, rb_rows):
            for c0 in range(0, D, LANES):
                cols = pl.ds(c0, LANES)
                acc = jnp.zeros((rb_rows, LANES), F32)
                for t in range(CONV_WIDTH):
                    src = _shifted(ext, sh, H - (CONV_WIDTH - 1) + t, r0, rb_rows, cols)
                    acc = acc + src * w_ref[pl.ds(t, 1), cols]
                cv_ref[pl.ds(r0, rb_rows), cols] = acc + v_ref[0:1, cols]
        z_ref[...] = _ln_silu(cv_ref[...], v_ref[1:2, :], v_ref[2:3, :]).astype(BF16)
        ext[0:H, :] = ext[ts:ts + H, :]

    return pl.pallas_call(
        body, name=name, grid=(S // ts,),
        in_specs=[pl.BlockSpec((ts, D2), lambda i: (i, 0)), pl.BlockSpec((H, D), lambda i: (0, 0)),
                  pl.BlockSpec((8, D), lambda i: (0, 0))],
        out_specs=[row, row, row],
        out_shape=[jax.ShapeDtypeStruct((S, D), F32), jax.ShapeDtypeStruct((S, D), F32),
                   jax.ShapeDtypeStruct((S, D), BF16)],
        scratch_shapes=[pltpu.VMEM((ts + H, D), F32), pltpu.VMEM((7, ts + H - 8, D), F32)],
        compiler_params=_cparams(("arbitrary",)),
    )(u, wdw, vec)


def _conv_bwd(dz, cv, v, u, wdw, vec, *, name, ts=256):
    S, D = cv.shape
    ts = min(ts, S)
    H = CONV_HALO
    nt = S // ts
    per = ts // H
    rev = lambda i: (nt - 1 - i, 0)
    row = pl.BlockSpec((ts, D), rev)

    rb_rows = min(CONV_BLOCK_ROWS, ts)
    nsh = ts + H - 8

    def body(dz_ref, cv_ref, v_ref, vh_ref, u_ref, w_ref, vec_ref, du_ref, dw_ref, part_ref, dbu_ref,
             dext, vext, dsh, vsh, dwacc, dvbuf):
        i = pl.program_id(0)

        @pl.when(i == 0)
        def _():
            dext[ts:ts + H, :] = jnp.zeros((H, D), F32)
            dwacc[...] = jnp.zeros_like(dwacc)
            part_ref[...] = jnp.zeros_like(part_ref)
            dbu_ref[...] = jnp.zeros_like(dbu_ref)

        _, vjp = jax.vjp(_ln_silu, cv_ref[...], vec_ref[1:2, :], vec_ref[2:3, :])
        dcv, dg, db = vjp(dz_ref[...])
        part_ref[0:1, :] += jnp.sum(dcv, axis=0, keepdims=True)
        part_ref[1:2, :] += dg
        part_ref[2:3, :] += db
        dext[0:ts, :] = dcv
        vext[0:H, :] = vh_ref[...] * jnp.where(i == nt - 1, 0.0, 1.0)
        vext[H:H + ts, :] = v_ref[...]
        _shift_copies(dext, dsh, nsh)
        _shift_copies(vext, vsh, nsh)
        for r0 in range(0, ts, rb_rows):
            for c0 in range(0, D, LANES):
                cols = pl.ds(c0, LANES)
                dblk = dext[pl.ds(r0, rb_rows), cols]
                dv = jnp.zeros((rb_rows, LANES), F32)
                for t in range(CONV_WIDTH):
                    prod = dblk * _shifted(vext, vsh, H - (CONV_WIDTH - 1) + t, r0, rb_rows, cols)
                    parts = [prod[s:s + SUBLANES, :] for s in range(0, rb_rows, SUBLANES)]
                    while len(parts) > 1:
                        parts = [parts[k] + parts[k + 1] for k in range(0, len(parts), 2)]
                    dwacc[pl.ds(t * SUBLANES, SUBLANES), cols] += parts[0]
                    dv = dv + _shifted(dext, dsh, CONV_WIDTH - 1 - t, r0, rb_rows, cols) * w_ref[pl.ds(t, 1), cols]
                dvbuf[pl.ds(r0, rb_rows), cols] = dv
        dv = dvbuf[...]
        uu = u_ref[...]
        a, g = uu[:, :D], uu[:, D:]
        sg = _sigmoid(g)
        da = dv * sg
        dg_ = dv * a * sg * (1.0 - sg)
        du = jnp.concatenate([da, dg_], axis=-1)
        du_ref[...] = du.astype(BF16)
        dbu_ref[0:1, :] += jnp.sum(du, axis=0, keepdims=True)
        dext[ts:ts + H, :] = dext[0:H, :]

        @pl.when(i == nt - 1)
        def _():
            dw_ref[...] = jnp.zeros_like(dw_ref)
            for t in range(CONV_WIDTH):
                dw_ref[pl.ds(t, 1), :] = jnp.sum(dwacc[pl.ds(t * SUBLANES, SUBLANES), :], axis=0, keepdims=True)

    return pl.pallas_call(
        body, name=name, grid=(nt,),
        in_specs=[row, row, row,
                  pl.BlockSpec((H, D), lambda i: (jnp.maximum((nt - 1 - i) * per - 1, 0), 0)),
                  pl.BlockSpec((ts, 2 * D), rev), pl.BlockSpec((H, D), lambda i: (0, 0)),
                  pl.BlockSpec((8, D), lambda i: (0, 0))],
        out_specs=[pl.BlockSpec((ts, 2 * D), rev), pl.BlockSpec((H, D), lambda i: (0, 0)),
                   pl.BlockSpec((8, D), lambda i: (0, 0)), pl.BlockSpec((8, 2 * D), lambda i: (0, 0))],
        out_shape=[jax.ShapeDtypeStruct((S, 2 * D), BF16), jax.ShapeDtypeStruct((H, D), F32),
                   jax.ShapeDtypeStruct((8, D), F32), jax.ShapeDtypeStruct((8, 2 * D), F32)],
        scratch_shapes=[pltpu.VMEM((ts + H, D), F32), pltpu.VMEM((ts + H, D), F32),
                        pltpu.VMEM((7, nsh, D), F32), pltpu.VMEM((7, nsh, D), F32),
                        pltpu.VMEM((CONV_WIDTH * SUBLANES, D), F32), pltpu.VMEM((ts, D), F32)],
        compiler_params=_cparams(("arbitrary",)),
    )(dz, cv, v, v, u, wdw, vec)


def _rope(x, c, s, half):
    x1, x2 = x[:, :half], x[:, half:]
    return jnp.concatenate([x1 * c - x2 * s, x2 * c + x1 * s], axis=-1)


def _rope_t(d, c, s, half):
    d1, d2 = d[:, :half], d[:, half:]
    return jnp.concatenate([d1 * c + d2 * s, d2 * c - d1 * s], axis=-1)


def _gn_gate(y, gate, g, b):
    mu = jnp.mean(y, axis=-1, keepdims=True)
    var = jnp.mean(jnp.square(y - mu), axis=-1, keepdims=True)
    return _silu(gate) * ((y - mu) * lax.rsqrt(var + EPS) * g + b)


def _dot(a, b, mode="nn"):
    return lax.dot_general(a, b, _DIMS[mode], preferred_element_type=F32)


def _ret_tables(H):
    lg = jnp.log(1.0 - 2.0 ** (-5.0 - jnp.arange(H, dtype=F32)))
    idx = jnp.arange(CHUNK, dtype=F32)
    dmat = jnp.exp(lg[:, None, None] * jnp.abs(idx[:, None] - idx[None, :]))
    xi = jnp.exp(lg[:, None] * (idx + 1.0))[..., None]
    zeta = jnp.exp(lg[:, None] * (CHUNK - 1.0 - idx))[..., None]
    dec = jnp.exp(lg * CHUNK)[:, None, None]
    return dmat, xi, zeta, dec


def _ret_specs(R, dk, dv, half, order):
    H = RET_HEADS
    C = CHUNK
    nq = H
    return dict(
        q=pl.BlockSpec((R, dk), lambda h, n: (order(n), h)),
        k=pl.BlockSpec((R, dk), lambda h, n: (order(n), nq + h)),
        v=pl.BlockSpec((R, dv), lambda h, n: (order(n), nq + h)),
        gate=pl.BlockSpec((R, dv), lambda h, n: (order(n), 2 * nq + h)),
        rope=pl.BlockSpec((R, half), lambda h, n: (order(n), 0)),
        dmat=pl.BlockSpec((None, C, C), lambda h, n: (h, 0, 0)),
        col=pl.BlockSpec((None, C, 1), lambda h, n: (h, 0, 0)),
        one=pl.BlockSpec((None, 1, 1), lambda h, n: (h, 0, 0)),
        gn=pl.BlockSpec((None, 1, dv), lambda h, n: (h, 0, 0)),
        yv=pl.BlockSpec((R, dv), lambda h, n: (order(n), h)),
        yk=pl.BlockSpec((R, dk), lambda h, n: (order(n), h)),
    )


def _ret_fwd(proj, cos, sin, tables, gn_g, gn_b, *, name, cps=4):
    S = proj.shape[0]
    D = proj.shape[1] // 6
    H, C = RET_HEADS, CHUNK
    dk, dv, half = D // H, 2 * D // H, D // H // 2
    nc = S // C
    cps = min(cps, nc)
    R = cps * C
    scale = dk ** -0.5
    sp = _ret_specs(R, dk, dv, half, lambda n: n)
    dmat, xi, zeta, dec = tables

    def body(q_ref, k_ref, v_ref, g_ref, cos_ref, sin_ref, dm_ref, xi_ref, ze_ref, dec_ref, gg_ref, gb_ref,
             y_ref, y2_ref, st_ref, state):
        @pl.when(pl.program_id(1) == 0)
        def _():
            state[...] = jnp.zeros_like(state)

        dm, xv, zv, dc = dm_ref[...], xi_ref[...], ze_ref[...], dec_ref[...]
        for j in range(cps):
            rows = pl.ds(j * C, C)
            cs, sn = cos_ref[rows, :], sin_ref[rows, :]
            qr = _rope(q_ref[rows, :].astype(F32), cs, sn, half)
            kr = _rope(k_ref[rows, :].astype(F32), cs, sn, half) * scale
            vb = v_ref[rows, :]
            p = (_dot(qr.astype(BF16), kr.astype(BF16), "nt") * dm).astype(BF16)
            st = state[...]
            stb = st.astype(BF16)
            st_ref[j] = stb
            y = _dot(p, vb) + _dot((qr * xv).astype(BF16), stb)
            state[...] = st * dc + _dot((kr * zv).astype(BF16), vb, "tn")
            y_ref[rows, :] = y
            y2_ref[rows, :] = _gn_gate(y, g_ref[rows, :].astype(F32), gg_ref[...], gb_ref[...]).astype(BF16)

    return pl.pallas_call(
        body, name=name, grid=(H, nc // cps),
        in_specs=[sp["q"], sp["k"], sp["v"], sp["gate"], sp["rope"], sp["rope"], sp["dmat"], sp["col"],
                  sp["col"], sp["one"], sp["gn"], sp["gn"]],
        out_specs=[sp["yv"], sp["yv"], pl.BlockSpec((None, cps, dk, dv), lambda h, n: (h, n, 0, 0))],
        out_shape=[jax.ShapeDtypeStruct((S, 2 * D), F32), jax.ShapeDtypeStruct((S, 2 * D), BF16),
                   jax.ShapeDtypeStruct((H, nc, dk, dv), BF16)],
        scratch_shapes=[pltpu.VMEM((dk, dv), F32)],
        compiler_params=_cparams(("arbitrary", "arbitrary")),
    )(proj, proj, proj, proj, cos, sin, dmat, xi, zeta, dec, gn_g, gn_b)


def _ret_bwd(proj, cos, sin, tables, gn_g, gn_b, y, dy2, states, *, name, cps=4):
    S = proj.shape[0]
    D = proj.shape[1] // 6
    H, C = RET_HEADS, CHUNK
    dk, dv, half = D // H, 2 * D // H, D // H // 2
    nc = S // C
    cps = min(cps, nc)
    ns = nc // cps
    R = cps * C
    scale = dk ** -0.5
    order = lambda n: ns - 1 - n
    sp = _ret_specs(R, dk, dv, half, order)
    dmat, xi, zeta, dec = tables

    def body(q_ref, k_ref, v_ref, g_ref, cos_ref, sin_ref, dm_ref, xi_ref, ze_ref, dec_ref, gg_ref, gb_ref,
             y_ref, dy2_ref, st_ref, dq_ref, dk_ref, dv_ref, dg_ref, dgg_ref, dgb_ref, gst):
        @pl.when(pl.program_id(1) == 0)
        def _():
            gst[...] = jnp.zeros_like(gst)
            dgg_ref[...] = jnp.zeros_like(dgg_ref)
            dgb_ref[...] = jnp.zeros_like(dgb_ref)

        dm, xv, zv, dc = dm_ref[...], xi_ref[...], ze_ref[...], dec_ref[...]
        for j in reversed(range(cps)):
            rows = pl.ds(j * C, C)
            cs, sn = cos_ref[rows, :], sin_ref[rows, :]
            _, vjp = jax.vjp(_gn_gate, y_ref[rows, :], g_ref[rows, :].astype(F32), gg_ref[...], gb_ref[...])
            dy, dgate, dgg, dgb = vjp(dy2_ref[rows, :])
            dgg_ref[...] += dgg
            dgb_ref[...] += dgb
            dg_ref[rows, :] = dgate.astype(BF16)
            dyb = dy.astype(BF16)
            qr = _rope(q_ref[rows, :].astype(F32), cs, sn, half)
            kr = _rope(k_ref[rows, :].astype(F32), cs, sn, half) * scale
            qb, kb, vb = qr.astype(BF16), kr.astype(BF16), v_ref[rows, :]
            p = (_dot(qb, kb, "nt") * dm).astype(BF16)
            g = gst[...]
            gb16 = g.astype(BF16)
            sprev = st_ref[j]
            dvv = _dot(p, dyb, "tn") + _dot((kr * zv).astype(BF16), gb16)
            dpb = (_dot(dyb, vb, "nt") * dm).astype(BF16)
            dqr = _dot(dpb, kb) + _dot(dyb, sprev, "nt") * xv
            dkr = _dot(dpb, qb, "tn") + _dot(vb, gb16, "nt") * zv
            gst[...] = g * dc + _dot((qr * xv).astype(BF16), dyb, "tn")
            dq_ref[rows, :] = _rope_t(dqr, cs, sn, half).astype(BF16)
            dk_ref[rows, :] = _rope_t(dkr * scale, cs, sn, half).astype(BF16)
            dv_ref[rows, :] = dvv.astype(BF16)

    return pl.pallas_call(
        body, name=name, grid=(H, ns),
        in_specs=[sp["q"], sp["k"], sp["v"], sp["gate"], sp["rope"], sp["rope"], sp["dmat"], sp["col"],
                  sp["col"], sp["one"], sp["gn"], sp["gn"], sp["yv"], sp["yv"],
                  pl.BlockSpec((None, cps, dk, dv), lambda h, n: (h, order(n), 0, 0))],
        out_specs=[sp["yk"], sp["yk"], sp["yv"], sp["yv"], sp["gn"], sp["gn"]],
        out_shape=[jax.ShapeDtypeStruct((S, D), BF16), jax.ShapeDtypeStruct((S, D), BF16),
                   jax.ShapeDtypeStruct((S, 2 * D), BF16), jax.ShapeDtypeStruct((S, 2 * D), BF16),
                   jax.ShapeDtypeStruct((H, 1, dv), F32), jax.ShapeDtypeStruct((H, 1, dv), F32)],
        scratch_shapes=[pltpu.VMEM((dk, dv), F32)],
        compiler_params=_cparams(("arbitrary", "arbitrary")),
    )(proj, proj, proj, proj, cos, sin, dmat, xi, zeta, dec, gn_g, gn_b, y, dy2, states)


def _rows_tile(rows, cols, n_arrays):
    cap = max(8, V7X_VMEM_LIMIT // 3 // (n_arrays * 2 * 4 * cols))
    t = rows
    while t > cap and t % 2 == 0:
        t //= 2
    return t


def _add_half(g, r, half_idx, *, name):
    P, R, Cc = g.shape
    hR = R // 2
    tr = _rows_tile(hR, Cc, 3)
    nb = hR // tr

    def body(h_ref, g_ref, r_ref, o_ref):
        o_ref[...] = (g_ref[...] + r_ref[...]).astype(BF16)

    return pl.pallas_call(
        body, name=name,
        grid_spec=pltpu.PrefetchScalarGridSpec(
            num_scalar_prefetch=1, grid=(P, nb),
            in_specs=[pl.BlockSpec((None, tr, Cc), lambda s, i, h: (s, h[0] * nb + i, 0)),
                      pl.BlockSpec((None, tr, Cc), lambda s, i, h: (s, i, 0))],
            out_specs=pl.BlockSpec((None, tr, Cc), lambda s, i, h: (s, i, 0))),
        out_shape=jax.ShapeDtypeStruct((P, hR, Cc), BF16), compiler_params=_cparams(("parallel", "parallel")),
    )(half_idx, g, r)


def _sum_slots(x, *, name):
    L, NS, R, Cc = x.shape
    tr = _rows_tile(R, Cc, NS + 1)

    def body(x_ref, o_ref):
        acc = x_ref[0].astype(F32)
        for s in range(1, NS):
            acc = acc + x_ref[s].astype(F32)
        o_ref[...] = acc

    return pl.pallas_call(
        body, name=name, grid=(L, R // tr),
        in_specs=[pl.BlockSpec((None, NS, tr, Cc), lambda l, i: (l, 0, i, 0))],
        out_specs=pl.BlockSpec((None, tr, Cc), lambda l, i: (l, i, 0)),
        out_shape=jax.ShapeDtypeStruct((L, R, Cc), F32), compiler_params=_cparams(("parallel", "parallel")),
    )(x)


def _adam_store(g, w_ref, m_ref, v_ref, go_ref, d_ref, mo_ref, vo_ref):
    mn = ADAM_B1 * m_ref[...] + (1.0 - ADAM_B1) * g
    vn = ADAM_B2 * v_ref[...] + (1.0 - ADAM_B2) * jnp.square(g)
    m_hat = mn / (1.0 - ADAM_B1 ** ADAM_STEP)
    v_hat = vn / (1.0 - ADAM_B2 ** ADAM_STEP)
    go_ref[...] = g
    d_ref[...] = -ADAM_LR * (m_hat / (jnp.sqrt(v_hat) + ADAM_EPS) + ADAM_WD * w_ref[...])
    mo_ref[...] = mn
    vo_ref[...] = vn


def _adamw(gslots, w, m, v, *, name):
    L, NS, R, Cc = gslots.shape
    tr = _rows_tile(R, Cc, NS + 7)
    gspec = pl.BlockSpec((None, NS, tr, Cc), lambda l, i: (l, 0, i, 0))
    spec = pl.BlockSpec((None, tr, Cc), lambda l, i: (l, i, 0))

    def body(g_ref, *refs):
        g = g_ref[0]
        for s in range(1, NS):
            g = g + g_ref[s]
        _adam_store(g, *refs)

    sd = jax.ShapeDtypeStruct((L, R, Cc), F32)
    return pl.pallas_call(
        body, name=name, grid=(L, R // tr), in_specs=[gspec, spec, spec, spec],
        out_specs=[spec, spec, spec, spec], out_shape=[sd, sd, sd, sd],
        compiler_params=_cparams(("parallel", "parallel")),
    )(gslots, w, m, v)


def _adamw_halves(g_mine, g_sib, half_idx, w, m, v, *, name):
    L, hR, Cc = g_mine.shape
    tr = _rows_tile(hR, Cc, 9)
    nbh = hR // tr
    gspec = pl.BlockSpec((None, tr, Cc), lambda l, i, h: (l, i % nbh, 0))
    spec = pl.BlockSpec((None, tr, Cc), lambda l, i, h: (l, i, 0))

    def body(h_ref, gm_ref, gs_ref, *refs):
        mine = (pl.program_id(1) // nbh) == h_ref[0]
        _adam_store(jnp.where(mine, gm_ref[...], gs_ref[...]), *refs)

    sd = jax.ShapeDtypeStruct((L, 2 * hR, Cc), F32)
    return pl.pallas_call(
        body, name=name,
        grid_spec=pltpu.PrefetchScalarGridSpec(
            num_scalar_prefetch=1, grid=(L, 2 * nbh), in_specs=[gspec, gspec, spec, spec, spec],
            out_specs=[spec, spec, spec, spec]),
        out_shape=[sd, sd, sd, sd], compiler_params=_cparams(("parallel", "parallel")),
    )(half_idx, g_mine, g_sib, w, m, v)


def _me():
    return lax.axis_index("x"), lax.axis_index("y"), lax.axis_index("c")


def _flip(v, bit):
    return 1 - v if bit else v


def _allgather8(x, *, name):
    def body(x_ref, out_ref, send_sems, recv_sems, loc_sem):
        mx, my, mc = _me()
        me = 4 * mx + 2 * my + mc
        loc = pltpu.make_async_copy(x_ref, out_ref.at[me], loc_sem)
        loc.start()
        sends, recvs = [], []
        for k in range(1, N_DEV):
            px, py, pc = _flip(mx, k & 4), _flip(my, k & 2), _flip(mc, k & 1)
            sends.append(pltpu.make_async_remote_copy(
                src_ref=x_ref, dst_ref=out_ref.at[me], send_sem=send_sems.at[k - 1],
                recv_sem=recv_sems.at[k - 1], device_id=(px, py, pc), device_id_type=MESH))
            recvs.append(pltpu.make_async_remote_copy(
                src_ref=x_ref, dst_ref=out_ref.at[4 * px + 2 * py + pc], send_sem=send_sems.at[k - 1],
                recv_sem=recv_sems.at[k - 1], device_id=(px, py, pc), device_id_type=MESH))
        for cp in sends:
            cp.start()
        for cp in recvs:
            cp.wait_recv()
        for cp in sends:
            cp.wait_send()
        loc.wait()

    return pl.pallas_call(
        body, name=name, in_specs=[ANY], out_specs=ANY,
        out_shape=jax.ShapeDtypeStruct((N_DEV,) + x.shape, x.dtype),
        scratch_shapes=[pltpu.SemaphoreType.DMA((N_DEV - 1,)), pltpu.SemaphoreType.DMA((N_DEV - 1,)),
                        pltpu.SemaphoreType.DMA],
    )(x)


def _gather_chips(arrays, *, name):
    n = len(arrays)

    def body(*refs):
        ins, outs = refs[:n], refs[n:2 * n]
        ici_send, ici_recv, d2d_send, d2d_recv, own_send, own_recv = refs[2 * n:]
        mx, my, mc = _me()
        me = 2 * mx + my
        sib = (mx, my, 1 - mc)
        locs, sends, lands, passes, gifts = [], [], [], [], []
        for a in range(n):
            h = arrays[a].shape[0] // 2
            mine, other = pl.ds(mc * h, h), pl.ds((1 - mc) * h, h)
            locs.append(pltpu.make_async_remote_copy(
                src_ref=ins[a], dst_ref=outs[a].at[me], send_sem=own_send.at[a], recv_sem=own_recv.at[a],
                device_id=sib, device_id_type=MESH))
            for k in range(1, N_CHIPS):
                px, py = _flip(mx, k & 2), _flip(my, k & 1)
                peer = 2 * px + py
                ici = dict(send_sem=ici_send.at[a, k - 1], recv_sem=ici_recv.at[a, k - 1],
                           device_id=(px, py, mc), device_id_type=MESH)
                d2d = dict(send_sem=d2d_send.at[a, k - 1], recv_sem=d2d_recv.at[a, k - 1],
                           device_id=sib, device_id_type=MESH)
                sends.append(pltpu.make_async_remote_copy(
                    src_ref=ins[a].at[mine], dst_ref=outs[a].at[me, mine], **ici))
                lands.append(pltpu.make_async_remote_copy(
                    src_ref=ins[a].at[mine], dst_ref=outs[a].at[peer, mine], **ici))
                passes.append(pltpu.make_async_remote_copy(
                    src_ref=outs[a].at[peer, mine], dst_ref=outs[a].at[peer, mine], **d2d))
                gifts.append(pltpu.make_async_remote_copy(
                    src_ref=outs[a].at[peer, other], dst_ref=outs[a].at[peer, other], **d2d))
        for cp in locs + sends:
            cp.start()
        for land, fwd in zip(lands, passes):
            land.wait_recv()
            fwd.start()
        for cp in gifts + locs:
            cp.wait_recv()
        for cp in sends + passes + locs:
            cp.wait_send()

    nsem = (n, N_CHIPS - 1)
    return pl.pallas_call(
        body, name=name, in_specs=[ANY] * n, out_specs=[ANY] * n,
        out_shape=[jax.ShapeDtypeStruct((N_CHIPS,) + a.shape, a.dtype) for a in arrays],
        scratch_shapes=[pltpu.SemaphoreType.DMA(nsem), pltpu.SemaphoreType.DMA(nsem), pltpu.SemaphoreType.DMA(nsem),
                        pltpu.SemaphoreType.DMA(nsem), pltpu.SemaphoreType.DMA((n,)), pltpu.SemaphoreType.DMA((n,))],
    )(*arrays)


def _swap_half(arrays, *, name):
    n = len(arrays)

    def body(*refs):
        ins, outs = refs[:n], refs[n:2 * n]
        send_sems, recv_sems = refs[2 * n:]
        mx, my, mc = _me()
        cps = []
        for a in range(n):
            P, R, _ = arrays[a].shape
            cps.append(pltpu.make_async_remote_copy(
                src_ref=ins[a].at[pl.ds(0, P), pl.ds((1 - mc) * (R // 2), R // 2)], dst_ref=outs[a],
                send_sem=send_sems.at[a], recv_sem=recv_sems.at[a],
                device_id=(mx, my, 1 - mc), device_id_type=MESH))
        for cp in cps:
            cp.start()
        for cp in cps:
            cp.wait_recv()
        for cp in cps:
            cp.wait_send()

    return pl.pallas_call(
        body, name=name, in_specs=[ANY] * n, out_specs=[ANY] * n,
        out_shape=[jax.ShapeDtypeStruct((a.shape[0], a.shape[1] // 2, a.shape[2]), a.dtype) for a in arrays],
        scratch_shapes=[pltpu.SemaphoreType.DMA((n,)), pltpu.SemaphoreType.DMA((n,))],
    )(*arrays)


def _swap_sibling(arrays, *, name):
    n = len(arrays)

    def body(*refs):
        ins, outs = refs[:n], refs[n:2 * n]
        send_sems, recv_sems = refs[2 * n:]
        mx, my, mc = _me()
        cps = [pltpu.make_async_remote_copy(
            src_ref=ins[a], dst_ref=outs[a], send_sem=send_sems.at[a], recv_sem=recv_sems.at[a],
            device_id=(mx, my, 1 - mc), device_id_type=MESH) for a in range(n)]
        for cp in cps:
            cp.start()
        for cp in cps:
            cp.wait_recv()
        for cp in cps:
            cp.wait_send()

    return pl.pallas_call(
        body, name=name, in_specs=[ANY] * n, out_specs=[ANY] * n,
        out_shape=[jax.ShapeDtypeStruct(a.shape, a.dtype) for a in arrays],
        scratch_shapes=[pltpu.SemaphoreType.DMA((n,)), pltpu.SemaphoreType.DMA((n,))],
    )(*arrays)


def _scatter_chips(groups, *, name):
    flat = [(gi, l, a) for gi, g in enumerate(groups) for l, a in enumerate(g)]
    n = len(flat)
    ng = len(groups)

    def body(*refs):
        ins, outs = refs[:n], refs[n:n + ng]
        send_sems, recv_sems, loc_sems = refs[n + ng:]
        mx, my, mc = _me()
        me = 2 * mx + my
        locs, sends, recvs = [], [], []
        for a, (gi, l, _) in enumerate(flat):
            locs.append(pltpu.make_async_copy(ins[a].at[me], outs[gi].at[l, me], loc_sems.at[a]))
            for k in range(1, N_CHIPS):
                px, py = _flip(mx, k & 2), _flip(my, k & 1)
                peer = 2 * px + py
                sems = dict(send_sem=send_sems.at[a, k - 1], recv_sem=recv_sems.at[a, k - 1],
                            device_id=(px, py, mc), device_id_type=MESH)
                sends.append(pltpu.make_async_remote_copy(
                    src_ref=ins[a].at[peer], dst_ref=outs[gi].at[l, me], **sems))
                recvs.append(pltpu.make_async_remote_copy(
                    src_ref=ins[a].at[peer], dst_ref=outs[gi].at[l, peer], **sems))
        for cp in locs + sends:
            cp.start()
        for cp in recvs:
            cp.wait_recv()
        for cp in sends:
            cp.wait_send()
        for cp in locs:
            cp.wait()

    return pl.pallas_call(
        body, name=name, in_specs=[ANY] * n, out_specs=[ANY] * ng,
        out_shape=[jax.ShapeDtypeStruct((len(g),) + g[0].shape, g[0].dtype) for g in groups],
        scratch_shapes=[pltpu.SemaphoreType.DMA((n, N_CHIPS - 1)), pltpu.SemaphoreType.DMA((n, N_CHIPS - 1)),
                        pltpu.SemaphoreType.DMA((n,))],
    )(*[a for _, _, a in flat])


BIG = ("conv_w_pw1", "conv_w_pw2", "ret_w_in", "ret_w_out", "mlp_w1", "mlp_w2")
COLS = ("conv_w_pw1", "ret_w_in", "mlp_w1")
SMALL = ("ada_b", "norm_mix_g", "norm_mlp_g", "conv_b_pw1", "conv_w_dw", "conv_b_dw", "conv_ln_g", "conv_ln_b",
         "conv_b_pw2", "ret_gn_g", "ret_gn_b", "final_norm_g")
SMALL_SHARDED = ("conv_w_dw", "ret_gn_g", "ret_gn_b")
WEIGHTS = ("ada_w", "ada_b", "norm_mix_g", "norm_mlp_g", "conv_w_pw1", "conv_b_pw1", "conv_w_dw", "conv_b_dw",
           "conv_ln_g", "conv_ln_b", "conv_w_pw2", "conv_b_pw2", "ret_w_in", "ret_gn_g", "ret_gn_b", "ret_w_out",
           "mlp_w1", "mlp_w2", "final_norm_g")


def _vec8(rows, D):
    rows = [r.reshape(1, D).astype(F32) for r in rows]
    return jnp.concatenate(rows + [jnp.zeros((8 - len(rows), D), F32)], axis=0)


def _unshard_last(g):
    nd = g.ndim
    t = jnp.transpose(g, tuple(range(1, nd - 1)) + (0, nd - 1))
    return t.reshape(t.shape[:-2] + (t.shape[-2] * t.shape[-1],))


def _pack(parts):
    flat = jnp.concatenate([p.reshape(-1).astype(F32) for p in parts])
    pad = (-flat.shape[0]) % 1024
    return jnp.concatenate([flat, jnp.zeros((pad,), F32)]).reshape(-1, 128)


def _unpack(packed, shapes):
    flat = packed.reshape(-1)
    out, pos = [], 0
    for s in shapes:
        n = math.prod(s)
        out.append(flat[pos:pos + n].reshape(s))
        pos += n
    return out


def kernel(x, c, ada_w, ada_b, norm_mix_g, norm_mlp_g, conv_w_pw1, conv_b_pw1, conv_w_dw, conv_b_dw, conv_ln_g, conv_ln_b, conv_w_pw2, conv_b_pw2, ret_w_in, ret_gn_g, ret_gn_b, ret_w_out, mlp_w1, mlp_w2, final_norm_g, loss_target, m_ada_w, m_ada_b, m_norm_mix_g, m_norm_mlp_g, m_conv_w_pw1, m_conv_b_pw1, m_conv_w_dw, m_conv_b_dw, m_conv_ln_g, m_conv_ln_b, m_conv_w_pw2, m_conv_b_pw2, m_ret_w_in, m_ret_gn_g, m_ret_gn_b, m_ret_w_out, m_mlp_w1, m_mlp_w2, m_final_norm_g, v_ada_w, v_ada_b, v_norm_mix_g, v_norm_mlp_g, v_conv_w_pw1, v_conv_b_pw1, v_conv_w_dw, v_conv_b_dw, v_conv_ln_g, v_conv_ln_b, v_conv_w_pw2, v_conv_b_pw2, v_ret_w_in, v_ret_gn_g, v_ret_gn_b, v_ret_w_out, v_mlp_w1, v_mlp_w2, v_final_norm_g):
    W = dict(ada_w=ada_w, ada_b=ada_b, norm_mix_g=norm_mix_g, norm_mlp_g=norm_mlp_g, conv_w_pw1=conv_w_pw1,
             conv_b_pw1=conv_b_pw1, conv_w_dw=conv_w_dw, conv_b_dw=conv_b_dw, conv_ln_g=conv_ln_g,
             conv_ln_b=conv_ln_b, conv_w_pw2=conv_w_pw2, conv_b_pw2=conv_b_pw2, ret_w_in=ret_w_in,
             ret_gn_g=ret_gn_g, ret_gn_b=ret_gn_b, ret_w_out=ret_w_out, mlp_w1=mlp_w1, mlp_w2=mlp_w2,
             final_norm_g=final_norm_g)
    Mo = dict(ada_w=m_ada_w, ada_b=m_ada_b, norm_mix_g=m_norm_mix_g, norm_mlp_g=m_norm_mlp_g,
              conv_w_pw1=m_conv_w_pw1, conv_b_pw1=m_conv_b_pw1, conv_w_dw=m_conv_w_dw, conv_b_dw=m_conv_b_dw,
              conv_ln_g=m_conv_ln_g, conv_ln_b=m_conv_ln_b, conv_w_pw2=m_conv_w_pw2, conv_b_pw2=m_conv_b_pw2,
              ret_w_in=m_ret_w_in, ret_gn_g=m_ret_gn_g, ret_gn_b=m_ret_gn_b, ret_w_out=m_ret_w_out,
              mlp_w1=m_mlp_w1, mlp_w2=m_mlp_w2, final_norm_g=m_final_norm_g)
    Vo = dict(ada_w=v_ada_w, ada_b=v_ada_b, norm_mix_g=v_norm_mix_g, norm_mlp_g=v_norm_mlp_g,
              conv_w_pw1=v_conv_w_pw1, conv_b_pw1=v_conv_b_pw1, conv_w_dw=v_conv_w_dw, conv_b_dw=v_conv_b_dw,
              conv_ln_g=v_conv_ln_g, conv_ln_b=v_conv_ln_b, conv_w_pw2=v_conv_w_pw2, conv_b_pw2=v_conv_b_pw2,
              ret_w_in=v_ret_w_in, ret_gn_g=v_ret_gn_g, ret_gn_b=v_ret_gn_b, ret_w_out=v_ret_w_out,
              mlp_w1=v_mlp_w1, mlp_w2=v_mlp_w2, final_norm_g=v_final_norm_g)

    S, D = x.shape[1], x.shape[2]
    depth = ada_w.shape[0]
    H = RET_HEADS
    dv = 2 * D // H
    xs = x.reshape(S, D)
    target = loss_target.reshape(S, D)
    mx, my, mc = _me()
    chip = 2 * mx + my
    dev = 4 * mx + 2 * my + mc

    send = [W[nm][l].astype(BF16) for nm in BIG for l in range(W[nm].shape[0])]
    send += [W[nm] for nm in SMALL_SHARDED]
    got = _gather_chips(send, name="gather_weights")
    Wg, pos = {}, 0
    for nm in BIG:
        L = W[nm].shape[0]
        Wg[nm] = got[pos:pos + L]
        pos += L
    full_small = {nm: _unshard_last(got[pos + i]) for i, nm in enumerate(SMALL_SHARDED)}

    def wfull(nm, l):
        g = Wg[nm][l]
        return g.reshape(g.shape[0] * g.shape[1], g.shape[2])

    c_all = _allgather8(c.reshape(8, D // 8), name="gather_c").reshape(N_DEV, D)
    cs_ada = ada_w.shape[2]
    bias_sh = lax.dynamic_slice_in_dim(ada_b.reshape(depth, N_CHIPS, cs_ada), chip, 1, axis=1)
    mod_sh = _mm(c_all, ada_w, mode="nn", name="ada_fwd", b3d=True, tn=cs_ada, a_silu=True,
                 bias=bias_sh.reshape(1, depth * cs_ada))
    mod_all = _allgather8(mod_sh, name="gather_mod")[0::2]
    mod_me = lax.dynamic_slice_in_dim(mod_all, dev, 1, axis=1).reshape(N_CHIPS, depth, cs_ada)
    mod = jnp.transpose(mod_me, (1, 0, 2)).reshape(depth, 6, D)

    pos_ids = jnp.arange(S, dtype=F32)
    dk = D // H
    inv = ROPE_BASE ** (-jnp.arange(0, dk, 2, dtype=F32) / dk)
    ang = pos_ids[:, None] * inv[None, :]
    cos_t, sin_t = jnp.cos(ang), jnp.sin(ang)
    tables = _ret_tables(H)
    gn_g_full = full_small["ret_gn_g"].reshape(-1, H, 1, dv)
    gn_b_full = full_small["ret_gn_b"].reshape(-1, H, 1, dv)
    wdw_full = full_small["conv_w_dw"]

    def wdw_pad(j):
        return jnp.concatenate([wdw_full[j], jnp.zeros((CONV_HALO - CONV_WIDTH, D), F32)], axis=0)

    saved = []
    xa, y_prev, gate_prev = xs, None, None
    for l in range(depth):
        j = l // 2
        sv = {}
        vec_a = _vec8([gate_prev if gate_prev is not None else jnp.zeros((D,), F32), norm_mix_g[l], mod[l, 0],
                       mod[l, 1]], D)
        xa, h = _row_fwd(xa, y_prev, vec_a, name="row_fwd" if y_prev is not None else "row_fwd_first")
        sv.update(xa=xa, h=h, vec_a=vec_a)
        if l % 2 == 0:
            u = _mm(h, Wg["conv_w_pw1"][j], mode="nn", name="pw1_fwd", b3d=True,
                    tn=Wg["conv_w_pw1"][j].shape[2], bias=conv_b_pw1[j].reshape(1, -1))
            cvec = _vec8([conv_b_dw[j], conv_ln_g[j], conv_ln_b[j]], D)
            v_glu, cv, z = _conv_fwd(u, wdw_pad(j), cvec, name="conv_fwd")
            ymix = _mm(z, wfull("conv_w_pw2", j), mode="nn", name="pw2_fwd", bias=conv_b_pw2[j].reshape(1, -1))
            sv.update(u=u, v_glu=v_glu, cv=cv, z=z, cvec=cvec)
        else:
            proj = _mm(h, Wg["ret_w_in"][j], mode="nn", name="win_fwd", b3d=True, out_dtype=BF16,
                       tn=Wg["ret_w_in"][j].shape[2])
            yr, y2, states = _ret_fwd(proj, cos_t, sin_t, tables, gn_g_full[j], gn_b_full[j], name="ret_fwd")
            ymix = _mm(y2, wfull("ret_w_out", j), mode="nn", name="wout_fwd")
            sv.update(proj=proj, yr=yr, y2=y2, states=states)
        vec_b = _vec8([mod[l, 2], norm_mlp_g[l], mod[l, 3], mod[l, 4]], D)
        xb, h2 = _row_fwd(xa, ymix, vec_b, name="row_fwd")
        ra, p = _mm(h2, Wg["mlp_w1"][l], mode="nn", name="w1_fwd", b3d=True, tn=Wg["mlp_w1"][l].shape[2],
                    epi="relu2")
        mo = _mm(p, wfull("mlp_w2", l), mode="nn", name="w2_fwd")
        sv.update(ymix=ymix, xb=xb, h2=h2, ra=ra, p=p, mo=mo, vec_b=vec_b)
        saved.append(sv)
        xa, y_prev, gate_prev = xb, mo, mod[l, 5]

    fvec = _vec8([gate_prev, final_norm_g], D)
    dx, dyb, fpart = _final(xa, y_prev, target, fvec, name="final")
    loss = lax.psum(jnp.sum(fpart[2]), ("x", "y", "c"))
    G = {nm: [None] * W[nm].shape[0] for nm in BIG}
    dmod = [[None] * 6 for _ in range(depth)]
    dmod[depth - 1][5] = fpart[0]
    sg = dict(norm_mix_g=[None] * depth, norm_mlp_g=[None] * depth, final_norm_g=fpart[1])
    n_conv, n_ret = conv_w_pw1.shape[0], ret_w_in.shape[0]
    for nm in ("conv_b_pw1", "conv_w_dw", "conv_b_dw", "conv_ln_g", "conv_ln_b", "conv_b_pw2"):
        sg[nm] = [None] * n_conv
    for nm in ("ret_gn_g", "ret_gn_b"):
        sg[nm] = [None] * n_ret

    for l in reversed(range(depth)):
        j = l // 2
        sv = saved[l]
        w1, w2 = Wg["mlp_w1"][l], wfull("mlp_w2", l)
        cs1 = w1.shape[2]
        da = _mm(dyb, w2, mode="nt", name="w2_dx", out_dtype=BF16, epi="mul2", extra=sv["ra"])
        gw2 = _mm(sv["p"], dyb, mode="tn", name="w2_dw")
        G["mlp_w2"][l] = gw2.reshape(N_CHIPS, gw2.shape[0] // N_CHIPS, gw2.shape[1])
        G["mlp_w1"][l] = _mm(sv["h2"], da, mode="tn", name="w1_dw", out3d=(N_CHIPS, cs1), tn=cs1)
        dh2 = _mm(da, w1, mode="nt", name="w1_dx", b3d=True, tk=cs1)
        dx, dyb, part = _row_bwd(sv["xb"], dh2, dx, sv["ymix"], sv["vec_b"], name="row_bwd")
        dmod[l][2], sg["norm_mlp_g"][l], dmod[l][3], dmod[l][4] = part[0], part[1], part[2], part[3]
        if l % 2 == 0:
            sg["conv_b_pw2"][j] = part[4]
            wp1, wp2 = Wg["conv_w_pw1"][j], wfull("conv_w_pw2", j)
            csp = wp1.shape[2]
            dz = _mm(dyb, wp2, mode="nt", name="pw2_dx")
            gp2 = _mm(sv["z"], dyb, mode="tn", name="pw2_dw")
            G["conv_w_pw2"][j] = gp2.reshape(N_CHIPS, gp2.shape[0] // N_CHIPS, gp2.shape[1])
            du, dwdw, cpart, dbu = _conv_bwd(dz, sv["cv"], sv["v_glu"], sv["u"], wdw_pad(j), sv["cvec"],
                                             name="conv_bwd")
            sg["conv_w_dw"][j] = dwdw[:CONV_WIDTH]
            sg["conv_b_dw"][j], sg["conv_ln_g"][j], sg["conv_ln_b"][j] = cpart[0], cpart[1], cpart[2]
            sg["conv_b_pw1"][j] = dbu[0]
            G["conv_w_pw1"][j] = _mm(sv["h"], du, mode="tn", name="pw1_dw", out3d=(N_CHIPS, csp), tn=csp)
            dh = _mm(du, wp1, mode="nt", name="pw1_dx", b3d=True, tk=csp)
        else:
            wi, wo = Wg["ret_w_in"][j], wfull("ret_w_out", j)
            csi = wi.shape[2]
            dy2 = _mm(dyb, wo, mode="nt", name="wout_dx")
            gwo = _mm(sv["y2"], dyb, mode="tn", name="wout_dw")
            G["ret_w_out"][j] = gwo.reshape(N_CHIPS, gwo.shape[0] // N_CHIPS, gwo.shape[1])
            dq, dkk, dvv, dgt, dgg, dgb = _ret_bwd(sv["proj"], cos_t, sin_t, tables, gn_g_full[j], gn_b_full[j],
                                                   sv["yr"], dy2, sv["states"], name="ret_bwd")
            sg["ret_gn_g"][j], sg["ret_gn_b"][j] = dgg.reshape(H, dv), dgb.reshape(H, dv)
            dproj = jnp.concatenate([dq, dkk, dvv, dgt], axis=1)
            G["ret_w_in"][j] = _mm(sv["h"], dproj, mode="tn", name="win_dw", out3d=(N_CHIPS, csi), tn=csi)
            dh = _mm(dproj, wi, mode="nt", name="win_dx", b3d=True, tk=csi)
        yp = saved[l - 1]["mo"] if l > 0 else dh
        dx, dyb, part = _row_bwd(sv["xa"], dh, dx, yp, sv["vec_a"], name="row_bwd")
        sg["norm_mix_g"][l], dmod[l][0], dmod[l][1] = part[1], part[2], part[3]
        if l > 0:
            dmod[l - 1][5] = part[0]
    grad_x = dx.reshape(x.shape)

    dmod_me = jnp.stack([jnp.stack(r) for r in dmod]).reshape(depth, 6 * D)
    sgrads = dict(ada_b=dmod_me)
    for nm in SMALL[1:]:
        sgrads[nm] = sg[nm] if nm == "final_norm_g" else jnp.stack(sg[nm])
    full_shapes = [sgrads[nm].shape for nm in SMALL]
    packed_all = _allgather8(_pack([sgrads[nm] for nm in SMALL]), name="gather_small_grads")
    sums = _unpack(_sum_slots(packed_all[None], name="sum_small_grads"), full_shapes)
    gsm = {}
    for nm, g in zip(SMALL, sums):
        if nm in SMALL_SHARDED:
            n = g.shape[-1] // N_CHIPS
            g = lax.dynamic_slice_in_dim(g.reshape(g.shape[:-1] + (N_CHIPS, n)), chip, 1, axis=g.ndim - 1)
            g = g.reshape(g.shape[:-2] + (n,))
        gsm[nm] = g.reshape(W[nm].shape)
    pw, pm, pv, pg = (_pack([t[nm] for nm in SMALL]) for t in (W, Mo, Vo, gsm))
    e4 = lambda a: a.reshape((1, 1) + a.shape)
    e3 = lambda a: a.reshape((1,) + a.shape)
    sres = _adamw(e4(pg), e3(pw), e3(pm), e3(pv), name="adamw_small")
    shard_shapes = [W[nm].shape for nm in SMALL]
    small_out = [dict(zip(SMALL, _unpack(r, shard_shapes))) for r in sres]

    n_mod_rows = depth * 6 * D // 128
    dmod_all = packed_all[:, :n_mod_rows].reshape(N_DEV, depth, N_CHIPS, cs_ada)
    dmod_cols = lax.dynamic_slice_in_dim(dmod_all, chip, 1, axis=2).reshape(N_DEV, depth * cs_ada)
    kpad = 128 - N_DEV
    dmod_pad = jnp.concatenate([dmod_cols, jnp.zeros((kpad, depth * cs_ada), F32)], axis=0)
    ct_pad = jnp.concatenate([c_all.T, jnp.zeros((D, kpad), F32)], axis=1)
    g_ada = _mm(ct_pad, dmod_pad, mode="nn", name="ada_dw", a_silu=True, out3d=(depth, cs_ada), tn=cs_ada)
    ada_out = _adamw(g_ada.reshape(depth, 1, D, cs_ada), ada_w, m_ada_w, v_ada_w, name="adamw_ada")

    flat_g = [G[nm][l] for nm in BIG for l in range(len(G[nm]))]
    sib = _swap_half(flat_g, name="swap_grads")
    half_idx = mc.astype(jnp.int32).reshape(1)
    chip_sum = [_add_half(a, b, half_idx, name="add_grads") for a, b in zip(flat_g, sib)]
    groups, pos = [], 0
    for nm in BIG:
        L = len(G[nm])
        groups.append(chip_sum[pos:pos + L])
        pos += L
    slots = _scatter_chips(groups, name="scatter_grads")
    halves = [_sum_slots(sl, name="sum_grads") for sl in slots]
    sib_halves = _swap_sibling(halves, name="swap_totals")
    big_out = {nm: _adamw_halves(hm, hs, half_idx, W[nm], Mo[nm], Vo[nm], name="adamw_big")
               for nm, hm, hs in zip(BIG, halves, sib_halves)}

    def res(nm, i):
        if nm == "ada_w":
            return ada_out[i]
        if nm in big_out:
            return big_out[nm][i]
        return small_out[i][nm]

    return (loss, grad_x, *[res(nm, 0) for nm in WEIGHTS], *[res(nm, 1) for nm in WEIGHTS],
            *[res(nm, 2) for nm in WEIGHTS], *[res(nm, 3) for nm in WEIGHTS])
```

```python
import functools
import math

import jax
import jax.numpy as jnp
from jax import lax
from jax.experimental import pallas as pl
from jax.experimental.pallas import tpu as pltpu

F32 = jnp.float32
BF16 = jnp.bfloat16
MESH = pl.DeviceIdType.MESH

EPS = 1e-6
CHUNK = 64
CONV_WIDTH = 31
CONV_HALO = 32
SUBLANES = 8
LANES = 128
CONV_BLOCK_ROWS = 128
RET_HEADS = 4
ROPE_BASE = 10000.0
ADAM_LR = 0.001
ADAM_B1 = 0.9
ADAM_B2 = 0.999
ADAM_EPS = 1e-08
ADAM_WD = 0.01
ADAM_STEP = 10
N_CHIPS = 4
N_DEV = 8
V7X_VMEM_LIMIT = 48 * 1024 * 1024
ANY = pl.BlockSpec(memory_space=pl.ANY)


def _cparams(sem=None):
    return pltpu.CompilerParams(dimension_semantics=sem, vmem_limit_bytes=V7X_VMEM_LIMIT)


def _sigmoid(x):
    return jax.nn.sigmoid(x)


def _silu(x):
    return x * _sigmoid(x)


_DIMS = {
    "nn": (((1,), (0,)), ((), ())),
    "nt": (((1,), (1,)), ((), ())),
    "tn": (((0,), (0,)), ((), ())),
}


def _mm(a, b, *, mode, name, out_dtype=F32, tm=1024, tn=1024, tk=1024, b3d=False, out3d=None,
        bias=None, epi=None, extra=None, a_silu=False):
    if mode == "tn":
        K, M = a.shape
    else:
        M, K = a.shape
    if b3d:
        P, R, Cs = b.shape
        bshape = (R, P * Cs)
    else:
        bshape = b.shape
    N = bshape[0] if mode == "nt" else bshape[1]
    assert (bshape[1] if mode == "nt" else bshape[0]) == K, (name, a.shape, b.shape)
    tm, tn, tk = min(tm, M), min(tn, N), min(tk, K)
    assert M % tm == 0 and N % tn == 0 and K % tk == 0, (name, M, N, K, tm, tn, tk)
    nk = K // tk

    if mode == "tn":
        a_spec = pl.BlockSpec((tk, tm), lambda i, j, k: (k, i))
    else:
        a_spec = pl.BlockSpec((tm, tk), lambda i, j, k: (i, k))
    if mode == "nt":
        if b3d:
            nb = Cs // tk
            assert Cs % tk == 0
            b_spec = pl.BlockSpec((None, tn, tk), lambda i, j, k: (k // nb, j, k % nb))
        else:
            b_spec = pl.BlockSpec((tn, tk), lambda i, j, k: (j, k))
    else:
        if b3d:
            nb = Cs // tn
            assert Cs % tn == 0
            b_spec = pl.BlockSpec((None, tk, tn), lambda i, j, k: (j // nb, k, j % nb))
        else:
            b_spec = pl.BlockSpec((tk, tn), lambda i, j, k: (k, j))
    in_specs = [a_spec, b_spec]
    args = [a, b]
    if bias is not None:
        in_specs.append(pl.BlockSpec((1, tn), lambda i, j, k: (0, j)))
        args.append(bias)
    if extra is not None:
        in_specs.append(pl.BlockSpec((tm, tn), lambda i, j, k: (i, j)))
        args.append(extra)

    if out3d is not None:
        P_o, Cs_o = out3d
        assert P_o * Cs_o == N and Cs_o % tn == 0
        nbo = Cs_o // tn
        o_spec = pl.BlockSpec((None, tm, tn), lambda i, j, k: (j // nbo, i, j % nbo))
        o_shape = (P_o, M, Cs_o)
    else:
        o_spec = pl.BlockSpec((tm, tn), lambda i, j, k: (i, j))
        o_shape = (M, N)
    if epi == "relu2":
        out_shape = [jax.ShapeDtypeStruct(o_shape, BF16), jax.ShapeDtypeStruct(o_shape, BF16)]
        out_specs = [o_spec, o_spec]
    else:
        out_shape = jax.ShapeDtypeStruct(o_shape, out_dtype)
        out_specs = o_spec
    n_out = 2 if epi == "relu2" else 1
    dims = _DIMS[mode]
    has_bias, has_extra = bias is not None, extra is not None

    def body(*refs):
        a_ref, b_ref = refs[0], refs[1]
        pos = 2
        bias_ref = extra_ref = None
        if has_bias:
            bias_ref = refs[pos]
            pos += 1
        if has_extra:
            extra_ref = refs[pos]
            pos += 1
        outs = refs[pos:pos + n_out]
        acc_ref = refs[pos + n_out] if nk > 1 else None

        def partial():
            av = a_ref[...]
            if a_silu:
                av = _silu(av)
            return lax.dot_general(av, b_ref[...], dims, preferred_element_type=F32)

        def finish(r):
            if has_bias:
                r = r + bias_ref[...]
            if epi == "relu2":
                rr = jnp.maximum(r, 0.0)
                outs[0][...] = rr.astype(BF16)
                outs[1][...] = (rr * rr).astype(BF16)
            elif epi == "mul2":
                outs[0][...] = (r * 2.0 * extra_ref[...].astype(F32)).astype(outs[0].dtype)
            else:
                outs[0][...] = r.astype(outs[0].dtype)

        if nk == 1:
            finish(partial())
        else:
            k = pl.program_id(2)

            @pl.when(k == 0)
            def _():
                acc_ref[...] = jnp.zeros_like(acc_ref)

            acc_ref[...] += partial()

            @pl.when(k == nk - 1)
            def _():
                finish(acc_ref[...])

    return pl.pallas_call(
        body, name=name, grid=(M // tm, N // tn, nk), in_specs=in_specs, out_specs=out_specs,
        out_shape=out_shape,
        scratch_shapes=[pltpu.VMEM((tm, tn), F32)] if nk > 1 else [],
        compiler_params=_cparams(("parallel", "parallel", "arbitrary")),
    )(*args)


def _modnorm(x, gain, shift, scale):
    y = x * lax.rsqrt(jnp.mean(x * x, axis=-1, keepdims=True) + EPS)
    return (y * gain) * (1.0 + scale) + shift


def _row_fwd(xprev, y, vec, *, name, ts=512):
    S, D = xprev.shape
    ts = min(ts, S)
    has_res = y is not None
    row = pl.BlockSpec((ts, D), lambda i: (i, 0))
    vspec = pl.BlockSpec((8, D), lambda i: (0, 0))

    def body(*refs):
        if has_res:
            xp_ref, y_ref, v_ref, x_ref, h_ref = refs
            x = xp_ref[...] + v_ref[0:1, :] * y_ref[...]
            x_ref[...] = x
        else:
            xp_ref, v_ref, h_ref = refs
            x = xp_ref[...]
        h_ref[...] = _modnorm(x, v_ref[1:2, :], v_ref[2:3, :], v_ref[3:4, :]).astype(BF16)

    if has_res:
        return pl.pallas_call(
            body, name=name, grid=(S // ts,), in_specs=[row, row, vspec], out_specs=[row, row],
            out_shape=[jax.ShapeDtypeStruct((S, D), F32), jax.ShapeDtypeStruct((S, D), BF16)],
            compiler_params=_cparams(("parallel",)),
        )(xprev, y, vec)
    h = pl.pallas_call(
        body, name=name, grid=(S // ts,), in_specs=[row, vspec], out_specs=row,
        out_shape=jax.ShapeDtypeStruct((S, D), BF16),
        compiler_params=_cparams(("parallel",)),
    )(xprev, vec)
    return xprev, h


def _row_bwd(xin, dh, dxout, yprev, vec, *, name, ts=512):
    S, D = xin.shape
    ts = min(ts, S)
    row = pl.BlockSpec((ts, D), lambda i: (i, 0))
    vspec = pl.BlockSpec((8, D), lambda i: (0, 0))

    def body(x_ref, dh_ref, dx_ref, y_ref, v_ref, dxin_ref, dy_ref, part_ref):
        @pl.when(pl.program_id(0) == 0)
        def _():
            part_ref[...] = jnp.zeros_like(part_ref)

        gate = v_ref[0:1, :]
        _, vjp = jax.vjp(_modnorm, x_ref[...], v_ref[1:2, :], v_ref[2:3, :], v_ref[3:4, :])
        dxn, dgain, dshift, dscale = vjp(dh_ref[...])
        dxin = dx_ref[...] + dxn
        dxin_ref[...] = dxin
        dy = dxin * gate
        dy_ref[...] = dy.astype(BF16)
        part_ref[0:1, :] += jnp.sum(dxin * y_ref[...], axis=0, keepdims=True)
        part_ref[1:2, :] += dgain
        part_ref[2:3, :] += dshift
        part_ref[3:4, :] += dscale
        part_ref[4:5, :] += jnp.sum(dy, axis=0, keepdims=True)

    return pl.pallas_call(
        body, name=name, grid=(S // ts,), in_specs=[row, row, row, row, vspec],
        out_specs=[row, row, vspec],
        out_shape=[jax.ShapeDtypeStruct((S, D), F32), jax.ShapeDtypeStruct((S, D), BF16),
                   jax.ShapeDtypeStruct((8, D), F32)],
        compiler_params=_cparams(("arbitrary",)),
    )(xin, dh, dxout, yprev, vec)


def _final(xprev, y, target, vec, *, name, ts=512):
    S, D = xprev.shape
    ts = min(ts, S)
    row = pl.BlockSpec((ts, D), lambda i: (i, 0))
    vspec = pl.BlockSpec((8, D), lambda i: (0, 0))

    def norm(x, gain):
        return x * lax.rsqrt(jnp.mean(x * x, axis=-1, keepdims=True) + EPS) * gain

    def body(xp_ref, y_ref, t_ref, v_ref, dx_ref, dy_ref, part_ref):
        @pl.when(pl.program_id(0) == 0)
        def _():
            part_ref[...] = jnp.zeros_like(part_ref)

        gate = v_ref[0:1, :]
        yv = y_ref[...]
        x = xp_ref[...] + gate * yv
        out, vjp = jax.vjp(norm, x, v_ref[1:2, :])
        err = out - t_ref[...]
        dx, dgain = vjp(err * (1.0 / D))
        dx_ref[...] = dx
        dy_ref[...] = (dx * gate).astype(BF16)
        part_ref[0:1, :] += jnp.sum(dx * yv, axis=0, keepdims=True)
        part_ref[1:2, :] += dgain
        part_ref[2:3, :] += jnp.sum(err * err, axis=0, keepdims=True) * (0.5 / D)

    return pl.pallas_call(
        body, name=name, grid=(S // ts,), in_specs=[row, row, row, vspec], out_specs=[row, row, vspec],
        out_shape=[jax.ShapeDtypeStruct((S, D), F32), jax.ShapeDtypeStruct((S, D), BF16),
                   jax.ShapeDtypeStruct((8, D), F32)],
        compiler_params=_cparams(("arbitrary",)),
    )(xprev, y, target, vec)


def _ln_silu(cv, g, b):
    mu = jnp.mean(cv, axis=-1, keepdims=True)
    var = jnp.mean(jnp.square(cv - mu), axis=-1, keepdims=True)
    u = (cv - mu) * lax.rsqrt(var + EPS) * g + b
    return _silu(u)


def _shift_copies(ext, sh, n):
    for b in range(1, SUBLANES):
        sh[b - 1, 0:n, :] = ext[pl.ds(b, n), :]


def _shifted(ext, sh, off, r0, rows, cols):
    a, b = divmod(off, SUBLANES)
    if b == 0:
        return ext[pl.ds(SUBLANES * a + r0, rows), cols]
    return sh[b - 1, pl.ds(SUBLANES * a + r0, rows), cols]


def _conv_fwd(u, wdw, vec, *, name, ts=256):
    S, D2 = u.shape
    D = D2 // 2
    ts = min(ts, S)
    H = CONV_HALO
    row = pl.BlockSpec((ts, D), lambda i: (i, 0))

    rb_rows = min(CONV_BLOCK_ROWS, ts)

    def body(u_ref, w_ref, v_ref, vo_ref, cv_ref, z_ref, ext, sh):
        @pl.when(pl.program_id(0) == 0)
        def _():
            ext[0:H, :] = jnp.zeros((H, D), F32)

        uu = u_ref[...]
        v = uu[:, :D] * _sigmoid(uu[:, D:])
        vo_ref[...] = v
        ext[H:H + ts, :] = v
        _shift_copies(ext, sh, ts + H - 8)
        for r0 in range(0, ts, rb_rows):
            for c0 in range(0, D, LANES):
                cols = pl.ds(c0, LANES)
                acc = jnp.zeros((rb_rows, LANES), F32)
                for t in range(CONV_WIDTH):
                    src = _shifted(ext, sh, H - (CONV_WIDTH - 1) + t, r0, rb_rows, cols)
                    acc = acc + src * w_ref[pl.ds(t, 1), cols]
                cv_ref[pl.ds(r0, rb_rows), cols] = acc + v_ref[0:1, cols]
        z_ref[...] = _ln_silu(cv_ref[...], v_ref[1:2, :], v_ref[2:3, :]).astype(BF16)
        ext[0:H, :] = ext[ts:ts + H, :]

    return pl.pallas_call(
        body, name=name, grid=(S // ts,),
        in_specs=[pl.BlockSpec((ts, D2), lambda i: (i, 0)), pl.BlockSpec((H, D), lambda i: (0, 0)),
                  pl.BlockSpec((8, D), lambda i: (0, 0))],
        out_specs=[row, row, row],
        out_shape=[jax.ShapeDtypeStruct((S, D), F32), jax.ShapeDtypeStruct((S, D), F32),
                   jax.ShapeDtypeStruct((S, D), BF16)],
        scratch_shapes=[pltpu.VMEM((ts + H, D), F32), pltpu.VMEM((7, ts + H - 8, D), F32)],
        compiler_params=_cparams(("arbitrary",)),
    )(u, wdw, vec)


def _conv_bwd(dz, cv, v, u, wdw, vec, *, name, ts=256):
    S, D = cv.shape
    ts = min(ts, S)
    H = CONV_HALO
    nt = S // ts
    per = ts // H
    rev = lambda i: (nt - 1 - i, 0)
    row = pl.BlockSpec((ts, D), rev)

    rb_rows = min(CONV_BLOCK_ROWS, ts)
    nsh = ts + H - 8

    def body(dz_ref, cv_ref, v_ref, vh_ref, u_ref, w_ref, vec_ref, du_ref, dw_ref, part_ref, dbu_ref,
             dext, vext, dsh, vsh, dwacc, dvbuf):
        i = pl.program_id(0)

        @pl.when(i == 0)
        def _():
            dext[ts:ts + H, :] = jnp.zeros((H, D), F32)
            dwacc[...] = jnp.zeros_like(dwacc)
            part_ref[...] = jnp.zeros_like(part_ref)
            dbu_ref[...] = jnp.zeros_like(dbu_ref)

        _, vjp = jax.vjp(_ln_silu, cv_ref[...], vec_ref[1:2, :], vec_ref[2:3, :])
        dcv, dg, db = vjp(dz_ref[...])
        part_ref[0:1, :] += jnp.sum(dcv, axis=0, keepdims=True)
        part_ref[1:2, :] += dg
        part_ref[2:3, :] += db
        dext[0:ts, :] = dcv
        vext[0:H, :] = vh_ref[...] * jnp.where(i == nt - 1, 0.0, 1.0)
        vext[H:H + ts, :] = v_ref[...]
        _shift_copies(dext, dsh, nsh)
        _shift_copies(vext, vsh, nsh)
        for r0 in range(0, ts, rb_rows):
            for c0 in range(0, D, LANES):
                cols = pl.ds(c0, LANES)
                dblk = dext[pl.ds(r0, rb_rows), cols]
                dv = jnp.zeros((rb_rows, LANES), F32)
                for t in range(CONV_WIDTH):
                    prod = dblk * _shifted(vext, vsh, H - (CONV_WIDTH - 1) + t, r0, rb_rows, cols)
                    parts = [prod[s:s + SUBLANES, :] for s in range(0, rb_rows, SUBLANES)]
                    while len(parts) > 1:
                        parts = [parts[k] + parts[k + 1] for k in range(0, len(parts), 2)]
                    dwacc[pl.ds(t * SUBLANES, SUBLANES), cols] += parts[0]
                    dv = dv + _shifted(dext, dsh, CONV_WIDTH - 1 - t, r0, rb_rows, cols) * w_ref[pl.ds(t, 1), cols]
                dvbuf[pl.ds(r0, rb_rows), cols] = dv
        dv = dvbuf[...]
        uu = u_ref[...]
        a, g = uu[:, :D], uu[:, D:]
        sg = _sigmoid(g)
        da = dv * sg
        dg_ = dv * a * sg * (1.0 - sg)
        du = jnp.concatenate([da, dg_], axis=-1)
        du_ref[...] = du.astype(BF16)
        dbu_ref[0:1, :] += jnp.sum(du, axis=0, keepdims=True)
        dext[ts:ts + H, :] = dext[0:H, :]

        @pl.when(i == nt - 1)
        def _():
            dw_ref[...] = jnp.zeros_like(dw_ref)
            for t in range(CONV_WIDTH):
                dw_ref[pl.ds(t, 1), :] = jnp.sum(dwacc[pl.ds(t * SUBLANES, SUBLANES), :], axis=0, keepdims=True)

    return pl.pallas_call(
        body, name=name, grid=(nt,),
        in_specs=[row, row, row,
                  pl.BlockSpec((H, D), lambda i: (jnp.maximum((nt - 1 - i) * per - 1, 0), 0)),
                  pl.BlockSpec((ts, 2 * D), rev), pl.BlockSpec((H, D), lambda i: (0, 0)),
                  pl.BlockSpec((8, D), lambda i: (0, 0))],
        out_specs=[pl.BlockSpec((ts, 2 * D), rev), pl.BlockSpec((H, D), lambda i: (0, 0)),
                   pl.BlockSpec((8, D), lambda i: (0, 0)), pl.BlockSpec((8, 2 * D), lambda i: (0, 0))],
        out_shape=[jax.ShapeDtypeStruct((S, 2 * D), BF16), jax.ShapeDtypeStruct((H, D), F32),
                   jax.ShapeDtypeStruct((8, D), F32), jax.ShapeDtypeStruct((8, 2 * D), F32)],
        scratch_shapes=[pltpu.VMEM((ts + H, D), F32), pltpu.VMEM((ts + H, D), F32),
                        pltpu.VMEM((7, nsh, D), F32), pltpu.VMEM((7, nsh, D), F32),
                        pltpu.VMEM((CONV_WIDTH * SUBLANES, D), F32), pltpu.VMEM((ts, D), F32)],
        compiler_params=_cparams(("arbitrary",)),
    )(dz, cv, v, v, u, wdw, vec)


def _rope(x, c, s, half):
    x1, x2 = x[:, :half], x[:, half:]
    return jnp.concatenate([x1 * c - x2 * s, x2 * c + x1 * s], axis=-1)


def _rope_t(d, c, s, half):
    d1, d2 = d[:, :half], d[:, half:]
    return jnp.concatenate([d1 * c + d2 * s, d2 * c - d1 * s], axis=-1)


def _gn_gate(y, gate, g, b):
    mu = jnp.mean(y, axis=-1, keepdims=True)
    var = jnp.mean(jnp.square(y - mu), axis=-1, keepdims=True)
    return _silu(gate) * ((y - mu) * lax.rsqrt(var + EPS) * g + b)


def _dot(a, b, mode="nn"):
    return lax.dot_general(a, b, _DIMS[mode], preferred_element_type=F32)


def _ret_tables(H):
    lg = jnp.log(1.0 - 2.0 ** (-5.0 - jnp.arange(H, dtype=F32)))
    idx = jnp.arange(CHUNK, dtype=F32)
    dmat = jnp.exp(lg[:, None, None] * jnp.abs(idx[:, None] - idx[None, :]))
    xi = jnp.exp(lg[:, None] * (idx + 1.0))[..., None]
    zeta = jnp.exp(lg[:, None] * (CHUNK - 1.0 - idx))[..., None]
    dec = jnp.exp(lg * CHUNK)[:, None, None]
    return dmat, xi, zeta, dec


def _ret_specs(R, dk, dv, half, order):
    H = RET_HEADS
    C = CHUNK
    nq = H
    return dict(
        q=pl.BlockSpec((R, dk), lambda h, n: (order(n), h)),
        k=pl.BlockSpec((R, dk), lambda h, n: (order(n), nq + h)),
        v=pl.BlockSpec((R, dv), lambda h, n: (order(n), nq + h)),
        gate=pl.BlockSpec((R, dv), lambda h, n: (order(n), 2 * nq + h)),
        rope=pl.BlockSpec((R, half), lambda h, n: (order(n), 0)),
        dmat=pl.BlockSpec((None, C, C), lambda h, n: (h, 0, 0)),
        col=pl.BlockSpec((None, C, 1), lambda h, n: (h, 0, 0)),
        one=pl.BlockSpec((None, 1, 1), lambda h, n: (h, 0, 0)),
        gn=pl.BlockSpec((None, 1, dv), lambda h, n: (h, 0, 0)),
        yv=pl.BlockSpec((R, dv), lambda h, n: (order(n), h)),
        yk=pl.BlockSpec((R, dk), lambda h, n: (order(n), h)),
    )


def _ret_fwd(proj, cos, sin, tables, gn_g, gn_b, *, name, cps=4):
    S = proj.shape[0]
    D = proj.shape[1] // 6
    H, C = RET_HEADS, CHUNK
    dk, dv, half = D // H, 2 * D // H, D // H // 2
    nc = S // C
    cps = min(cps, nc)
    R = cps * C
    scale = dk ** -0.5
    sp = _ret_specs(R, dk, dv, half, lambda n: n)
    dmat, xi, zeta, dec = tables

    def body(q_ref, k_ref, v_ref, g_ref, cos_ref, sin_ref, dm_ref, xi_ref, ze_ref, dec_ref, gg_ref, gb_ref,
             y_ref, y2_ref, st_ref, state):
        @pl.when(pl.program_id(1) == 0)
        def _():
            state[...] = jnp.zeros_like(state)

        dm, xv, zv, dc = dm_ref[...], xi_ref[...], ze_ref[...], dec_ref[...]
        for j in range(cps):
            rows = pl.ds(j * C, C)
            cs, sn = cos_ref[rows, :], sin_ref[rows, :]
            qr = _rope(q_ref[rows, :].astype(F32), cs, sn, half)
            kr = _rope(k_ref[rows, :].astype(F32), cs, sn, half) * scale
            vb = v_ref[rows, :]
            p = (_dot(qr.astype(BF16), kr.astype(BF16), "nt") * dm).astype(BF16)
            st = state[...]
            stb = st.astype(BF16)
            st_ref[j] = stb
            y = _dot(p, vb) + _dot((qr * xv).astype(BF16), stb)
            state[...] = st * dc + _dot((kr * zv).astype(BF16), vb, "tn")
            y_ref[rows, :] = y
            y2_ref[rows, :] = _gn_gate(y, g_ref[rows, :].astype(F32), gg_ref[...], gb_ref[...]).astype(BF16)

    return pl.pallas_call(
        body, name=name, grid=(H, nc // cps),
        in_specs=[sp["q"], sp["k"], sp["v"], sp["gate"], sp["rope"], sp["rope"], sp["dmat"], sp["col"],
                  sp["col"], sp["one"], sp["gn"], sp["gn"]],
        out_specs=[sp["yv"], sp["yv"], pl.BlockSpec((None, cps, dk, dv), lambda h, n: (h, n, 0, 0))],
        out_shape=[jax.ShapeDtypeStruct((S, 2 * D), F32), jax.ShapeDtypeStruct((S, 2 * D), BF16),
                   jax.ShapeDtypeStruct((H, nc, dk, dv), BF16)],
        scratch_shapes=[pltpu.VMEM((dk, dv), F32)],
        compiler_params=_cparams(("arbitrary", "arbitrary")),
    )(proj, proj, proj, proj, cos, sin, dmat, xi, zeta, dec, gn_g, gn_b)


def _ret_bwd(proj, cos, sin, tables, gn_g, gn_b, y, dy2, states, *, name, cps=4):
    S = proj.shape[0]
    D = proj.shape[1] // 6
    H, C = RET_HEADS, CHUNK
    dk, dv, half = D // H, 2 * D // H, D // H // 2
    nc = S // C
    cps = min(cps, nc)
    ns = nc // cps
    R = cps * C
    scale = dk ** -0.5
    order = lambda n: ns - 1 - n
    sp = _ret_specs(R, dk, dv, half, order)
    dmat, xi, zeta, dec = tables

    def body(q_ref, k_ref, v_ref, g_ref, cos_ref, sin_ref, dm_ref, xi_ref, ze_ref, dec_ref, gg_ref, gb_ref,
             y_ref, dy2_ref, st_ref, dq_ref, dk_ref, dv_ref, dg_ref, dgg_ref, dgb_ref, gst):
        @pl.when(pl.program_id(1) == 0)
        def _():
            gst[...] = jnp.zeros_like(gst)
            dgg_ref[...] = jnp.zeros_like(dgg_ref)
            dgb_ref[...] = jnp.zeros_like(dgb_ref)

        dm, xv, zv, dc = dm_ref[...], xi_ref[...], ze_ref[...], dec_ref[...]
        for j in reversed(range(cps)):
            rows = pl.ds(j * C, C)
            cs, sn = cos_ref[rows, :], sin_ref[rows, :]
            _, vjp = jax.vjp(_gn_gate, y_ref[rows, :], g_ref[rows, :].astype(F32), gg_ref[...], gb_ref[...])
            dy, dgate, dgg, dgb = vjp(dy2_ref[rows, :])
            dgg_ref[...] += dgg
            dgb_ref[...] += dgb
            dg_ref[rows, :] = dgate.astype(BF16)
            dyb = dy.astype(BF16)
            qr = _rope(q_ref[rows, :].astype(F32), cs, sn, half)
            kr = _rope(k_ref[rows, :].astype(F32), cs, sn, half) * scale
            qb, kb, vb = qr.astype(BF16), kr.astype(BF16), v_ref[rows, :]
            p = (_dot(qb, kb, "nt") * dm).astype(BF16)
            g = gst[...]
            gb16 = g.astype(BF16)
            sprev = st_ref[j]
            dvv = _dot(p, dyb, "tn") + _dot((kr * zv).astype(BF16), gb16)
            dpb = (_dot(dyb, vb, "nt") * dm).astype(BF16)
            dqr = _dot(dpb, kb) + _dot(dyb, sprev, "nt") * xv
            dkr = _dot(dpb, qb, "tn") + _dot(vb, gb16, "nt") * zv
            gst[...] = g * dc + _dot((qr * xv).astype(BF16), dyb, "tn")
            dq_ref[rows, :] = _rope_t(dqr, cs, sn, half).astype(BF16)
            dk_ref[rows, :] = _rope_t(dkr * scale, cs, sn, half).astype(BF16)
            dv_ref[rows, :] = dvv.astype(BF16)

    return pl.pallas_call(
        body, name=name, grid=(H, ns),
        in_specs=[sp["q"], sp["k"], sp["v"], sp["gate"], sp["rope"], sp["rope"], sp["dmat"], sp["col"],
                  sp["col"], sp["one"], sp["gn"], sp["gn"], sp["yv"], sp["yv"],
                  pl.BlockSpec((None, cps, dk, dv), lambda h, n: (h, order(n), 0, 0))],
        out_specs=[sp["yk"], sp["yk"], sp["yv"], sp["yv"], sp["gn"], sp["gn"]],
        out_shape=[jax.ShapeDtypeStruct((S, D), BF16), jax.ShapeDtypeStruct((S, D), BF16),
                   jax.ShapeDtypeStruct((S, 2 * D), BF16), jax.ShapeDtypeStruct((S, 2 * D), BF16),
                   jax.ShapeDtypeStruct((H, 1, dv), F32), jax.ShapeDtypeStruct((H, 1, dv), F32)],
        scratch_shapes=[pltpu.VMEM((dk, dv), F32)],
        compiler_params=_cparams(("arbitrary", "arbitrary")),
    )(proj, proj, proj, proj, cos, sin, dmat, xi, zeta, dec, gn_g, gn_b, y, dy2, states)


def _rows_tile(rows, cols, n_arrays):
    cap = max(8, V7X_VMEM_LIMIT // 3 // (n_arrays * 2 * 4 * cols))
    t = rows
    while t > cap and t % 2 == 0:
        t //= 2
    return t


def _add_half(g, r, half_idx, *, name):
    P, R, Cc = g.shape
    hR = R // 2
    tr = _rows_tile(hR, Cc, 3)
    nb = hR // tr

    def body(h_ref, g_ref, r_ref, o_ref):
        o_ref[...] = (g_ref[...] + r_ref[...]).astype(BF16)

    return pl.pallas_call(
        body, name=name,
        grid_spec=pltpu.PrefetchScalarGridSpec(
            num_scalar_prefetch=1, grid=(P, nb),
            in_specs=[pl.BlockSpec((None, tr, Cc), lambda s, i, h: (s, h[0] * nb + i, 0)),
                      pl.BlockSpec((None, tr, Cc), lambda s, i, h: (s, i, 0))],
            out_specs=pl.BlockSpec((None, tr, Cc), lambda s, i, h: (s, i, 0))),
        out_shape=jax.ShapeDtypeStruct((P, hR, Cc), BF16), compiler_params=_cparams(("parallel", "parallel")),
    )(half_idx, g, r)


def _sum_slots(x, *, name):
    L, NS, R, Cc = x.shape
    tr = _rows_tile(R, Cc, NS + 1)

    def body(x_ref, o_ref):
        acc = x_ref[0].astype(F32)
        for s in range(1, NS):
            acc = acc + x_ref[s].astype(F32)
        o_ref[...] = acc

    return pl.pallas_call(
        body, name=name, grid=(L, R // tr),
        in_specs=[pl.BlockSpec((None, NS, tr, Cc), lambda l, i: (l, 0, i, 0))],
        out_specs=pl.BlockSpec((None, tr, Cc), lambda l, i: (l, i, 0)),
        out_shape=jax.ShapeDtypeStruct((L, R, Cc), F32), compiler_params=_cparams(("parallel", "parallel")),
    )(x)


def _adam_store(g, w_ref, m_ref, v_ref, go_ref, d_ref, mo_ref, vo_ref):
    mn = ADAM_B1 * m_ref[...] + (1.0 - ADAM_B1) * g
    vn = ADAM_B2 * v_ref[...] + (1.0 - ADAM_B2) * jnp.square(g)
    m_hat = mn / (1.0 - ADAM_B1 ** ADAM_STEP)
    v_hat = vn / (1.0 - ADAM_B2 ** ADAM_STEP)
    go_ref[...] = g
    d_ref[...] = -ADAM_LR * (m_hat / (jnp.sqrt(v_hat) + ADAM_EPS) + ADAM_WD * w_ref[...])
    mo_ref[...] = mn
    vo_ref[...] = vn


def _adamw(gslots, w, m, v, *, name):
    L, NS, R, Cc = gslots.shape
    tr = _rows_tile(R, Cc, NS + 7)
    gspec = pl.BlockSpec((None, NS, tr, Cc), lambda l, i: (l, 0, i, 0))
    spec = pl.BlockSpec((None, tr, Cc), lambda l, i: (l, i, 0))

    def body(g_ref, *refs):
        g = g_ref[0]
        for s in range(1, NS):
            g = g + g_ref[s]
        _adam_store(g, *refs)

    sd = jax.ShapeDtypeStruct((L, R, Cc), F32)
    return pl.pallas_call(
        body, name=name, grid=(L, R // tr), in_specs=[gspec, spec, spec, spec],
        out_specs=[spec, spec, spec, spec], out_shape=[sd, sd, sd, sd],
        compiler_params=_cparams(("parallel", "parallel")),
    )(gslots, w, m, v)


def _adamw_halves(g_mine, g_sib, half_idx, w, m, v, *, name):
    L, hR, Cc = g_mine.shape
    tr = _rows_tile(hR, Cc, 9)
    nbh = hR // tr
    gspec = pl.BlockSpec((None, tr, Cc), lambda l, i, h: (l, i % nbh, 0))
    spec = pl.BlockSpec((None, tr, Cc), lambda l, i, h: (l, i, 0))

    def body(h_ref, gm_ref, gs_ref, *refs):
        mine = (pl.program_id(1) // nbh) == h_ref[0]
        _adam_store(jnp.where(mine, gm_ref[...], gs_ref[...]), *refs)

    sd = jax.ShapeDtypeStruct((L, 2 * hR, Cc), F32)
    return pl.pallas_call(
        body, name=name,
        grid_spec=pltpu.PrefetchScalarGridSpec(
            num_scalar_prefetch=1, grid=(L, 2 * nbh), in_specs=[gspec, gspec, spec, spec, spec],
            out_specs=[spec, spec, spec, spec]),
        out_shape=[sd, sd, sd, sd], compiler_params=_cparams(("parallel", "parallel")),
    )(half_idx, g_mine, g_sib, w, m, v)


def _me():
    return lax.axis_index("x"), lax.axis_index("y"), lax.axis_index("c")


def _flip(v, bit):
    return 1 - v if bit else v


def _allgather8(x, *, name):
    def body(x_ref, out_ref, send_sems, recv_sems, loc_sem):
        mx, my, mc = _me()
        me = 4 * mx + 2 * my + mc
        loc = pltpu.make_async_copy(x_ref, out_ref.at[me], loc_sem)
        loc.start()
        sends, recvs = [], []
        for k in range(1, N_DEV):
            px, py, pc = _flip(mx, k & 4), _flip(my, k & 2), _flip(mc, k & 1)
            sends.append(pltpu.make_async_remote_copy(
                src_ref=x_ref, dst_ref=out_ref.at[me], send_sem=send_sems.at[k - 1],
                recv_sem=recv_sems.at[k - 1], device_id=(px, py, pc), device_id_type=MESH))
            recvs.append(pltpu.make_async_remote_copy(
                src_ref=x_ref, dst_ref=out_ref.at[4 * px + 2 * py + pc], send_sem=send_sems.at[k - 1],
                recv_sem=recv_sems.at[k - 1], device_id=(px, py, pc), device_id_type=MESH))
        for cp in sends:
            cp.start()
        for cp in recvs:
            cp.wait_recv()
        for cp in sends:
            cp.wait_send()
        loc.wait()

    return pl.pallas_call(
        body, name=name, in_specs=[ANY], out_specs=ANY,
        out_shape=jax.ShapeDtypeStruct((N_DEV,) + x.shape, x.dtype),
        scratch_shapes=[pltpu.SemaphoreType.DMA((N_DEV - 1,)), pltpu.SemaphoreType.DMA((N_DEV - 1,)),
                        pltpu.SemaphoreType.DMA],
    )(x)


def _gather_chips(arrays, *, name):
    n = len(arrays)

    def body(*refs):
        ins, outs = refs[:n], refs[n:2 * n]
        ici_send, ici_recv, d2d_send, d2d_recv, own_send, own_recv = refs[2 * n:]
        mx, my, mc = _me()
        me = 2 * mx + my
        sib = (mx, my, 1 - mc)
        locs, sends, lands, passes, gifts = [], [], [], [], []
        for a in range(n):
            h = arrays[a].shape[0] // 2
            mine, other = pl.ds(mc * h, h), pl.ds((1 - mc) * h, h)
            locs.append(pltpu.make_async_remote_copy(
                src_ref=ins[a], dst_ref=outs[a].at[me], send_sem=own_send.at[a], recv_sem=own_recv.at[a],
                device_id=sib, device_id_type=MESH))
            for k in range(1, N_CHIPS):
                px, py = _flip(mx, k & 2), _flip(my, k & 1)
                peer = 2 * px + py
                ici = dict(send_sem=ici_send.at[a, k - 1], recv_sem=ici_recv.at[a, k - 1],
                           device_id=(px, py, mc), device_id_type=MESH)
                d2d = dict(send_sem=d2d_send.at[a, k - 1], recv_sem=d2d_recv.at[a, k - 1],
                           device_id=sib, device_id_type=MESH)
                sends.append(pltpu.make_async_remote_copy(
                    src_ref=ins[a].at[mine], dst_ref=outs[a].at[me, mine], **ici))
                lands.append(pltpu.make_async_remote_copy(
                    src_ref=ins[a].at[mine], dst_ref=outs[a].at[peer, mine], **ici))
                passes.append(pltpu.make_async_remote_copy(
                    src_ref=outs[a].at[peer, mine], dst_ref=outs[a].at[peer, mine], **d2d))
                gifts.append(pltpu.make_async_remote_copy(
                    src_ref=outs[a].at[peer, other], dst_ref=outs[a].at[peer, other], **d2d))
        for cp in locs + sends:
            cp.start()
        for land, fwd in zip(lands, passes):
            land.wait_recv()
            fwd.start()
        for cp in gifts + locs:
            cp.wait_recv()
        for cp in sends + passes + locs:
            cp.wait_send()

    nsem = (n, N_CHIPS - 1)
    return pl.pallas_call(
        body, name=name, in_specs=[ANY] * n, out_specs=[ANY] * n,
        out_shape=[jax.ShapeDtypeStruct((N_CHIPS,) + a.shape, a.dtype) for a in arrays],
        scratch_shapes=[pltpu.SemaphoreType.DMA(nsem), pltpu.SemaphoreType.DMA(nsem), pltpu.SemaphoreType.DMA(nsem),
                        pltpu.SemaphoreType.DMA(nsem), pltpu.SemaphoreType.DMA((n,)), pltpu.SemaphoreType.DMA((n,))],
    )(*arrays)


def _swap_half(arrays, *, name):
    n = len(arrays)

    def body(*refs):
        ins, outs = refs[:n], refs[n:2 * n]
        send_sems, recv_sems = refs[2 * n:]
        mx, my, mc = _me()
        cps = []
        for a in range(n):
            P, R, _ = arrays[a].shape
            cps.append(pltpu.make_async_remote_copy(
                src_ref=ins[a].at[pl.ds(0, P), pl.ds((1 - mc) * (R // 2), R // 2)], dst_ref=outs[a],
                send_sem=send_sems.at[a], recv_sem=recv_sems.at[a],
                device_id=(mx, my, 1 - mc), device_id_type=MESH))
        for cp in cps:
            cp.start()
        for cp in cps:
            cp.wait_recv()
        for cp in cps:
            cp.wait_send()

    return pl.pallas_call(
        body, name=name, in_specs=[ANY] * n, out_specs=[ANY] * n,
        out_shape=[jax.ShapeDtypeStruct((a.shape[0], a.shape[1] // 2, a.shape[2]), a.dtype) for a in arrays],
        scratch_shapes=[pltpu.SemaphoreType.DMA((n,)), pltpu.SemaphoreType.DMA((n,))],
    )(*arrays)


def _swap_sibling(arrays, *, name):
    n = len(arrays)

    def body(*refs):
        ins, outs = refs[:n], refs[n:2 * n]
        send_sems, recv_sems = refs[2 * n:]
        mx, my, mc = _me()
        cps = [pltpu.make_async_remote_copy(
            src_ref=ins[a], dst_ref=outs[a], send_sem=send_sems.at[a], recv_sem=recv_sems.at[a],
            device_id=(mx, my, 1 - mc), device_id_type=MESH) for a in range(n)]
        for cp in cps:
            cp.start()
        for cp in cps:
            cp.wait_recv()
        for cp in cps:
            cp.wait_send()

    return pl.pallas_call(
        body, name=name, in_specs=[ANY] * n, out_specs=[ANY] * n,
        out_shape=[jax.ShapeDtypeStruct(a.shape, a.dtype) for a in arrays],
        scratch_shapes=[pltpu.SemaphoreType.DMA((n,)), pltpu.SemaphoreType.DMA((n,))],
    )(*arrays)


def _plan_scatter(srcs, lands):
    mx, my, mc = _me()
    copies = []
    for a in range(len(srcs)):
        for k in range(1, N_CHIPS):
            px, py = _flip(mx, k & 2), _flip(my, k & 1)
            copies.append((srcs[a].at[2 * px + py], lands[a].at[k - 1], lands[a].at[k - 1], (px, py, mc)))
    return copies


def _plan_gather(srcs, lands):
    mx, my, mc = _me()
    me = 2 * mx + my
    copies = []
    for a in range(len(srcs)):
        copies.append((srcs[a], lands[a].at[me], lands[a].at[me], (mx, my, 1 - mc)))
        for k in range(1, N_CHIPS):
            px, py = _flip(mx, k & 2), _flip(my, k & 1)
            copies.append((srcs[a], lands[a].at[me], lands[a].at[2 * px + py], (px, py, mc)))
    return copies


def _copy(c, k, send_sems, recv_sems, landing=False):
    src, dst, land, dev = c
    return pltpu.make_async_remote_copy(src_ref=src, dst_ref=land if landing else dst, send_sem=send_sems.at[k],
                                        recv_sem=recv_sems.at[k], device_id=dev, device_id_type=MESH)


def _exchange(srcs, land_shapes, plan, ncopies, *, name):
    ni, nl = len(srcs), len(land_shapes)

    def body(*refs):
        send_sems, recv_sems = refs[ni + nl:]
        copies = plan(refs[:ni], refs[ni:ni + nl])
        for k, c in enumerate(copies):
            _copy(c, k, send_sems, recv_sems).start()
        for k, c in enumerate(copies):
            _copy(c, k, send_sems, recv_sems, landing=True).wait_recv()
        for k, c in enumerate(copies):
            _copy(c, k, send_sems, recv_sems).wait_send()

    return pl.pallas_call(
        body, name=name, in_specs=[ANY] * ni, out_specs=[ANY] * nl, out_shape=list(land_shapes),
        scratch_shapes=[pltpu.SemaphoreType.DMA((ncopies,)), pltpu.SemaphoreType.DMA((ncopies,))],
    )(*srcs)


HBM_SPEC = pl.BlockSpec(memory_space=pltpu.HBM)
SEM_SPEC = pl.BlockSpec(memory_space=pltpu.SEMAPHORE)
SPLIT_EFFECT = pltpu.SideEffectType.DATAFLOW_SIDE_EFFECTING


def _exchange_start(srcs, land_shapes, plan, ncopies, after, *, name):
    ni, nl = len(srcs), len(land_shapes)

    def body(*refs):
        in_refs, land_refs = refs[:ni], refs[ni:ni + nl]
        send_sems, recv_sems = refs[ni + nl + 1], refs[ni + nl + 2]
        token = refs[-1]
        for k, c in enumerate(plan(in_refs, land_refs)):
            _copy(c, k, send_sems, recv_sems).start()
        token[...] = jnp.zeros_like(token)

    bufs = [pltpu.with_memory_space_constraint(a, pltpu.HBM) for a in srcs]
    bufs += [pltpu.with_memory_space_constraint(lax.empty(s.shape, s.dtype), pltpu.HBM) for s in land_shapes]
    outs = pl.pallas_call(
        body, name=name,
        in_specs=[HBM_SPEC] * (ni + nl) + [ANY],
        out_specs=(SEM_SPEC, SEM_SPEC, *[HBM_SPEC] * (ni + nl), pl.BlockSpec(memory_space=pltpu.VMEM)),
        out_shape=(pltpu.SemaphoreType.DMA((ncopies,)), pltpu.SemaphoreType.DMA((ncopies,)),
                   *[pltpu.HBM(b.shape, b.dtype) for b in bufs], jax.ShapeDtypeStruct((8, 128), F32)),
        input_output_aliases={i: 2 + i for i in range(ni + nl)},
        compiler_params=pltpu.CompilerParams(has_side_effects=SPLIT_EFFECT),
    )(*bufs, after)
    return outs[:-1], outs[-1]


def _exchange_wait(started, ni, plan, after, *, name):
    send_sems, recv_sems = started[0], started[1]
    bufs = list(started[2:])
    nb = len(bufs)

    def body(*refs):
        in_refs, land_refs = refs[:ni], refs[ni:nb]
        send, recv = refs[nb], refs[nb + 1]
        for k, c in enumerate(plan(in_refs, land_refs)):
            cp = _copy(c, k, send, recv, landing=True)
            cp.wait_send()
            cp.wait_recv()

    outs = pl.pallas_call(
        body, name=name, in_specs=[HBM_SPEC] * nb + [SEM_SPEC, SEM_SPEC, ANY], out_specs=[HBM_SPEC] * nb,
        out_shape=[pltpu.HBM(b.shape, b.dtype) for b in bufs],
        input_output_aliases={i: i for i in range(nb)},
        compiler_params=pltpu.CompilerParams(has_side_effects=SPLIT_EFFECT),
    )(*bufs, send_sems, recv_sems, after)
    return list(outs[:ni]), list(outs[ni:])


def _sum_own(own, recv, chip_idx, *, name):
    _, R, Cc = own.shape
    tr = _rows_tile(R, Cc, 5)

    def body(s_ref, o_ref, r_ref, t_ref):
        acc = o_ref[...].astype(F32)
        for s in range(N_CHIPS - 1):
            acc = acc + r_ref[s].astype(F32)
        t_ref[...] = acc

    return pl.pallas_call(
        body, name=name,
        grid_spec=pltpu.PrefetchScalarGridSpec(
            num_scalar_prefetch=1, grid=(R // tr,),
            in_specs=[pl.BlockSpec((None, tr, Cc), lambda i, s: (s[0], i, 0)),
                      pl.BlockSpec((N_CHIPS - 1, tr, Cc), lambda i, s: (0, i, 0))],
            out_specs=pl.BlockSpec((tr, Cc), lambda i, s: (i, 0))),
        out_shape=jax.ShapeDtypeStruct((R, Cc), F32), compiler_params=_cparams(("parallel",)),
    )(chip_idx, own, recv)


BIG = ("conv_w_pw1", "conv_w_pw2", "ret_w_in", "ret_w_out", "mlp_w1", "mlp_w2")
COLS = ("conv_w_pw1", "ret_w_in", "mlp_w1")
SMALL = ("ada_b", "norm_mix_g", "norm_mlp_g", "conv_b_pw1", "conv_w_dw", "conv_b_dw", "conv_ln_g", "conv_ln_b",
         "conv_b_pw2", "ret_gn_g", "ret_gn_b", "final_norm_g")
SMALL_SHARDED = ("conv_w_dw", "ret_gn_g", "ret_gn_b")
WEIGHTS = ("ada_w", "ada_b", "norm_mix_g", "norm_mlp_g", "conv_w_pw1", "conv_b_pw1", "conv_w_dw", "conv_b_dw",
           "conv_ln_g", "conv_ln_b", "conv_w_pw2", "conv_b_pw2", "ret_w_in", "ret_gn_g", "ret_gn_b", "ret_w_out",
           "mlp_w1", "mlp_w2", "final_norm_g")


def _vec8(rows, D):
    rows = [r.reshape(1, D).astype(F32) for r in rows]
    return jnp.concatenate(rows + [jnp.zeros((8 - len(rows), D), F32)], axis=0)


def _unshard_last(g):
    nd = g.ndim
    t = jnp.transpose(g, tuple(range(1, nd - 1)) + (0, nd - 1))
    return t.reshape(t.shape[:-2] + (t.shape[-2] * t.shape[-1],))


def _pack(parts):
    flat = jnp.concatenate([p.reshape(-1).astype(F32) for p in parts])
    pad = (-flat.shape[0]) % 1024
    return jnp.concatenate([flat, jnp.zeros((pad,), F32)]).reshape(-1, 128)


def _unpack(packed, shapes):
    flat = packed.reshape(-1)
    out, pos = [], 0
    for s in shapes:
        n = math.prod(s)
        out.append(flat[pos:pos + n].reshape(s))
        pos += n
    return out


def kernel(x, c, ada_w, ada_b, norm_mix_g, norm_mlp_g, conv_w_pw1, conv_b_pw1, conv_w_dw, conv_b_dw, conv_ln_g, conv_ln_b, conv_w_pw2, conv_b_pw2, ret_w_in, ret_gn_g, ret_gn_b, ret_w_out, mlp_w1, mlp_w2, final_norm_g, loss_target, m_ada_w, m_ada_b, m_norm_mix_g, m_norm_mlp_g, m_conv_w_pw1, m_conv_b_pw1, m_conv_w_dw, m_conv_b_dw, m_conv_ln_g, m_conv_ln_b, m_conv_w_pw2, m_conv_b_pw2, m_ret_w_in, m_ret_gn_g, m_ret_gn_b, m_ret_w_out, m_mlp_w1, m_mlp_w2, m_final_norm_g, v_ada_w, v_ada_b, v_norm_mix_g, v_norm_mlp_g, v_conv_w_pw1, v_conv_b_pw1, v_conv_w_dw, v_conv_b_dw, v_conv_ln_g, v_conv_ln_b, v_conv_w_pw2, v_conv_b_pw2, v_ret_w_in, v_ret_gn_g, v_ret_gn_b, v_ret_w_out, v_mlp_w1, v_mlp_w2, v_final_norm_g):
    W = dict(ada_w=ada_w, ada_b=ada_b, norm_mix_g=norm_mix_g, norm_mlp_g=norm_mlp_g, conv_w_pw1=conv_w_pw1,
             conv_b_pw1=conv_b_pw1, conv_w_dw=conv_w_dw, conv_b_dw=conv_b_dw, conv_ln_g=conv_ln_g,
             conv_ln_b=conv_ln_b, conv_w_pw2=conv_w_pw2, conv_b_pw2=conv_b_pw2, ret_w_in=ret_w_in,
             ret_gn_g=ret_gn_g, ret_gn_b=ret_gn_b, ret_w_out=ret_w_out, mlp_w1=mlp_w1, mlp_w2=mlp_w2,
             final_norm_g=final_norm_g)
    Mo = dict(ada_w=m_ada_w, ada_b=m_ada_b, norm_mix_g=m_norm_mix_g, norm_mlp_g=m_norm_mlp_g,
              conv_w_pw1=m_conv_w_pw1, conv_b_pw1=m_conv_b_pw1, conv_w_dw=m_conv_w_dw, conv_b_dw=m_conv_b_dw,
              conv_ln_g=m_conv_ln_g, conv_ln_b=m_conv_ln_b, conv_w_pw2=m_conv_w_pw2, conv_b_pw2=m_conv_b_pw2,
              ret_w_in=m_ret_w_in, ret_gn_g=m_ret_gn_g, ret_gn_b=m_ret_gn_b, ret_w_out=m_ret_w_out,
              mlp_w1=m_mlp_w1, mlp_w2=m_mlp_w2, final_norm_g=m_final_norm_g)
    Vo = dict(ada_w=v_ada_w, ada_b=v_ada_b, norm_mix_g=v_norm_mix_g, norm_mlp_g=v_norm_mlp_g,
              conv_w_pw1=v_conv_w_pw1, conv_b_pw1=v_conv_b_pw1, conv_w_dw=v_conv_w_dw, conv_b_dw=v_conv_b_dw,
              conv_ln_g=v_conv_ln_g, conv_ln_b=v_conv_ln_b, conv_w_pw2=v_conv_w_pw2, conv_b_pw2=v_conv_b_pw2,
              ret_w_in=v_ret_w_in, ret_gn_g=v_ret_gn_g, ret_gn_b=v_ret_gn_b, ret_w_out=v_ret_w_out,
              mlp_w1=v_mlp_w1, mlp_w2=v_mlp_w2, final_norm_g=v_final_norm_g)

    S, D = x.shape[1], x.shape[2]
    depth = ada_w.shape[0]
    H = RET_HEADS
    dv = 2 * D // H
    xs = x.reshape(S, D)
    target = loss_target.reshape(S, D)
    mx, my, mc = _me()
    chip = 2 * mx + my
    dev = 4 * mx + 2 * my + mc

    def layer_weights(l):
        mixer = ("conv_w_pw1", "conv_w_pw2") if l % 2 == 0 else ("ret_w_in", "ret_w_out")
        return [(nm, l // 2) for nm in mixer] + [("mlp_w1", l), ("mlp_w2", l)]

    keys0 = layer_weights(0)
    got0 = _gather_chips([W[nm][i].astype(BF16) for nm, i in keys0] + [W[nm] for nm in SMALL_SHARDED],
                         name="gather_weights")
    Wg = dict(zip(keys0, got0))
    full_small = {nm: _unshard_last(got0[len(keys0) + i]) for i, nm in enumerate(SMALL_SHARDED)}
    pending, order = {}, got0[0]
    for l in range(1, depth):
        keys = layer_weights(l)
        srcs = [W[nm][i].astype(BF16) for nm, i in keys]
        shapes = [jax.ShapeDtypeStruct((N_CHIPS,) + s.shape, BF16) for s in srcs]
        started, order = _exchange_start(srcs, shapes, _plan_gather, 4 * len(srcs), order, name=f"gather_start_{l}")
        pending[l] = (keys, started)

    def wfull(nm, l):
        g = Wg[nm, l]
        return g.reshape(g.shape[0] * g.shape[1], g.shape[2])

    c = c + order[0, 0]
    c_all = _allgather8(c.reshape(8, D // 8), name="gather_c").reshape(N_DEV, D)
    cs_ada = ada_w.shape[2]
    bias_sh = lax.dynamic_slice_in_dim(ada_b.reshape(depth, N_CHIPS, cs_ada), chip, 1, axis=1)
    mod_sh = _mm(c_all, ada_w, mode="nn", name="ada_fwd", b3d=True, tn=cs_ada, a_silu=True,
                 bias=bias_sh.reshape(1, depth * cs_ada))
    mod_all = _allgather8(mod_sh, name="gather_mod")[0::2]
    mod_me = lax.dynamic_slice_in_dim(mod_all, dev, 1, axis=1).reshape(N_CHIPS, depth, cs_ada)
    mod = jnp.transpose(mod_me, (1, 0, 2)).reshape(depth, 6, D)

    pos_ids = jnp.arange(S, dtype=F32)
    dk = D // H
    inv = ROPE_BASE ** (-jnp.arange(0, dk, 2, dtype=F32) / dk)
    ang = pos_ids[:, None] * inv[None, :]
    cos_t, sin_t = jnp.cos(ang), jnp.sin(ang)
    tables = _ret_tables(H)
    gn_g_full = full_small["ret_gn_g"].reshape(-1, H, 1, dv)
    gn_b_full = full_small["ret_gn_b"].reshape(-1, H, 1, dv)
    wdw_full = full_small["conv_w_dw"]

    def wdw_pad(j):
        return jnp.concatenate([wdw_full[j], jnp.zeros((CONV_HALO - CONV_WIDTH, D), F32)], axis=0)

    saved = []
    xa, y_prev, gate_prev = xs, None, None
    for l in range(depth):
        j = l // 2
        sv = {}
        if l in pending:
            keys, started = pending.pop(l)
            _, lands = _exchange_wait(started, len(keys), _plan_gather, y_prev, name=f"gather_wait_{l}")
            Wg.update(zip(keys, lands))
        vec_a = _vec8([gate_prev if gate_prev is not None else jnp.zeros((D,), F32), norm_mix_g[l], mod[l, 0],
                       mod[l, 1]], D)
        xa, h = _row_fwd(xa, y_prev, vec_a, name="row_fwd" if y_prev is not None else "row_fwd_first")
        sv.update(xa=xa, h=h, vec_a=vec_a)
        if l % 2 == 0:
            u = _mm(h, Wg["conv_w_pw1", j], mode="nn", name="pw1_fwd", b3d=True,
                    tn=Wg["conv_w_pw1", j].shape[2], bias=conv_b_pw1[j].reshape(1, -1))
            cvec = _vec8([conv_b_dw[j], conv_ln_g[j], conv_ln_b[j]], D)
            v_glu, cv, z = _conv_fwd(u, wdw_pad(j), cvec, name="conv_fwd")
            ymix = _mm(z, wfull("conv_w_pw2", j), mode="nn", name="pw2_fwd", bias=conv_b_pw2[j].reshape(1, -1))
            sv.update(u=u, v_glu=v_glu, cv=cv, z=z, cvec=cvec)
        else:
            proj = _mm(h, Wg["ret_w_in", j], mode="nn", name="win_fwd", b3d=True, out_dtype=BF16,
                       tn=Wg["ret_w_in", j].shape[2])
            yr, y2, states = _ret_fwd(proj, cos_t, sin_t, tables, gn_g_full[j], gn_b_full[j], name="ret_fwd")
            ymix = _mm(y2, wfull("ret_w_out", j), mode="nn", name="wout_fwd")
            sv.update(proj=proj, yr=yr, y2=y2, states=states)
        vec_b = _vec8([mod[l, 2], norm_mlp_g[l], mod[l, 3], mod[l, 4]], D)
        xb, h2 = _row_fwd(xa, ymix, vec_b, name="row_fwd")
        ra, p = _mm(h2, Wg["mlp_w1", l], mode="nn", name="w1_fwd", b3d=True, tn=Wg["mlp_w1", l].shape[2],
                    epi="relu2")
        mo = _mm(p, wfull("mlp_w2", l), mode="nn", name="w2_fwd")
        sv.update(ymix=ymix, xb=xb, h2=h2, ra=ra, p=p, mo=mo, vec_b=vec_b)
        saved.append(sv)
        xa, y_prev, gate_prev = xb, mo, mod[l, 5]

    fvec = _vec8([gate_prev, final_norm_g], D)
    dx, dyb, fpart = _final(xa, y_prev, target, fvec, name="final")
    loss = lax.psum(jnp.sum(fpart[2]), ("x", "y", "c"))
    G = {nm: [None] * W[nm].shape[0] for nm in BIG}
    dmod = [[None] * 6 for _ in range(depth)]
    dmod[depth - 1][5] = fpart[0]
    sg = dict(norm_mix_g=[None] * depth, norm_mlp_g=[None] * depth, final_norm_g=fpart[1])
    n_conv, n_ret = conv_w_pw1.shape[0], ret_w_in.shape[0]
    for nm in ("conv_b_pw1", "conv_w_dw", "conv_b_dw", "conv_ln_g", "conv_ln_b", "conv_b_pw2"):
        sg[nm] = [None] * n_conv
    for nm in ("ret_gn_g", "ret_gn_b"):
        sg[nm] = [None] * n_ret

    half_idx = mc.astype(jnp.int32).reshape(1)
    chip_idx = chip.astype(jnp.int32).reshape(1)

    def chip_sums(keys, tag):
        flat = [G[nm][i] for nm, i in keys]
        sib = _swap_half(flat, name="swap_grads_" + tag)
        sums = [_add_half(a, b, half_idx, name="add_grads") for a, b in zip(flat, sib)]
        shapes = [jax.ShapeDtypeStruct((N_CHIPS - 1,) + s.shape[1:], BF16) for s in sums]
        return sums, shapes

    late_layers = list(range(depth // 2, depth))
    early_layers = list(range(depth // 2))
    late_keys = [k for l in late_layers for k in layer_weights(l)]
    early_keys = [k for l in early_layers for k in layer_weights(l)]
    late_started = None

    for l in reversed(range(depth)):
        j = l // 2
        sv = saved[l]
        if l == depth // 2 - 1:
            sums, shapes = chip_sums(late_keys, "late")
            late_started, tok = _exchange_start(sums, shapes, _plan_scatter, 3 * len(sums), fpart,
                                                name="scatter_start")
            sv["vec_b"] = sv["vec_b"] + tok[:, :1]
        w1, w2 = Wg["mlp_w1", l], wfull("mlp_w2", l)
        cs1 = w1.shape[2]
        da = _mm(dyb, w2, mode="nt", name="w2_dx", out_dtype=BF16, epi="mul2", extra=sv["ra"])
        gw2 = _mm(sv["p"], dyb, mode="tn", name="w2_dw")
        G["mlp_w2"][l] = gw2.reshape(N_CHIPS, gw2.shape[0] // N_CHIPS, gw2.shape[1])
        G["mlp_w1"][l] = _mm(sv["h2"], da, mode="tn", name="w1_dw", out3d=(N_CHIPS, cs1), tn=cs1)
        dh2 = _mm(da, w1, mode="nt", name="w1_dx", b3d=True, tk=cs1)
        dx, dyb, part = _row_bwd(sv["xb"], dh2, dx, sv["ymix"], sv["vec_b"], name="row_bwd")
        dmod[l][2], sg["norm_mlp_g"][l], dmod[l][3], dmod[l][4] = part[0], part[1], part[2], part[3]
        if l % 2 == 0:
            sg["conv_b_pw2"][j] = part[4]
            wp1, wp2 = Wg["conv_w_pw1", j], wfull("conv_w_pw2", j)
            csp = wp1.shape[2]
            dz = _mm(dyb, wp2, mode="nt", name="pw2_dx")
            gp2 = _mm(sv["z"], dyb, mode="tn", name="pw2_dw")
            G["conv_w_pw2"][j] = gp2.reshape(N_CHIPS, gp2.shape[0] // N_CHIPS, gp2.shape[1])
            du, dwdw, cpart, dbu = _conv_bwd(dz, sv["cv"], sv["v_glu"], sv["u"], wdw_pad(j), sv["cvec"],
                                             name="conv_bwd")
            sg["conv_w_dw"][j] = dwdw[:CONV_WIDTH]
            sg["conv_b_dw"][j], sg["conv_ln_g"][j], sg["conv_ln_b"][j] = cpart[0], cpart[1], cpart[2]
            sg["conv_b_pw1"][j] = dbu[0]
            G["conv_w_pw1"][j] = _mm(sv["h"], du, mode="tn", name="pw1_dw", out3d=(N_CHIPS, csp), tn=csp)
            dh = _mm(du, wp1, mode="nt", name="pw1_dx", b3d=True, tk=csp)
        else:
            wi, wo = Wg["ret_w_in", j], wfull("ret_w_out", j)
            csi = wi.shape[2]
            dy2 = _mm(dyb, wo, mode="nt", name="wout_dx")
            gwo = _mm(sv["y2"], dyb, mode="tn", name="wout_dw")
            G["ret_w_out"][j] = gwo.reshape(N_CHIPS, gwo.shape[0] // N_CHIPS, gwo.shape[1])
            dq, dkk, dvv, dgt, dgg, dgb = _ret_bwd(sv["proj"], cos_t, sin_t, tables, gn_g_full[j], gn_b_full[j],
                                                   sv["yr"], dy2, sv["states"], name="ret_bwd")
            sg["ret_gn_g"][j], sg["ret_gn_b"][j] = dgg.reshape(H, dv), dgb.reshape(H, dv)
            dproj = jnp.concatenate([dq, dkk, dvv, dgt], axis=1)
            G["ret_w_in"][j] = _mm(sv["h"], dproj, mode="tn", name="win_dw", out3d=(N_CHIPS, csi), tn=csi)
            dh = _mm(dproj, wi, mode="nt", name="win_dx", b3d=True, tk=csi)
        yp = saved[l - 1]["mo"] if l > 0 else dh
        dx, dyb, part = _row_bwd(sv["xa"], dh, dx, yp, sv["vec_a"], name="row_bwd")
        sg["norm_mix_g"][l], dmod[l][0], dmod[l][1] = part[1], part[2], part[3]
        if l > 0:
            dmod[l - 1][5] = part[0]
    grad_x = dx.reshape(x.shape)

    dmod_me = jnp.stack([jnp.stack(r) for r in dmod]).reshape(depth, 6 * D)
    sgrads = dict(ada_b=dmod_me)
    for nm in SMALL[1:]:
        sgrads[nm] = sg[nm] if nm == "final_norm_g" else jnp.stack(sg[nm])
    full_shapes = [sgrads[nm].shape for nm in SMALL]
    packed_all = _allgather8(_pack([sgrads[nm] for nm in SMALL]), name="gather_small_grads")
    sums = _unpack(_sum_slots(packed_all[None], name="sum_small_grads"), full_shapes)
    gsm = {}
    for nm, g in zip(SMALL, sums):
        if nm in SMALL_SHARDED:
            n = g.shape[-1] // N_CHIPS
            g = lax.dynamic_slice_in_dim(g.reshape(g.shape[:-1] + (N_CHIPS, n)), chip, 1, axis=g.ndim - 1)
            g = g.reshape(g.shape[:-2] + (n,))
        gsm[nm] = g.reshape(W[nm].shape)
    pw, pm, pv, pg = (_pack([t[nm] for nm in SMALL]) for t in (W, Mo, Vo, gsm))
    e4 = lambda a: a.reshape((1, 1) + a.shape)
    e3 = lambda a: a.reshape((1,) + a.shape)
    sres = _adamw(e4(pg), e3(pw), e3(pm), e3(pv), name="adamw_small")
    shard_shapes = [W[nm].shape for nm in SMALL]
    small_out = [dict(zip(SMALL, _unpack(r, shard_shapes))) for r in sres]

    n_mod_rows = depth * 6 * D // 128
    dmod_all = packed_all[:, :n_mod_rows].reshape(N_DEV, depth, N_CHIPS, cs_ada)
    dmod_cols = lax.dynamic_slice_in_dim(dmod_all, chip, 1, axis=2).reshape(N_DEV, depth * cs_ada)
    kpad = 128 - N_DEV
    dmod_pad = jnp.concatenate([dmod_cols, jnp.zeros((kpad, depth * cs_ada), F32)], axis=0)
    ct_pad = jnp.concatenate([c_all.T, jnp.zeros((D, kpad), F32)], axis=1)
    g_ada = _mm(ct_pad, dmod_pad, mode="nn", name="ada_dw", a_silu=True, out3d=(depth, cs_ada), tn=cs_ada)
    ada_out = _adamw(g_ada.reshape(depth, 1, D, cs_ada), ada_w, m_ada_w, v_ada_w, name="adamw_ada")

    sums, shapes = chip_sums(early_keys, "early")
    early_lands = _exchange(sums, shapes, _plan_scatter, 3 * len(sums), name="scatter_grads")
    late_sums, late_lands = _exchange_wait(late_started, len(late_keys), _plan_scatter, early_lands[-1],
                                           name="scatter_wait")
    total = {}
    for keys, own, lands in ((early_keys, sums, early_lands), (late_keys, late_sums, late_lands)):
        for k, o, r in zip(keys, own, lands):
            total[k] = _sum_own(o, r, chip_idx, name="sum_grads")
    halves = [jnp.stack([total[nm, i] for i in range(len(G[nm]))]) for nm in BIG]
    sib_halves = _swap_sibling(halves, name="swap_totals")
    big_out = {nm: _adamw_halves(hm, hs, half_idx, W[nm], Mo[nm], Vo[nm], name="adamw_big")
               for nm, hm, hs in zip(BIG, halves, sib_halves)}

    def res(nm, i):
        if nm == "ada_w":
            return ada_out[i]
        if nm in big_out:
            return big_out[nm][i]
        return small_out[i][nm]

    return (loss, grad_x, *[res(nm, 0) for nm in WEIGHTS], *[res(nm, 1) for nm in WEIGHTS],
            *[res(nm, 2) for nm in WEIGHTS], *[res(nm, 3) for nm in WEIGHTS])
```

```python
import functools
import math

import jax
import jax.numpy as jnp
from jax import lax
from jax.experimental import pallas as pl
from jax.experimental.pallas import tpu as pltpu

F32 = jnp.float32
BF16 = jnp.bfloat16
MESH = pl.DeviceIdType.MESH

EPS = 1e-6
CHUNK = 64
CONV_WIDTH = 31
CONV_HALO = 32
SUBLANES = 8
LANES = 128
CONV_BLOCK_ROWS = 128
RET_HEADS = 4
ROPE_BASE = 10000.0
ADAM_LR = 0.001
ADAM_B1 = 0.9
ADAM_B2 = 0.999
ADAM_EPS = 1e-08
ADAM_WD = 0.01
ADAM_STEP = 10
N_CHIPS = 4
N_DEV = 8
V7X_VMEM_LIMIT = 48 * 1024 * 1024
ANY = pl.BlockSpec(memory_space=pl.ANY)


def _cparams(sem=None):
    return pltpu.CompilerParams(dimension_semantics=sem, vmem_limit_bytes=V7X_VMEM_LIMIT)


def _sigmoid(x):
    return jax.nn.sigmoid(x)


def _silu(x):
    return x * _sigmoid(x)


_DIMS = {
    "nn": (((1,), (0,)), ((), ())),
    "nt": (((1,), (1,)), ((), ())),
    "tn": (((0,), (0,)), ((), ())),
}


def _mm(a, b, *, mode, name, out_dtype=F32, tm=1024, tn=1024, tk=1024, b3d=False, out3d=None,
        bias=None, epi=None, extra=None, a_silu=False):
    if mode == "tn":
        K, M = a.shape
    else:
        M, K = a.shape
    if b3d:
        P, R, Cs = b.shape
        bshape = (R, P * Cs)
    else:
        bshape = b.shape
    N = bshape[0] if mode == "nt" else bshape[1]
    assert (bshape[1] if mode == "nt" else bshape[0]) == K, (name, a.shape, b.shape)
    tm, tn, tk = min(tm, M), min(tn, N), min(tk, K)
    assert M % tm == 0 and N % tn == 0 and K % tk == 0, (name, M, N, K, tm, tn, tk)
    nk = K // tk

    if mode == "tn":
        a_spec = pl.BlockSpec((tk, tm), lambda i, j, k: (k, i))
    else:
        a_spec = pl.BlockSpec((tm, tk), lambda i, j, k: (i, k))
    if mode == "nt":
        if b3d:
            nb = Cs // tk
            assert Cs % tk == 0
            b_spec = pl.BlockSpec((None, tn, tk), lambda i, j, k: (k // nb, j, k % nb))
        else:
            b_spec = pl.BlockSpec((tn, tk), lambda i, j, k: (j, k))
    else:
        if b3d:
            nb = Cs // tn
            assert Cs % tn == 0
            b_spec = pl.BlockSpec((None, tk, tn), lambda i, j, k: (j // nb, k, j % nb))
        else:
            b_spec = pl.BlockSpec((tk, tn), lambda i, j, k: (k, j))
    in_specs = [a_spec, b_spec]
    args = [a, b]
    if bias is not None:
        in_specs.append(pl.BlockSpec((1, tn), lambda i, j, k: (0, j)))
        args.append(bias)
    if extra is not None:
        in_specs.append(pl.BlockSpec((tm, tn), lambda i, j, k: (i, j)))
        args.append(extra)

    if out3d is not None:
        P_o, Cs_o = out3d
        assert P_o * Cs_o == N and Cs_o % tn == 0
        nbo = Cs_o // tn
        o_spec = pl.BlockSpec((None, tm, tn), lambda i, j, k: (j // nbo, i, j % nbo))
        o_shape = (P_o, M, Cs_o)
    else:
        o_spec = pl.BlockSpec((tm, tn), lambda i, j, k: (i, j))
        o_shape = (M, N)
    if epi == "relu2":
        out_shape = [jax.ShapeDtypeStruct(o_shape, BF16), jax.ShapeDtypeStruct(o_shape, BF16)]
        out_specs = [o_spec, o_spec]
    else:
        out_shape = jax.ShapeDtypeStruct(o_shape, out_dtype)
        out_specs = o_spec
    n_out = 2 if epi == "relu2" else 1
    dims = _DIMS[mode]
    has_bias, has_extra = bias is not None, extra is not None

    def body(*refs):
        a_ref, b_ref = refs[0], refs[1]
        pos = 2
        bias_ref = extra_ref = None
        if has_bias:
            bias_ref = refs[pos]
            pos += 1
        if has_extra:
            extra_ref = refs[pos]
            pos += 1
        outs = refs[pos:pos + n_out]
        acc_ref = refs[pos + n_out] if nk > 1 else None

        def partial():
            av = a_ref[...]
            if a_silu:
                av = _silu(av)
            return lax.dot_general(av, b_ref[...], dims, preferred_element_type=F32)

        def finish(r):
            if has_bias:
                r = r + bias_ref[...]
            if epi == "relu2":
                rr = jnp.maximum(r, 0.0)
                outs[0][...] = rr.astype(BF16)
                outs[1][...] = (rr * rr).astype(BF16)
            elif epi == "mul2":
                outs[0][...] = (r * 2.0 * extra_ref[...].astype(F32)).astype(outs[0].dtype)
            else:
                outs[0][...] = r.astype(outs[0].dtype)

        if nk == 1:
            finish(partial())
        else:
            k = pl.program_id(2)

            @pl.when(k == 0)
            def _():
                acc_ref[...] = jnp.zeros_like(acc_ref)

            acc_ref[...] += partial()

            @pl.when(k == nk - 1)
            def _():
                finish(acc_ref[...])

    return pl.pallas_call(
        body, name=name, grid=(M // tm, N // tn, nk), in_specs=in_specs, out_specs=out_specs,
        out_shape=out_shape,
        scratch_shapes=[pltpu.VMEM((tm, tn), F32)] if nk > 1 else [],
        compiler_params=_cparams(("parallel", "parallel", "arbitrary")),
    )(*args)


def _modnorm(x, gain, shift, scale):
    y = x * lax.rsqrt(jnp.mean(x * x, axis=-1, keepdims=True) + EPS)
    return (y * gain) * (1.0 + scale) + shift


def _row_fwd(xprev, y, vec, *, name, ts=512):
    S, D = xprev.shape
    ts = min(ts, S)
    has_res = y is not None
    row = pl.BlockSpec((ts, D), lambda i: (i, 0))
    vspec = pl.BlockSpec((8, D), lambda i: (0, 0))

    def body(*refs):
        if has_res:
            xp_ref, y_ref, v_ref, x_ref, h_ref = refs
            x = xp_ref[...] + v_ref[0:1, :] * y_ref[...]
            x_ref[...] = x
        else:
            xp_ref, v_ref, h_ref = refs
            x = xp_ref[...]
        h_ref[...] = _modnorm(x, v_ref[1:2, :], v_ref[2:3, :], v_ref[3:4, :]).astype(BF16)

    if has_res:
        return pl.pallas_call(
            body, name=name, grid=(S // ts,), in_specs=[row, row, vspec], out_specs=[row, row],
            out_shape=[jax.ShapeDtypeStruct((S, D), F32), jax.ShapeDtypeStruct((S, D), BF16)],
            compiler_params=_cparams(("parallel",)),
        )(xprev, y, vec)
    h = pl.pallas_call(
        body, name=name, grid=(S // ts,), in_specs=[row, vspec], out_specs=row,
        out_shape=jax.ShapeDtypeStruct((S, D), BF16),
        compiler_params=_cparams(("parallel",)),
    )(xprev, vec)
    return xprev, h


def _row_bwd(xin, dh, dxout, yprev, vec, *, name, ts=512):
    S, D = xin.shape
    ts = min(ts, S)
    row = pl.BlockSpec((ts, D), lambda i: (i, 0))
    vspec = pl.BlockSpec((8, D), lambda i: (0, 0))

    def body(x_ref, dh_ref, dx_ref, y_ref, v_ref, dxin_ref, dy_ref, part_ref):
        @pl.when(pl.program_id(0) == 0)
        def _():
            part_ref[...] = jnp.zeros_like(part_ref)

        gate = v_ref[0:1, :]
        _, vjp = jax.vjp(_modnorm, x_ref[...], v_ref[1:2, :], v_ref[2:3, :], v_ref[3:4, :])
        dxn, dgain, dshift, dscale = vjp(dh_ref[...])
        dxin = dx_ref[...] + dxn
        dxin_ref[...] = dxin
        dy = dxin * gate
        dy_ref[...] = dy.astype(BF16)
        part_ref[0:1, :] += jnp.sum(dxin * y_ref[...], axis=0, keepdims=True)
        part_ref[1:2, :] += dgain
        part_ref[2:3, :] += dshift
        part_ref[3:4, :] += dscale
        part_ref[4:5, :] += jnp.sum(dy, axis=0, keepdims=True)

    return pl.pallas_call(
        body, name=name, grid=(S // ts,), in_specs=[row, row, row, row, vspec],
        out_specs=[row, row, vspec],
        out_shape=[jax.ShapeDtypeStruct((S, D), F32), jax.ShapeDtypeStruct((S, D), BF16),
                   jax.ShapeDtypeStruct((8, D), F32)],
        compiler_params=_cparams(("arbitrary",)),
    )(xin, dh, dxout, yprev, vec)


def _final(xprev, y, target, vec, *, name, ts=512):
    S, D = xprev.shape
    ts = min(ts, S)
    row = pl.BlockSpec((ts, D), lambda i: (i, 0))
    vspec = pl.BlockSpec((8, D), lambda i: (0, 0))

    def norm(x, gain):
        return x * lax.rsqrt(jnp.mean(x * x, axis=-1, keepdims=True) + EPS) * gain

    def body(xp_ref, y_ref, t_ref, v_ref, dx_ref, dy_ref, part_ref):
        @pl.when(pl.program_id(0) == 0)
        def _():
            part_ref[...] = jnp.zeros_like(part_ref)

        gate = v_ref[0:1, :]
        yv = y_ref[...]
        x = xp_ref[...] + gate * yv
        out, vjp = jax.vjp(norm, x, v_ref[1:2, :])
        err = out - t_ref[...]
        dx, dgain = vjp(err * (1.0 / D))
        dx_ref[...] = dx
        dy_ref[...] = (dx * gate).astype(BF16)
        part_ref[0:1, :] += jnp.sum(dx * yv, axis=0, keepdims=True)
        part_ref[1:2, :] += dgain
        part_ref[2:3, :] += jnp.sum(err * err, axis=0, keepdims=True) * (0.5 / D)

    return pl.pallas_call(
        body, name=name, grid=(S // ts,), in_specs=[row, row, row, vspec], out_specs=[row, row, vspec],
        out_shape=[jax.ShapeDtypeStruct((S, D), F32), jax.ShapeDtypeStruct((S, D), BF16),
                   jax.ShapeDtypeStruct((8, D), F32)],
        compiler_params=_cparams(("arbitrary",)),
    )(xprev, y, target, vec)


def _ln_silu(cv, g, b):
    mu = jnp.mean(cv, axis=-1, keepdims=True)
    var = jnp.mean(jnp.square(cv - mu), axis=-1, keepdims=True)
    u = (cv - mu) * lax.rsqrt(var + EPS) * g + b
    return _silu(u)


def _shift_copies(ext, sh, n):
    for b in range(1, SUBLANES):
        sh[b - 1, 0:n, :] = ext[pl.ds(b, n), :]


def _shifted(ext, sh, off, r0, rows, cols):
    a, b = divmod(off, SUBLANES)
    if b == 0:
        return ext[pl.ds(SUBLANES * a + r0, rows), cols]
    return sh[b - 1, pl.ds(SUBLANES * a + r0, rows), cols]


def _conv_fwd(u, wdw, vec, *, name, ts=256):
    S, D2 = u.shape
    D = D2 // 2
    ts = min(ts, S)
    H = CONV_HALO
    row = pl.BlockSpec((ts, D), lambda i: (i, 0))

    rb_rows = min(CONV_BLOCK_ROWS, ts)

    def body(u_ref, w_ref, v_ref, vo_ref, cv_ref, z_ref, ext, sh):
        @pl.when(pl.program_id(0) == 0)
        def _():
            ext[0:H, :] = jnp.zeros((H, D), F32)

        uu = u_ref[...]
        v = uu[:, :D] * _sigmoid(uu[:, D:])
        vo_ref[...] = v
        ext[H:H + ts, :] = v
        _shift_copies(ext, sh, ts + H - 8)
        for r0 in range(0, ts, rb_rows):
            for c0 in range(0, D, LANES):
                cols = pl.ds(c0, LANES)
                acc = jnp.zeros((rb_rows, LANES), F32)
                for t in range(CONV_WIDTH):
                    src = _shifted(ext, sh, H - (CONV_WIDTH - 1) + t, r0, rb_rows, cols)
                    acc = acc + src * w_ref[pl.ds(t, 1), cols]
                cv_ref[pl.ds(r0, rb_rows), cols] = acc + v_ref[0:1, cols]
        z_ref[...] = _ln_silu(cv_ref[...], v_ref[1:2, :], v_ref[2:3, :]).astype(BF16)
        ext[0:H, :] = ext[ts:ts + H, :]

    return pl.pallas_call(
        body, name=name, grid=(S // ts,),
        in_specs=[pl.BlockSpec((ts, D2), lambda i: (i, 0)), pl.BlockSpec((H, D), lambda i: (0, 0)),
                  pl.BlockSpec((8, D), lambda i: (0, 0))],
        out_specs=[row, row, row],
        out_shape=[jax.ShapeDtypeStruct((S, D), F32), jax.ShapeDtypeStruct((S, D), F32),
                   jax.ShapeDtypeStruct((S, D), BF16)],
        scratch_shapes=[pltpu.VMEM((ts + H, D), F32), pltpu.VMEM((7, ts + H - 8, D), F32)],
        compiler_params=_cparams(("arbitrary",)),
    )(u, wdw, vec)


def _conv_bwd(dz, cv, v, u, wdw, vec, *, name, ts=256):
    S, D = cv.shape
    ts = min(ts, S)
    H = CONV_HALO
    nt = S // ts
    per = ts // H
    rev = lambda i: (nt - 1 - i, 0)
    row = pl.BlockSpec((ts, D), rev)

    rb_rows = min(CONV_BLOCK_ROWS, ts)
    nsh = ts + H - 8

    def body(dz_ref, cv_ref, v_ref, vh_ref, u_ref, w_ref, vec_ref, du_ref, dw_ref, part_ref, dbu_ref,
             dext, vext, dsh, vsh, dwacc, dvbuf):
        i = pl.program_id(0)

        @pl.when(i == 0)
        def _():
            dext[ts:ts + H, :] = jnp.zeros((H, D), F32)
            dwacc[...] = jnp.zeros_like(dwacc)
            part_ref[...] = jnp.zeros_like(part_ref)
            dbu_ref[...] = jnp.zeros_like(dbu_ref)

        _, vjp = jax.vjp(_ln_silu, cv_ref[...], vec_ref[1:2, :], vec_ref[2:3, :])
        dcv, dg, db = vjp(dz_ref[...])
        part_ref[0:1, :] += jnp.sum(dcv, axis=0, keepdims=True)
        part_ref[1:2, :] += dg
        part_ref[2:3, :] += db
        dext[0:ts, :] = dcv
        vext[0:H, :] = vh_ref[...] * jnp.where(i == nt - 1, 0.0, 1.0)
        vext[H:H + ts, :] = v_ref[...]
        _shift_copies(dext, dsh, nsh)
        _shift_copies(vext, vsh, nsh)
        for r0 in range(0, ts, rb_rows):
            for c0 in range(0, D, LANES):
                cols = pl.ds(c0, LANES)
                dblk = dext[pl.ds(r0, rb_rows), cols]
                dv = jnp.zeros((rb_rows, LANES), F32)
                for t in range(CONV_WIDTH):
                    prod = dblk * _shifted(vext, vsh, H - (CONV_WIDTH - 1) + t, r0, rb_rows, cols)
                    parts = [prod[s:s + SUBLANES, :] for s in range(0, rb_rows, SUBLANES)]
                    while len(parts) > 1:
                        parts = [parts[k] + parts[k + 1] for k in range(0, len(parts), 2)]
                    dwacc[pl.ds(t * SUBLANES, SUBLANES), cols] += parts[0]
                    dv = dv + _shifted(dext, dsh, CONV_WIDTH - 1 - t, r0, rb_rows, cols) * w_ref[pl.ds(t, 1), cols]
                dvbuf[pl.ds(r0, rb_rows), cols] = dv
        dv = dvbuf[...]
        uu = u_ref[...]
        a, g = uu[:, :D], uu[:, D:]
        sg = _sigmoid(g)
        da = dv * sg
        dg_ = dv * a * sg * (1.0 - sg)
        du = jnp.concatenate([da, dg_], axis=-1)
        du_ref[...] = du.astype(BF16)
        dbu_ref[0:1, :] += jnp.sum(du, axis=0, keepdims=True)
        dext[ts:ts + H, :] = dext[0:H, :]

        @pl.when(i == nt - 1)
        def _():
            dw_ref[...] = jnp.zeros_like(dw_ref)
            for t in range(CONV_WIDTH):
                dw_ref[pl.ds(t, 1), :] = jnp.sum(dwacc[pl.ds(t * SUBLANES, SUBLANES), :], axis=0, keepdims=True)

    return pl.pallas_call(
        body, name=name, grid=(nt,),
        in_specs=[row, row, row,
                  pl.BlockSpec((H, D), lambda i: (jnp.maximum((nt - 1 - i) * per - 1, 0), 0)),
                  pl.BlockSpec((ts, 2 * D), rev), pl.BlockSpec((H, D), lambda i: (0, 0)),
                  pl.BlockSpec((8, D), lambda i: (0, 0))],
        out_specs=[pl.BlockSpec((ts, 2 * D), rev), pl.BlockSpec((H, D), lambda i: (0, 0)),
                   pl.BlockSpec((8, D), lambda i: (0, 0)), pl.BlockSpec((8, 2 * D), lambda i: (0, 0))],
        out_shape=[jax.ShapeDtypeStruct((S, 2 * D), BF16), jax.ShapeDtypeStruct((H, D), F32),
                   jax.ShapeDtypeStruct((8, D), F32), jax.ShapeDtypeStruct((8, 2 * D), F32)],
        scratch_shapes=[pltpu.VMEM((ts + H, D), F32), pltpu.VMEM((ts + H, D), F32),
                        pltpu.VMEM((7, nsh, D), F32), pltpu.VMEM((7, nsh, D), F32),
                        pltpu.VMEM((CONV_WIDTH * SUBLANES, D), F32), pltpu.VMEM((ts, D), F32)],
        compiler_params=_cparams(("arbitrary",)),
    )(dz, cv, v, v, u, wdw, vec)


def _rope(x, c, s, half):
    x1, x2 = x[:, :half], x[:, half:]
    return jnp.concatenate([x1 * c - x2 * s, x2 * c + x1 * s], axis=-1)


def _rope_t(d, c, s, half):
    d1, d2 = d[:, :half], d[:, half:]
    return jnp.concatenate([d1 * c + d2 * s, d2 * c - d1 * s], axis=-1)


def _gn_gate(y, gate, g, b):
    mu = jnp.mean(y, axis=-1, keepdims=True)
    var = jnp.mean(jnp.square(y - mu), axis=-1, keepdims=True)
    return _silu(gate) * ((y - mu) * lax.rsqrt(var + EPS) * g + b)


def _dot(a, b, mode="nn"):
    return lax.dot_general(a, b, _DIMS[mode], preferred_element_type=F32)


def _ret_tables(H):
    lg = jnp.log(1.0 - 2.0 ** (-5.0 - jnp.arange(H, dtype=F32)))
    idx = jnp.arange(CHUNK, dtype=F32)
    dmat = jnp.exp(lg[:, None, None] * jnp.abs(idx[:, None] - idx[None, :]))
    xi = jnp.exp(lg[:, None] * (idx + 1.0))[..., None]
    zeta = jnp.exp(lg[:, None] * (CHUNK - 1.0 - idx))[..., None]
    dec = jnp.exp(lg * CHUNK)[:, None, None]
    return dmat, xi, zeta, dec


def _ret_specs(R, dk, dv, half, order):
    H = RET_HEADS
    C = CHUNK
    nq = H
    return dict(
        q=pl.BlockSpec((R, dk), lambda h, n: (order(n), h)),
        k=pl.BlockSpec((R, dk), lambda h, n: (order(n), nq + h)),
        v=pl.BlockSpec((R, dv), lambda h, n: (order(n), nq + h)),
        gate=pl.BlockSpec((R, dv), lambda h, n: (order(n), 2 * nq + h)),
        rope=pl.BlockSpec((R, half), lambda h, n: (order(n), 0)),
        dmat=pl.BlockSpec((None, C, C), lambda h, n: (h, 0, 0)),
        col=pl.BlockSpec((None, C, 1), lambda h, n: (h, 0, 0)),
        one=pl.BlockSpec((None, 1, 1), lambda h, n: (h, 0, 0)),
        gn=pl.BlockSpec((None, 1, dv), lambda h, n: (h, 0, 0)),
        yv=pl.BlockSpec((R, dv), lambda h, n: (order(n), h)),
        yk=pl.BlockSpec((R, dk), lambda h, n: (order(n), h)),
    )


def _ret_fwd(proj, cos, sin, tables, gn_g, gn_b, *, name, cps=4):
    S = proj.shape[0]
    D = proj.shape[1] // 6
    H, C = RET_HEADS, CHUNK
    dk, dv, half = D // H, 2 * D // H, D // H // 2
    nc = S // C
    cps = min(cps, nc)
    R = cps * C
    scale = dk ** -0.5
    sp = _ret_specs(R, dk, dv, half, lambda n: n)
    dmat, xi, zeta, dec = tables

    def body(q_ref, k_ref, v_ref, g_ref, cos_ref, sin_ref, dm_ref, xi_ref, ze_ref, dec_ref, gg_ref, gb_ref,
             y_ref, y2_ref, st_ref, state):
        @pl.when(pl.program_id(1) == 0)
        def _():
            state[...] = jnp.zeros_like(state)

        dm, xv, zv, dc = dm_ref[...], xi_ref[...], ze_ref[...], dec_ref[...]
        for j in range(cps):
            rows = pl.ds(j * C, C)
            cs, sn = cos_ref[rows, :], sin_ref[rows, :]
            qr = _rope(q_ref[rows, :].astype(F32), cs, sn, half)
            kr = _rope(k_ref[rows, :].astype(F32), cs, sn, half) * scale
            vb = v_ref[rows, :]
            p = (_dot(qr.astype(BF16), kr.astype(BF16), "nt") * dm).astype(BF16)
            st = state[...]
            stb = st.astype(BF16)
            st_ref[j] = stb
            y = _dot(p, vb) + _dot((qr * xv).astype(BF16), stb)
            state[...] = st * dc + _dot((kr * zv).astype(BF16), vb, "tn")
            y_ref[rows, :] = y
            y2_ref[rows, :] = _gn_gate(y, g_ref[rows, :].astype(F32), gg_ref[...], gb_ref[...]).astype(BF16)

    return pl.pallas_call(
        body, name=name, grid=(H, nc // cps),
        in_specs=[sp["q"], sp["k"], sp["v"], sp["gate"], sp["rope"], sp["rope"], sp["dmat"], sp["col"],
                  sp["col"], sp["one"], sp["gn"], sp["gn"]],
        out_specs=[sp["yv"], sp["yv"], pl.BlockSpec((None, cps, dk, dv), lambda h, n: (h, n, 0, 0))],
        out_shape=[jax.ShapeDtypeStruct((S, 2 * D), F32), jax.ShapeDtypeStruct((S, 2 * D), BF16),
                   jax.ShapeDtypeStruct((H, nc, dk, dv), BF16)],
        scratch_shapes=[pltpu.VMEM((dk, dv), F32)],
        compiler_params=_cparams(("arbitrary", "arbitrary")),
    )(proj, proj, proj, proj, cos, sin, dmat, xi, zeta, dec, gn_g, gn_b)


def _ret_bwd(proj, cos, sin, tables, gn_g, gn_b, y, dy2, states, *, name, cps=4):
    S = proj.shape[0]
    D = proj.shape[1] // 6
    H, C = RET_HEADS, CHUNK
    dk, dv, half = D // H, 2 * D // H, D // H // 2
    nc = S // C
    cps = min(cps, nc)
    ns = nc // cps
    R = cps * C
    scale = dk ** -0.5
    order = lambda n: ns - 1 - n
    sp = _ret_specs(R, dk, dv, half, order)
    dmat, xi, zeta, dec = tables

    def body(q_ref, k_ref, v_ref, g_ref, cos_ref, sin_ref, dm_ref, xi_ref, ze_ref, dec_ref, gg_ref, gb_ref,
             y_ref, dy2_ref, st_ref, dq_ref, dk_ref, dv_ref, dg_ref, dgg_ref, dgb_ref, gst):
        @pl.when(pl.program_id(1) == 0)
        def _():
            gst[...] = jnp.zeros_like(gst)
            dgg_ref[...] = jnp.zeros_like(dgg_ref)
            dgb_ref[...] = jnp.zeros_like(dgb_ref)

        dm, xv, zv, dc = dm_ref[...], xi_ref[...], ze_ref[...], dec_ref[...]
        for j in reversed(range(cps)):
            rows = pl.ds(j * C, C)
            cs, sn = cos_ref[rows, :], sin_ref[rows, :]
            _, vjp = jax.vjp(_gn_gate, y_ref[rows, :], g_ref[rows, :].astype(F32), gg_ref[...], gb_ref[...])
            dy, dgate, dgg, dgb = vjp(dy2_ref[rows, :])
            dgg_ref[...] += dgg
            dgb_ref[...] += dgb
            dg_ref[rows, :] = dgate.astype(BF16)
            dyb = dy.astype(BF16)
            qr = _rope(q_ref[rows, :].astype(F32), cs, sn, half)
            kr = _rope(k_ref[rows, :].astype(F32), cs, sn, half) * scale
            qb, kb, vb = qr.astype(BF16), kr.astype(BF16), v_ref[rows, :]
            p = (_dot(qb, kb, "nt") * dm).astype(BF16)
            g = gst[...]
            gb16 = g.astype(BF16)
            sprev = st_ref[j]
            dvv = _dot(p, dyb, "tn") + _dot((kr * zv).astype(BF16), gb16)
            dpb = (_dot(dyb, vb, "nt") * dm).astype(BF16)
            dqr = _dot(dpb, kb) + _dot(dyb, sprev, "nt") * xv
            dkr = _dot(dpb, qb, "tn") + _dot(vb, gb16, "nt") * zv
            gst[...] = g * dc + _dot((qr * xv).astype(BF16), dyb, "tn")
            dq_ref[rows, :] = _rope_t(dqr, cs, sn, half).astype(BF16)
            dk_ref[rows, :] = _rope_t(dkr * scale, cs, sn, half).astype(BF16)
            dv_ref[rows, :] = dvv.astype(BF16)

    return pl.pallas_call(
        body, name=name, grid=(H, ns),
        in_specs=[sp["q"], sp["k"], sp["v"], sp["gate"], sp["rope"], sp["rope"], sp["dmat"], sp["col"],
                  sp["col"], sp["one"], sp["gn"], sp["gn"], sp["yv"], sp["yv"],
                  pl.BlockSpec((None, cps, dk, dv), lambda h, n: (h, order(n), 0, 0))],
        out_specs=[sp["yk"], sp["yk"], sp["yv"], sp["yv"], sp["gn"], sp["gn"]],
        out_shape=[jax.ShapeDtypeStruct((S, D), BF16), jax.ShapeDtypeStruct((S, D), BF16),
                   jax.ShapeDtypeStruct((S, 2 * D), BF16), jax.ShapeDtypeStruct((S, 2 * D), BF16),
                   jax.ShapeDtypeStruct((H, 1, dv), F32), jax.ShapeDtypeStruct((H, 1, dv), F32)],
        scratch_shapes=[pltpu.VMEM((dk, dv), F32)],
        compiler_params=_cparams(("arbitrary", "arbitrary")),
    )(proj, proj, proj, proj, cos, sin, dmat, xi, zeta, dec, gn_g, gn_b, y, dy2, states)


def _rows_tile(rows, cols, n_arrays):
    cap = max(8, V7X_VMEM_LIMIT // 3 // (n_arrays * 2 * 4 * cols))
    t = rows
    while t > cap and t % 2 == 0:
        t //= 2
    return t


def _add_half(g, r, half_idx, *, name):
    P, R, Cc = g.shape
    hR = R // 2
    tr = _rows_tile(hR, Cc, 3)
    nb = hR // tr

    def body(h_ref, g_ref, r_ref, o_ref):
        o_ref[...] = (g_ref[...] + r_ref[...]).astype(BF16)

    return pl.pallas_call(
        body, name=name,
        grid_spec=pltpu.PrefetchScalarGridSpec(
            num_scalar_prefetch=1, grid=(P, nb),
            in_specs=[pl.BlockSpec((None, tr, Cc), lambda s, i, h: (s, h[0] * nb + i, 0)),
                      pl.BlockSpec((None, tr, Cc), lambda s, i, h: (s, i, 0))],
            out_specs=pl.BlockSpec((None, tr, Cc), lambda s, i, h: (s, i, 0))),
        out_shape=jax.ShapeDtypeStruct((P, hR, Cc), BF16), compiler_params=_cparams(("parallel", "parallel")),
    )(half_idx, g, r)


def _sum_slots(x, *, name):
    L, NS, R, Cc = x.shape
    tr = _rows_tile(R, Cc, NS + 1)

    def body(x_ref, o_ref):
        acc = x_ref[0].astype(F32)
        for s in range(1, NS):
            acc = acc + x_ref[s].astype(F32)
        o_ref[...] = acc

    return pl.pallas_call(
        body, name=name, grid=(L, R // tr),
        in_specs=[pl.BlockSpec((None, NS, tr, Cc), lambda l, i: (l, 0, i, 0))],
        out_specs=pl.BlockSpec((None, tr, Cc), lambda l, i: (l, i, 0)),
        out_shape=jax.ShapeDtypeStruct((L, R, Cc), F32), compiler_params=_cparams(("parallel", "parallel")),
    )(x)


def _adam_store(g, w_ref, m_ref, v_ref, go_ref, d_ref, mo_ref, vo_ref):
    mn = ADAM_B1 * m_ref[...] + (1.0 - ADAM_B1) * g
    vn = ADAM_B2 * v_ref[...] + (1.0 - ADAM_B2) * jnp.square(g)
    m_hat = mn / (1.0 - ADAM_B1 ** ADAM_STEP)
    v_hat = vn / (1.0 - ADAM_B2 ** ADAM_STEP)
    go_ref[...] = g
    d_ref[...] = -ADAM_LR * (m_hat / (jnp.sqrt(v_hat) + ADAM_EPS) + ADAM_WD * w_ref[...])
    mo_ref[...] = mn
    vo_ref[...] = vn


def _adamw(gslots, w, m, v, *, name):
    L, NS, R, Cc = gslots.shape
    tr = _rows_tile(R, Cc, NS + 7)
    gspec = pl.BlockSpec((None, NS, tr, Cc), lambda l, i: (l, 0, i, 0))
    spec = pl.BlockSpec((None, tr, Cc), lambda l, i: (l, i, 0))

    def body(g_ref, *refs):
        g = g_ref[0]
        for s in range(1, NS):
            g = g + g_ref[s]
        _adam_store(g, *refs)

    sd = jax.ShapeDtypeStruct((L, R, Cc), F32)
    return pl.pallas_call(
        body, name=name, grid=(L, R // tr), in_specs=[gspec, spec, spec, spec],
        out_specs=[spec, spec, spec, spec], out_shape=[sd, sd, sd, sd],
        compiler_params=_cparams(("parallel", "parallel")),
    )(gslots, w, m, v)


def _adamw_halves(g_mine, g_sib, half_idx, w, m, v, *, name):
    L, hR, Cc = g_mine.shape
    tr = _rows_tile(hR, Cc, 9)
    nbh = hR // tr
    gspec = pl.BlockSpec((None, tr, Cc), lambda l, i, h: (l, i % nbh, 0))
    spec = pl.BlockSpec((None, tr, Cc), lambda l, i, h: (l, i, 0))

    def body(h_ref, gm_ref, gs_ref, *refs):
        mine = (pl.program_id(1) // nbh) == h_ref[0]
        _adam_store(jnp.where(mine, gm_ref[...], gs_ref[...]), *refs)

    sd = jax.ShapeDtypeStruct((L, 2 * hR, Cc), F32)
    return pl.pallas_call(
        body, name=name,
        grid_spec=pltpu.PrefetchScalarGridSpec(
            num_scalar_prefetch=1, grid=(L, 2 * nbh), in_specs=[gspec, gspec, spec, spec, spec],
            out_specs=[spec, spec, spec, spec]),
        out_shape=[sd, sd, sd, sd], compiler_params=_cparams(("parallel", "parallel")),
    )(half_idx, g_mine, g_sib, w, m, v)


def _me():
    return lax.axis_index("x"), lax.axis_index("y"), lax.axis_index("c")


def _flip(v, bit):
    return 1 - v if bit else v


def _allgather8(x, *, name):
    def body(x_ref, out_ref, send_sems, recv_sems, loc_sem):
        mx, my, mc = _me()
        me = 4 * mx + 2 * my + mc
        loc = pltpu.make_async_copy(x_ref, out_ref.at[me], loc_sem)
        loc.start()
        sends, recvs = [], []
        for k in range(1, N_DEV):
            px, py, pc = _flip(mx, k & 4), _flip(my, k & 2), _flip(mc, k & 1)
            sends.append(pltpu.make_async_remote_copy(
                src_ref=x_ref, dst_ref=out_ref.at[me], send_sem=send_sems.at[k - 1],
                recv_sem=recv_sems.at[k - 1], device_id=(px, py, pc), device_id_type=MESH))
            recvs.append(pltpu.make_async_remote_copy(
                src_ref=x_ref, dst_ref=out_ref.at[4 * px + 2 * py + pc], send_sem=send_sems.at[k - 1],
                recv_sem=recv_sems.at[k - 1], device_id=(px, py, pc), device_id_type=MESH))
        for cp in sends:
            cp.start()
        for cp in recvs:
            cp.wait_recv()
        for cp in sends:
            cp.wait_send()
        loc.wait()

    return pl.pallas_call(
        body, name=name, in_specs=[ANY], out_specs=ANY,
        out_shape=jax.ShapeDtypeStruct((N_DEV,) + x.shape, x.dtype),
        scratch_shapes=[pltpu.SemaphoreType.DMA((N_DEV - 1,)), pltpu.SemaphoreType.DMA((N_DEV - 1,)),
                        pltpu.SemaphoreType.DMA],
    )(x)


def _gather_chips(arrays, after, *, name):
    n = len(arrays)

    def body(*refs):
        ins, outs = refs[:n], refs[n + 1:2 * n + 1]
        ici_send, ici_recv, d2d_send, d2d_recv, own_send, own_recv = refs[2 * n + 1:]
        mx, my, mc = _me()
        me = 2 * mx + my
        sib = (mx, my, 1 - mc)
        locs, sends, lands, passes, gifts = [], [], [], [], []
        for a in range(n):
            h = arrays[a].shape[0] // 2
            mine, other = pl.ds(mc * h, h), pl.ds((1 - mc) * h, h)
            locs.append(pltpu.make_async_remote_copy(
                src_ref=ins[a], dst_ref=outs[a].at[me], send_sem=own_send.at[a], recv_sem=own_recv.at[a],
                device_id=sib, device_id_type=MESH))
            for k in range(1, N_CHIPS):
                px, py = _flip(mx, k & 2), _flip(my, k & 1)
                peer = 2 * px + py
                ici = dict(send_sem=ici_send.at[a, k - 1], recv_sem=ici_recv.at[a, k - 1],
                           device_id=(px, py, mc), device_id_type=MESH)
                d2d = dict(send_sem=d2d_send.at[a, k - 1], recv_sem=d2d_recv.at[a, k - 1],
                           device_id=sib, device_id_type=MESH)
                sends.append(pltpu.make_async_remote_copy(
                    src_ref=ins[a].at[mine], dst_ref=outs[a].at[me, mine], **ici))
                lands.append(pltpu.make_async_remote_copy(
                    src_ref=ins[a].at[mine], dst_ref=outs[a].at[peer, mine], **ici))
                passes.append(pltpu.make_async_remote_copy(
                    src_ref=outs[a].at[peer, mine], dst_ref=outs[a].at[peer, mine], **d2d))
                gifts.append(pltpu.make_async_remote_copy(
                    src_ref=outs[a].at[peer, other], dst_ref=outs[a].at[peer, other], **d2d))
        for cp in locs + sends:
            cp.start()
        for land, fwd in zip(lands, passes):
            land.wait_recv()
            fwd.start()
        for cp in gifts + locs:
            cp.wait_recv()
        for cp in sends + passes + locs:
            cp.wait_send()

    nsem = (n, N_CHIPS - 1)
    return pl.pallas_call(
        body, name=name, in_specs=[ANY] * (n + 1), out_specs=[ANY] * n,
        out_shape=[jax.ShapeDtypeStruct((N_CHIPS,) + a.shape, a.dtype) for a in arrays],
        scratch_shapes=[pltpu.SemaphoreType.DMA(nsem), pltpu.SemaphoreType.DMA(nsem), pltpu.SemaphoreType.DMA(nsem),
                        pltpu.SemaphoreType.DMA(nsem), pltpu.SemaphoreType.DMA((n,)), pltpu.SemaphoreType.DMA((n,))],
    )(*arrays, after)


def _swap_half(arrays, *, name):
    n = len(arrays)

    def body(*refs):
        ins, outs = refs[:n], refs[n:2 * n]
        send_sems, recv_sems = refs[2 * n:]
        mx, my, mc = _me()
        cps = []
        for a in range(n):
            P, R, _ = arrays[a].shape
            cps.append(pltpu.make_async_remote_copy(
                src_ref=ins[a].at[pl.ds(0, P), pl.ds((1 - mc) * (R // 2), R // 2)], dst_ref=outs[a],
                send_sem=send_sems.at[a], recv_sem=recv_sems.at[a],
                device_id=(mx, my, 1 - mc), device_id_type=MESH))
        for cp in cps:
            cp.start()
        for cp in cps:
            cp.wait_recv()
        for cp in cps:
            cp.wait_send()

    return pl.pallas_call(
        body, name=name, in_specs=[ANY] * n, out_specs=[ANY] * n,
        out_shape=[jax.ShapeDtypeStruct((a.shape[0], a.shape[1] // 2, a.shape[2]), a.dtype) for a in arrays],
        scratch_shapes=[pltpu.SemaphoreType.DMA((n,)), pltpu.SemaphoreType.DMA((n,))],
    )(*arrays)


def _swap_sibling(arrays, *, name):
    n = len(arrays)

    def body(*refs):
        ins, outs = refs[:n], refs[n:2 * n]
        send_sems, recv_sems = refs[2 * n:]
        mx, my, mc = _me()
        cps = [pltpu.make_async_remote_copy(
            src_ref=ins[a], dst_ref=outs[a], send_sem=send_sems.at[a], recv_sem=recv_sems.at[a],
            device_id=(mx, my, 1 - mc), device_id_type=MESH) for a in range(n)]
        for cp in cps:
            cp.start()
        for cp in cps:
            cp.wait_recv()
        for cp in cps:
            cp.wait_send()

    return pl.pallas_call(
        body, name=name, in_specs=[ANY] * n, out_specs=[ANY] * n,
        out_shape=[jax.ShapeDtypeStruct(a.shape, a.dtype) for a in arrays],
        scratch_shapes=[pltpu.SemaphoreType.DMA((n,)), pltpu.SemaphoreType.DMA((n,))],
    )(*arrays)


def _plan_scatter(srcs, lands):
    mx, my, mc = _me()
    copies = []
    for a in range(len(srcs)):
        for k in range(1, N_CHIPS):
            px, py = _flip(mx, k & 2), _flip(my, k & 1)
            copies.append((srcs[a].at[2 * px + py], lands[a].at[k - 1], lands[a].at[k - 1], (px, py, mc)))
    return copies


def _plan_gather(srcs, lands):
    mx, my, mc = _me()
    me = 2 * mx + my
    copies = []
    for a in range(len(srcs)):
        copies.append((srcs[a], lands[a].at[me], lands[a].at[me], (mx, my, 1 - mc)))
        for k in range(1, N_CHIPS):
            px, py = _flip(mx, k & 2), _flip(my, k & 1)
            copies.append((srcs[a], lands[a].at[me], lands[a].at[2 * px + py], (px, py, mc)))
    return copies


def _copy(c, k, send_sems, recv_sems, landing=False):
    src, dst, land, dev = c
    return pltpu.make_async_remote_copy(src_ref=src, dst_ref=land if landing else dst, send_sem=send_sems.at[k],
                                        recv_sem=recv_sems.at[k], device_id=dev, device_id_type=MESH)


def _exchange(srcs, land_shapes, plan, ncopies, *, name):
    ni, nl = len(srcs), len(land_shapes)

    def body(*refs):
        send_sems, recv_sems = refs[ni + nl:]
        copies = plan(refs[:ni], refs[ni:ni + nl])
        for k, c in enumerate(copies):
            _copy(c, k, send_sems, recv_sems).start()
        for k, c in enumerate(copies):
            _copy(c, k, send_sems, recv_sems, landing=True).wait_recv()
        for k, c in enumerate(copies):
            _copy(c, k, send_sems, recv_sems).wait_send()

    return pl.pallas_call(
        body, name=name, in_specs=[ANY] * ni, out_specs=[ANY] * nl, out_shape=list(land_shapes),
        scratch_shapes=[pltpu.SemaphoreType.DMA((ncopies,)), pltpu.SemaphoreType.DMA((ncopies,))],
    )(*srcs)


HBM_SPEC = pl.BlockSpec(memory_space=pltpu.HBM)
SEM_SPEC = pl.BlockSpec(memory_space=pltpu.SEMAPHORE)
SPLIT_EFFECT = pltpu.SideEffectType.DATAFLOW_SIDE_EFFECTING


def _exchange_start(srcs, land_shapes, plan, ncopies, after, *, name):
    ni, nl = len(srcs), len(land_shapes)

    def body(*refs):
        in_refs, land_refs = refs[:ni], refs[ni:ni + nl]
        send_sems, recv_sems = refs[ni + nl + 1], refs[ni + nl + 2]
        token = refs[-1]
        for k, c in enumerate(plan(in_refs, land_refs)):
            _copy(c, k, send_sems, recv_sems).start()
        token[...] = jnp.zeros_like(token)

    bufs = [pltpu.with_memory_space_constraint(a, pltpu.HBM) for a in srcs]
    bufs += [pltpu.with_memory_space_constraint(lax.empty(s.shape, s.dtype), pltpu.HBM) for s in land_shapes]
    outs = pl.pallas_call(
        body, name=name,
        in_specs=[HBM_SPEC] * (ni + nl) + [ANY],
        out_specs=(SEM_SPEC, SEM_SPEC, *[HBM_SPEC] * (ni + nl), pl.BlockSpec(memory_space=pltpu.VMEM)),
        out_shape=(pltpu.SemaphoreType.DMA((ncopies,)), pltpu.SemaphoreType.DMA((ncopies,)),
                   *[pltpu.HBM(b.shape, b.dtype) for b in bufs], jax.ShapeDtypeStruct((8, 128), F32)),
        input_output_aliases={i: 2 + i for i in range(ni + nl)},
        compiler_params=pltpu.CompilerParams(has_side_effects=SPLIT_EFFECT),
    )(*bufs, after)
    return outs[:-1], outs[-1]


def _exchange_wait(started, ni, plan, after, *, name):
    send_sems, recv_sems = started[0], started[1]
    bufs = list(started[2:])
    nb = len(bufs)

    def body(*refs):
        in_refs, land_refs = refs[:ni], refs[ni:nb]
        send, recv = refs[nb], refs[nb + 1]
        for k, c in enumerate(plan(in_refs, land_refs)):
            cp = _copy(c, k, send, recv, landing=True)
            cp.wait_send()
            cp.wait_recv()

    outs = pl.pallas_call(
        body, name=name, in_specs=[HBM_SPEC] * nb + [SEM_SPEC, SEM_SPEC, ANY], out_specs=[HBM_SPEC] * nb,
        out_shape=[pltpu.HBM(b.shape, b.dtype) for b in bufs],
        input_output_aliases={i: i for i in range(nb)},
        compiler_params=pltpu.CompilerParams(has_side_effects=SPLIT_EFFECT),
    )(*bufs, send_sems, recv_sems, after)
    return list(outs[:ni]), list(outs[ni:])


def _sum_own(own, recv, chip_idx, *, name):
    _, R, Cc = own.shape
    tr = _rows_tile(R, Cc, 5)

    def body(s_ref, o_ref, r_ref, t_ref):
        acc = o_ref[...].astype(F32)
        for s in range(N_CHIPS - 1):
            acc = acc + r_ref[s].astype(F32)
        t_ref[...] = acc

    return pl.pallas_call(
        body, name=name,
        grid_spec=pltpu.PrefetchScalarGridSpec(
            num_scalar_prefetch=1, grid=(R // tr,),
            in_specs=[pl.BlockSpec((None, tr, Cc), lambda i, s: (s[0], i, 0)),
                      pl.BlockSpec((N_CHIPS - 1, tr, Cc), lambda i, s: (0, i, 0))],
            out_specs=pl.BlockSpec((tr, Cc), lambda i, s: (i, 0))),
        out_shape=jax.ShapeDtypeStruct((R, Cc), F32), compiler_params=_cparams(("parallel",)),
    )(chip_idx, own, recv)


BIG = ("conv_w_pw1", "conv_w_pw2", "ret_w_in", "ret_w_out", "mlp_w1", "mlp_w2")
COLS = ("conv_w_pw1", "ret_w_in", "mlp_w1")
SMALL = ("ada_b", "norm_mix_g", "norm_mlp_g", "conv_b_pw1", "conv_w_dw", "conv_b_dw", "conv_ln_g", "conv_ln_b",
         "conv_b_pw2", "ret_gn_g", "ret_gn_b", "final_norm_g")
SMALL_SHARDED = ("conv_w_dw", "ret_gn_g", "ret_gn_b")
WEIGHTS = ("ada_w", "ada_b", "norm_mix_g", "norm_mlp_g", "conv_w_pw1", "conv_b_pw1", "conv_w_dw", "conv_b_dw",
           "conv_ln_g", "conv_ln_b", "conv_w_pw2", "conv_b_pw2", "ret_w_in", "ret_gn_g", "ret_gn_b", "ret_w_out",
           "mlp_w1", "mlp_w2", "final_norm_g")


def _vec8(rows, D):
    rows = [r.reshape(1, D).astype(F32) for r in rows]
    return jnp.concatenate(rows + [jnp.zeros((8 - len(rows), D), F32)], axis=0)


def _unshard_last(g):
    nd = g.ndim
    t = jnp.transpose(g, tuple(range(1, nd - 1)) + (0, nd - 1))
    return t.reshape(t.shape[:-2] + (t.shape[-2] * t.shape[-1],))


def _pack(parts):
    flat = jnp.concatenate([p.reshape(-1).astype(F32) for p in parts])
    pad = (-flat.shape[0]) % 1024
    return jnp.concatenate([flat, jnp.zeros((pad,), F32)]).reshape(-1, 128)


def _unpack(packed, shapes):
    flat = packed.reshape(-1)
    out, pos = [], 0
    for s in shapes:
        n = math.prod(s)
        out.append(flat[pos:pos + n].reshape(s))
        pos += n
    return out


def kernel(x, c, ada_w, ada_b, norm_mix_g, norm_mlp_g, conv_w_pw1, conv_b_pw1, conv_w_dw, conv_b_dw, conv_ln_g, conv_ln_b, conv_w_pw2, conv_b_pw2, ret_w_in, ret_gn_g, ret_gn_b, ret_w_out, mlp_w1, mlp_w2, final_norm_g, loss_target, m_ada_w, m_ada_b, m_norm_mix_g, m_norm_mlp_g, m_conv_w_pw1, m_conv_b_pw1, m_conv_w_dw, m_conv_b_dw, m_conv_ln_g, m_conv_ln_b, m_conv_w_pw2, m_conv_b_pw2, m_ret_w_in, m_ret_gn_g, m_ret_gn_b, m_ret_w_out, m_mlp_w1, m_mlp_w2, m_final_norm_g, v_ada_w, v_ada_b, v_norm_mix_g, v_norm_mlp_g, v_conv_w_pw1, v_conv_b_pw1, v_conv_w_dw, v_conv_b_dw, v_conv_ln_g, v_conv_ln_b, v_conv_w_pw2, v_conv_b_pw2, v_ret_w_in, v_ret_gn_g, v_ret_gn_b, v_ret_w_out, v_mlp_w1, v_mlp_w2, v_final_norm_g):
    W = dict(ada_w=ada_w, ada_b=ada_b, norm_mix_g=norm_mix_g, norm_mlp_g=norm_mlp_g, conv_w_pw1=conv_w_pw1,
             conv_b_pw1=conv_b_pw1, conv_w_dw=conv_w_dw, conv_b_dw=conv_b_dw, conv_ln_g=conv_ln_g,
             conv_ln_b=conv_ln_b, conv_w_pw2=conv_w_pw2, conv_b_pw2=conv_b_pw2, ret_w_in=ret_w_in,
             ret_gn_g=ret_gn_g, ret_gn_b=ret_gn_b, ret_w_out=ret_w_out, mlp_w1=mlp_w1, mlp_w2=mlp_w2,
             final_norm_g=final_norm_g)
    Mo = dict(ada_w=m_ada_w, ada_b=m_ada_b, norm_mix_g=m_norm_mix_g, norm_mlp_g=m_norm_mlp_g,
              conv_w_pw1=m_conv_w_pw1, conv_b_pw1=m_conv_b_pw1, conv_w_dw=m_conv_w_dw, conv_b_dw=m_conv_b_dw,
              conv_ln_g=m_conv_ln_g, conv_ln_b=m_conv_ln_b, conv_w_pw2=m_conv_w_pw2, conv_b_pw2=m_conv_b_pw2,
              ret_w_in=m_ret_w_in, ret_gn_g=m_ret_gn_g, ret_gn_b=m_ret_gn_b, ret_w_out=m_ret_w_out,
              mlp_w1=m_mlp_w1, mlp_w2=m_mlp_w2, final_norm_g=m_final_norm_g)
    Vo = dict(ada_w=v_ada_w, ada_b=v_ada_b, norm_mix_g=v_norm_mix_g, norm_mlp_g=v_norm_mlp_g,
              conv_w_pw1=v_conv_w_pw1, conv_b_pw1=v_conv_b_pw1, conv_w_dw=v_conv_w_dw, conv_b_dw=v_conv_b_dw,
              conv_ln_g=v_conv_ln_g, conv_ln_b=v_conv_ln_b, conv_w_pw2=v_conv_w_pw2, conv_b_pw2=v_conv_b_pw2,
              ret_w_in=v_ret_w_in, ret_gn_g=v_ret_gn_g, ret_gn_b=v_ret_gn_b, ret_w_out=v_ret_w_out,
              mlp_w1=v_mlp_w1, mlp_w2=v_mlp_w2, final_norm_g=v_final_norm_g)

    S, D = x.shape[1], x.shape[2]
    depth = ada_w.shape[0]
    H = RET_HEADS
    dv = 2 * D // H
    xs = x.reshape(S, D)
    target = loss_target.reshape(S, D)
    mx, my, mc = _me()
    chip = 2 * mx + my
    dev = 4 * mx + 2 * my + mc

    def layer_weights(l):
        mixer = ("conv_w_pw1", "conv_w_pw2") if l % 2 == 0 else ("ret_w_in", "ret_w_out")
        return [(nm, l // 2) for nm in mixer] + [("mlp_w1", l), ("mlp_w2", l)]

    c_all = _allgather8(c.reshape(8, D // 8), name="gather_c").reshape(N_DEV, D)
    cs_ada = ada_w.shape[2]
    bias_sh = lax.dynamic_slice_in_dim(ada_b.reshape(depth, N_CHIPS, cs_ada), chip, 1, axis=1)
    mod_sh = _mm(c_all, ada_w, mode="nn", name="ada_fwd", b3d=True, tn=cs_ada, a_silu=True,
                 bias=bias_sh.reshape(1, depth * cs_ada))
    mod_all = _allgather8(mod_sh, name="gather_mod")[0::2]
    mod_me = lax.dynamic_slice_in_dim(mod_all, dev, 1, axis=1).reshape(N_CHIPS, depth, cs_ada)
    mod = jnp.transpose(mod_me, (1, 0, 2)).reshape(depth, 6, D)

    keys0 = layer_weights(0)
    got0 = _gather_chips([W[nm][i].astype(BF16) for nm, i in keys0] + [W[nm] for nm in SMALL_SHARDED], mod,
                         name="gather_weights")
    Wg = dict(zip(keys0, got0))
    full_small = {nm: _unshard_last(got0[len(keys0) + i]) for i, nm in enumerate(SMALL_SHARDED)}
    pending, order = {}, got0[0]
    for l in range(1, depth):
        keys = layer_weights(l)
        srcs = [W[nm][i].astype(BF16) for nm, i in keys]
        shapes = [jax.ShapeDtypeStruct((N_CHIPS,) + s.shape, BF16) for s in srcs]
        started, order = _exchange_start(srcs, shapes, _plan_gather, 4 * len(srcs), order, name=f"gather_start_{l}")
        pending[l] = (keys, started)
    mod = mod + order[0, 0]

    def wfull(nm, l):
        g = Wg[nm, l]
        return g.reshape(g.shape[0] * g.shape[1], g.shape[2])

    pos_ids = jnp.arange(S, dtype=F32)
    dk = D // H
    inv = ROPE_BASE ** (-jnp.arange(0, dk, 2, dtype=F32) / dk)
    ang = pos_ids[:, None] * inv[None, :]
    cos_t, sin_t = jnp.cos(ang), jnp.sin(ang)
    tables = _ret_tables(H)
    gn_g_full = full_small["ret_gn_g"].reshape(-1, H, 1, dv)
    gn_b_full = full_small["ret_gn_b"].reshape(-1, H, 1, dv)
    wdw_full = full_small["conv_w_dw"]

    def wdw_pad(j):
        return jnp.concatenate([wdw_full[j], jnp.zeros((CONV_HALO - CONV_WIDTH, D), F32)], axis=0)

    saved = []
    xa, y_prev, gate_prev = xs, None, None
    for l in range(depth):
        j = l // 2
        sv = {}
        if l in pending:
            keys, started = pending.pop(l)
            _, lands = _exchange_wait(started, len(keys), _plan_gather, y_prev, name=f"gather_wait_{l}")
            Wg.update(zip(keys, lands))
        vec_a = _vec8([gate_prev if gate_prev is not None else jnp.zeros((D,), F32), norm_mix_g[l], mod[l, 0],
                       mod[l, 1]], D)
        xa, h = _row_fwd(xa, y_prev, vec_a, name="row_fwd" if y_prev is not None else "row_fwd_first")
        sv.update(xa=xa, h=h, vec_a=vec_a)
        if l % 2 == 0:
            u = _mm(h, Wg["conv_w_pw1", j], mode="nn", name="pw1_fwd", b3d=True,
                    tn=Wg["conv_w_pw1", j].shape[2], bias=conv_b_pw1[j].reshape(1, -1))
            cvec = _vec8([conv_b_dw[j], conv_ln_g[j], conv_ln_b[j]], D)
            v_glu, cv, z = _conv_fwd(u, wdw_pad(j), cvec, name="conv_fwd")
            ymix = _mm(z, wfull("conv_w_pw2", j), mode="nn", name="pw2_fwd", bias=conv_b_pw2[j].reshape(1, -1))
            sv.update(u=u, v_glu=v_glu, cv=cv, z=z, cvec=cvec)
        else:
            proj = _mm(h, Wg["ret_w_in", j], mode="nn", name="win_fwd", b3d=True, out_dtype=BF16,
                       tn=Wg["ret_w_in", j].shape[2])
            yr, y2, states = _ret_fwd(proj, cos_t, sin_t, tables, gn_g_full[j], gn_b_full[j], name="ret_fwd")
            ymix = _mm(y2, wfull("ret_w_out", j), mode="nn", name="wout_fwd")
            sv.update(proj=proj, yr=yr, y2=y2, states=states)
        vec_b = _vec8([mod[l, 2], norm_mlp_g[l], mod[l, 3], mod[l, 4]], D)
        xb, h2 = _row_fwd(xa, ymix, vec_b, name="row_fwd")
        ra, p = _mm(h2, Wg["mlp_w1", l], mode="nn", name="w1_fwd", b3d=True, tn=Wg["mlp_w1", l].shape[2],
                    epi="relu2")
        mo = _mm(p, wfull("mlp_w2", l), mode="nn", name="w2_fwd")
        sv.update(ymix=ymix, xb=xb, h2=h2, ra=ra, p=p, mo=mo, vec_b=vec_b)
        saved.append(sv)
        xa, y_prev, gate_prev = xb, mo, mod[l, 5]

    fvec = _vec8([gate_prev, final_norm_g], D)
    dx, dyb, fpart = _final(xa, y_prev, target, fvec, name="final")
    loss = lax.psum(jnp.sum(fpart[2]), ("x", "y", "c"))
    G = {nm: [None] * W[nm].shape[0] for nm in BIG}
    dmod = [[None] * 6 for _ in range(depth)]
    dmod[depth - 1][5] = fpart[0]
    sg = dict(norm_mix_g=[None] * depth, norm_mlp_g=[None] * depth, final_norm_g=fpart[1])
    n_conv, n_ret = conv_w_pw1.shape[0], ret_w_in.shape[0]
    for nm in ("conv_b_pw1", "conv_w_dw", "conv_b_dw", "conv_ln_g", "conv_ln_b", "conv_b_pw2"):
        sg[nm] = [None] * n_conv
    for nm in ("ret_gn_g", "ret_gn_b"):
        sg[nm] = [None] * n_ret

    half_idx = mc.astype(jnp.int32).reshape(1)
    chip_idx = chip.astype(jnp.int32).reshape(1)

    def chip_sums(keys, tag):
        flat = [G[nm][i] for nm, i in keys]
        sib = _swap_half(flat, name="swap_grads_" + tag)
        sums = [_add_half(a, b, half_idx, name="add_grads") for a, b in zip(flat, sib)]
        shapes = [jax.ShapeDtypeStruct((N_CHIPS - 1,) + s.shape[1:], BF16) for s in sums]
        return sums, shapes

    launch = {depth // 2 - 1: list(range(depth // 2, depth))}
    launch.update({l - 1: [l] for l in range(1, depth // 2)})
    early_keys = layer_weights(0)
    in_flight = []

    for l in reversed(range(depth)):
        j = l // 2
        sv = saved[l]
        if l in launch:
            keys = [k for ll in launch[l] for k in layer_weights(ll)]
            sums, shapes = chip_sums(keys, f"from{launch[l][0]}")
            started, tok = _exchange_start(sums, shapes, _plan_scatter, 3 * len(sums), fpart,
                                           name=f"scatter_start_{launch[l][0]}")
            in_flight.append((keys, started, launch[l][0]))
            sv["vec_b"] = sv["vec_b"] + tok[:, :1]
        w1, w2 = Wg["mlp_w1", l], wfull("mlp_w2", l)
        cs1 = w1.shape[2]
        da = _mm(dyb, w2, mode="nt", name="w2_dx", out_dtype=BF16, epi="mul2", extra=sv["ra"])
        gw2 = _mm(sv["p"], dyb, mode="tn", name="w2_dw")
        G["mlp_w2"][l] = gw2.reshape(N_CHIPS, gw2.shape[0] // N_CHIPS, gw2.shape[1])
        G["mlp_w1"][l] = _mm(sv["h2"], da, mode="tn", name="w1_dw", out3d=(N_CHIPS, cs1), tn=cs1)
        dh2 = _mm(da, w1, mode="nt", name="w1_dx", b3d=True, tk=cs1)
        dx, dyb, part = _row_bwd(sv["xb"], dh2, dx, sv["ymix"], sv["vec_b"], name="row_bwd")
        dmod[l][2], sg["norm_mlp_g"][l], dmod[l][3], dmod[l][4] = part[0], part[1], part[2], part[3]
        if l % 2 == 0:
            sg["conv_b_pw2"][j] = part[4]
            wp1, wp2 = Wg["conv_w_pw1", j], wfull("conv_w_pw2", j)
            csp = wp1.shape[2]
            dz = _mm(dyb, wp2, mode="nt", name="pw2_dx")
            gp2 = _mm(sv["z"], dyb, mode="tn", name="pw2_dw")
            G["conv_w_pw2"][j] = gp2.reshape(N_CHIPS, gp2.shape[0] // N_CHIPS, gp2.shape[1])
            du, dwdw, cpart, dbu = _conv_bwd(dz, sv["cv"], sv["v_glu"], sv["u"], wdw_pad(j), sv["cvec"],
                                             name="conv_bwd")
            sg["conv_w_dw"][j] = dwdw[:CONV_WIDTH]
            sg["conv_b_dw"][j], sg["conv_ln_g"][j], sg["conv_ln_b"][j] = cpart[0], cpart[1], cpart[2]
            sg["conv_b_pw1"][j] = dbu[0]
            G["conv_w_pw1"][j] = _mm(sv["h"], du, mode="tn", name="pw1_dw", out3d=(N_CHIPS, csp), tn=csp)
            dh = _mm(du, wp1, mode="nt", name="pw1_dx", b3d=True, tk=csp)
        else:
            wi, wo = Wg["ret_w_in", j], wfull("ret_w_out", j)
            csi = wi.shape[2]
            dy2 = _mm(dyb, wo, mode="nt", name="wout_dx")
            gwo = _mm(sv["y2"], dyb, mode="tn", name="wout_dw")
            G["ret_w_out"][j] = gwo.reshape(N_CHIPS, gwo.shape[0] // N_CHIPS, gwo.shape[1])
            dq, dkk, dvv, dgt, dgg, dgb = _ret_bwd(sv["proj"], cos_t, sin_t, tables, gn_g_full[j], gn_b_full[j],
                                                   sv["yr"], dy2, sv["states"], name="ret_bwd")
            sg["ret_gn_g"][j], sg["ret_gn_b"][j] = dgg.reshape(H, dv), dgb.reshape(H, dv)
            dproj = jnp.concatenate([dq, dkk, dvv, dgt], axis=1)
            G["ret_w_in"][j] = _mm(sv["h"], dproj, mode="tn", name="win_dw", out3d=(N_CHIPS, csi), tn=csi)
            dh = _mm(dproj, wi, mode="nt", name="win_dx", b3d=True, tk=csi)
        yp = saved[l - 1]["mo"] if l > 0 else dh
        dx, dyb, part = _row_bwd(sv["xa"], dh, dx, yp, sv["vec_a"], name="row_bwd")
        sg["norm_mix_g"][l], dmod[l][0], dmod[l][1] = part[1], part[2], part[3]
        if l > 0:
            dmod[l - 1][5] = part[0]
    grad_x = dx.reshape(x.shape)

    dmod_me = jnp.stack([jnp.stack(r) for r in dmod]).reshape(depth, 6 * D)
    sgrads = dict(ada_b=dmod_me)
    for nm in SMALL[1:]:
        sgrads[nm] = sg[nm] if nm == "final_norm_g" else jnp.stack(sg[nm])
    full_shapes = [sgrads[nm].shape for nm in SMALL]
    packed_all = _allgather8(_pack([sgrads[nm] for nm in SMALL]), name="gather_small_grads")
    sums = _unpack(_sum_slots(packed_all[None], name="sum_small_grads"), full_shapes)
    gsm = {}
    for nm, g in zip(SMALL, sums):
        if nm in SMALL_SHARDED:
            n = g.shape[-1] // N_CHIPS
            g = lax.dynamic_slice_in_dim(g.reshape(g.shape[:-1] + (N_CHIPS, n)), chip, 1, axis=g.ndim - 1)
            g = g.reshape(g.shape[:-2] + (n,))
        gsm[nm] = g.reshape(W[nm].shape)
    pw, pm, pv, pg = (_pack([t[nm] for nm in SMALL]) for t in (W, Mo, Vo, gsm))
    e4 = lambda a: a.reshape((1, 1) + a.shape)
    e3 = lambda a: a.reshape((1,) + a.shape)
    sres = _adamw(e4(pg), e3(pw), e3(pm), e3(pv), name="adamw_small")
    shard_shapes = [W[nm].shape for nm in SMALL]
    small_out = [dict(zip(SMALL, _unpack(r, shard_shapes))) for r in sres]

    n_mod_rows = depth * 6 * D // 128
    dmod_all = packed_all[:, :n_mod_rows].reshape(N_DEV, depth, N_CHIPS, cs_ada)
    dmod_cols = lax.dynamic_slice_in_dim(dmod_all, chip, 1, axis=2).reshape(N_DEV, depth * cs_ada)
    kpad = 128 - N_DEV
    dmod_pad = jnp.concatenate([dmod_cols, jnp.zeros((kpad, depth * cs_ada), F32)], axis=0)
    ct_pad = jnp.concatenate([c_all.T, jnp.zeros((D, kpad), F32)], axis=1)
    g_ada = _mm(ct_pad, dmod_pad, mode="nn", name="ada_dw", a_silu=True, out3d=(depth, cs_ada), tn=cs_ada)
    ada_out = _adamw(g_ada.reshape(depth, 1, D, cs_ada), ada_w, m_ada_w, v_ada_w, name="adamw_ada")

    sums, shapes = chip_sums(early_keys, "from0")
    early_lands = _exchange(sums, shapes, _plan_scatter, 3 * len(sums), name="scatter_grads")
    done = [(early_keys, sums, early_lands)]
    for keys, started, first in in_flight:
        own, lands = _exchange_wait(started, len(keys), _plan_scatter, early_lands[-1], name=f"scatter_wait_{first}")
        done.append((keys, own, lands))
    total = {}
    for keys, own, lands in done:
        for k, o, r in zip(keys, own, lands):
            total[k] = _sum_own(o, r, chip_idx, name="sum_grads")
    halves = [jnp.stack([total[nm, i] for i in range(len(G[nm]))]) for nm in BIG]
    sib_halves = _swap_sibling(halves, name="swap_totals")
    big_out = {nm: _adamw_halves(hm, hs, half_idx, W[nm], Mo[nm], Vo[nm], name="adamw_big")
               for nm, hm, hs in zip(BIG, halves, sib_halves)}

    def res(nm, i):
        if nm == "ada_w":
            return ada_out[i]
        if nm in big_out:
            return big_out[nm][i]
        return small_out[i][nm]

    return (loss, grad_x, *[res(nm, 0) for nm in WEIGHTS], *[res(nm, 1) for nm in WEIGHTS],
            *[res(nm, 2) for nm in WEIGHTS], *[res(nm, 3) for nm in WEIGHTS])
```

```python
import functools
import math

import jax
import jax.numpy as jnp
from jax import lax
from jax.experimental import pallas as pl
from jax.experimental.pallas import tpu as pltpu

F32 = jnp.float32
BF16 = jnp.bfloat16
MESH = pl.DeviceIdType.MESH

EPS = 1e-6
CHUNK = 64
CONV_WIDTH = 31
CONV_HALO = 32
SUBLANES = 8
LANES = 128
CONV_BLOCK_ROWS = 128
RET_HEADS = 4
ROPE_BASE = 10000.0
ADAM_LR = 0.001
ADAM_B1 = 0.9
ADAM_B2 = 0.999
ADAM_EPS = 1e-08
ADAM_WD = 0.01
ADAM_STEP = 10
N_CHIPS = 4
N_DEV = 8
V7X_VMEM_LIMIT = 48 * 1024 * 1024
ANY = pl.BlockSpec(memory_space=pl.ANY)


def _cparams(sem=None):
    return pltpu.CompilerParams(dimension_semantics=sem, vmem_limit_bytes=V7X_VMEM_LIMIT)


def _sigmoid(x):
    return jax.nn.sigmoid(x)


def _silu(x):
    return x * _sigmoid(x)


_DIMS = {
    "nn": (((1,), (0,)), ((), ())),
    "nt": (((1,), (1,)), ((), ())),
    "tn": (((0,), (0,)), ((), ())),
}


def _mm(a, b, *, mode, name, out_dtype=F32, tm=1024, tn=1024, tk=1024, b3d=False, out3d=None,
        bias=None, epi=None, extra=None, a_silu=False):
    if mode == "tn":
        K, M = a.shape
    else:
        M, K = a.shape
    if b3d:
        P, R, Cs = b.shape
        bshape = (R, P * Cs)
    else:
        bshape = b.shape
    N = bshape[0] if mode == "nt" else bshape[1]
    assert (bshape[1] if mode == "nt" else bshape[0]) == K, (name, a.shape, b.shape)
    tm, tn, tk = min(tm, M), min(tn, N), min(tk, K)
    assert M % tm == 0 and N % tn == 0 and K % tk == 0, (name, M, N, K, tm, tn, tk)
    nk = K // tk

    if mode == "tn":
        a_spec = pl.BlockSpec((tk, tm), lambda i, j, k: (k, i))
    else:
        a_spec = pl.BlockSpec((tm, tk), lambda i, j, k: (i, k))
    if mode == "nt":
        if b3d:
            nb = Cs // tk
            assert Cs % tk == 0
            b_spec = pl.BlockSpec((None, tn, tk), lambda i, j, k: (k // nb, j, k % nb))
        else:
            b_spec = pl.BlockSpec((tn, tk), lambda i, j, k: (j, k))
    else:
        if b3d:
            nb = Cs // tn
            assert Cs % tn == 0
            b_spec = pl.BlockSpec((None, tk, tn), lambda i, j, k: (j // nb, k, j % nb))
        else:
            b_spec = pl.BlockSpec((tk, tn), lambda i, j, k: (k, j))
    in_specs = [a_spec, b_spec]
    args = [a, b]
    if bias is not None:
        in_specs.append(pl.BlockSpec((1, tn), lambda i, j, k: (0, j)))
        args.append(bias)
    if extra is not None:
        in_specs.append(pl.BlockSpec((tm, tn), lambda i, j, k: (i, j)))
        args.append(extra)

    if out3d is not None:
        P_o, Cs_o = out3d
        assert P_o * Cs_o == N and Cs_o % tn == 0
        nbo = Cs_o // tn
        o_spec = pl.BlockSpec((None, tm, tn), lambda i, j, k: (j // nbo, i, j % nbo))
        o_shape = (P_o, M, Cs_o)
    else:
        o_spec = pl.BlockSpec((tm, tn), lambda i, j, k: (i, j))
        o_shape = (M, N)
    if epi == "relu2":
        out_shape = [jax.ShapeDtypeStruct(o_shape, BF16), jax.ShapeDtypeStruct(o_shape, BF16)]
        out_specs = [o_spec, o_spec]
    else:
        out_shape = jax.ShapeDtypeStruct(o_shape, out_dtype)
        out_specs = o_spec
    n_out = 2 if epi == "relu2" else 1
    dims = _DIMS[mode]
    has_bias, has_extra = bias is not None, extra is not None

    def body(*refs):
        a_ref, b_ref = refs[0], refs[1]
        pos = 2
        bias_ref = extra_ref = None
        if has_bias:
            bias_ref = refs[pos]
            pos += 1
        if has_extra:
            extra_ref = refs[pos]
            pos += 1
        outs = refs[pos:pos + n_out]
        acc_ref = refs[pos + n_out] if nk > 1 else None

        def partial():
            av = a_ref[...]
            if a_silu:
                av = _silu(av)
            return lax.dot_general(av, b_ref[...], dims, preferred_element_type=F32)

        def finish(r):
            if has_bias:
                r = r + bias_ref[...]
            if epi == "relu2":
                rr = jnp.maximum(r, 0.0)
                outs[0][...] = rr.astype(BF16)
                outs[1][...] = (rr * rr).astype(BF16)
            elif epi == "mul2":
                outs[0][...] = (r * 2.0 * extra_ref[...].astype(F32)).astype(outs[0].dtype)
            else:
                outs[0][...] = r.astype(outs[0].dtype)

        if nk == 1:
            finish(partial())
        else:
            k = pl.program_id(2)

            @pl.when(k == 0)
            def _():
                acc_ref[...] = jnp.zeros_like(acc_ref)

            acc_ref[...] += partial()

            @pl.when(k == nk - 1)
            def _():
                finish(acc_ref[...])

    return pl.pallas_call(
        body, name=name, grid=(M // tm, N // tn, nk), in_specs=in_specs, out_specs=out_specs,
        out_shape=out_shape,
        scratch_shapes=[pltpu.VMEM((tm, tn), F32)] if nk > 1 else [],
        compiler_params=_cparams(("parallel", "parallel", "arbitrary")),
    )(*args)


def _modnorm(x, gain, shift, scale):
    y = x * lax.rsqrt(jnp.mean(x * x, axis=-1, keepdims=True) + EPS)
    return (y * gain) * (1.0 + scale) + shift


def _row_fwd(xprev, y, vec, *, name, ts=512):
    S, D = xprev.shape
    ts = min(ts, S)
    has_res = y is not None
    row = pl.BlockSpec((ts, D), lambda i: (i, 0))
    vspec = pl.BlockSpec((8, D), lambda i: (0, 0))

    def body(*refs):
        if has_res:
            xp_ref, y_ref, v_ref, x_ref, h_ref = refs
            x = xp_ref[...] + v_ref[0:1, :] * y_ref[...]
            x_ref[...] = x
        else:
            xp_ref, v_ref, h_ref = refs
            x = xp_ref[...]
        h_ref[...] = _modnorm(x, v_ref[1:2, :], v_ref[2:3, :], v_ref[3:4, :]).astype(BF16)

    if has_res:
        return pl.pallas_call(
            body, name=name, grid=(S // ts,), in_specs=[row, row, vspec], out_specs=[row, row],
            out_shape=[jax.ShapeDtypeStruct((S, D), F32), jax.ShapeDtypeStruct((S, D), BF16)],
            compiler_params=_cparams(("parallel",)),
        )(xprev, y, vec)
    h = pl.pallas_call(
        body, name=name, grid=(S // ts,), in_specs=[row, vspec], out_specs=row,
        out_shape=jax.ShapeDtypeStruct((S, D), BF16),
        compiler_params=_cparams(("parallel",)),
    )(xprev, vec)
    return xprev, h


def _row_bwd(xin, dh, dxout, yprev, vec, *, name, ts=512):
    S, D = xin.shape
    ts = min(ts, S)
    row = pl.BlockSpec((ts, D), lambda i: (i, 0))
    vspec = pl.BlockSpec((8, D), lambda i: (0, 0))

    def body(x_ref, dh_ref, dx_ref, y_ref, v_ref, dxin_ref, dy_ref, part_ref):
        @pl.when(pl.program_id(0) == 0)
        def _():
            part_ref[...] = jnp.zeros_like(part_ref)

        gate = v_ref[0:1, :]
        _, vjp = jax.vjp(_modnorm, x_ref[...], v_ref[1:2, :], v_ref[2:3, :], v_ref[3:4, :])
        dxn, dgain, dshift, dscale = vjp(dh_ref[...].astype(F32))
        dxin = dx_ref[...] + dxn
        dxin_ref[...] = dxin
        dy = dxin * gate
        dy_ref[...] = dy.astype(BF16)
        part_ref[0:1, :] += jnp.sum(dxin * y_ref[...], axis=0, keepdims=True)
        part_ref[1:2, :] += dgain
        part_ref[2:3, :] += dshift
        part_ref[3:4, :] += dscale
        part_ref[4:5, :] += jnp.sum(dy, axis=0, keepdims=True)

    return pl.pallas_call(
        body, name=name, grid=(S // ts,), in_specs=[row, row, row, row, vspec],
        out_specs=[row, row, vspec],
        out_shape=[jax.ShapeDtypeStruct((S, D), F32), jax.ShapeDtypeStruct((S, D), BF16),
                   jax.ShapeDtypeStruct((8, D), F32)],
        compiler_params=_cparams(("arbitrary",)),
    )(xin, dh, dxout, yprev, vec)


def _final(xprev, y, target, vec, *, name, ts=512):
    S, D = xprev.shape
    ts = min(ts, S)
    row = pl.BlockSpec((ts, D), lambda i: (i, 0))
    vspec = pl.BlockSpec((8, D), lambda i: (0, 0))

    def norm(x, gain):
        return x * lax.rsqrt(jnp.mean(x * x, axis=-1, keepdims=True) + EPS) * gain

    def body(xp_ref, y_ref, t_ref, v_ref, dx_ref, dy_ref, part_ref):
        @pl.when(pl.program_id(0) == 0)
        def _():
            part_ref[...] = jnp.zeros_like(part_ref)

        gate = v_ref[0:1, :]
        yv = y_ref[...]
        x = xp_ref[...] + gate * yv
        out, vjp = jax.vjp(norm, x, v_ref[1:2, :])
        err = out - t_ref[...]
        dx, dgain = vjp(err * (1.0 / D))
        dx_ref[...] = dx
        dy_ref[...] = (dx * gate).astype(BF16)
        part_ref[0:1, :] += jnp.sum(dx * yv, axis=0, keepdims=True)
        part_ref[1:2, :] += dgain
        part_ref[2:3, :] += jnp.sum(err * err, axis=0, keepdims=True) * (0.5 / D)

    return pl.pallas_call(
        body, name=name, grid=(S // ts,), in_specs=[row, row, row, vspec], out_specs=[row, row, vspec],
        out_shape=[jax.ShapeDtypeStruct((S, D), F32), jax.ShapeDtypeStruct((S, D), BF16),
                   jax.ShapeDtypeStruct((8, D), F32)],
        compiler_params=_cparams(("arbitrary",)),
    )(xprev, y, target, vec)


def _ln_silu(cv, g, b):
    mu = jnp.mean(cv, axis=-1, keepdims=True)
    var = jnp.mean(jnp.square(cv - mu), axis=-1, keepdims=True)
    u = (cv - mu) * lax.rsqrt(var + EPS) * g + b
    return _silu(u)


def _shift_copies(ext, sh, n):
    for b in range(1, SUBLANES):
        sh[b - 1, 0:n, :] = ext[pl.ds(b, n), :]


def _shifted(ext, sh, off, r0, rows, cols):
    a, b = divmod(off, SUBLANES)
    if b == 0:
        return ext[pl.ds(SUBLANES * a + r0, rows), cols]
    return sh[b - 1, pl.ds(SUBLANES * a + r0, rows), cols]


def _conv_fwd(u, wdw, vec, *, name, ts=256):
    S, D2 = u.shape
    D = D2 // 2
    ts = min(ts, S)
    H = CONV_HALO
    row = pl.BlockSpec((ts, D), lambda i: (i, 0))

    rb_rows = min(CONV_BLOCK_ROWS, ts)

    def body(u_ref, w_ref, v_ref, vo_ref, cv_ref, z_ref, ext, sh):
        @pl.when(pl.program_id(0) == 0)
        def _():
            ext[0:H, :] = jnp.zeros((H, D), F32)

        uu = u_ref[...]
        v = uu[:, :D] * _sigmoid(uu[:, D:])
        vo_ref[...] = v
        ext[H:H + ts, :] = v
        _shift_copies(ext, sh, ts + H - 8)
        for r0 in range(0, ts, rb_rows):
            for c0 in range(0, D, LANES):
                cols = pl.ds(c0, LANES)
                acc = jnp.zeros((rb_rows, LANES), F32)
                for t in range(CONV_WIDTH):
                    src = _shifted(ext, sh, H - (CONV_WIDTH - 1) + t, r0, rb_rows, cols)
                    acc = acc + src * w_ref[pl.ds(t, 1), cols]
                cv_ref[pl.ds(r0, rb_rows), cols] = acc + v_ref[0:1, cols]
        z_ref[...] = _ln_silu(cv_ref[...], v_ref[1:2, :], v_ref[2:3, :]).astype(BF16)
        ext[0:H, :] = ext[ts:ts + H, :]

    return pl.pallas_call(
        body, name=name, grid=(S // ts,),
        in_specs=[pl.BlockSpec((ts, D2), lambda i: (i, 0)), pl.BlockSpec((H, D), lambda i: (0, 0)),
                  pl.BlockSpec((8, D), lambda i: (0, 0))],
        out_specs=[row, row, row],
        out_shape=[jax.ShapeDtypeStruct((S, D), F32), jax.ShapeDtypeStruct((S, D), F32),
                   jax.ShapeDtypeStruct((S, D), BF16)],
        scratch_shapes=[pltpu.VMEM((ts + H, D), F32), pltpu.VMEM((7, ts + H - 8, D), F32)],
        compiler_params=_cparams(("arbitrary",)),
    )(u, wdw, vec)


def _conv_bwd(dz, cv, v, u, wdw, vec, *, name, ts=256):
    S, D = cv.shape
    ts = min(ts, S)
    H = CONV_HALO
    nt = S // ts
    per = ts // H
    rev = lambda i: (nt - 1 - i, 0)
    row = pl.BlockSpec((ts, D), rev)

    rb_rows = min(CONV_BLOCK_ROWS, ts)
    nsh = ts + H - 8

    def body(dz_ref, cv_ref, v_ref, vh_ref, u_ref, w_ref, vec_ref, du_ref, dw_ref, part_ref, dbu_ref,
             dext, vext, dsh, vsh, dwacc, dvbuf):
        i = pl.program_id(0)

        @pl.when(i == 0)
        def _():
            dext[ts:ts + H, :] = jnp.zeros((H, D), F32)
            dwacc[...] = jnp.zeros_like(dwacc)
            part_ref[...] = jnp.zeros_like(part_ref)
            dbu_ref[...] = jnp.zeros_like(dbu_ref)

        _, vjp = jax.vjp(_ln_silu, cv_ref[...], vec_ref[1:2, :], vec_ref[2:3, :])
        dcv, dg, db = vjp(dz_ref[...])
        part_ref[0:1, :] += jnp.sum(dcv, axis=0, keepdims=True)
        part_ref[1:2, :] += dg
        part_ref[2:3, :] += db
        dext[0:ts, :] = dcv
        vext[0:H, :] = vh_ref[...] * jnp.where(i == nt - 1, 0.0, 1.0)
        vext[H:H + ts, :] = v_ref[...]
        _shift_copies(dext, dsh, nsh)
        _shift_copies(vext, vsh, nsh)
        for r0 in range(0, ts, rb_rows):
            for c0 in range(0, D, LANES):
                cols = pl.ds(c0, LANES)
                dblk = dext[pl.ds(r0, rb_rows), cols]
                dv = jnp.zeros((rb_rows, LANES), F32)
                for t in range(CONV_WIDTH):
                    prod = dblk * _shifted(vext, vsh, H - (CONV_WIDTH - 1) + t, r0, rb_rows, cols)
                    parts = [prod[s:s + SUBLANES, :] for s in range(0, rb_rows, SUBLANES)]
                    while len(parts) > 1:
                        parts = [parts[k] + parts[k + 1] for k in range(0, len(parts), 2)]
                    dwacc[pl.ds(t * SUBLANES, SUBLANES), cols] += parts[0]
                    dv = dv + _shifted(dext, dsh, CONV_WIDTH - 1 - t, r0, rb_rows, cols) * w_ref[pl.ds(t, 1), cols]
                dvbuf[pl.ds(r0, rb_rows), cols] = dv
        dv = dvbuf[...]
        uu = u_ref[...]
        a, g = uu[:, :D], uu[:, D:]
        sg = _sigmoid(g)
        da = dv * sg
        dg_ = dv * a * sg * (1.0 - sg)
        du = jnp.concatenate([da, dg_], axis=-1)
        du_ref[...] = du.astype(BF16)
        dbu_ref[0:1, :] += jnp.sum(du, axis=0, keepdims=True)
        dext[ts:ts + H, :] = dext[0:H, :]

        @pl.when(i == nt - 1)
        def _():
            dw_ref[...] = jnp.zeros_like(dw_ref)
            for t in range(CONV_WIDTH):
                dw_ref[pl.ds(t, 1), :] = jnp.sum(dwacc[pl.ds(t * SUBLANES, SUBLANES), :], axis=0, keepdims=True)

    return pl.pallas_call(
        body, name=name, grid=(nt,),
        in_specs=[row, row, row,
                  pl.BlockSpec((H, D), lambda i: (jnp.maximum((nt - 1 - i) * per - 1, 0), 0)),
                  pl.BlockSpec((ts, 2 * D), rev), pl.BlockSpec((H, D), lambda i: (0, 0)),
                  pl.BlockSpec((8, D), lambda i: (0, 0))],
        out_specs=[pl.BlockSpec((ts, 2 * D), rev), pl.BlockSpec((H, D), lambda i: (0, 0)),
                   pl.BlockSpec((8, D), lambda i: (0, 0)), pl.BlockSpec((8, 2 * D), lambda i: (0, 0))],
        out_shape=[jax.ShapeDtypeStruct((S, 2 * D), BF16), jax.ShapeDtypeStruct((H, D), F32),
                   jax.ShapeDtypeStruct((8, D), F32), jax.ShapeDtypeStruct((8, 2 * D), F32)],
        scratch_shapes=[pltpu.VMEM((ts + H, D), F32), pltpu.VMEM((ts + H, D), F32),
                        pltpu.VMEM((7, nsh, D), F32), pltpu.VMEM((7, nsh, D), F32),
                        pltpu.VMEM((CONV_WIDTH * SUBLANES, D), F32), pltpu.VMEM((ts, D), F32)],
        compiler_params=_cparams(("arbitrary",)),
    )(dz, cv, v, v, u, wdw, vec)


def _rope(x, c, s, half):
    x1, x2 = x[:, :half], x[:, half:]
    return jnp.concatenate([x1 * c - x2 * s, x2 * c + x1 * s], axis=-1)


def _rope_t(d, c, s, half):
    d1, d2 = d[:, :half], d[:, half:]
    return jnp.concatenate([d1 * c + d2 * s, d2 * c - d1 * s], axis=-1)


def _gn_gate(y, gate, g, b):
    mu = jnp.mean(y, axis=-1, keepdims=True)
    var = jnp.mean(jnp.square(y - mu), axis=-1, keepdims=True)
    return _silu(gate) * ((y - mu) * lax.rsqrt(var + EPS) * g + b)


def _dot(a, b, mode="nn"):
    return lax.dot_general(a, b, _DIMS[mode], preferred_element_type=F32)


def _ret_tables(H):
    lg = jnp.log(1.0 - 2.0 ** (-5.0 - jnp.arange(H, dtype=F32)))
    idx = jnp.arange(CHUNK, dtype=F32)
    dmat = jnp.exp(lg[:, None, None] * jnp.abs(idx[:, None] - idx[None, :]))
    xi = jnp.exp(lg[:, None] * (idx + 1.0))[..., None]
    zeta = jnp.exp(lg[:, None] * (CHUNK - 1.0 - idx))[..., None]
    dec = jnp.exp(lg * CHUNK)[:, None, None]
    return dmat, xi, zeta, dec


RET_HEADS_PER_STEP = 4


def _ret_specs(R, dk, dv, half, hps, order):
    C = CHUNK
    ng = RET_HEADS // hps
    return dict(
        q=pl.BlockSpec((R, hps * dk), lambda h, n: (order(n), h)),
        k=pl.BlockSpec((R, hps * dk), lambda h, n: (order(n), ng + h)),
        v=pl.BlockSpec((R, hps * dv), lambda h, n: (order(n), ng + h)),
        gate=pl.BlockSpec((R, hps * dv), lambda h, n: (order(n), 2 * ng + h)),
        rope=pl.BlockSpec((R, half), lambda h, n: (order(n), 0)),
        dmat=pl.BlockSpec((hps, C, C), lambda h, n: (h, 0, 0)),
        col=pl.BlockSpec((hps, C, 1), lambda h, n: (h, 0, 0)),
        one=pl.BlockSpec((hps, 1, 1), lambda h, n: (h, 0, 0)),
        gn=pl.BlockSpec((hps, 1, dv), lambda h, n: (h, 0, 0)),
        yv=pl.BlockSpec((R, hps * dv), lambda h, n: (order(n), h)),
        yk=pl.BlockSpec((R, hps * dk), lambda h, n: (order(n), h)),
    )


def _ret_fwd(proj, cos, sin, tables, gn_g, gn_b, *, name, cps=4):
    S = proj.shape[0]
    D = proj.shape[1] // 6
    H, C, hps = RET_HEADS, CHUNK, RET_HEADS_PER_STEP
    dk, dv, half = D // H, 2 * D // H, D // H // 2
    nc = S // C
    cps = min(cps, nc)
    R = cps * C
    scale = dk ** -0.5
    sp = _ret_specs(R, dk, dv, half, hps, lambda n: n)
    dmat, xi, zeta, dec = tables

    def body(q_ref, k_ref, v_ref, g_ref, cos_ref, sin_ref, dm_ref, xi_ref, ze_ref, dec_ref, gg_ref, gb_ref,
             y_ref, y2_ref, st_ref, state):
        @pl.when(pl.program_id(1) == 0)
        def _():
            state[...] = jnp.zeros_like(state)

        for j in range(cps):
            rows = pl.ds(j * C, C)
            cs, sn = cos_ref[rows, :], sin_ref[rows, :]
            for hh in range(hps):
                ck, cv = pl.ds(hh * dk, dk), pl.ds(hh * dv, dv)
                dm, xv, zv, dc = dm_ref[hh], xi_ref[hh], ze_ref[hh], dec_ref[hh]
                qr = _rope(q_ref[rows, ck].astype(F32), cs, sn, half)
                kr = _rope(k_ref[rows, ck].astype(F32), cs, sn, half) * scale
                vb = v_ref[rows, cv]
                p = (_dot(qr.astype(BF16), kr.astype(BF16), "nt") * dm).astype(BF16)
                st = state[hh]
                stb = st.astype(BF16)
                st_ref[hh, j] = stb
                y = _dot(p, vb) + _dot((qr * xv).astype(BF16), stb)
                state[hh] = st * dc + _dot((kr * zv).astype(BF16), vb, "tn")
                y_ref[rows, cv] = y
                y2_ref[rows, cv] = _gn_gate(y, g_ref[rows, cv].astype(F32), gg_ref[hh], gb_ref[hh]).astype(BF16)

    return pl.pallas_call(
        body, name=name, grid=(H // hps, nc // cps),
        in_specs=[sp["q"], sp["k"], sp["v"], sp["gate"], sp["rope"], sp["rope"], sp["dmat"], sp["col"],
                  sp["col"], sp["one"], sp["gn"], sp["gn"]],
        out_specs=[sp["yv"], sp["yv"], pl.BlockSpec((hps, cps, dk, dv), lambda h, n: (h, n, 0, 0))],
        out_shape=[jax.ShapeDtypeStruct((S, 2 * D), F32), jax.ShapeDtypeStruct((S, 2 * D), BF16),
                   jax.ShapeDtypeStruct((H, nc, dk, dv), BF16)],
        scratch_shapes=[pltpu.VMEM((hps, dk, dv), F32)],
        compiler_params=_cparams(("arbitrary", "arbitrary")),
    )(proj, proj, proj, proj, cos, sin, dmat, xi, zeta, dec, gn_g, gn_b)


def _ret_bwd(proj, cos, sin, tables, gn_g, gn_b, y, dy2, states, *, name, cps=4):
    S = proj.shape[0]
    D = proj.shape[1] // 6
    H, C, hps = RET_HEADS, CHUNK, RET_HEADS_PER_STEP
    dk, dv, half = D // H, 2 * D // H, D // H // 2
    nc = S // C
    cps = min(cps, nc)
    ns = nc // cps
    R = cps * C
    scale = dk ** -0.5
    order = lambda n: ns - 1 - n
    sp = _ret_specs(R, dk, dv, half, hps, order)
    dmat, xi, zeta, dec = tables

    def body(q_ref, k_ref, v_ref, g_ref, cos_ref, sin_ref, dm_ref, xi_ref, ze_ref, dec_ref, gg_ref, gb_ref,
             y_ref, dy2_ref, st_ref, dq_ref, dk_ref, dv_ref, dg_ref, dgg_ref, dgb_ref, gst):
        @pl.when(pl.program_id(1) == 0)
        def _():
            gst[...] = jnp.zeros_like(gst)
            dgg_ref[...] = jnp.zeros_like(dgg_ref)
            dgb_ref[...] = jnp.zeros_like(dgb_ref)

        for j in reversed(range(cps)):
            rows = pl.ds(j * C, C)
            cs, sn = cos_ref[rows, :], sin_ref[rows, :]
            for hh in range(hps):
                ck, cv = pl.ds(hh * dk, dk), pl.ds(hh * dv, dv)
                dm, xv, zv, dc = dm_ref[hh], xi_ref[hh], ze_ref[hh], dec_ref[hh]
                _, vjp = jax.vjp(_gn_gate, y_ref[rows, cv], g_ref[rows, cv].astype(F32), gg_ref[hh], gb_ref[hh])
                dy, dgate, dgg, dgb = vjp(dy2_ref[rows, cv])
                dgg_ref[hh] += dgg
                dgb_ref[hh] += dgb
                dg_ref[rows, cv] = dgate.astype(BF16)
                dyb = dy.astype(BF16)
                qr = _rope(q_ref[rows, ck].astype(F32), cs, sn, half)
                kr = _rope(k_ref[rows, ck].astype(F32), cs, sn, half) * scale
                qb, kb, vb = qr.astype(BF16), kr.astype(BF16), v_ref[rows, cv]
                p = (_dot(qb, kb, "nt") * dm).astype(BF16)
                g = gst[hh]
                gb16 = g.astype(BF16)
                sprev = st_ref[hh, j]
                dvv = _dot(p, dyb, "tn") + _dot((kr * zv).astype(BF16), gb16)
                dpb = (_dot(dyb, vb, "nt") * dm).astype(BF16)
                dqr = _dot(dpb, kb) + _dot(dyb, sprev, "nt") * xv
                dkr = _dot(dpb, qb, "tn") + _dot(vb, gb16, "nt") * zv
                gst[hh] = g * dc + _dot((qr * xv).astype(BF16), dyb, "tn")
                dq_ref[rows, ck] = _rope_t(dqr, cs, sn, half).astype(BF16)
                dk_ref[rows, ck] = _rope_t(dkr * scale, cs, sn, half).astype(BF16)
                dv_ref[rows, cv] = dvv.astype(BF16)

    return pl.pallas_call(
        body, name=name, grid=(H // hps, ns),
        in_specs=[sp["q"], sp["k"], sp["v"], sp["gate"], sp["rope"], sp["rope"], sp["dmat"], sp["col"],
                  sp["col"], sp["one"], sp["gn"], sp["gn"], sp["yv"], sp["yv"],
                  pl.BlockSpec((hps, cps, dk, dv), lambda h, n: (h, order(n), 0, 0))],
        out_specs=[sp["yk"], sp["yk"], sp["yv"], sp["yv"], sp["gn"], sp["gn"]],
        out_shape=[jax.ShapeDtypeStruct((S, D), BF16), jax.ShapeDtypeStruct((S, D), BF16),
                   jax.ShapeDtypeStruct((S, 2 * D), BF16), jax.ShapeDtypeStruct((S, 2 * D), BF16),
                   jax.ShapeDtypeStruct((H, 1, dv), F32), jax.ShapeDtypeStruct((H, 1, dv), F32)],
        scratch_shapes=[pltpu.VMEM((hps, dk, dv), F32)],
        compiler_params=_cparams(("arbitrary", "arbitrary")),
    )(proj, proj, proj, proj, cos, sin, dmat, xi, zeta, dec, gn_g, gn_b, y, dy2, states)


def _rows_tile(rows, cols, n_arrays):
    cap = max(8, V7X_VMEM_LIMIT // 3 // (n_arrays * 2 * 4 * cols))
    t = rows
    while t > cap and t % 2 == 0:
        t //= 2
    return t


def _add_half(g, r, half_idx, *, name):
    P, R, Cc = g.shape
    hR = R // 2
    tr = _rows_tile(hR, Cc, 3)
    nb = hR // tr

    def body(h_ref, g_ref, r_ref, o_ref):
        o_ref[...] = (g_ref[...] + r_ref[...]).astype(BF16)

    return pl.pallas_call(
        body, name=name,
        grid_spec=pltpu.PrefetchScalarGridSpec(
            num_scalar_prefetch=1, grid=(P, nb),
            in_specs=[pl.BlockSpec((None, tr, Cc), lambda s, i, h: (s, h[0] * nb + i, 0)),
                      pl.BlockSpec((None, tr, Cc), lambda s, i, h: (s, i, 0))],
            out_specs=pl.BlockSpec((None, tr, Cc), lambda s, i, h: (s, i, 0))),
        out_shape=jax.ShapeDtypeStruct((P, hR, Cc), BF16), compiler_params=_cparams(("parallel", "parallel")),
    )(half_idx, g, r)


def _sum_slots(x, *, name):
    L, NS, R, Cc = x.shape
    tr = _rows_tile(R, Cc, NS + 1)

    def body(x_ref, o_ref):
        acc = x_ref[0].astype(F32)
        for s in range(1, NS):
            acc = acc + x_ref[s].astype(F32)
        o_ref[...] = acc

    return pl.pallas_call(
        body, name=name, grid=(L, R // tr),
        in_specs=[pl.BlockSpec((None, NS, tr, Cc), lambda l, i: (l, 0, i, 0))],
        out_specs=pl.BlockSpec((None, tr, Cc), lambda l, i: (l, i, 0)),
        out_shape=jax.ShapeDtypeStruct((L, R, Cc), F32), compiler_params=_cparams(("parallel", "parallel")),
    )(x)


def _adam_store(g, w_ref, m_ref, v_ref, go_ref, d_ref, mo_ref, vo_ref):
    mn = ADAM_B1 * m_ref[...] + (1.0 - ADAM_B1) * g
    vn = ADAM_B2 * v_ref[...] + (1.0 - ADAM_B2) * jnp.square(g)
    m_hat = mn / (1.0 - ADAM_B1 ** ADAM_STEP)
    v_hat = vn / (1.0 - ADAM_B2 ** ADAM_STEP)
    go_ref[...] = g
    d_ref[...] = -ADAM_LR * (m_hat / (jnp.sqrt(v_hat) + ADAM_EPS) + ADAM_WD * w_ref[...])
    mo_ref[...] = mn
    vo_ref[...] = vn


def _adamw(gslots, w, m, v, *, name):
    L, NS, R, Cc = gslots.shape
    tr = _rows_tile(R, Cc, NS + 7)
    gspec = pl.BlockSpec((None, NS, tr, Cc), lambda l, i: (l, 0, i, 0))
    spec = pl.BlockSpec((None, tr, Cc), lambda l, i: (l, i, 0))

    def body(g_ref, *refs):
        g = g_ref[0]
        for s in range(1, NS):
            g = g + g_ref[s]
        _adam_store(g, *refs)

    sd = jax.ShapeDtypeStruct((L, R, Cc), F32)
    return pl.pallas_call(
        body, name=name, grid=(L, R // tr), in_specs=[gspec, spec, spec, spec],
        out_specs=[spec, spec, spec, spec], out_shape=[sd, sd, sd, sd],
        compiler_params=_cparams(("parallel", "parallel")),
    )(gslots, w, m, v)


def _adamw_halves(g_mine, g_sib, half_idx, w, m, v, *, name):
    L, hR, Cc = g_mine.shape
    tr = _rows_tile(hR, Cc, 9)
    nbh = hR // tr
    gspec = pl.BlockSpec((None, tr, Cc), lambda l, i, h: (l, i % nbh, 0))
    spec = pl.BlockSpec((None, tr, Cc), lambda l, i, h: (l, i, 0))

    def body(h_ref, gm_ref, gs_ref, *refs):
        mine = (pl.program_id(1) // nbh) == h_ref[0]
        _adam_store(jnp.where(mine, gm_ref[...], gs_ref[...]), *refs)

    sd = jax.ShapeDtypeStruct((L, 2 * hR, Cc), F32)
    return pl.pallas_call(
        body, name=name,
        grid_spec=pltpu.PrefetchScalarGridSpec(
            num_scalar_prefetch=1, grid=(L, 2 * nbh), in_specs=[gspec, gspec, spec, spec, spec],
            out_specs=[spec, spec, spec, spec]),
        out_shape=[sd, sd, sd, sd], compiler_params=_cparams(("parallel", "parallel")),
    )(half_idx, g_mine, g_sib, w, m, v)


def _me():
    return lax.axis_index("x"), lax.axis_index("y"), lax.axis_index("c")


def _flip(v, bit):
    return 1 - v if bit else v


def _allgather8(x, *, name):
    def body(x_ref, out_ref, send_sems, recv_sems, loc_sem):
        mx, my, mc = _me()
        me = 4 * mx + 2 * my + mc
        loc = pltpu.make_async_copy(x_ref, out_ref.at[me], loc_sem)
        loc.start()
        sends, recvs = [], []
        for k in range(1, N_DEV):
            px, py, pc = _flip(mx, k & 4), _flip(my, k & 2), _flip(mc, k & 1)
            sends.append(pltpu.make_async_remote_copy(
                src_ref=x_ref, dst_ref=out_ref.at[me], send_sem=send_sems.at[k - 1],
                recv_sem=recv_sems.at[k - 1], device_id=(px, py, pc), device_id_type=MESH))
            recvs.append(pltpu.make_async_remote_copy(
                src_ref=x_ref, dst_ref=out_ref.at[4 * px + 2 * py + pc], send_sem=send_sems.at[k - 1],
                recv_sem=recv_sems.at[k - 1], device_id=(px, py, pc), device_id_type=MESH))
        for cp in sends:
            cp.start()
        for cp in recvs:
            cp.wait_recv()
        for cp in sends:
            cp.wait_send()
        loc.wait()

    return pl.pallas_call(
        body, name=name, in_specs=[ANY], out_specs=ANY,
        out_shape=jax.ShapeDtypeStruct((N_DEV,) + x.shape, x.dtype),
        scratch_shapes=[pltpu.SemaphoreType.DMA((N_DEV - 1,)), pltpu.SemaphoreType.DMA((N_DEV - 1,)),
                        pltpu.SemaphoreType.DMA],
    )(x)


def _gather_chips(arrays, after, *, name):
    n = len(arrays)

    def body(*refs):
        ins, outs = refs[:n], refs[n + 1:2 * n + 1]
        ici_send, ici_recv, d2d_send, d2d_recv, own_send, own_recv = refs[2 * n + 1:]
        mx, my, mc = _me()
        me = 2 * mx + my
        sib = (mx, my, 1 - mc)
        locs, sends, lands, passes, gifts = [], [], [], [], []
        for a in range(n):
            h = arrays[a].shape[0] // 2
            mine, other = pl.ds(mc * h, h), pl.ds((1 - mc) * h, h)
            locs.append(pltpu.make_async_remote_copy(
                src_ref=ins[a], dst_ref=outs[a].at[me], send_sem=own_send.at[a], recv_sem=own_recv.at[a],
                device_id=sib, device_id_type=MESH))
            for k in range(1, N_CHIPS):
                px, py = _flip(mx, k & 2), _flip(my, k & 1)
                peer = 2 * px + py
                ici = dict(send_sem=ici_send.at[a, k - 1], recv_sem=ici_recv.at[a, k - 1],
                           device_id=(px, py, mc), device_id_type=MESH)
                d2d = dict(send_sem=d2d_send.at[a, k - 1], recv_sem=d2d_recv.at[a, k - 1],
                           device_id=sib, device_id_type=MESH)
                sends.append(pltpu.make_async_remote_copy(
                    src_ref=ins[a].at[mine], dst_ref=outs[a].at[me, mine], **ici))
                lands.append(pltpu.make_async_remote_copy(
                    src_ref=ins[a].at[mine], dst_ref=outs[a].at[peer, mine], **ici))
                passes.append(pltpu.make_async_remote_copy(
                    src_ref=outs[a].at[peer, mine], dst_ref=outs[a].at[peer, mine], **d2d))
                gifts.append(pltpu.make_async_remote_copy(
                    src_ref=outs[a].at[peer, other], dst_ref=outs[a].at[peer, other], **d2d))
        for cp in locs + sends:
            cp.start()
        for land, fwd in zip(lands, passes):
            land.wait_recv()
            fwd.start()
        for cp in gifts + locs:
            cp.wait_recv()
        for cp in sends + passes + locs:
            cp.wait_send()

    nsem = (n, N_CHIPS - 1)
    return pl.pallas_call(
        body, name=name, in_specs=[ANY] * (n + 1), out_specs=[ANY] * n,
        out_shape=[jax.ShapeDtypeStruct((N_CHIPS,) + a.shape, a.dtype) for a in arrays],
        scratch_shapes=[pltpu.SemaphoreType.DMA(nsem), pltpu.SemaphoreType.DMA(nsem), pltpu.SemaphoreType.DMA(nsem),
                        pltpu.SemaphoreType.DMA(nsem), pltpu.SemaphoreType.DMA((n,)), pltpu.SemaphoreType.DMA((n,))],
    )(*arrays, after)


def _swap_half(arrays, *, name):
    n = len(arrays)

    def body(*refs):
        ins, outs = refs[:n], refs[n:2 * n]
        send_sems, recv_sems = refs[2 * n:]
        mx, my, mc = _me()
        cps = []
        for a in range(n):
            P, R, _ = arrays[a].shape
            cps.append(pltpu.make_async_remote_copy(
                src_ref=ins[a].at[pl.ds(0, P), pl.ds((1 - mc) * (R // 2), R // 2)], dst_ref=outs[a],
                send_sem=send_sems.at[a], recv_sem=recv_sems.at[a],
                device_id=(mx, my, 1 - mc), device_id_type=MESH))
        for cp in cps:
            cp.start()
        for cp in cps:
            cp.wait_recv()
        for cp in cps:
            cp.wait_send()

    return pl.pallas_call(
        body, name=name, in_specs=[ANY] * n, out_specs=[ANY] * n,
        out_shape=[jax.ShapeDtypeStruct((a.shape[0], a.shape[1] // 2, a.shape[2]), a.dtype) for a in arrays],
        scratch_shapes=[pltpu.SemaphoreType.DMA((n,)), pltpu.SemaphoreType.DMA((n,))],
    )(*arrays)


def _swap_sibling(arrays, *, name):
    n = len(arrays)

    def body(*refs):
        ins, outs = refs[:n], refs[n:2 * n]
        send_sems, recv_sems = refs[2 * n:]
        mx, my, mc = _me()
        cps = [pltpu.make_async_remote_copy(
            src_ref=ins[a], dst_ref=outs[a], send_sem=send_sems.at[a], recv_sem=recv_sems.at[a],
            device_id=(mx, my, 1 - mc), device_id_type=MESH) for a in range(n)]
        for cp in cps:
            cp.start()
        for cp in cps:
            cp.wait_recv()
        for cp in cps:
            cp.wait_send()

    return pl.pallas_call(
        body, name=name, in_specs=[ANY] * n, out_specs=[ANY] * n,
        out_shape=[jax.ShapeDtypeStruct(a.shape, a.dtype) for a in arrays],
        scratch_shapes=[pltpu.SemaphoreType.DMA((n,)), pltpu.SemaphoreType.DMA((n,))],
    )(*arrays)


def _plan_scatter(srcs, lands):
    mx, my, mc = _me()
    copies = []
    for a in range(len(srcs)):
        for k in range(1, N_CHIPS):
            px, py = _flip(mx, k & 2), _flip(my, k & 1)
            copies.append((srcs[a].at[2 * px + py], lands[a].at[k - 1], lands[a].at[k - 1], (px, py, mc)))
    return copies


def _plan_gather(srcs, lands):
    mx, my, mc = _me()
    me = 2 * mx + my
    copies = []
    for a in range(len(srcs)):
        copies.append((srcs[a], lands[a].at[me], lands[a].at[me], (mx, my, 1 - mc)))
        for k in range(1, N_CHIPS):
            px, py = _flip(mx, k & 2), _flip(my, k & 1)
            copies.append((srcs[a], lands[a].at[me], lands[a].at[2 * px + py], (px, py, mc)))
    return copies


def _copy(c, k, send_sems, recv_sems, landing=False):
    src, dst, land, dev = c
    return pltpu.make_async_remote_copy(src_ref=src, dst_ref=land if landing else dst, send_sem=send_sems.at[k],
                                        recv_sem=recv_sems.at[k], device_id=dev, device_id_type=MESH)


def _exchange(srcs, land_shapes, plan, ncopies, *, name):
    ni, nl = len(srcs), len(land_shapes)

    def body(*refs):
        send_sems, recv_sems = refs[ni + nl:]
        copies = plan(refs[:ni], refs[ni:ni + nl])
        for k, c in enumerate(copies):
            _copy(c, k, send_sems, recv_sems).start()
        for k, c in enumerate(copies):
            _copy(c, k, send_sems, recv_sems, landing=True).wait_recv()
        for k, c in enumerate(copies):
            _copy(c, k, send_sems, recv_sems).wait_send()

    return pl.pallas_call(
        body, name=name, in_specs=[ANY] * ni, out_specs=[ANY] * nl, out_shape=list(land_shapes),
        scratch_shapes=[pltpu.SemaphoreType.DMA((ncopies,)), pltpu.SemaphoreType.DMA((ncopies,))],
    )(*srcs)


HBM_SPEC = pl.BlockSpec(memory_space=pltpu.HBM)
SEM_SPEC = pl.BlockSpec(memory_space=pltpu.SEMAPHORE)
SPLIT_EFFECT = pltpu.SideEffectType.DATAFLOW_SIDE_EFFECTING


def _exchange_start(srcs, land_shapes, plan, ncopies, after, *, name):
    ni, nl = len(srcs), len(land_shapes)

    def body(*refs):
        in_refs, land_refs = refs[:ni], refs[ni:ni + nl]
        send_sems, recv_sems = refs[ni + nl + 1], refs[ni + nl + 2]
        token = refs[-1]
        for k, c in enumerate(plan(in_refs, land_refs)):
            _copy(c, k, send_sems, recv_sems).start()
        token[...] = jnp.zeros_like(token)

    bufs = [pltpu.with_memory_space_constraint(a, pltpu.HBM) for a in srcs]
    bufs += [pltpu.with_memory_space_constraint(lax.empty(s.shape, s.dtype), pltpu.HBM) for s in land_shapes]
    outs = pl.pallas_call(
        body, name=name,
        in_specs=[HBM_SPEC] * (ni + nl) + [ANY],
        out_specs=(SEM_SPEC, SEM_SPEC, *[HBM_SPEC] * (ni + nl), pl.BlockSpec(memory_space=pltpu.VMEM)),
        out_shape=(pltpu.SemaphoreType.DMA((ncopies,)), pltpu.SemaphoreType.DMA((ncopies,)),
                   *[pltpu.HBM(b.shape, b.dtype) for b in bufs], jax.ShapeDtypeStruct((8, 128), F32)),
        input_output_aliases={i: 2 + i for i in range(ni + nl)},
        compiler_params=pltpu.CompilerParams(has_side_effects=SPLIT_EFFECT),
    )(*bufs, after)
    return outs[:-1], outs[-1]


def _exchange_wait(started, ni, plan, after, *, name):
    send_sems, recv_sems = started[0], started[1]
    bufs = list(started[2:])
    nb = len(bufs)

    def body(*refs):
        in_refs, land_refs = refs[:ni], refs[ni:nb]
        send, recv = refs[nb], refs[nb + 1]
        for k, c in enumerate(plan(in_refs, land_refs)):
            cp = _copy(c, k, send, recv, landing=True)
            cp.wait_send()
            cp.wait_recv()

    outs = pl.pallas_call(
        body, name=name, in_specs=[HBM_SPEC] * nb + [SEM_SPEC, SEM_SPEC, ANY], out_specs=[HBM_SPEC] * nb,
        out_shape=[pltpu.HBM(b.shape, b.dtype) for b in bufs],
        input_output_aliases={i: i for i in range(nb)},
        compiler_params=pltpu.CompilerParams(has_side_effects=SPLIT_EFFECT),
    )(*bufs, send_sems, recv_sems, after)
    return list(outs[:ni]), list(outs[ni:])


def _sum_own(own, recv, chip_idx, *, name):
    _, R, Cc = own.shape
    tr = _rows_tile(R, Cc, 5)

    def body(s_ref, o_ref, r_ref, t_ref):
        acc = o_ref[...].astype(F32)
        for s in range(N_CHIPS - 1):
            acc = acc + r_ref[s].astype(F32)
        t_ref[...] = acc

    return pl.pallas_call(
        body, name=name,
        grid_spec=pltpu.PrefetchScalarGridSpec(
            num_scalar_prefetch=1, grid=(R // tr,),
            in_specs=[pl.BlockSpec((None, tr, Cc), lambda i, s: (s[0], i, 0)),
                      pl.BlockSpec((N_CHIPS - 1, tr, Cc), lambda i, s: (0, i, 0))],
            out_specs=pl.BlockSpec((tr, Cc), lambda i, s: (i, 0))),
        out_shape=jax.ShapeDtypeStruct((R, Cc), F32), compiler_params=_cparams(("parallel",)),
    )(chip_idx, own, recv)


BIG = ("conv_w_pw1", "conv_w_pw2", "ret_w_in", "ret_w_out", "mlp_w1", "mlp_w2")
COLS = ("conv_w_pw1", "ret_w_in", "mlp_w1")
SMALL = ("ada_b", "norm_mix_g", "norm_mlp_g", "conv_b_pw1", "conv_w_dw", "conv_b_dw", "conv_ln_g", "conv_ln_b",
         "conv_b_pw2", "ret_gn_g", "ret_gn_b", "final_norm_g")
SMALL_SHARDED = ("conv_w_dw", "ret_gn_g", "ret_gn_b")
WEIGHTS = ("ada_w", "ada_b", "norm_mix_g", "norm_mlp_g", "conv_w_pw1", "conv_b_pw1", "conv_w_dw", "conv_b_dw",
           "conv_ln_g", "conv_ln_b", "conv_w_pw2", "conv_b_pw2", "ret_w_in", "ret_gn_g", "ret_gn_b", "ret_w_out",
           "mlp_w1", "mlp_w2", "final_norm_g")


def _vec8(rows, D):
    rows = [r.reshape(1, D).astype(F32) for r in rows]
    return jnp.concatenate(rows + [jnp.zeros((8 - len(rows), D), F32)], axis=0)


def _unshard_last(g):
    nd = g.ndim
    t = jnp.transpose(g, tuple(range(1, nd - 1)) + (0, nd - 1))
    return t.reshape(t.shape[:-2] + (t.shape[-2] * t.shape[-1],))


def _pack(parts):
    flat = jnp.concatenate([p.reshape(-1).astype(F32) for p in parts])
    pad = (-flat.shape[0]) % 1024
    return jnp.concatenate([flat, jnp.zeros((pad,), F32)]).reshape(-1, 128)


def _unpack(packed, shapes):
    flat = packed.reshape(-1)
    out, pos = [], 0
    for s in shapes:
        n = math.prod(s)
        out.append(flat[pos:pos + n].reshape(s))
        pos += n
    return out


def kernel(x, c, ada_w, ada_b, norm_mix_g, norm_mlp_g, conv_w_pw1, conv_b_pw1, conv_w_dw, conv_b_dw, conv_ln_g, conv_ln_b, conv_w_pw2, conv_b_pw2, ret_w_in, ret_gn_g, ret_gn_b, ret_w_out, mlp_w1, mlp_w2, final_norm_g, loss_target, m_ada_w, m_ada_b, m_norm_mix_g, m_norm_mlp_g, m_conv_w_pw1, m_conv_b_pw1, m_conv_w_dw, m_conv_b_dw, m_conv_ln_g, m_conv_ln_b, m_conv_w_pw2, m_conv_b_pw2, m_ret_w_in, m_ret_gn_g, m_ret_gn_b, m_ret_w_out, m_mlp_w1, m_mlp_w2, m_final_norm_g, v_ada_w, v_ada_b, v_norm_mix_g, v_norm_mlp_g, v_conv_w_pw1, v_conv_b_pw1, v_conv_w_dw, v_conv_b_dw, v_conv_ln_g, v_conv_ln_b, v_conv_w_pw2, v_conv_b_pw2, v_ret_w_in, v_ret_gn_g, v_ret_gn_b, v_ret_w_out, v_mlp_w1, v_mlp_w2, v_final_norm_g):
    W = dict(ada_w=ada_w, ada_b=ada_b, norm_mix_g=norm_mix_g, norm_mlp_g=norm_mlp_g, conv_w_pw1=conv_w_pw1,
             conv_b_pw1=conv_b_pw1, conv_w_dw=conv_w_dw, conv_b_dw=conv_b_dw, conv_ln_g=conv_ln_g,
             conv_ln_b=conv_ln_b, conv_w_pw2=conv_w_pw2, conv_b_pw2=conv_b_pw2, ret_w_in=ret_w_in,
             ret_gn_g=ret_gn_g, ret_gn_b=ret_gn_b, ret_w_out=ret_w_out, mlp_w1=mlp_w1, mlp_w2=mlp_w2,
             final_norm_g=final_norm_g)
    Mo = dict(ada_w=m_ada_w, ada_b=m_ada_b, norm_mix_g=m_norm_mix_g, norm_mlp_g=m_norm_mlp_g,
              conv_w_pw1=m_conv_w_pw1, conv_b_pw1=m_conv_b_pw1, conv_w_dw=m_conv_w_dw, conv_b_dw=m_conv_b_dw,
              conv_ln_g=m_conv_ln_g, conv_ln_b=m_conv_ln_b, conv_w_pw2=m_conv_w_pw2, conv_b_pw2=m_conv_b_pw2,
              ret_w_in=m_ret_w_in, ret_gn_g=m_ret_gn_g, ret_gn_b=m_ret_gn_b, ret_w_out=m_ret_w_out,
              mlp_w1=m_mlp_w1, mlp_w2=m_mlp_w2, final_norm_g=m_final_norm_g)
    Vo = dict(ada_w=v_ada_w, ada_b=v_ada_b, norm_mix_g=v_norm_mix_g, norm_mlp_g=v_norm_mlp_g,
              conv_w_pw1=v_conv_w_pw1, conv_b_pw1=v_conv_b_pw1, conv_w_dw=v_conv_w_dw, conv_b_dw=v_conv_b_dw,
              conv_ln_g=v_conv_ln_g, conv_ln_b=v_conv_ln_b, conv_w_pw2=v_conv_w_pw2, conv_b_pw2=v_conv_b_pw2,
              ret_w_in=v_ret_w_in, ret_gn_g=v_ret_gn_g, ret_gn_b=v_ret_gn_b, ret_w_out=v_ret_w_out,
              mlp_w1=v_mlp_w1, mlp_w2=v_mlp_w2, final_norm_g=v_final_norm_g)

    S, D = x.shape[1], x.shape[2]
    depth = ada_w.shape[0]
    H = RET_HEADS
    dv = 2 * D // H
    xs = x.reshape(S, D)
    target = loss_target.reshape(S, D)
    mx, my, mc = _me()
    chip = 2 * mx + my
    dev = 4 * mx + 2 * my + mc

    def layer_weights(l):
        mixer = ("conv_w_pw1", "conv_w_pw2") if l % 2 == 0 else ("ret_w_in", "ret_w_out")
        return [(nm, l // 2) for nm in mixer] + [("mlp_w1", l), ("mlp_w2", l)]

    c_all = _allgather8(c.reshape(8, D // 8), name="gather_c").reshape(N_DEV, D)
    cs_ada = ada_w.shape[2]
    bias_sh = lax.dynamic_slice_in_dim(ada_b.reshape(depth, N_CHIPS, cs_ada), chip, 1, axis=1)
    mod_sh = _mm(c_all, ada_w, mode="nn", name="ada_fwd", b3d=True, tn=cs_ada, a_silu=True,
                 bias=bias_sh.reshape(1, depth * cs_ada))
    mod_all = _allgather8(mod_sh, name="gather_mod")[0::2]
    mod_me = lax.dynamic_slice_in_dim(mod_all, dev, 1, axis=1).reshape(N_CHIPS, depth, cs_ada)
    mod = jnp.transpose(mod_me, (1, 0, 2)).reshape(depth, 6, D)

    keys0 = layer_weights(0)
    got0 = _gather_chips([W[nm][i].astype(BF16) for nm, i in keys0] + [W[nm] for nm in SMALL_SHARDED], mod,
                         name="gather_weights")
    Wg = dict(zip(keys0, got0))
    full_small = {nm: _unshard_last(got0[len(keys0) + i]) for i, nm in enumerate(SMALL_SHARDED)}
    pending, order = {}, got0[0]
    for l in range(1, depth):
        keys = layer_weights(l)
        srcs = [W[nm][i].astype(BF16) for nm, i in keys]
        shapes = [jax.ShapeDtypeStruct((N_CHIPS,) + s.shape, BF16) for s in srcs]
        started, order = _exchange_start(srcs, shapes, _plan_gather, 4 * len(srcs), order, name=f"gather_start_{l}")
        pending[l] = (keys, started)
    mod = mod + order[0, 0]

    def wfull(nm, l):
        g = Wg[nm, l]
        return g.reshape(g.shape[0] * g.shape[1], g.shape[2])

    pos_ids = jnp.arange(S, dtype=F32)
    dk = D // H
    inv = ROPE_BASE ** (-jnp.arange(0, dk, 2, dtype=F32) / dk)
    ang = pos_ids[:, None] * inv[None, :]
    cos_t, sin_t = jnp.cos(ang), jnp.sin(ang)
    tables = _ret_tables(H)
    gn_g_full = full_small["ret_gn_g"].reshape(-1, H, 1, dv)
    gn_b_full = full_small["ret_gn_b"].reshape(-1, H, 1, dv)
    wdw_full = full_small["conv_w_dw"]

    def wdw_pad(j):
        return jnp.concatenate([wdw_full[j], jnp.zeros((CONV_HALO - CONV_WIDTH, D), F32)], axis=0)

    saved = []
    xa, y_prev, gate_prev = xs, None, None
    for l in range(depth):
        j = l // 2
        sv = {}
        if l in pending:
            keys, started = pending.pop(l)
            _, lands = _exchange_wait(started, len(keys), _plan_gather, y_prev, name=f"gather_wait_{l}")
            Wg.update(zip(keys, lands))
        vec_a = _vec8([gate_prev if gate_prev is not None else jnp.zeros((D,), F32), norm_mix_g[l], mod[l, 0],
                       mod[l, 1]], D)
        xa, h = _row_fwd(xa, y_prev, vec_a, name="row_fwd" if y_prev is not None else "row_fwd_first")
        sv.update(xa=xa, h=h, vec_a=vec_a)
        if l % 2 == 0:
            u = _mm(h, Wg["conv_w_pw1", j], mode="nn", name="pw1_fwd", b3d=True,
                    tn=Wg["conv_w_pw1", j].shape[2], bias=conv_b_pw1[j].reshape(1, -1))
            cvec = _vec8([conv_b_dw[j], conv_ln_g[j], conv_ln_b[j]], D)
            v_glu, cv, z = _conv_fwd(u, wdw_pad(j), cvec, name="conv_fwd")
            ymix = _mm(z, wfull("conv_w_pw2", j), mode="nn", name="pw2_fwd", bias=conv_b_pw2[j].reshape(1, -1))
            sv.update(u=u, v_glu=v_glu, cv=cv, z=z, cvec=cvec)
        else:
            proj = _mm(h, Wg["ret_w_in", j], mode="nn", name="win_fwd", b3d=True, out_dtype=BF16,
                       tn=Wg["ret_w_in", j].shape[2])
            yr, y2, states = _ret_fwd(proj, cos_t, sin_t, tables, gn_g_full[j], gn_b_full[j], name="ret_fwd")
            ymix = _mm(y2, wfull("ret_w_out", j), mode="nn", name="wout_fwd")
            sv.update(proj=proj, yr=yr, y2=y2, states=states)
        vec_b = _vec8([mod[l, 2], norm_mlp_g[l], mod[l, 3], mod[l, 4]], D)
        xb, h2 = _row_fwd(xa, ymix, vec_b, name="row_fwd")
        ra, p = _mm(h2, Wg["mlp_w1", l], mode="nn", name="w1_fwd", b3d=True, tn=Wg["mlp_w1", l].shape[2],
                    epi="relu2")
        mo = _mm(p, wfull("mlp_w2", l), mode="nn", name="w2_fwd")
        sv.update(ymix=ymix, xb=xb, h2=h2, ra=ra, p=p, mo=mo, vec_b=vec_b)
        saved.append(sv)
        xa, y_prev, gate_prev = xb, mo, mod[l, 5]

    fvec = _vec8([gate_prev, final_norm_g], D)
    dx, dyb, fpart = _final(xa, y_prev, target, fvec, name="final")
    loss = lax.psum(jnp.sum(fpart[2]), ("x", "y", "c"))
    G = {nm: [None] * W[nm].shape[0] for nm in BIG}
    dmod = [[None] * 6 for _ in range(depth)]
    dmod[depth - 1][5] = fpart[0]
    sg = dict(norm_mix_g=[None] * depth, norm_mlp_g=[None] * depth, final_norm_g=fpart[1])
    n_conv, n_ret = conv_w_pw1.shape[0], ret_w_in.shape[0]
    for nm in ("conv_b_pw1", "conv_w_dw", "conv_b_dw", "conv_ln_g", "conv_ln_b", "conv_b_pw2"):
        sg[nm] = [None] * n_conv
    for nm in ("ret_gn_g", "ret_gn_b"):
        sg[nm] = [None] * n_ret

    half_idx = mc.astype(jnp.int32).reshape(1)
    chip_idx = chip.astype(jnp.int32).reshape(1)

    def chip_sums(keys, tag):
        flat = [G[nm][i] for nm, i in keys]
        sib = _swap_half(flat, name="swap_grads_" + tag)
        sums = [_add_half(a, b, half_idx, name="add_grads") for a, b in zip(flat, sib)]
        shapes = [jax.ShapeDtypeStruct((N_CHIPS - 1,) + s.shape[1:], BF16) for s in sums]
        return sums, shapes

    launch = {depth // 2 - 1: list(range(depth // 2, depth))}
    launch.update({l - 1: [l] for l in range(1, depth // 2)})
    early_keys = layer_weights(0)
    in_flight = []

    for l in reversed(range(depth)):
        j = l // 2
        sv = saved[l]
        if l in launch:
            keys = [k for ll in launch[l] for k in layer_weights(ll)]
            sums, shapes = chip_sums(keys, f"from{launch[l][0]}")
            started, tok = _exchange_start(sums, shapes, _plan_scatter, 3 * len(sums), fpart,
                                           name=f"scatter_start_{launch[l][0]}")
            in_flight.append((keys, started, launch[l][0]))
            sv["vec_b"] = sv["vec_b"] + tok[:, :1]
        w1, w2 = Wg["mlp_w1", l], wfull("mlp_w2", l)
        cs1 = w1.shape[2]
        da = _mm(dyb, w2, mode="nt", name="w2_dx", out_dtype=BF16, epi="mul2", extra=sv["ra"])
        gw2 = _mm(sv["p"], dyb, mode="tn", name="w2_dw")
        G["mlp_w2"][l] = gw2.reshape(N_CHIPS, gw2.shape[0] // N_CHIPS, gw2.shape[1])
        G["mlp_w1"][l] = _mm(sv["h2"], da, mode="tn", name="w1_dw", out3d=(N_CHIPS, cs1), tn=cs1)
        dh2 = _mm(da, w1, mode="nt", name="w1_dx", b3d=True, tk=cs1, out_dtype=BF16)
        dx, dyb, part = _row_bwd(sv["xb"], dh2, dx, sv["ymix"], sv["vec_b"], name="row_bwd")
        dmod[l][2], sg["norm_mlp_g"][l], dmod[l][3], dmod[l][4] = part[0], part[1], part[2], part[3]
        if l % 2 == 0:
            sg["conv_b_pw2"][j] = part[4]
            wp1, wp2 = Wg["conv_w_pw1", j], wfull("conv_w_pw2", j)
            csp = wp1.shape[2]
            dz = _mm(dyb, wp2, mode="nt", name="pw2_dx")
            gp2 = _mm(sv["z"], dyb, mode="tn", name="pw2_dw")
            G["conv_w_pw2"][j] = gp2.reshape(N_CHIPS, gp2.shape[0] // N_CHIPS, gp2.shape[1])
            du, dwdw, cpart, dbu = _conv_bwd(dz, sv["cv"], sv["v_glu"], sv["u"], wdw_pad(j), sv["cvec"],
                                             name="conv_bwd")
            sg["conv_w_dw"][j] = dwdw[:CONV_WIDTH]
            sg["conv_b_dw"][j], sg["conv_ln_g"][j], sg["conv_ln_b"][j] = cpart[0], cpart[1], cpart[2]
            sg["conv_b_pw1"][j] = dbu[0]
            G["conv_w_pw1"][j] = _mm(sv["h"], du, mode="tn", name="pw1_dw", out3d=(N_CHIPS, csp), tn=csp)
            dh = _mm(du, wp1, mode="nt", name="pw1_dx", b3d=True, tk=csp, out_dtype=BF16)
        else:
            wi, wo = Wg["ret_w_in", j], wfull("ret_w_out", j)
            csi = wi.shape[2]
            dy2 = _mm(dyb, wo, mode="nt", name="wout_dx")
            gwo = _mm(sv["y2"], dyb, mode="tn", name="wout_dw")
            G["ret_w_out"][j] = gwo.reshape(N_CHIPS, gwo.shape[0] // N_CHIPS, gwo.shape[1])
            dq, dkk, dvv, dgt, dgg, dgb = _ret_bwd(sv["proj"], cos_t, sin_t, tables, gn_g_full[j], gn_b_full[j],
                                                   sv["yr"], dy2, sv["states"], name="ret_bwd")
            sg["ret_gn_g"][j], sg["ret_gn_b"][j] = dgg.reshape(H, dv), dgb.reshape(H, dv)
            dproj = jnp.concatenate([dq, dkk, dvv, dgt], axis=1)
            G["ret_w_in"][j] = _mm(sv["h"], dproj, mode="tn", name="win_dw", out3d=(N_CHIPS, csi), tn=csi)
            dh = _mm(dproj, wi, mode="nt", name="win_dx", b3d=True, tk=csi, out_dtype=BF16)
        yp = saved[l - 1]["mo"] if l > 0 else dh
        dx, dyb, part = _row_bwd(sv["xa"], dh, dx, yp, sv["vec_a"], name="row_bwd")
        sg["norm_mix_g"][l], dmod[l][0], dmod[l][1] = part[1], part[2], part[3]
        if l > 0:
            dmod[l - 1][5] = part[0]
    grad_x = dx.reshape(x.shape)

    dmod_me = jnp.stack([jnp.stack(r) for r in dmod]).reshape(depth, 6 * D)
    sgrads = dict(ada_b=dmod_me)
    for nm in SMALL[1:]:
        sgrads[nm] = sg[nm] if nm == "final_norm_g" else jnp.stack(sg[nm])
    full_shapes = [sgrads[nm].shape for nm in SMALL]
    packed_all = _allgather8(_pack([sgrads[nm] for nm in SMALL]), name="gather_small_grads")
    sums = _unpack(_sum_slots(packed_all[None], name="sum_small_grads"), full_shapes)
    gsm = {}
    for nm, g in zip(SMALL, sums):
        if nm in SMALL_SHARDED:
            n = g.shape[-1] // N_CHIPS
            g = lax.dynamic_slice_in_dim(g.reshape(g.shape[:-1] + (N_CHIPS, n)), chip, 1, axis=g.ndim - 1)
            g = g.reshape(g.shape[:-2] + (n,))
        gsm[nm] = g.reshape(W[nm].shape)
    pw, pm, pv, pg = (_pack([t[nm] for nm in SMALL]) for t in (W, Mo, Vo, gsm))
    e4 = lambda a: a.reshape((1, 1) + a.shape)
    e3 = lambda a: a.reshape((1,) + a.shape)
    sres = _adamw(e4(pg), e3(pw), e3(pm), e3(pv), name="adamw_small")
    shard_shapes = [W[nm].shape for nm in SMALL]
    small_out = [dict(zip(SMALL, _unpack(r, shard_shapes))) for r in sres]

    n_mod_rows = depth * 6 * D // 128
    dmod_all = packed_all[:, :n_mod_rows].reshape(N_DEV, depth, N_CHIPS, cs_ada)
    dmod_cols = lax.dynamic_slice_in_dim(dmod_all, chip, 1, axis=2).reshape(N_DEV, depth * cs_ada)
    kpad = 128 - N_DEV
    dmod_pad = jnp.concatenate([dmod_cols, jnp.zeros((kpad, depth * cs_ada), F32)], axis=0)
    ct_pad = jnp.concatenate([c_all.T, jnp.zeros((D, kpad), F32)], axis=1)
    g_ada = _mm(ct_pad, dmod_pad, mode="nn", name="ada_dw", a_silu=True, out3d=(depth, cs_ada), tn=cs_ada)
    ada_out = _adamw(g_ada.reshape(depth, 1, D, cs_ada), ada_w, m_ada_w, v_ada_w, name="adamw_ada")

    sums, shapes = chip_sums(early_keys, "from0")
    early_lands = _exchange(sums, shapes, _plan_scatter, 3 * len(sums), name="scatter_grads")
    done = [(early_keys, sums, early_lands)]
    for keys, started, first in in_flight:
        own, lands = _exchange_wait(started, len(keys), _plan_scatter, early_lands[-1], name=f"scatter_wait_{first}")
        done.append((keys, own, lands))
    total = {}
    for keys, own, lands in done:
        for k, o, r in zip(keys, own, lands):
            total[k] = _sum_own(o, r, chip_idx, name="sum_grads")
    halves = [jnp.stack([total[nm, i] for i in range(len(G[nm]))]) for nm in BIG]
    sib_halves = _swap_sibling(halves, name="swap_totals")
    big_out = {nm: _adamw_halves(hm, hs, half_idx, W[nm], Mo[nm], Vo[nm], name="adamw_big")
               for nm, hm, hs in zip(BIG, halves, sib_halves)}

    def res(nm, i):
        if nm == "ada_w":
            return ada_out[i]
        if nm in big_out:
            return big_out[nm][i]
        return small_out[i][nm]

    return (loss, grad_x, *[res(nm, 0) for nm in WEIGHTS], *[res(nm, 1) for nm in WEIGHTS],
            *[res(nm, 2) for nm in WEIGHTS], *[res(nm, 3) for nm in WEIGHTS])
```

```python
import functools
import math

import jax
import jax.numpy as jnp
from jax import lax
from jax.experimental import pallas as pl
from jax.experimental.pallas import tpu as pltpu

F32 = jnp.float32
BF16 = jnp.bfloat16
MESH = pl.DeviceIdType.MESH

EPS = 1e-6
CHUNK = 64
CONV_WIDTH = 31
CONV_HALO = 32
SUBLANES = 8
LANES = 128
CONV_BLOCK_ROWS = 128
RET_HEADS = 4
ROPE_BASE = 10000.0
ADAM_LR = 0.001
ADAM_B1 = 0.9
ADAM_B2 = 0.999
ADAM_EPS = 1e-08
ADAM_WD = 0.01
ADAM_STEP = 10
N_CHIPS = 4
N_DEV = 8
V7X_VMEM_LIMIT = 48 * 1024 * 1024
ANY = pl.BlockSpec(memory_space=pl.ANY)


def _cparams(sem=None):
    return pltpu.CompilerParams(dimension_semantics=sem, vmem_limit_bytes=V7X_VMEM_LIMIT)


def _sigmoid(x):
    return jax.nn.sigmoid(x)


def _silu(x):
    return x * _sigmoid(x)


_DIMS = {
    "nn": (((1,), (0,)), ((), ())),
    "nt": (((1,), (1,)), ((), ())),
    "tn": (((0,), (0,)), ((), ())),
}


def _mm(a, b, *, mode, name, out_dtype=F32, tm=2048, tn=1024, tk=1024, b3d=False, out3d=None,
        bias=None, epi=None, extra=None, a_silu=False):
    if mode == "tn":
        K, M = a.shape
    else:
        M, K = a.shape
    if b3d:
        P, R, Cs = b.shape
        bshape = (R, P * Cs)
    else:
        bshape = b.shape
    N = bshape[0] if mode == "nt" else bshape[1]
    assert (bshape[1] if mode == "nt" else bshape[0]) == K, (name, a.shape, b.shape)
    tm, tn, tk = min(tm, M), min(tn, N), min(tk, K)
    assert M % tm == 0 and N % tn == 0 and K % tk == 0, (name, M, N, K, tm, tn, tk)
    nk = K // tk

    if mode == "tn":
        a_spec = pl.BlockSpec((tk, tm), lambda i, j, k: (k, i))
    else:
        a_spec = pl.BlockSpec((tm, tk), lambda i, j, k: (i, k))
    if mode == "nt":
        if b3d:
            nb = Cs // tk
            assert Cs % tk == 0
            b_spec = pl.BlockSpec((None, tn, tk), lambda i, j, k: (k // nb, j, k % nb))
        else:
            b_spec = pl.BlockSpec((tn, tk), lambda i, j, k: (j, k))
    else:
        if b3d:
            nb = Cs // tn
            assert Cs % tn == 0
            b_spec = pl.BlockSpec((None, tk, tn), lambda i, j, k: (j // nb, k, j % nb))
        else:
            b_spec = pl.BlockSpec((tk, tn), lambda i, j, k: (k, j))
    in_specs = [a_spec, b_spec]
    args = [a, b]
    if bias is not None:
        in_specs.append(pl.BlockSpec((1, tn), lambda i, j, k: (0, j)))
        args.append(bias)
    if extra is not None:
        in_specs.append(pl.BlockSpec((tm, tn), lambda i, j, k: (i, j)))
        args.append(extra)

    if out3d is not None:
        P_o, Cs_o = out3d
        assert P_o * Cs_o == N and Cs_o % tn == 0
        nbo = Cs_o // tn
        o_spec = pl.BlockSpec((None, tm, tn), lambda i, j, k: (j // nbo, i, j % nbo))
        o_shape = (P_o, M, Cs_o)
    else:
        o_spec = pl.BlockSpec((tm, tn), lambda i, j, k: (i, j))
        o_shape = (M, N)
    if epi == "relu2":
        out_shape = [jax.ShapeDtypeStruct(o_shape, BF16), jax.ShapeDtypeStruct(o_shape, BF16)]
        out_specs = [o_spec, o_spec]
    else:
        out_shape = jax.ShapeDtypeStruct(o_shape, out_dtype)
        out_specs = o_spec
    n_out = 2 if epi == "relu2" else 1
    dims = _DIMS[mode]
    has_bias, has_extra = bias is not None, extra is not None

    def body(*refs):
        a_ref, b_ref = refs[0], refs[1]
        pos = 2
        bias_ref = extra_ref = None
        if has_bias:
            bias_ref = refs[pos]
            pos += 1
        if has_extra:
            extra_ref = refs[pos]
            pos += 1
        outs = refs[pos:pos + n_out]
        acc_ref = refs[pos + n_out] if nk > 1 else None

        def partial():
            av = a_ref[...]
            if a_silu:
                av = _silu(av)
            return lax.dot_general(av, b_ref[...], dims, preferred_element_type=F32)

        def finish(r):
            if has_bias:
                r = r + bias_ref[...]
            if epi == "relu2":
                rr = jnp.maximum(r, 0.0)
                outs[0][...] = rr.astype(BF16)
                outs[1][...] = (rr * rr).astype(BF16)
            elif epi == "mul2":
                outs[0][...] = (r * 2.0 * extra_ref[...].astype(F32)).astype(outs[0].dtype)
            else:
                outs[0][...] = r.astype(outs[0].dtype)

        if nk == 1:
            finish(partial())
        else:
            k = pl.program_id(2)

            @pl.when(k == 0)
            def _():
                acc_ref[...] = jnp.zeros_like(acc_ref)

            acc_ref[...] += partial()

            @pl.when(k == nk - 1)
            def _():
                finish(acc_ref[...])

    return pl.pallas_call(
        body, name=name, grid=(M // tm, N // tn, nk), in_specs=in_specs, out_specs=out_specs,
        out_shape=out_shape,
        scratch_shapes=[pltpu.VMEM((tm, tn), F32)] if nk > 1 else [],
        compiler_params=_cparams(("parallel", "parallel", "arbitrary")),
    )(*args)


def _modnorm(x, gain, shift, scale):
    y = x * lax.rsqrt(jnp.mean(x * x, axis=-1, keepdims=True) + EPS)
    return (y * gain) * (1.0 + scale) + shift


def _row_fwd(xprev, y, vec, *, name, ts=512):
    S, D = xprev.shape
    ts = min(ts, S)
    has_res = y is not None
    row = pl.BlockSpec((ts, D), lambda i: (i, 0))
    vspec = pl.BlockSpec((8, D), lambda i: (0, 0))

    def body(*refs):
        if has_res:
            xp_ref, y_ref, v_ref, x_ref, h_ref = refs
            x = xp_ref[...] + v_ref[0:1, :] * y_ref[...]
            x_ref[...] = x
        else:
            xp_ref, v_ref, h_ref = refs
            x = xp_ref[...]
        h_ref[...] = _modnorm(x, v_ref[1:2, :], v_ref[2:3, :], v_ref[3:4, :]).astype(BF16)

    if has_res:
        return pl.pallas_call(
            body, name=name, grid=(S // ts,), in_specs=[row, row, vspec], out_specs=[row, row],
            out_shape=[jax.ShapeDtypeStruct((S, D), F32), jax.ShapeDtypeStruct((S, D), BF16)],
            compiler_params=_cparams(("parallel",)),
        )(xprev, y, vec)
    h = pl.pallas_call(
        body, name=name, grid=(S // ts,), in_specs=[row, vspec], out_specs=row,
        out_shape=jax.ShapeDtypeStruct((S, D), BF16),
        compiler_params=_cparams(("parallel",)),
    )(xprev, vec)
    return xprev, h


def _row_bwd(xin, dh, dxout, yprev, vec, *, name, ts=512):
    S, D = xin.shape
    ts = min(ts, S)
    row = pl.BlockSpec((ts, D), lambda i: (i, 0))
    vspec = pl.BlockSpec((8, D), lambda i: (0, 0))

    def body(x_ref, dh_ref, dx_ref, y_ref, v_ref, dxin_ref, dy_ref, part_ref):
        @pl.when(pl.program_id(0) == 0)
        def _():
            part_ref[...] = jnp.zeros_like(part_ref)

        gate = v_ref[0:1, :]
        _, vjp = jax.vjp(_modnorm, x_ref[...], v_ref[1:2, :], v_ref[2:3, :], v_ref[3:4, :])
        dxn, dgain, dshift, dscale = vjp(dh_ref[...].astype(F32))
        dxin = dx_ref[...] + dxn
        dxin_ref[...] = dxin
        dy = dxin * gate
        dy_ref[...] = dy.astype(BF16)
        part_ref[0:1, :] += jnp.sum(dxin * y_ref[...], axis=0, keepdims=True)
        part_ref[1:2, :] += dgain
        part_ref[2:3, :] += dshift
        part_ref[3:4, :] += dscale
        part_ref[4:5, :] += jnp.sum(dy, axis=0, keepdims=True)

    return pl.pallas_call(
        body, name=name, grid=(S // ts,), in_specs=[row, row, row, row, vspec],
        out_specs=[row, row, vspec],
        out_shape=[jax.ShapeDtypeStruct((S, D), F32), jax.ShapeDtypeStruct((S, D), BF16),
                   jax.ShapeDtypeStruct((8, D), F32)],
        compiler_params=_cparams(("arbitrary",)),
    )(xin, dh, dxout, yprev, vec)


def _final(xprev, y, target, vec, *, name, ts=512):
    S, D = xprev.shape
    ts = min(ts, S)
    row = pl.BlockSpec((ts, D), lambda i: (i, 0))
    vspec = pl.BlockSpec((8, D), lambda i: (0, 0))

    def norm(x, gain):
        return x * lax.rsqrt(jnp.mean(x * x, axis=-1, keepdims=True) + EPS) * gain

    def body(xp_ref, y_ref, t_ref, v_ref, dx_ref, dy_ref, part_ref):
        @pl.when(pl.program_id(0) == 0)
        def _():
            part_ref[...] = jnp.zeros_like(part_ref)

        gate = v_ref[0:1, :]
        yv = y_ref[...]
        x = xp_ref[...] + gate * yv
        out, vjp = jax.vjp(norm, x, v_ref[1:2, :])
        err = out - t_ref[...]
        dx, dgain = vjp(err * (1.0 / D))
        dx_ref[...] = dx
        dy_ref[...] = (dx * gate).astype(BF16)
        part_ref[0:1, :] += jnp.sum(dx * yv, axis=0, keepdims=True)
        part_ref[1:2, :] += dgain
        part_ref[2:3, :] += jnp.sum(err * err, axis=0, keepdims=True) * (0.5 / D)

    return pl.pallas_call(
        body, name=name, grid=(S // ts,), in_specs=[row, row, row, vspec], out_specs=[row, row, vspec],
        out_shape=[jax.ShapeDtypeStruct((S, D), F32), jax.ShapeDtypeStruct((S, D), BF16),
                   jax.ShapeDtypeStruct((8, D), F32)],
        compiler_params=_cparams(("arbitrary",)),
    )(xprev, y, target, vec)


def _ln_silu(cv, g, b):
    mu = jnp.mean(cv, axis=-1, keepdims=True)
    var = jnp.mean(jnp.square(cv - mu), axis=-1, keepdims=True)
    u = (cv - mu) * lax.rsqrt(var + EPS) * g + b
    return _silu(u)


def _shift_copies(ext, sh, n):
    for b in range(1, SUBLANES):
        sh[b - 1, 0:n, :] = ext[pl.ds(b, n), :]


def _shifted(ext, sh, off, r0, rows, cols):
    a, b = divmod(off, SUBLANES)
    if b == 0:
        return ext[pl.ds(SUBLANES * a + r0, rows), cols]
    return sh[b - 1, pl.ds(SUBLANES * a + r0, rows), cols]


def _conv_fwd(u, wdw, vec, *, name, ts=256):
    S, D2 = u.shape
    D = D2 // 2
    ts = min(ts, S)
    H = CONV_HALO
    row = pl.BlockSpec((ts, D), lambda i: (i, 0))

    rb_rows = min(CONV_BLOCK_ROWS, ts)

    def body(u_ref, w_ref, v_ref, vo_ref, cv_ref, z_ref, ext, sh):
        @pl.when(pl.program_id(0) == 0)
        def _():
            ext[0:H, :] = jnp.zeros((H, D), F32)

        uu = u_ref[...]
        v = uu[:, :D] * _sigmoid(uu[:, D:])
        vo_ref[...] = v
        ext[H:H + ts, :] = v
        _shift_copies(ext, sh, ts + H - 8)
        for r0 in range(0, ts, rb_rows):
            for c0 in range(0, D, LANES):
                cols = pl.ds(c0, LANES)
                acc = jnp.zeros((rb_rows, LANES), F32)
                for t in range(CONV_WIDTH):
                    src = _shifted(ext, sh, H - (CONV_WIDTH - 1) + t, r0, rb_rows, cols)
                    acc = acc + src * w_ref[pl.ds(t, 1), cols]
                cv_ref[pl.ds(r0, rb_rows), cols] = acc + v_ref[0:1, cols]
        z_ref[...] = _ln_silu(cv_ref[...], v_ref[1:2, :], v_ref[2:3, :]).astype(BF16)
        ext[0:H, :] = ext[ts:ts + H, :]

    return pl.pallas_call(
        body, name=name, grid=(S // ts,),
        in_specs=[pl.BlockSpec((ts, D2), lambda i: (i, 0)), pl.BlockSpec((H, D), lambda i: (0, 0)),
                  pl.BlockSpec((8, D), lambda i: (0, 0))],
        out_specs=[row, row, row],
        out_shape=[jax.ShapeDtypeStruct((S, D), F32), jax.ShapeDtypeStruct((S, D), F32),
                   jax.ShapeDtypeStruct((S, D), BF16)],
        scratch_shapes=[pltpu.VMEM((ts + H, D), F32), pltpu.VMEM((7, ts + H - 8, D), F32)],
        compiler_params=_cparams(("arbitrary",)),
    )(u, wdw, vec)


def _conv_bwd(dz, cv, v, u, wdw, vec, *, name, ts=256):
    S, D = cv.shape
    ts = min(ts, S)
    H = CONV_HALO
    nt = S // ts
    per = ts // H
    rev = lambda i: (nt - 1 - i, 0)
    row = pl.BlockSpec((ts, D), rev)

    rb_rows = min(CONV_BLOCK_ROWS, ts)
    nsh = ts + H - 8

    def body(dz_ref, cv_ref, v_ref, vh_ref, u_ref, w_ref, vec_ref, du_ref, dw_ref, part_ref, dbu_ref,
             dext, vext, dsh, vsh, dwacc, dvbuf):
        i = pl.program_id(0)

        @pl.when(i == 0)
        def _():
            dext[ts:ts + H, :] = jnp.zeros((H, D), F32)
            dwacc[...] = jnp.zeros_like(dwacc)
            part_ref[...] = jnp.zeros_like(part_ref)
            dbu_ref[...] = jnp.zeros_like(dbu_ref)

        _, vjp = jax.vjp(_ln_silu, cv_ref[...], vec_ref[1:2, :], vec_ref[2:3, :])
        dcv, dg, db = vjp(dz_ref[...])
        part_ref[0:1, :] += jnp.sum(dcv, axis=0, keepdims=True)
        part_ref[1:2, :] += dg
        part_ref[2:3, :] += db
        dext[0:ts, :] = dcv
        vext[0:H, :] = vh_ref[...] * jnp.where(i == nt - 1, 0.0, 1.0)
        vext[H:H + ts, :] = v_ref[...]
        _shift_copies(dext, dsh, nsh)
        _shift_copies(vext, vsh, nsh)
        for r0 in range(0, ts, rb_rows):
            for c0 in range(0, D, LANES):
                cols = pl.ds(c0, LANES)
                dblk = dext[pl.ds(r0, rb_rows), cols]
                dv = jnp.zeros((rb_rows, LANES), F32)
                for t in range(CONV_WIDTH):
                    prod = dblk * _shifted(vext, vsh, H - (CONV_WIDTH - 1) + t, r0, rb_rows, cols)
                    parts = [prod[s:s + SUBLANES, :] for s in range(0, rb_rows, SUBLANES)]
                    while len(parts) > 1:
                        parts = [parts[k] + parts[k + 1] for k in range(0, len(parts), 2)]
                    dwacc[pl.ds(t * SUBLANES, SUBLANES), cols] += parts[0]
                    dv = dv + _shifted(dext, dsh, CONV_WIDTH - 1 - t, r0, rb_rows, cols) * w_ref[pl.ds(t, 1), cols]
                dvbuf[pl.ds(r0, rb_rows), cols] = dv
        dv = dvbuf[...]
        uu = u_ref[...]
        a, g = uu[:, :D], uu[:, D:]
        sg = _sigmoid(g)
        da = dv * sg
        dg_ = dv * a * sg * (1.0 - sg)
        du = jnp.concatenate([da, dg_], axis=-1)
        du_ref[...] = du.astype(BF16)
        dbu_ref[0:1, :] += jnp.sum(du, axis=0, keepdims=True)
        dext[ts:ts + H, :] = dext[0:H, :]

        @pl.when(i == nt - 1)
        def _():
            dw_ref[...] = jnp.zeros_like(dw_ref)
            for t in range(CONV_WIDTH):
                dw_ref[pl.ds(t, 1), :] = jnp.sum(dwacc[pl.ds(t * SUBLANES, SUBLANES), :], axis=0, keepdims=True)

    return pl.pallas_call(
        body, name=name, grid=(nt,),
        in_specs=[row, row, row,
                  pl.BlockSpec((H, D), lambda i: (jnp.maximum((nt - 1 - i) * per - 1, 0), 0)),
                  pl.BlockSpec((ts, 2 * D), rev), pl.BlockSpec((H, D), lambda i: (0, 0)),
                  pl.BlockSpec((8, D), lambda i: (0, 0))],
        out_specs=[pl.BlockSpec((ts, 2 * D), rev), pl.BlockSpec((H, D), lambda i: (0, 0)),
                   pl.BlockSpec((8, D), lambda i: (0, 0)), pl.BlockSpec((8, 2 * D), lambda i: (0, 0))],
        out_shape=[jax.ShapeDtypeStruct((S, 2 * D), BF16), jax.ShapeDtypeStruct((H, D), F32),
                   jax.ShapeDtypeStruct((8, D), F32), jax.ShapeDtypeStruct((8, 2 * D), F32)],
        scratch_shapes=[pltpu.VMEM((ts + H, D), F32), pltpu.VMEM((ts + H, D), F32),
                        pltpu.VMEM((7, nsh, D), F32), pltpu.VMEM((7, nsh, D), F32),
                        pltpu.VMEM((CONV_WIDTH * SUBLANES, D), F32), pltpu.VMEM((ts, D), F32)],
        compiler_params=_cparams(("arbitrary",)),
    )(dz, cv, v, v, u, wdw, vec)


def _rope(x, c, s, half):
    x1, x2 = x[:, :half], x[:, half:]
    return jnp.concatenate([x1 * c - x2 * s, x2 * c + x1 * s], axis=-1)


def _rope_t(d, c, s, half):
    d1, d2 = d[:, :half], d[:, half:]
    return jnp.concatenate([d1 * c + d2 * s, d2 * c - d1 * s], axis=-1)


def _gn_gate(y, gate, g, b):
    mu = jnp.mean(y, axis=-1, keepdims=True)
    var = jnp.mean(jnp.square(y - mu), axis=-1, keepdims=True)
    return _silu(gate) * ((y - mu) * lax.rsqrt(var + EPS) * g + b)


def _dot(a, b, mode="nn"):
    return lax.dot_general(a, b, _DIMS[mode], preferred_element_type=F32)


def _ret_tables(H):
    lg = jnp.log(1.0 - 2.0 ** (-5.0 - jnp.arange(H, dtype=F32)))
    idx = jnp.arange(CHUNK, dtype=F32)
    dmat = jnp.exp(lg[:, None, None] * jnp.abs(idx[:, None] - idx[None, :]))
    xi = jnp.exp(lg[:, None] * (idx + 1.0))[..., None]
    zeta = jnp.exp(lg[:, None] * (CHUNK - 1.0 - idx))[..., None]
    dec = jnp.exp(lg * CHUNK)[:, None, None]
    return dmat, xi, zeta, dec


RET_HEADS_PER_STEP = 4


def _ret_specs(R, dk, dv, half, hps, order):
    C = CHUNK
    ng = RET_HEADS // hps
    return dict(
        q=pl.BlockSpec((R, hps * dk), lambda h, n: (order(n), h)),
        k=pl.BlockSpec((R, hps * dk), lambda h, n: (order(n), ng + h)),
        v=pl.BlockSpec((R, hps * dv), lambda h, n: (order(n), ng + h)),
        gate=pl.BlockSpec((R, hps * dv), lambda h, n: (order(n), 2 * ng + h)),
        rope=pl.BlockSpec((R, half), lambda h, n: (order(n), 0)),
        dmat=pl.BlockSpec((hps, C, C), lambda h, n: (h, 0, 0)),
        col=pl.BlockSpec((hps, C, 1), lambda h, n: (h, 0, 0)),
        one=pl.BlockSpec((hps, 1, 1), lambda h, n: (h, 0, 0)),
        gn=pl.BlockSpec((hps, 1, dv), lambda h, n: (h, 0, 0)),
        yv=pl.BlockSpec((R, hps * dv), lambda h, n: (order(n), h)),
        yk=pl.BlockSpec((R, hps * dk), lambda h, n: (order(n), h)),
    )


def _ret_fwd(proj, cos, sin, tables, gn_g, gn_b, *, name, cps=4):
    S = proj.shape[0]
    D = proj.shape[1] // 6
    H, C, hps = RET_HEADS, CHUNK, RET_HEADS_PER_STEP
    dk, dv, half = D // H, 2 * D // H, D // H // 2
    nc = S // C
    cps = min(cps, nc)
    R = cps * C
    scale = dk ** -0.5
    sp = _ret_specs(R, dk, dv, half, hps, lambda n: n)
    dmat, xi, zeta, dec = tables

    def body(q_ref, k_ref, v_ref, g_ref, cos_ref, sin_ref, dm_ref, xi_ref, ze_ref, dec_ref, gg_ref, gb_ref,
             y_ref, y2_ref, st_ref, state):
        @pl.when(pl.program_id(1) == 0)
        def _():
            state[...] = jnp.zeros_like(state)

        for j in range(cps):
            rows = pl.ds(j * C, C)
            cs, sn = cos_ref[rows, :], sin_ref[rows, :]
            for hh in range(hps):
                ck, cv = pl.ds(hh * dk, dk), pl.ds(hh * dv, dv)
                dm, xv, zv, dc = dm_ref[hh], xi_ref[hh], ze_ref[hh], dec_ref[hh]
                qr = _rope(q_ref[rows, ck].astype(F32), cs, sn, half)
                kr = _rope(k_ref[rows, ck].astype(F32), cs, sn, half) * scale
                vb = v_ref[rows, cv]
                p = (_dot(qr.astype(BF16), kr.astype(BF16), "nt") * dm).astype(BF16)
                st = state[hh]
                stb = st.astype(BF16)
                st_ref[hh, j] = stb
                y = _dot(p, vb) + _dot((qr * xv).astype(BF16), stb)
                state[hh] = st * dc + _dot((kr * zv).astype(BF16), vb, "tn")
                y_ref[rows, cv] = y
                y2_ref[rows, cv] = _gn_gate(y, g_ref[rows, cv].astype(F32), gg_ref[hh], gb_ref[hh]).astype(BF16)

    return pl.pallas_call(
        body, name=name, grid=(H // hps, nc // cps),
        in_specs=[sp["q"], sp["k"], sp["v"], sp["gate"], sp["rope"], sp["rope"], sp["dmat"], sp["col"],
                  sp["col"], sp["one"], sp["gn"], sp["gn"]],
        out_specs=[sp["yv"], sp["yv"], pl.BlockSpec((hps, cps, dk, dv), lambda h, n: (h, n, 0, 0))],
        out_shape=[jax.ShapeDtypeStruct((S, 2 * D), F32), jax.ShapeDtypeStruct((S, 2 * D), BF16),
                   jax.ShapeDtypeStruct((H, nc, dk, dv), BF16)],
        scratch_shapes=[pltpu.VMEM((hps, dk, dv), F32)],
        compiler_params=_cparams(("arbitrary", "arbitrary")),
    )(proj, proj, proj, proj, cos, sin, dmat, xi, zeta, dec, gn_g, gn_b)


def _ret_bwd(proj, cos, sin, tables, gn_g, gn_b, y, dy2, states, *, name, cps=4):
    S = proj.shape[0]
    D = proj.shape[1] // 6
    H, C, hps = RET_HEADS, CHUNK, RET_HEADS_PER_STEP
    dk, dv, half = D // H, 2 * D // H, D // H // 2
    nc = S // C
    cps = min(cps, nc)
    ns = nc // cps
    R = cps * C
    scale = dk ** -0.5
    order = lambda n: ns - 1 - n
    sp = _ret_specs(R, dk, dv, half, hps, order)
    dmat, xi, zeta, dec = tables

    def body(q_ref, k_ref, v_ref, g_ref, cos_ref, sin_ref, dm_ref, xi_ref, ze_ref, dec_ref, gg_ref, gb_ref,
             y_ref, dy2_ref, st_ref, dq_ref, dk_ref, dv_ref, dg_ref, dgg_ref, dgb_ref, gst):
        @pl.when(pl.program_id(1) == 0)
        def _():
            gst[...] = jnp.zeros_like(gst)
            dgg_ref[...] = jnp.zeros_like(dgg_ref)
            dgb_ref[...] = jnp.zeros_like(dgb_ref)

        for j in reversed(range(cps)):
            rows = pl.ds(j * C, C)
            cs, sn = cos_ref[rows, :], sin_ref[rows, :]
            for hh in range(hps):
                ck, cv = pl.ds(hh * dk, dk), pl.ds(hh * dv, dv)
                dm, xv, zv, dc = dm_ref[hh], xi_ref[hh], ze_ref[hh], dec_ref[hh]
                _, vjp = jax.vjp(_gn_gate, y_ref[rows, cv], g_ref[rows, cv].astype(F32), gg_ref[hh], gb_ref[hh])
                dy, dgate, dgg, dgb = vjp(dy2_ref[rows, cv])
                dgg_ref[hh] += dgg
                dgb_ref[hh] += dgb
                dg_ref[rows, cv] = dgate.astype(BF16)
                dyb = dy.astype(BF16)
                qr = _rope(q_ref[rows, ck].astype(F32), cs, sn, half)
                kr = _rope(k_ref[rows, ck].astype(F32), cs, sn, half) * scale
                qb, kb, vb = qr.astype(BF16), kr.astype(BF16), v_ref[rows, cv]
                p = (_dot(qb, kb, "nt") * dm).astype(BF16)
                g = gst[hh]
                gb16 = g.astype(BF16)
                sprev = st_ref[hh, j]
                dvv = _dot(p, dyb, "tn") + _dot((kr * zv).astype(BF16), gb16)
                dpb = (_dot(dyb, vb, "nt") * dm).astype(BF16)
                dqr = _dot(dpb, kb) + _dot(dyb, sprev, "nt") * xv
                dkr = _dot(dpb, qb, "tn") + _dot(vb, gb16, "nt") * zv
                gst[hh] = g * dc + _dot((qr * xv).astype(BF16), dyb, "tn")
                dq_ref[rows, ck] = _rope_t(dqr, cs, sn, half).astype(BF16)
                dk_ref[rows, ck] = _rope_t(dkr * scale, cs, sn, half).astype(BF16)
                dv_ref[rows, cv] = dvv.astype(BF16)

    return pl.pallas_call(
        body, name=name, grid=(H // hps, ns),
        in_specs=[sp["q"], sp["k"], sp["v"], sp["gate"], sp["rope"], sp["rope"], sp["dmat"], sp["col"],
                  sp["col"], sp["one"], sp["gn"], sp["gn"], sp["yv"], sp["yv"],
                  pl.BlockSpec((hps, cps, dk, dv), lambda h, n: (h, order(n), 0, 0))],
        out_specs=[sp["yk"], sp["yk"], sp["yv"], sp["yv"], sp["gn"], sp["gn"]],
        out_shape=[jax.ShapeDtypeStruct((S, D), BF16), jax.ShapeDtypeStruct((S, D), BF16),
                   jax.ShapeDtypeStruct((S, 2 * D), BF16), jax.ShapeDtypeStruct((S, 2 * D), BF16),
                   jax.ShapeDtypeStruct((H, 1, dv), F32), jax.ShapeDtypeStruct((H, 1, dv), F32)],
        scratch_shapes=[pltpu.VMEM((hps, dk, dv), F32)],
        compiler_params=_cparams(("arbitrary", "arbitrary")),
    )(proj, proj, proj, proj, cos, sin, dmat, xi, zeta, dec, gn_g, gn_b, y, dy2, states)


def _rows_tile(rows, cols, n_arrays):
    cap = max(8, V7X_VMEM_LIMIT // 3 // (n_arrays * 2 * 4 * cols))
    t = rows
    while t > cap and t % 2 == 0:
        t //= 2
    return t


def _add_half(g, r, half_idx, *, name):
    P, R, Cc = g.shape
    hR = R // 2
    tr = _rows_tile(hR, Cc, 3)
    nb = hR // tr

    def body(h_ref, g_ref, r_ref, o_ref):
        o_ref[...] = (g_ref[...] + r_ref[...]).astype(BF16)

    return pl.pallas_call(
        body, name=name,
        grid_spec=pltpu.PrefetchScalarGridSpec(
            num_scalar_prefetch=1, grid=(P, nb),
            in_specs=[pl.BlockSpec((None, tr, Cc), lambda s, i, h: (s, h[0] * nb + i, 0)),
                      pl.BlockSpec((None, tr, Cc), lambda s, i, h: (s, i, 0))],
            out_specs=pl.BlockSpec((None, tr, Cc), lambda s, i, h: (s, i, 0))),
        out_shape=jax.ShapeDtypeStruct((P, hR, Cc), BF16), compiler_params=_cparams(("parallel", "parallel")),
    )(half_idx, g, r)


def _sum_slots(x, *, name):
    L, NS, R, Cc = x.shape
    tr = _rows_tile(R, Cc, NS + 1)

    def body(x_ref, o_ref):
        acc = x_ref[0].astype(F32)
        for s in range(1, NS):
            acc = acc + x_ref[s].astype(F32)
        o_ref[...] = acc

    return pl.pallas_call(
        body, name=name, grid=(L, R // tr),
        in_specs=[pl.BlockSpec((None, NS, tr, Cc), lambda l, i: (l, 0, i, 0))],
        out_specs=pl.BlockSpec((None, tr, Cc), lambda l, i: (l, i, 0)),
        out_shape=jax.ShapeDtypeStruct((L, R, Cc), F32), compiler_params=_cparams(("parallel", "parallel")),
    )(x)


def _adam_store(g, w_ref, m_ref, v_ref, go_ref, d_ref, mo_ref, vo_ref):
    mn = ADAM_B1 * m_ref[...] + (1.0 - ADAM_B1) * g
    vn = ADAM_B2 * v_ref[...] + (1.0 - ADAM_B2) * jnp.square(g)
    m_hat = mn / (1.0 - ADAM_B1 ** ADAM_STEP)
    v_hat = vn / (1.0 - ADAM_B2 ** ADAM_STEP)
    go_ref[...] = g
    d_ref[...] = -ADAM_LR * (m_hat / (jnp.sqrt(v_hat) + ADAM_EPS) + ADAM_WD * w_ref[...])
    mo_ref[...] = mn
    vo_ref[...] = vn


def _adamw(gslots, w, m, v, *, name):
    L, NS, R, Cc = gslots.shape
    tr = _rows_tile(R, Cc, NS + 7)
    gspec = pl.BlockSpec((None, NS, tr, Cc), lambda l, i: (l, 0, i, 0))
    spec = pl.BlockSpec((None, tr, Cc), lambda l, i: (l, i, 0))

    def body(g_ref, *refs):
        g = g_ref[0]
        for s in range(1, NS):
            g = g + g_ref[s]
        _adam_store(g, *refs)

    sd = jax.ShapeDtypeStruct((L, R, Cc), F32)
    return pl.pallas_call(
        body, name=name, grid=(L, R // tr), in_specs=[gspec, spec, spec, spec],
        out_specs=[spec, spec, spec, spec], out_shape=[sd, sd, sd, sd],
        compiler_params=_cparams(("parallel", "parallel")),
    )(gslots, w, m, v)


def _adamw_halves(g_mine, g_sib, half_idx, w, m, v, *, name):
    L, hR, Cc = g_mine.shape
    tr = _rows_tile(hR, Cc, 9)
    nbh = hR // tr
    gspec = pl.BlockSpec((None, tr, Cc), lambda l, i, h: (l, i % nbh, 0))
    spec = pl.BlockSpec((None, tr, Cc), lambda l, i, h: (l, i, 0))

    def body(h_ref, gm_ref, gs_ref, *refs):
        mine = (pl.program_id(1) // nbh) == h_ref[0]
        _adam_store(jnp.where(mine, gm_ref[...], gs_ref[...]), *refs)

    sd = jax.ShapeDtypeStruct((L, 2 * hR, Cc), F32)
    return pl.pallas_call(
        body, name=name,
        grid_spec=pltpu.PrefetchScalarGridSpec(
            num_scalar_prefetch=1, grid=(L, 2 * nbh), in_specs=[gspec, gspec, spec, spec, spec],
            out_specs=[spec, spec, spec, spec]),
        out_shape=[sd, sd, sd, sd], compiler_params=_cparams(("parallel", "parallel")),
    )(half_idx, g_mine, g_sib, w, m, v)


def _me():
    return lax.axis_index("x"), lax.axis_index("y"), lax.axis_index("c")


def _flip(v, bit):
    return 1 - v if bit else v


def _allgather8(x, *, name):
    def body(x_ref, out_ref, send_sems, recv_sems, loc_sem):
        mx, my, mc = _me()
        me = 4 * mx + 2 * my + mc
        loc = pltpu.make_async_copy(x_ref, out_ref.at[me], loc_sem)
        loc.start()
        sends, recvs = [], []
        for k in range(1, N_DEV):
            px, py, pc = _flip(mx, k & 4), _flip(my, k & 2), _flip(mc, k & 1)
            sends.append(pltpu.make_async_remote_copy(
                src_ref=x_ref, dst_ref=out_ref.at[me], send_sem=send_sems.at[k - 1],
                recv_sem=recv_sems.at[k - 1], device_id=(px, py, pc), device_id_type=MESH))
            recvs.append(pltpu.make_async_remote_copy(
                src_ref=x_ref, dst_ref=out_ref.at[4 * px + 2 * py + pc], send_sem=send_sems.at[k - 1],
                recv_sem=recv_sems.at[k - 1], device_id=(px, py, pc), device_id_type=MESH))
        for cp in sends:
            cp.start()
        for cp in recvs:
            cp.wait_recv()
        for cp in sends:
            cp.wait_send()
        loc.wait()

    return pl.pallas_call(
        body, name=name, in_specs=[ANY], out_specs=ANY,
        out_shape=jax.ShapeDtypeStruct((N_DEV,) + x.shape, x.dtype),
        scratch_shapes=[pltpu.SemaphoreType.DMA((N_DEV - 1,)), pltpu.SemaphoreType.DMA((N_DEV - 1,)),
                        pltpu.SemaphoreType.DMA],
    )(x)


def _gather_chips(arrays, after, *, name):
    n = len(arrays)

    def body(*refs):
        ins, outs = refs[:n], refs[n + 1:2 * n + 1]
        ici_send, ici_recv, d2d_send, d2d_recv, own_send, own_recv = refs[2 * n + 1:]
        mx, my, mc = _me()
        me = 2 * mx + my
        sib = (mx, my, 1 - mc)
        locs, sends, lands, passes, gifts = [], [], [], [], []
        for a in range(n):
            h = arrays[a].shape[0] // 2
            mine, other = pl.ds(mc * h, h), pl.ds((1 - mc) * h, h)
            locs.append(pltpu.make_async_remote_copy(
                src_ref=ins[a], dst_ref=outs[a].at[me], send_sem=own_send.at[a], recv_sem=own_recv.at[a],
                device_id=sib, device_id_type=MESH))
            for k in range(1, N_CHIPS):
                px, py = _flip(mx, k & 2), _flip(my, k & 1)
                peer = 2 * px + py
                ici = dict(send_sem=ici_send.at[a, k - 1], recv_sem=ici_recv.at[a, k - 1],
                           device_id=(px, py, mc), device_id_type=MESH)
                d2d = dict(send_sem=d2d_send.at[a, k - 1], recv_sem=d2d_recv.at[a, k - 1],
                           device_id=sib, device_id_type=MESH)
                sends.append(pltpu.make_async_remote_copy(
                    src_ref=ins[a].at[mine], dst_ref=outs[a].at[me, mine], **ici))
                lands.append(pltpu.make_async_remote_copy(
                    src_ref=ins[a].at[mine], dst_ref=outs[a].at[peer, mine], **ici))
                passes.append(pltpu.make_async_remote_copy(
                    src_ref=outs[a].at[peer, mine], dst_ref=outs[a].at[peer, mine], **d2d))
                gifts.append(pltpu.make_async_remote_copy(
                    src_ref=outs[a].at[peer, other], dst_ref=outs[a].at[peer, other], **d2d))
        for cp in locs + sends:
            cp.start()
        for land, fwd in zip(lands, passes):
            land.wait_recv()
            fwd.start()
        for cp in gifts + locs:
            cp.wait_recv()
        for cp in sends + passes + locs:
            cp.wait_send()

    nsem = (n, N_CHIPS - 1)
    return pl.pallas_call(
        body, name=name, in_specs=[ANY] * (n + 1), out_specs=[ANY] * n,
        out_shape=[jax.ShapeDtypeStruct((N_CHIPS,) + a.shape, a.dtype) for a in arrays],
        scratch_shapes=[pltpu.SemaphoreType.DMA(nsem), pltpu.SemaphoreType.DMA(nsem), pltpu.SemaphoreType.DMA(nsem),
                        pltpu.SemaphoreType.DMA(nsem), pltpu.SemaphoreType.DMA((n,)), pltpu.SemaphoreType.DMA((n,))],
    )(*arrays, after)


def _swap_half(arrays, *, name):
    n = len(arrays)

    def body(*refs):
        ins, outs = refs[:n], refs[n:2 * n]
        send_sems, recv_sems = refs[2 * n:]
        mx, my, mc = _me()
        cps = []
        for a in range(n):
            P, R, _ = arrays[a].shape
            cps.append(pltpu.make_async_remote_copy(
                src_ref=ins[a].at[pl.ds(0, P), pl.ds((1 - mc) * (R // 2), R // 2)], dst_ref=outs[a],
                send_sem=send_sems.at[a], recv_sem=recv_sems.at[a],
                device_id=(mx, my, 1 - mc), device_id_type=MESH))
        for cp in cps:
            cp.start()
        for cp in cps:
            cp.wait_recv()
        for cp in cps:
            cp.wait_send()

    return pl.pallas_call(
        body, name=name, in_specs=[ANY] * n, out_specs=[ANY] * n,
        out_shape=[jax.ShapeDtypeStruct((a.shape[0], a.shape[1] // 2, a.shape[2]), a.dtype) for a in arrays],
        scratch_shapes=[pltpu.SemaphoreType.DMA((n,)), pltpu.SemaphoreType.DMA((n,))],
    )(*arrays)


def _swap_sibling(arrays, *, name):
    n = len(arrays)

    def body(*refs):
        ins, outs = refs[:n], refs[n:2 * n]
        send_sems, recv_sems = refs[2 * n:]
        mx, my, mc = _me()
        cps = [pltpu.make_async_remote_copy(
            src_ref=ins[a], dst_ref=outs[a], send_sem=send_sems.at[a], recv_sem=recv_sems.at[a],
            device_id=(mx, my, 1 - mc), device_id_type=MESH) for a in range(n)]
        for cp in cps:
            cp.start()
        for cp in cps:
            cp.wait_recv()
        for cp in cps:
            cp.wait_send()

    return pl.pallas_call(
        body, name=name, in_specs=[ANY] * n, out_specs=[ANY] * n,
        out_shape=[jax.ShapeDtypeStruct(a.shape, a.dtype) for a in arrays],
        scratch_shapes=[pltpu.SemaphoreType.DMA((n,)), pltpu.SemaphoreType.DMA((n,))],
    )(*arrays)


def _plan_scatter(srcs, lands):
    mx, my, mc = _me()
    copies = []
    for a in range(len(srcs)):
        for k in range(1, N_CHIPS):
            px, py = _flip(mx, k & 2), _flip(my, k & 1)
            copies.append((srcs[a].at[2 * px + py], lands[a].at[k - 1], lands[a].at[k - 1], (px, py, mc)))
    return copies


def _plan_gather(srcs, lands):
    mx, my, mc = _me()
    me = 2 * mx + my
    copies = []
    for a in range(len(srcs)):
        copies.append((srcs[a], lands[a].at[me], lands[a].at[me], (mx, my, 1 - mc)))
        for k in range(1, N_CHIPS):
            px, py = _flip(mx, k & 2), _flip(my, k & 1)
            copies.append((srcs[a], lands[a].at[me], lands[a].at[2 * px + py], (px, py, mc)))
    return copies


def _copy(c, k, send_sems, recv_sems, landing=False):
    src, dst, land, dev = c
    return pltpu.make_async_remote_copy(src_ref=src, dst_ref=land if landing else dst, send_sem=send_sems.at[k],
                                        recv_sem=recv_sems.at[k], device_id=dev, device_id_type=MESH)


def _exchange(srcs, land_shapes, plan, ncopies, *, name):
    ni, nl = len(srcs), len(land_shapes)

    def body(*refs):
        send_sems, recv_sems = refs[ni + nl:]
        copies = plan(refs[:ni], refs[ni:ni + nl])
        for k, c in enumerate(copies):
            _copy(c, k, send_sems, recv_sems).start()
        for k, c in enumerate(copies):
            _copy(c, k, send_sems, recv_sems, landing=True).wait_recv()
        for k, c in enumerate(copies):
            _copy(c, k, send_sems, recv_sems).wait_send()

    return pl.pallas_call(
        body, name=name, in_specs=[ANY] * ni, out_specs=[ANY] * nl, out_shape=list(land_shapes),
        scratch_shapes=[pltpu.SemaphoreType.DMA((ncopies,)), pltpu.SemaphoreType.DMA((ncopies,))],
    )(*srcs)


HBM_SPEC = pl.BlockSpec(memory_space=pltpu.HBM)
SEM_SPEC = pl.BlockSpec(memory_space=pltpu.SEMAPHORE)
SPLIT_EFFECT = pltpu.SideEffectType.DATAFLOW_SIDE_EFFECTING


def _exchange_start(srcs, land_shapes, plan, ncopies, after, *, name):
    ni, nl = len(srcs), len(land_shapes)

    def body(*refs):
        in_refs, land_refs = refs[:ni], refs[ni:ni + nl]
        send_sems, recv_sems = refs[ni + nl + 1], refs[ni + nl + 2]
        token = refs[-1]
        for k, c in enumerate(plan(in_refs, land_refs)):
            _copy(c, k, send_sems, recv_sems).start()
        token[...] = jnp.zeros_like(token)

    bufs = [pltpu.with_memory_space_constraint(a, pltpu.HBM) for a in srcs]
    bufs += [pltpu.with_memory_space_constraint(lax.empty(s.shape, s.dtype), pltpu.HBM) for s in land_shapes]
    outs = pl.pallas_call(
        body, name=name,
        in_specs=[HBM_SPEC] * (ni + nl) + [ANY],
        out_specs=(SEM_SPEC, SEM_SPEC, *[HBM_SPEC] * (ni + nl), pl.BlockSpec(memory_space=pltpu.VMEM)),
        out_shape=(pltpu.SemaphoreType.DMA((ncopies,)), pltpu.SemaphoreType.DMA((ncopies,)),
                   *[pltpu.HBM(b.shape, b.dtype) for b in bufs], jax.ShapeDtypeStruct((8, 128), F32)),
        input_output_aliases={i: 2 + i for i in range(ni + nl)},
        compiler_params=pltpu.CompilerParams(has_side_effects=SPLIT_EFFECT),
    )(*bufs, after)
    return outs[:-1], outs[-1]


def _exchange_wait(started, ni, plan, after, *, name):
    send_sems, recv_sems = started[0], started[1]
    bufs = list(started[2:])
    nb = len(bufs)

    def body(*refs):
        in_refs, land_refs = refs[:ni], refs[ni:nb]
        send, recv = refs[nb], refs[nb + 1]
        for k, c in enumerate(plan(in_refs, land_refs)):
            cp = _copy(c, k, send, recv, landing=True)
            cp.wait_send()
            cp.wait_recv()

    outs = pl.pallas_call(
        body, name=name, in_specs=[HBM_SPEC] * nb + [SEM_SPEC, SEM_SPEC, ANY], out_specs=[HBM_SPEC] * nb,
        out_shape=[pltpu.HBM(b.shape, b.dtype) for b in bufs],
        input_output_aliases={i: i for i in range(nb)},
        compiler_params=pltpu.CompilerParams(has_side_effects=SPLIT_EFFECT),
    )(*bufs, send_sems, recv_sems, after)
    return list(outs[:ni]), list(outs[ni:])


def _sum_own(own, recv, chip_idx, *, name):
    _, R, Cc = own.shape
    tr = _rows_tile(R, Cc, 5)

    def body(s_ref, o_ref, r_ref, t_ref):
        acc = o_ref[...].astype(F32)
        for s in range(N_CHIPS - 1):
            acc = acc + r_ref[s].astype(F32)
        t_ref[...] = acc

    return pl.pallas_call(
        body, name=name,
        grid_spec=pltpu.PrefetchScalarGridSpec(
            num_scalar_prefetch=1, grid=(R // tr,),
            in_specs=[pl.BlockSpec((None, tr, Cc), lambda i, s: (s[0], i, 0)),
                      pl.BlockSpec((N_CHIPS - 1, tr, Cc), lambda i, s: (0, i, 0))],
            out_specs=pl.BlockSpec((tr, Cc), lambda i, s: (i, 0))),
        out_shape=jax.ShapeDtypeStruct((R, Cc), F32), compiler_params=_cparams(("parallel",)),
    )(chip_idx, own, recv)


BIG = ("conv_w_pw1", "conv_w_pw2", "ret_w_in", "ret_w_out", "mlp_w1", "mlp_w2")
COLS = ("conv_w_pw1", "ret_w_in", "mlp_w1")
SMALL = ("ada_b", "norm_mix_g", "norm_mlp_g", "conv_b_pw1", "conv_w_dw", "conv_b_dw", "conv_ln_g", "conv_ln_b",
         "conv_b_pw2", "ret_gn_g", "ret_gn_b", "final_norm_g")
SMALL_SHARDED = ("conv_w_dw", "ret_gn_g", "ret_gn_b")
WEIGHTS = ("ada_w", "ada_b", "norm_mix_g", "norm_mlp_g", "conv_w_pw1", "conv_b_pw1", "conv_w_dw", "conv_b_dw",
           "conv_ln_g", "conv_ln_b", "conv_w_pw2", "conv_b_pw2", "ret_w_in", "ret_gn_g", "ret_gn_b", "ret_w_out",
           "mlp_w1", "mlp_w2", "final_norm_g")


def _vec8(rows, D):
    rows = [r.reshape(1, D).astype(F32) for r in rows]
    return jnp.concatenate(rows + [jnp.zeros((8 - len(rows), D), F32)], axis=0)


def _unshard_last(g):
    nd = g.ndim
    t = jnp.transpose(g, tuple(range(1, nd - 1)) + (0, nd - 1))
    return t.reshape(t.shape[:-2] + (t.shape[-2] * t.shape[-1],))


def _pack(parts):
    flat = jnp.concatenate([p.reshape(-1).astype(F32) for p in parts])
    pad = (-flat.shape[0]) % 1024
    return jnp.concatenate([flat, jnp.zeros((pad,), F32)]).reshape(-1, 128)


def _unpack(packed, shapes):
    flat = packed.reshape(-1)
    out, pos = [], 0
    for s in shapes:
        n = math.prod(s)
        out.append(flat[pos:pos + n].reshape(s))
        pos += n
    return out


def kernel(x, c, ada_w, ada_b, norm_mix_g, norm_mlp_g, conv_w_pw1, conv_b_pw1, conv_w_dw, conv_b_dw, conv_ln_g, conv_ln_b, conv_w_pw2, conv_b_pw2, ret_w_in, ret_gn_g, ret_gn_b, ret_w_out, mlp_w1, mlp_w2, final_norm_g, loss_target, m_ada_w, m_ada_b, m_norm_mix_g, m_norm_mlp_g, m_conv_w_pw1, m_conv_b_pw1, m_conv_w_dw, m_conv_b_dw, m_conv_ln_g, m_conv_ln_b, m_conv_w_pw2, m_conv_b_pw2, m_ret_w_in, m_ret_gn_g, m_ret_gn_b, m_ret_w_out, m_mlp_w1, m_mlp_w2, m_final_norm_g, v_ada_w, v_ada_b, v_norm_mix_g, v_norm_mlp_g, v_conv_w_pw1, v_conv_b_pw1, v_conv_w_dw, v_conv_b_dw, v_conv_ln_g, v_conv_ln_b, v_conv_w_pw2, v_conv_b_pw2, v_ret_w_in, v_ret_gn_g, v_ret_gn_b, v_ret_w_out, v_mlp_w1, v_mlp_w2, v_final_norm_g):
    W = dict(ada_w=ada_w, ada_b=ada_b, norm_mix_g=norm_mix_g, norm_mlp_g=norm_mlp_g, conv_w_pw1=conv_w_pw1,
             conv_b_pw1=conv_b_pw1, conv_w_dw=conv_w_dw, conv_b_dw=conv_b_dw, conv_ln_g=conv_ln_g,
             conv_ln_b=conv_ln_b, conv_w_pw2=conv_w_pw2, conv_b_pw2=conv_b_pw2, ret_w_in=ret_w_in,
             ret_gn_g=ret_gn_g, ret_gn_b=ret_gn_b, ret_w_out=ret_w_out, mlp_w1=mlp_w1, mlp_w2=mlp_w2,
             final_norm_g=final_norm_g)
    Mo = dict(ada_w=m_ada_w, ada_b=m_ada_b, norm_mix_g=m_norm_mix_g, norm_mlp_g=m_norm_mlp_g,
              conv_w_pw1=m_conv_w_pw1, conv_b_pw1=m_conv_b_pw1, conv_w_dw=m_conv_w_dw, conv_b_dw=m_conv_b_dw,
              conv_ln_g=m_conv_ln_g, conv_ln_b=m_conv_ln_b, conv_w_pw2=m_conv_w_pw2, conv_b_pw2=m_conv_b_pw2,
              ret_w_in=m_ret_w_in, ret_gn_g=m_ret_gn_g, ret_gn_b=m_ret_gn_b, ret_w_out=m_ret_w_out,
              mlp_w1=m_mlp_w1, mlp_w2=m_mlp_w2, final_norm_g=m_final_norm_g)
    Vo = dict(ada_w=v_ada_w, ada_b=v_ada_b, norm_mix_g=v_norm_mix_g, norm_mlp_g=v_norm_mlp_g,
              conv_w_pw1=v_conv_w_pw1, conv_b_pw1=v_conv_b_pw1, conv_w_dw=v_conv_w_dw, conv_b_dw=v_conv_b_dw,
              conv_ln_g=v_conv_ln_g, conv_ln_b=v_conv_ln_b, conv_w_pw2=v_conv_w_pw2, conv_b_pw2=v_conv_b_pw2,
              ret_w_in=v_ret_w_in, ret_gn_g=v_ret_gn_g, ret_gn_b=v_ret_gn_b, ret_w_out=v_ret_w_out,
              mlp_w1=v_mlp_w1, mlp_w2=v_mlp_w2, final_norm_g=v_final_norm_g)

    S, D = x.shape[1], x.shape[2]
    depth = ada_w.shape[0]
    H = RET_HEADS
    dv = 2 * D // H
    xs = x.reshape(S, D)
    target = loss_target.reshape(S, D)
    mx, my, mc = _me()
    chip = 2 * mx + my
    dev = 4 * mx + 2 * my + mc

    def layer_weights(l):
        mixer = ("conv_w_pw1", "conv_w_pw2") if l % 2 == 0 else ("ret_w_in", "ret_w_out")
        return [(nm, l // 2) for nm in mixer] + [("mlp_w1", l), ("mlp_w2", l)]

    c_all = _allgather8(c.reshape(8, D // 8), name="gather_c").reshape(N_DEV, D)
    cs_ada = ada_w.shape[2]
    bias_sh = lax.dynamic_slice_in_dim(ada_b.reshape(depth, N_CHIPS, cs_ada), chip, 1, axis=1)
    mod_sh = _mm(c_all, ada_w, mode="nn", name="ada_fwd", b3d=True, tn=cs_ada, a_silu=True,
                 bias=bias_sh.reshape(1, depth * cs_ada))
    mod_all = _allgather8(mod_sh, name="gather_mod")[0::2]
    mod_me = lax.dynamic_slice_in_dim(mod_all, dev, 1, axis=1).reshape(N_CHIPS, depth, cs_ada)
    mod = jnp.transpose(mod_me, (1, 0, 2)).reshape(depth, 6, D)

    keys0 = layer_weights(0)
    got0 = _gather_chips([W[nm][i].astype(BF16) for nm, i in keys0] + [W[nm] for nm in SMALL_SHARDED], mod,
                         name="gather_weights")
    Wg = dict(zip(keys0, got0))
    full_small = {nm: _unshard_last(got0[len(keys0) + i]) for i, nm in enumerate(SMALL_SHARDED)}
    pending, order = {}, got0[0]
    for l in range(1, depth):
        keys = layer_weights(l)
        srcs = [W[nm][i].astype(BF16) for nm, i in keys]
        shapes = [jax.ShapeDtypeStruct((N_CHIPS,) + s.shape, BF16) for s in srcs]
        started, order = _exchange_start(srcs, shapes, _plan_gather, 4 * len(srcs), order, name=f"gather_start_{l}")
        pending[l] = (keys, started)
    mod = mod + order[0, 0]

    def wfull(nm, l):
        g = Wg[nm, l]
        return g.reshape(g.shape[0] * g.shape[1], g.shape[2])

    pos_ids = jnp.arange(S, dtype=F32)
    dk = D // H
    inv = ROPE_BASE ** (-jnp.arange(0, dk, 2, dtype=F32) / dk)
    ang = pos_ids[:, None] * inv[None, :]
    cos_t, sin_t = jnp.cos(ang), jnp.sin(ang)
    tables = _ret_tables(H)
    gn_g_full = full_small["ret_gn_g"].reshape(-1, H, 1, dv)
    gn_b_full = full_small["ret_gn_b"].reshape(-1, H, 1, dv)
    wdw_full = full_small["conv_w_dw"]

    def wdw_pad(j):
        return jnp.concatenate([wdw_full[j], jnp.zeros((CONV_HALO - CONV_WIDTH, D), F32)], axis=0)

    saved = []
    xa, y_prev, gate_prev = xs, None, None
    for l in range(depth):
        j = l // 2
        sv = {}
        if l in pending:
            keys, started = pending.pop(l)
            _, lands = _exchange_wait(started, len(keys), _plan_gather, y_prev, name=f"gather_wait_{l}")
            Wg.update(zip(keys, lands))
        vec_a = _vec8([gate_prev if gate_prev is not None else jnp.zeros((D,), F32), norm_mix_g[l], mod[l, 0],
                       mod[l, 1]], D)
        xa, h = _row_fwd(xa, y_prev, vec_a, name="row_fwd" if y_prev is not None else "row_fwd_first")
        sv.update(xa=xa, h=h, vec_a=vec_a)
        if l % 2 == 0:
            u = _mm(h, Wg["conv_w_pw1", j], mode="nn", name="pw1_fwd", b3d=True,
                    tn=Wg["conv_w_pw1", j].shape[2], bias=conv_b_pw1[j].reshape(1, -1))
            cvec = _vec8([conv_b_dw[j], conv_ln_g[j], conv_ln_b[j]], D)
            v_glu, cv, z = _conv_fwd(u, wdw_pad(j), cvec, name="conv_fwd")
            ymix = _mm(z, wfull("conv_w_pw2", j), mode="nn", name="pw2_fwd", bias=conv_b_pw2[j].reshape(1, -1))
            sv.update(u=u, v_glu=v_glu, cv=cv, z=z, cvec=cvec)
        else:
            proj = _mm(h, Wg["ret_w_in", j], mode="nn", name="win_fwd", b3d=True, out_dtype=BF16,
                       tn=Wg["ret_w_in", j].shape[2])
            yr, y2, states = _ret_fwd(proj, cos_t, sin_t, tables, gn_g_full[j], gn_b_full[j], name="ret_fwd")
            ymix = _mm(y2, wfull("ret_w_out", j), mode="nn", name="wout_fwd")
            sv.update(proj=proj, yr=yr, y2=y2, states=states)
        vec_b = _vec8([mod[l, 2], norm_mlp_g[l], mod[l, 3], mod[l, 4]], D)
        xb, h2 = _row_fwd(xa, ymix, vec_b, name="row_fwd")
        ra, p = _mm(h2, Wg["mlp_w1", l], mode="nn", name="w1_fwd", b3d=True, tn=Wg["mlp_w1", l].shape[2],
                    epi="relu2")
        mo = _mm(p, wfull("mlp_w2", l), mode="nn", name="w2_fwd")
        sv.update(ymix=ymix, xb=xb, h2=h2, ra=ra, p=p, mo=mo, vec_b=vec_b)
        saved.append(sv)
        xa, y_prev, gate_prev = xb, mo, mod[l, 5]

    fvec = _vec8([gate_prev, final_norm_g], D)
    dx, dyb, fpart = _final(xa, y_prev, target, fvec, name="final")
    loss = lax.psum(jnp.sum(fpart[2]), ("x", "y", "c"))
    G = {nm: [None] * W[nm].shape[0] for nm in BIG}
    dmod = [[None] * 6 for _ in range(depth)]
    dmod[depth - 1][5] = fpart[0]
    sg = dict(norm_mix_g=[None] * depth, norm_mlp_g=[None] * depth, final_norm_g=fpart[1])
    n_conv, n_ret = conv_w_pw1.shape[0], ret_w_in.shape[0]
    for nm in ("conv_b_pw1", "conv_w_dw", "conv_b_dw", "conv_ln_g", "conv_ln_b", "conv_b_pw2"):
        sg[nm] = [None] * n_conv
    for nm in ("ret_gn_g", "ret_gn_b"):
        sg[nm] = [None] * n_ret

    half_idx = mc.astype(jnp.int32).reshape(1)
    chip_idx = chip.astype(jnp.int32).reshape(1)

    def chip_sums(keys, tag):
        flat = [G[nm][i] for nm, i in keys]
        sib = _swap_half(flat, name="swap_grads_" + tag)
        sums = [_add_half(a, b, half_idx, name="add_grads") for a, b in zip(flat, sib)]
        shapes = [jax.ShapeDtypeStruct((N_CHIPS - 1,) + s.shape[1:], BF16) for s in sums]
        return sums, shapes

    launch = {depth // 2 - 1: list(range(depth // 2, depth))}
    launch.update({l - 1: [l] for l in range(1, depth // 2)})
    early_keys = layer_weights(0)
    in_flight = []

    for l in reversed(range(depth)):
        j = l // 2
        sv = saved[l]
        if l in launch:
            keys = [k for ll in launch[l] for k in layer_weights(ll)]
            sums, shapes = chip_sums(keys, f"from{launch[l][0]}")
            started, tok = _exchange_start(sums, shapes, _plan_scatter, 3 * len(sums), fpart,
                                           name=f"scatter_start_{launch[l][0]}")
            in_flight.append((keys, started, launch[l][0]))
            sv["vec_b"] = sv["vec_b"] + tok[:, :1]
        w1, w2 = Wg["mlp_w1", l], wfull("mlp_w2", l)
        cs1 = w1.shape[2]
        da = _mm(dyb, w2, mode="nt", name="w2_dx", out_dtype=BF16, epi="mul2", extra=sv["ra"])
        gw2 = _mm(sv["p"], dyb, mode="tn", name="w2_dw")
        G["mlp_w2"][l] = gw2.reshape(N_CHIPS, gw2.shape[0] // N_CHIPS, gw2.shape[1])
        G["mlp_w1"][l] = _mm(sv["h2"], da, mode="tn", name="w1_dw", out3d=(N_CHIPS, cs1), tn=cs1)
        dh2 = _mm(da, w1, mode="nt", name="w1_dx", b3d=True, tk=cs1, out_dtype=BF16)
        dx, dyb, part = _row_bwd(sv["xb"], dh2, dx, sv["ymix"], sv["vec_b"], name="row_bwd")
        dmod[l][2], sg["norm_mlp_g"][l], dmod[l][3], dmod[l][4] = part[0], part[1], part[2], part[3]
        if l % 2 == 0:
            sg["conv_b_pw2"][j] = part[4]
            wp1, wp2 = Wg["conv_w_pw1", j], wfull("conv_w_pw2", j)
            csp = wp1.shape[2]
            dz = _mm(dyb, wp2, mode="nt", name="pw2_dx")
            gp2 = _mm(sv["z"], dyb, mode="tn", name="pw2_dw")
            G["conv_w_pw2"][j] = gp2.reshape(N_CHIPS, gp2.shape[0] // N_CHIPS, gp2.shape[1])
            du, dwdw, cpart, dbu = _conv_bwd(dz, sv["cv"], sv["v_glu"], sv["u"], wdw_pad(j), sv["cvec"],
                                             name="conv_bwd")
            sg["conv_w_dw"][j] = dwdw[:CONV_WIDTH]
            sg["conv_b_dw"][j], sg["conv_ln_g"][j], sg["conv_ln_b"][j] = cpart[0], cpart[1], cpart[2]
            sg["conv_b_pw1"][j] = dbu[0]
            G["conv_w_pw1"][j] = _mm(sv["h"], du, mode="tn", name="pw1_dw", out3d=(N_CHIPS, csp), tn=csp)
            dh = _mm(du, wp1, mode="nt", name="pw1_dx", b3d=True, tk=csp, out_dtype=BF16)
        else:
            wi, wo = Wg["ret_w_in", j], wfull("ret_w_out", j)
            csi = wi.shape[2]
            dy2 = _mm(dyb, wo, mode="nt", name="wout_dx")
            gwo = _mm(sv["y2"], dyb, mode="tn", name="wout_dw")
            G["ret_w_out"][j] = gwo.reshape(N_CHIPS, gwo.shape[0] // N_CHIPS, gwo.shape[1])
            dq, dkk, dvv, dgt, dgg, dgb = _ret_bwd(sv["proj"], cos_t, sin_t, tables, gn_g_full[j], gn_b_full[j],
                                                   sv["yr"], dy2, sv["states"], name="ret_bwd")
            sg["ret_gn_g"][j], sg["ret_gn_b"][j] = dgg.reshape(H, dv), dgb.reshape(H, dv)
            dproj = jnp.concatenate([dq, dkk, dvv, dgt], axis=1)
            G["ret_w_in"][j] = _mm(sv["h"], dproj, mode="tn", name="win_dw", out3d=(N_CHIPS, csi), tn=csi)
            dh = _mm(dproj, wi, mode="nt", name="win_dx", b3d=True, tk=csi, out_dtype=BF16)
        yp = saved[l - 1]["mo"] if l > 0 else dh
        dx, dyb, part = _row_bwd(sv["xa"], dh, dx, yp, sv["vec_a"], name="row_bwd")
        sg["norm_mix_g"][l], dmod[l][0], dmod[l][1] = part[1], part[2], part[3]
        if l > 0:
            dmod[l - 1][5] = part[0]
    grad_x = dx.reshape(x.shape)

    dmod_me = jnp.stack([jnp.stack(r) for r in dmod]).reshape(depth, 6 * D)
    sgrads = dict(ada_b=dmod_me)
    for nm in SMALL[1:]:
        sgrads[nm] = sg[nm] if nm == "final_norm_g" else jnp.stack(sg[nm])
    full_shapes = [sgrads[nm].shape for nm in SMALL]
    packed_all = _allgather8(_pack([sgrads[nm] for nm in SMALL]), name="gather_small_grads")
    sums = _unpack(_sum_slots(packed_all[None], name="sum_small_grads"), full_shapes)
    gsm = {}
    for nm, g in zip(SMALL, sums):
        if nm in SMALL_SHARDED:
            n = g.shape[-1] // N_CHIPS
            g = lax.dynamic_slice_in_dim(g.reshape(g.shape[:-1] + (N_CHIPS, n)), chip, 1, axis=g.ndim - 1)
            g = g.reshape(g.shape[:-2] + (n,))
        gsm[nm] = g.reshape(W[nm].shape)
    pw, pm, pv, pg = (_pack([t[nm] for nm in SMALL]) for t in (W, Mo, Vo, gsm))
    e4 = lambda a: a.reshape((1, 1) + a.shape)
    e3 = lambda a: a.reshape((1,) + a.shape)
    sres = _adamw(e4(pg), e3(pw), e3(pm), e3(pv), name="adamw_small")
    shard_shapes = [W[nm].shape for nm in SMALL]
    small_out = [dict(zip(SMALL, _unpack(r, shard_shapes))) for r in sres]

    n_mod_rows = depth * 6 * D // 128
    dmod_all = packed_all[:, :n_mod_rows].reshape(N_DEV, depth, N_CHIPS, cs_ada)
    dmod_cols = lax.dynamic_slice_in_dim(dmod_all, chip, 1, axis=2).reshape(N_DEV, depth * cs_ada)
    kpad = 128 - N_DEV
    dmod_pad = jnp.concatenate([dmod_cols, jnp.zeros((kpad, depth * cs_ada), F32)], axis=0)
    ct_pad = jnp.concatenate([c_all.T, jnp.zeros((D, kpad), F32)], axis=1)
    g_ada = _mm(ct_pad, dmod_pad, mode="nn", name="ada_dw", a_silu=True, out3d=(depth, cs_ada), tn=cs_ada)
    ada_out = _adamw(g_ada.reshape(depth, 1, D, cs_ada), ada_w, m_ada_w, v_ada_w, name="adamw_ada")

    sums, shapes = chip_sums(early_keys, "from0")
    early_lands = _exchange(sums, shapes, _plan_scatter, 3 * len(sums), name="scatter_grads")
    done = [(early_keys, sums, early_lands)]
    for keys, started, first in in_flight:
        own, lands = _exchange_wait(started, len(keys), _plan_scatter, early_lands[-1], name=f"scatter_wait_{first}")
        done.append((keys, own, lands))
    total = {}
    for keys, own, lands in done:
        for k, o, r in zip(keys, own, lands):
            total[k] = _sum_own(o, r, chip_idx, name="sum_grads")
    halves = [jnp.stack([total[nm, i] for i in range(len(G[nm]))]) for nm in BIG]
    sib_halves = _swap_sibling(halves, name="swap_totals")
    big_out = {nm: _adamw_halves(hm, hs, half_idx, W[nm], Mo[nm], Vo[nm], name="adamw_big")
               for nm, hm, hs in zip(BIG, halves, sib_halves)}

    def res(nm, i):
        if nm == "ada_w":
            return ada_out[i]
        if nm in big_out:
            return big_out[nm][i]
        return small_out[i][nm]

    return (loss, grad_x, *[res(nm, 0) for nm in WEIGHTS], *[res(nm, 1) for nm in WEIGHTS],
            *[res(nm, 2) for nm in WEIGHTS], *[res(nm, 3) for nm in WEIGHTS])
```

```python
import functools
import math

import jax
import jax.numpy as jnp
from jax import lax
from jax.experimental import pallas as pl
from jax.experimental.pallas import tpu as pltpu

F32 = jnp.float32
BF16 = jnp.bfloat16
MESH = pl.DeviceIdType.MESH

EPS = 1e-6
CHUNK = 64
CONV_WIDTH = 31
CONV_HALO = 32
SUBLANES = 8
LANES = 128
CONV_BLOCK_ROWS = 128
RET_HEADS = 4
ROPE_BASE = 10000.0
ADAM_LR = 0.001
ADAM_B1 = 0.9
ADAM_B2 = 0.999
ADAM_EPS = 1e-08
ADAM_WD = 0.01
ADAM_STEP = 10
N_CHIPS = 4
N_DEV = 8
V7X_VMEM_LIMIT = 48 * 1024 * 1024
DW_TOKENS = 2048
ANY = pl.BlockSpec(memory_space=pl.ANY)


def _cparams(sem=None):
    return pltpu.CompilerParams(dimension_semantics=sem, vmem_limit_bytes=V7X_VMEM_LIMIT)


def _sigmoid(x):
    return jax.nn.sigmoid(x)


def _silu(x):
    return x * _sigmoid(x)


_DIMS = {
    "nn": (((1,), (0,)), ((), ())),
    "nt": (((1,), (1,)), ((), ())),
    "tn": (((0,), (0,)), ((), ())),
}


def _mm(a, b, *, mode, name, out_dtype=F32, tm=2048, tn=1024, tk=1024, b3d=False, out3d=None,
        bias=None, epi=None, extra=None, a_silu=False):
    if mode == "tn":
        K, M = a.shape
    else:
        M, K = a.shape
    if b3d:
        P, R, Cs = b.shape
        bshape = (R, P * Cs)
    else:
        bshape = b.shape
    N = bshape[0] if mode == "nt" else bshape[1]
    assert (bshape[1] if mode == "nt" else bshape[0]) == K, (name, a.shape, b.shape)
    tm, tn, tk = min(tm, M), min(tn, N), min(tk, K)
    assert M % tm == 0 and N % tn == 0 and K % tk == 0, (name, M, N, K, tm, tn, tk)
    nk = K // tk

    if mode == "tn":
        a_spec = pl.BlockSpec((tk, tm), lambda i, j, k: (k, i))
    else:
        a_spec = pl.BlockSpec((tm, tk), lambda i, j, k: (i, k))
    if mode == "nt":
        if b3d:
            nb = Cs // tk
            assert Cs % tk == 0
            b_spec = pl.BlockSpec((None, tn, tk), lambda i, j, k: (k // nb, j, k % nb))
        else:
            b_spec = pl.BlockSpec((tn, tk), lambda i, j, k: (j, k))
    else:
        if b3d:
            nb = Cs // tn
            assert Cs % tn == 0
            b_spec = pl.BlockSpec((None, tk, tn), lambda i, j, k: (j // nb, k, j % nb))
        else:
            b_spec = pl.BlockSpec((tk, tn), lambda i, j, k: (k, j))
    in_specs = [a_spec, b_spec]
    args = [a, b]
    if bias is not None:
        in_specs.append(pl.BlockSpec((1, tn), lambda i, j, k: (0, j)))
        args.append(bias)
    if extra is not None:
        in_specs.append(pl.BlockSpec((tm, tn), lambda i, j, k: (i, j)))
        args.append(extra)

    if out3d is not None:
        P_o, Cs_o = out3d
        assert P_o * Cs_o == N and Cs_o % tn == 0
        nbo = Cs_o // tn
        o_spec = pl.BlockSpec((None, tm, tn), lambda i, j, k: (j // nbo, i, j % nbo))
        o_shape = (P_o, M, Cs_o)
    else:
        o_spec = pl.BlockSpec((tm, tn), lambda i, j, k: (i, j))
        o_shape = (M, N)
    if epi == "relu2":
        out_shape = [jax.ShapeDtypeStruct(o_shape, BF16), jax.ShapeDtypeStruct(o_shape, BF16)]
        out_specs = [o_spec, o_spec]
    else:
        out_shape = jax.ShapeDtypeStruct(o_shape, out_dtype)
        out_specs = o_spec
    n_out = 2 if epi == "relu2" else 1
    dims = _DIMS[mode]
    has_bias, has_extra = bias is not None, extra is not None

    def body(*refs):
        a_ref, b_ref = refs[0], refs[1]
        pos = 2
        bias_ref = extra_ref = None
        if has_bias:
            bias_ref = refs[pos]
            pos += 1
        if has_extra:
            extra_ref = refs[pos]
            pos += 1
        outs = refs[pos:pos + n_out]
        acc_ref = refs[pos + n_out] if nk > 1 else None

        def partial():
            av = a_ref[...]
            if a_silu:
                av = _silu(av)
            return lax.dot_general(av, b_ref[...], dims, preferred_element_type=F32)

        def finish(r):
            if has_bias:
                r = r + bias_ref[...]
            if epi == "relu2":
                rr = jnp.maximum(r, 0.0)
                outs[0][...] = rr.astype(BF16)
                outs[1][...] = (rr * rr).astype(BF16)
            elif epi == "mul2":
                outs[0][...] = (r * 2.0 * extra_ref[...].astype(F32)).astype(outs[0].dtype)
            else:
                outs[0][...] = r.astype(outs[0].dtype)

        if nk == 1:
            finish(partial())
        else:
            k = pl.program_id(2)

            @pl.when(k == 0)
            def _():
                acc_ref[...] = jnp.zeros_like(acc_ref)

            acc_ref[...] += partial()

            @pl.when(k == nk - 1)
            def _():
                finish(acc_ref[...])

    return pl.pallas_call(
        body, name=name, grid=(M // tm, N // tn, nk), in_specs=in_specs, out_specs=out_specs,
        out_shape=out_shape,
        scratch_shapes=[pltpu.VMEM((tm, tn), F32)] if nk > 1 else [],
        compiler_params=_cparams(("parallel", "parallel", "arbitrary")),
    )(*args)


def _modnorm(x, gain, shift, scale):
    y = x * lax.rsqrt(jnp.mean(x * x, axis=-1, keepdims=True) + EPS)
    return (y * gain) * (1.0 + scale) + shift


def _row_fwd(xprev, y, vec, *, name, ts=512):
    S, D = xprev.shape
    ts = min(ts, S)
    has_res = y is not None
    row = pl.BlockSpec((ts, D), lambda i: (i, 0))
    vspec = pl.BlockSpec((8, D), lambda i: (0, 0))

    def body(*refs):
        if has_res:
            xp_ref, y_ref, v_ref, x_ref, h_ref = refs
            x = xp_ref[...] + v_ref[0:1, :] * y_ref[...]
            x_ref[...] = x
        else:
            xp_ref, v_ref, h_ref = refs
            x = xp_ref[...]
        h_ref[...] = _modnorm(x, v_ref[1:2, :], v_ref[2:3, :], v_ref[3:4, :]).astype(BF16)

    if has_res:
        return pl.pallas_call(
            body, name=name, grid=(S // ts,), in_specs=[row, row, vspec], out_specs=[row, row],
            out_shape=[jax.ShapeDtypeStruct((S, D), F32), jax.ShapeDtypeStruct((S, D), BF16)],
            compiler_params=_cparams(("parallel",)),
        )(xprev, y, vec)
    h = pl.pallas_call(
        body, name=name, grid=(S // ts,), in_specs=[row, vspec], out_specs=row,
        out_shape=jax.ShapeDtypeStruct((S, D), BF16),
        compiler_params=_cparams(("parallel",)),
    )(xprev, vec)
    return xprev, h


def _row_bwd(xin, dh, dxout, yprev, vec, *, name, ts=512):
    S, D = xin.shape
    ts = min(ts, S)
    row = pl.BlockSpec((ts, D), lambda i: (i, 0))
    vspec = pl.BlockSpec((8, D), lambda i: (0, 0))

    def body(x_ref, dh_ref, dx_ref, y_ref, v_ref, dxin_ref, dy_ref, part_ref):
        @pl.when(pl.program_id(0) == 0)
        def _():
            part_ref[...] = jnp.zeros_like(part_ref)

        gate = v_ref[0:1, :]
        _, vjp = jax.vjp(_modnorm, x_ref[...], v_ref[1:2, :], v_ref[2:3, :], v_ref[3:4, :])
        dxn, dgain, dshift, dscale = vjp(dh_ref[...].astype(F32))
        dxin = dx_ref[...] + dxn
        dxin_ref[...] = dxin
        dy = dxin * gate
        dy_ref[...] = dy.astype(BF16)
        part_ref[0:1, :] += jnp.sum(dxin * y_ref[...], axis=0, keepdims=True)
        part_ref[1:2, :] += dgain
        part_ref[2:3, :] += dshift
        part_ref[3:4, :] += dscale
        part_ref[4:5, :] += jnp.sum(dy, axis=0, keepdims=True)

    return pl.pallas_call(
        body, name=name, grid=(S // ts,), in_specs=[row, row, row, row, vspec],
        out_specs=[row, row, vspec],
        out_shape=[jax.ShapeDtypeStruct((S, D), F32), jax.ShapeDtypeStruct((S, D), BF16),
                   jax.ShapeDtypeStruct((8, D), F32)],
        compiler_params=_cparams(("arbitrary",)),
    )(xin, dh, dxout, yprev, vec)


def _final(xprev, y, target, vec, *, name, ts=512):
    S, D = xprev.shape
    ts = min(ts, S)
    row = pl.BlockSpec((ts, D), lambda i: (i, 0))
    vspec = pl.BlockSpec((8, D), lambda i: (0, 0))

    def norm(x, gain):
        return x * lax.rsqrt(jnp.mean(x * x, axis=-1, keepdims=True) + EPS) * gain

    def body(xp_ref, y_ref, t_ref, v_ref, dx_ref, dy_ref, part_ref):
        @pl.when(pl.program_id(0) == 0)
        def _():
            part_ref[...] = jnp.zeros_like(part_ref)

        gate = v_ref[0:1, :]
        yv = y_ref[...]
        x = xp_ref[...] + gate * yv
        out, vjp = jax.vjp(norm, x, v_ref[1:2, :])
        err = out - t_ref[...]
        dx, dgain = vjp(err * (1.0 / D))
        dx_ref[...] = dx
        dy_ref[...] = (dx * gate).astype(BF16)
        part_ref[0:1, :] += jnp.sum(dx * yv, axis=0, keepdims=True)
        part_ref[1:2, :] += dgain
        part_ref[2:3, :] += jnp.sum(err * err, axis=0, keepdims=True) * (0.5 / D)

    return pl.pallas_call(
        body, name=name, grid=(S // ts,), in_specs=[row, row, row, vspec], out_specs=[row, row, vspec],
        out_shape=[jax.ShapeDtypeStruct((S, D), F32), jax.ShapeDtypeStruct((S, D), BF16),
                   jax.ShapeDtypeStruct((8, D), F32)],
        compiler_params=_cparams(("arbitrary",)),
    )(xprev, y, target, vec)


def _ln_silu(cv, g, b):
    mu = jnp.mean(cv, axis=-1, keepdims=True)
    var = jnp.mean(jnp.square(cv - mu), axis=-1, keepdims=True)
    u = (cv - mu) * lax.rsqrt(var + EPS) * g + b
    return _silu(u)


def _shift_copies(ext, sh, n):
    for b in range(1, SUBLANES):
        sh[b - 1, 0:n, :] = ext[pl.ds(b, n), :]


def _shifted(ext, sh, off, r0, rows, cols):
    a, b = divmod(off, SUBLANES)
    if b == 0:
        return ext[pl.ds(SUBLANES * a + r0, rows), cols]
    return sh[b - 1, pl.ds(SUBLANES * a + r0, rows), cols]


def _conv_fwd(u, wdw, vec, *, name, ts=256):
    S, D2 = u.shape
    D = D2 // 2
    ts = min(ts, S)
    H = CONV_HALO
    row = pl.BlockSpec((ts, D), lambda i: (i, 0))

    rb_rows = min(CONV_BLOCK_ROWS, ts)

    def body(u_ref, w_ref, v_ref, vo_ref, cv_ref, z_ref, ext, sh):
        @pl.when(pl.program_id(0) == 0)
        def _():
            ext[0:H, :] = jnp.zeros((H, D), F32)

        uu = u_ref[...]
        v = uu[:, :D] * _sigmoid(uu[:, D:])
        vo_ref[...] = v
        ext[H:H + ts, :] = v
        _shift_copies(ext, sh, ts + H - 8)
        for r0 in range(0, ts, rb_rows):
            for c0 in range(0, D, LANES):
                cols = pl.ds(c0, LANES)
                acc = jnp.zeros((rb_rows, LANES), F32)
                for t in range(CONV_WIDTH):
                    src = _shifted(ext, sh, H - (CONV_WIDTH - 1) + t, r0, rb_rows, cols)
                    acc = acc + src * w_ref[pl.ds(t, 1), cols]
                cv_ref[pl.ds(r0, rb_rows), cols] = acc + v_ref[0:1, cols]
        z_ref[...] = _ln_silu(cv_ref[...], v_ref[1:2, :], v_ref[2:3, :]).astype(BF16)
        ext[0:H, :] = ext[ts:ts + H, :]

    return pl.pallas_call(
        body, name=name, grid=(S // ts,),
        in_specs=[pl.BlockSpec((ts, D2), lambda i: (i, 0)), pl.BlockSpec((H, D), lambda i: (0, 0)),
                  pl.BlockSpec((8, D), lambda i: (0, 0))],
        out_specs=[row, row, row],
        out_shape=[jax.ShapeDtypeStruct((S, D), F32), jax.ShapeDtypeStruct((S, D), F32),
                   jax.ShapeDtypeStruct((S, D), BF16)],
        scratch_shapes=[pltpu.VMEM((ts + H, D), F32), pltpu.VMEM((7, ts + H - 8, D), F32)],
        compiler_params=_cparams(("arbitrary",)),
    )(u, wdw, vec)


def _conv_bwd(dz, cv, v, u, wdw, vec, *, name, ts=256):
    S, D = cv.shape
    ts = min(ts, S)
    H = CONV_HALO
    nt = S // ts
    per = ts // H
    rev = lambda i: (nt - 1 - i, 0)
    row = pl.BlockSpec((ts, D), rev)

    rb_rows = min(CONV_BLOCK_ROWS, ts)
    nsh = ts + H - 8

    def body(dz_ref, cv_ref, v_ref, vh_ref, u_ref, w_ref, vec_ref, du_ref, dw_ref, part_ref, dbu_ref,
             dext, vext, dsh, vsh, dwacc, dvbuf):
        i = pl.program_id(0)

        @pl.when(i == 0)
        def _():
            dext[ts:ts + H, :] = jnp.zeros((H, D), F32)
            dwacc[...] = jnp.zeros_like(dwacc)
            part_ref[...] = jnp.zeros_like(part_ref)
            dbu_ref[...] = jnp.zeros_like(dbu_ref)

        _, vjp = jax.vjp(_ln_silu, cv_ref[...], vec_ref[1:2, :], vec_ref[2:3, :])
        dcv, dg, db = vjp(dz_ref[...])
        part_ref[0:1, :] += jnp.sum(dcv, axis=0, keepdims=True)
        part_ref[1:2, :] += dg
        part_ref[2:3, :] += db
        dext[0:ts, :] = dcv
        vext[0:H, :] = vh_ref[...] * jnp.where(i == nt - 1, 0.0, 1.0)
        vext[H:H + ts, :] = v_ref[...]
        _shift_copies(dext, dsh, nsh)
        _shift_copies(vext, vsh, nsh)
        for r0 in range(0, ts, rb_rows):
            for c0 in range(0, D, LANES):
                cols = pl.ds(c0, LANES)
                dblk = dext[pl.ds(r0, rb_rows), cols]
                dv = jnp.zeros((rb_rows, LANES), F32)
                for t in range(CONV_WIDTH):
                    prod = dblk * _shifted(vext, vsh, H - (CONV_WIDTH - 1) + t, r0, rb_rows, cols)
                    parts = [prod[s:s + SUBLANES, :] for s in range(0, rb_rows, SUBLANES)]
                    while len(parts) > 1:
                        parts = [parts[k] + parts[k + 1] for k in range(0, len(parts), 2)]
                    dwacc[pl.ds(t * SUBLANES, SUBLANES), cols] += parts[0]
                    dv = dv + _shifted(dext, dsh, CONV_WIDTH - 1 - t, r0, rb_rows, cols) * w_ref[pl.ds(t, 1), cols]
                dvbuf[pl.ds(r0, rb_rows), cols] = dv
        dv = dvbuf[...]
        uu = u_ref[...]
        a, g = uu[:, :D], uu[:, D:]
        sg = _sigmoid(g)
        da = dv * sg
        dg_ = dv * a * sg * (1.0 - sg)
        du = jnp.concatenate([da, dg_], axis=-1)
        du_ref[...] = du.astype(BF16)
        dbu_ref[0:1, :] += jnp.sum(du, axis=0, keepdims=True)
        dext[ts:ts + H, :] = dext[0:H, :]

        @pl.when(i == nt - 1)
        def _():
            dw_ref[...] = jnp.zeros_like(dw_ref)
            for t in range(CONV_WIDTH):
                dw_ref[pl.ds(t, 1), :] = jnp.sum(dwacc[pl.ds(t * SUBLANES, SUBLANES), :], axis=0, keepdims=True)

    return pl.pallas_call(
        body, name=name, grid=(nt,),
        in_specs=[row, row, row,
                  pl.BlockSpec((H, D), lambda i: (jnp.maximum((nt - 1 - i) * per - 1, 0), 0)),
                  pl.BlockSpec((ts, 2 * D), rev), pl.BlockSpec((H, D), lambda i: (0, 0)),
                  pl.BlockSpec((8, D), lambda i: (0, 0))],
        out_specs=[pl.BlockSpec((ts, 2 * D), rev), pl.BlockSpec((H, D), lambda i: (0, 0)),
                   pl.BlockSpec((8, D), lambda i: (0, 0)), pl.BlockSpec((8, 2 * D), lambda i: (0, 0))],
        out_shape=[jax.ShapeDtypeStruct((S, 2 * D), BF16), jax.ShapeDtypeStruct((H, D), F32),
                   jax.ShapeDtypeStruct((8, D), F32), jax.ShapeDtypeStruct((8, 2 * D), F32)],
        scratch_shapes=[pltpu.VMEM((ts + H, D), F32), pltpu.VMEM((ts + H, D), F32),
                        pltpu.VMEM((7, nsh, D), F32), pltpu.VMEM((7, nsh, D), F32),
                        pltpu.VMEM((CONV_WIDTH * SUBLANES, D), F32), pltpu.VMEM((ts, D), F32)],
        compiler_params=_cparams(("arbitrary",)),
    )(dz, cv, v, v, u, wdw, vec)


def _rope(x, c, s, half):
    x1, x2 = x[:, :half], x[:, half:]
    return jnp.concatenate([x1 * c - x2 * s, x2 * c + x1 * s], axis=-1)


def _rope_t(d, c, s, half):
    d1, d2 = d[:, :half], d[:, half:]
    return jnp.concatenate([d1 * c + d2 * s, d2 * c - d1 * s], axis=-1)


def _gn_gate(y, gate, g, b):
    mu = jnp.mean(y, axis=-1, keepdims=True)
    var = jnp.mean(jnp.square(y - mu), axis=-1, keepdims=True)
    return _silu(gate) * ((y - mu) * lax.rsqrt(var + EPS) * g + b)


def _dot(a, b, mode="nn"):
    return lax.dot_general(a, b, _DIMS[mode], preferred_element_type=F32)


def _ret_tables(H):
    lg = jnp.log(1.0 - 2.0 ** (-5.0 - jnp.arange(H, dtype=F32)))
    idx = jnp.arange(CHUNK, dtype=F32)
    dmat = jnp.exp(lg[:, None, None] * jnp.abs(idx[:, None] - idx[None, :]))
    xi = jnp.exp(lg[:, None] * (idx + 1.0))[..., None]
    zeta = jnp.exp(lg[:, None] * (CHUNK - 1.0 - idx))[..., None]
    dec = jnp.exp(lg * CHUNK)[:, None, None]
    return dmat, xi, zeta, dec


RET_HEADS_PER_STEP = 4


def _ret_specs(R, dk, dv, half, hps, order):
    C = CHUNK
    ng = RET_HEADS // hps
    return dict(
        q=pl.BlockSpec((R, hps * dk), lambda h, n: (order(n), h)),
        k=pl.BlockSpec((R, hps * dk), lambda h, n: (order(n), ng + h)),
        v=pl.BlockSpec((R, hps * dv), lambda h, n: (order(n), ng + h)),
        gate=pl.BlockSpec((R, hps * dv), lambda h, n: (order(n), 2 * ng + h)),
        rope=pl.BlockSpec((R, half), lambda h, n: (order(n), 0)),
        dmat=pl.BlockSpec((hps, C, C), lambda h, n: (h, 0, 0)),
        col=pl.BlockSpec((hps, C, 1), lambda h, n: (h, 0, 0)),
        one=pl.BlockSpec((hps, 1, 1), lambda h, n: (h, 0, 0)),
        gn=pl.BlockSpec((hps, 1, dv), lambda h, n: (h, 0, 0)),
        yv=pl.BlockSpec((R, hps * dv), lambda h, n: (order(n), h)),
        yk=pl.BlockSpec((R, hps * dk), lambda h, n: (order(n), h)),
    )


def _ret_fwd(proj, cos, sin, tables, gn_g, gn_b, *, name, cps=4):
    S = proj.shape[0]
    D = proj.shape[1] // 6
    H, C, hps = RET_HEADS, CHUNK, RET_HEADS_PER_STEP
    dk, dv, half = D // H, 2 * D // H, D // H // 2
    nc = S // C
    cps = min(cps, nc)
    R = cps * C
    scale = dk ** -0.5
    sp = _ret_specs(R, dk, dv, half, hps, lambda n: n)
    dmat, xi, zeta, dec = tables

    def body(q_ref, k_ref, v_ref, g_ref, cos_ref, sin_ref, dm_ref, xi_ref, ze_ref, dec_ref, gg_ref, gb_ref,
             y_ref, y2_ref, st_ref, state):
        @pl.when(pl.program_id(1) == 0)
        def _():
            state[...] = jnp.zeros_like(state)

        for j in range(cps):
            rows = pl.ds(j * C, C)
            cs, sn = cos_ref[rows, :], sin_ref[rows, :]
            for hh in range(hps):
                ck, cv = pl.ds(hh * dk, dk), pl.ds(hh * dv, dv)
                dm, xv, zv, dc = dm_ref[hh], xi_ref[hh], ze_ref[hh], dec_ref[hh]
                qr = _rope(q_ref[rows, ck].astype(F32), cs, sn, half)
                kr = _rope(k_ref[rows, ck].astype(F32), cs, sn, half) * scale
                vb = v_ref[rows, cv]
                p = (_dot(qr.astype(BF16), kr.astype(BF16), "nt") * dm).astype(BF16)
                st = state[hh]
                stb = st.astype(BF16)
                st_ref[hh, j] = stb
                y = _dot(p, vb) + _dot((qr * xv).astype(BF16), stb)
                state[hh] = st * dc + _dot((kr * zv).astype(BF16), vb, "tn")
                y_ref[rows, cv] = y
                y2_ref[rows, cv] = _gn_gate(y, g_ref[rows, cv].astype(F32), gg_ref[hh], gb_ref[hh]).astype(BF16)

    return pl.pallas_call(
        body, name=name, grid=(H // hps, nc // cps),
        in_specs=[sp["q"], sp["k"], sp["v"], sp["gate"], sp["rope"], sp["rope"], sp["dmat"], sp["col"],
                  sp["col"], sp["one"], sp["gn"], sp["gn"]],
        out_specs=[sp["yv"], sp["yv"], pl.BlockSpec((hps, cps, dk, dv), lambda h, n: (h, n, 0, 0))],
        out_shape=[jax.ShapeDtypeStruct((S, 2 * D), F32), jax.ShapeDtypeStruct((S, 2 * D), BF16),
                   jax.ShapeDtypeStruct((H, nc, dk, dv), BF16)],
        scratch_shapes=[pltpu.VMEM((hps, dk, dv), F32)],
        compiler_params=_cparams(("arbitrary", "arbitrary")),
    )(proj, proj, proj, proj, cos, sin, dmat, xi, zeta, dec, gn_g, gn_b)


def _ret_bwd(proj, cos, sin, tables, gn_g, gn_b, y, dy2, states, *, name, cps=4):
    S = proj.shape[0]
    D = proj.shape[1] // 6
    H, C, hps = RET_HEADS, CHUNK, RET_HEADS_PER_STEP
    dk, dv, half = D // H, 2 * D // H, D // H // 2
    nc = S // C
    cps = min(cps, nc)
    ns = nc // cps
    R = cps * C
    scale = dk ** -0.5
    order = lambda n: ns - 1 - n
    sp = _ret_specs(R, dk, dv, half, hps, order)
    dmat, xi, zeta, dec = tables

    def body(q_ref, k_ref, v_ref, g_ref, cos_ref, sin_ref, dm_ref, xi_ref, ze_ref, dec_ref, gg_ref, gb_ref,
             y_ref, dy2_ref, st_ref, dq_ref, dk_ref, dv_ref, dg_ref, dgg_ref, dgb_ref, gst):
        @pl.when(pl.program_id(1) == 0)
        def _():
            gst[...] = jnp.zeros_like(gst)
            dgg_ref[...] = jnp.zeros_like(dgg_ref)
            dgb_ref[...] = jnp.zeros_like(dgb_ref)

        for j in reversed(range(cps)):
            rows = pl.ds(j * C, C)
            cs, sn = cos_ref[rows, :], sin_ref[rows, :]
            for hh in range(hps):
                ck, cv = pl.ds(hh * dk, dk), pl.ds(hh * dv, dv)
                dm, xv, zv, dc = dm_ref[hh], xi_ref[hh], ze_ref[hh], dec_ref[hh]
                _, vjp = jax.vjp(_gn_gate, y_ref[rows, cv], g_ref[rows, cv].astype(F32), gg_ref[hh], gb_ref[hh])
                dy, dgate, dgg, dgb = vjp(dy2_ref[rows, cv])
                dgg_ref[hh] += dgg
                dgb_ref[hh] += dgb
                dg_ref[rows, cv] = dgate.astype(BF16)
                dyb = dy.astype(BF16)
                qr = _rope(q_ref[rows, ck].astype(F32), cs, sn, half)
                kr = _rope(k_ref[rows, ck].astype(F32), cs, sn, half) * scale
                qb, kb, vb = qr.astype(BF16), kr.astype(BF16), v_ref[rows, cv]
                p = (_dot(qb, kb, "nt") * dm).astype(BF16)
                g = gst[hh]
                gb16 = g.astype(BF16)
                sprev = st_ref[hh, j]
                dvv = _dot(p, dyb, "tn") + _dot((kr * zv).astype(BF16), gb16)
                dpb = (_dot(dyb, vb, "nt") * dm).astype(BF16)
                dqr = _dot(dpb, kb) + _dot(dyb, sprev, "nt") * xv
                dkr = _dot(dpb, qb, "tn") + _dot(vb, gb16, "nt") * zv
                gst[hh] = g * dc + _dot((qr * xv).astype(BF16), dyb, "tn")
                dq_ref[rows, ck] = _rope_t(dqr, cs, sn, half).astype(BF16)
                dk_ref[rows, ck] = _rope_t(dkr * scale, cs, sn, half).astype(BF16)
                dv_ref[rows, cv] = dvv.astype(BF16)

    return pl.pallas_call(
        body, name=name, grid=(H // hps, ns),
        in_specs=[sp["q"], sp["k"], sp["v"], sp["gate"], sp["rope"], sp["rope"], sp["dmat"], sp["col"],
                  sp["col"], sp["one"], sp["gn"], sp["gn"], sp["yv"], sp["yv"],
                  pl.BlockSpec((hps, cps, dk, dv), lambda h, n: (h, order(n), 0, 0))],
        out_specs=[sp["yk"], sp["yk"], sp["yv"], sp["yv"], sp["gn"], sp["gn"]],
        out_shape=[jax.ShapeDtypeStruct((S, D), BF16), jax.ShapeDtypeStruct((S, D), BF16),
                   jax.ShapeDtypeStruct((S, 2 * D), BF16), jax.ShapeDtypeStruct((S, 2 * D), BF16),
                   jax.ShapeDtypeStruct((H, 1, dv), F32), jax.ShapeDtypeStruct((H, 1, dv), F32)],
        scratch_shapes=[pltpu.VMEM((hps, dk, dv), F32)],
        compiler_params=_cparams(("arbitrary", "arbitrary")),
    )(proj, proj, proj, proj, cos, sin, dmat, xi, zeta, dec, gn_g, gn_b, y, dy2, states)


def _rows_tile(rows, cols, n_arrays):
    cap = max(8, V7X_VMEM_LIMIT // 3 // (n_arrays * 2 * 4 * cols))
    t = rows
    while t > cap and t % 2 == 0:
        t //= 2
    return t


def _add_half(g, r, half_idx, *, name):
    P, R, Cc = g.shape
    hR = R // 2
    tr = _rows_tile(hR, Cc, 3)
    nb = hR // tr

    def body(h_ref, g_ref, r_ref, o_ref):
        o_ref[...] = (g_ref[...] + r_ref[...]).astype(BF16)

    return pl.pallas_call(
        body, name=name,
        grid_spec=pltpu.PrefetchScalarGridSpec(
            num_scalar_prefetch=1, grid=(P, nb),
            in_specs=[pl.BlockSpec((None, tr, Cc), lambda s, i, h: (s, h[0] * nb + i, 0)),
                      pl.BlockSpec((None, tr, Cc), lambda s, i, h: (s, i, 0))],
            out_specs=pl.BlockSpec((None, tr, Cc), lambda s, i, h: (s, i, 0))),
        out_shape=jax.ShapeDtypeStruct((P, hR, Cc), BF16), compiler_params=_cparams(("parallel", "parallel")),
    )(half_idx, g, r)


def _sum_slots(x, *, name):
    L, NS, R, Cc = x.shape
    tr = _rows_tile(R, Cc, NS + 1)

    def body(x_ref, o_ref):
        acc = x_ref[0].astype(F32)
        for s in range(1, NS):
            acc = acc + x_ref[s].astype(F32)
        o_ref[...] = acc

    return pl.pallas_call(
        body, name=name, grid=(L, R // tr),
        in_specs=[pl.BlockSpec((None, NS, tr, Cc), lambda l, i: (l, 0, i, 0))],
        out_specs=pl.BlockSpec((None, tr, Cc), lambda l, i: (l, i, 0)),
        out_shape=jax.ShapeDtypeStruct((L, R, Cc), F32), compiler_params=_cparams(("parallel", "parallel")),
    )(x)


def _adam_store(g, w_ref, m_ref, v_ref, go_ref, d_ref, mo_ref, vo_ref):
    mn = ADAM_B1 * m_ref[...] + (1.0 - ADAM_B1) * g
    vn = ADAM_B2 * v_ref[...] + (1.0 - ADAM_B2) * jnp.square(g)
    m_hat = mn / (1.0 - ADAM_B1 ** ADAM_STEP)
    v_hat = vn / (1.0 - ADAM_B2 ** ADAM_STEP)
    go_ref[...] = g
    d_ref[...] = -ADAM_LR * (m_hat / (jnp.sqrt(v_hat) + ADAM_EPS) + ADAM_WD * w_ref[...])
    mo_ref[...] = mn
    vo_ref[...] = vn


def _adamw(gslots, w, m, v, *, name):
    L, NS, R, Cc = gslots.shape
    tr = _rows_tile(R, Cc, NS + 7)
    gspec = pl.BlockSpec((None, NS, tr, Cc), lambda l, i: (l, 0, i, 0))
    spec = pl.BlockSpec((None, tr, Cc), lambda l, i: (l, i, 0))

    def body(g_ref, *refs):
        g = g_ref[0]
        for s in range(1, NS):
            g = g + g_ref[s]
        _adam_store(g, *refs)

    sd = jax.ShapeDtypeStruct((L, R, Cc), F32)
    return pl.pallas_call(
        body, name=name, grid=(L, R // tr), in_specs=[gspec, spec, spec, spec],
        out_specs=[spec, spec, spec, spec], out_shape=[sd, sd, sd, sd],
        compiler_params=_cparams(("parallel", "parallel")),
    )(gslots, w, m, v)


def _adamw_halves(g_mine, g_sib, half_idx, w, m, v, *, name):
    L, hR, Cc = g_mine.shape
    tr = _rows_tile(hR, Cc, 9)
    nbh = hR // tr
    gspec = pl.BlockSpec((None, tr, Cc), lambda l, i, h: (l, i % nbh, 0))
    spec = pl.BlockSpec((None, tr, Cc), lambda l, i, h: (l, i, 0))

    def body(h_ref, gm_ref, gs_ref, *refs):
        mine = (pl.program_id(1) // nbh) == h_ref[0]
        _adam_store(jnp.where(mine, gm_ref[...], gs_ref[...]), *refs)

    sd = jax.ShapeDtypeStruct((L, 2 * hR, Cc), F32)
    return pl.pallas_call(
        body, name=name,
        grid_spec=pltpu.PrefetchScalarGridSpec(
            num_scalar_prefetch=1, grid=(L, 2 * nbh), in_specs=[gspec, gspec, spec, spec, spec],
            out_specs=[spec, spec, spec, spec]),
        out_shape=[sd, sd, sd, sd], compiler_params=_cparams(("parallel", "parallel")),
    )(half_idx, g_mine, g_sib, w, m, v)


def _me():
    return lax.axis_index("x"), lax.axis_index("y"), lax.axis_index("c")


def _flip(v, bit):
    return 1 - v if bit else v


def _allgather8(x, *, name):
    def body(x_ref, out_ref, send_sems, recv_sems, loc_sem):
        mx, my, mc = _me()
        me = 4 * mx + 2 * my + mc
        loc = pltpu.make_async_copy(x_ref, out_ref.at[me], loc_sem)
        loc.start()
        sends, recvs = [], []
        for k in range(1, N_DEV):
            px, py, pc = _flip(mx, k & 4), _flip(my, k & 2), _flip(mc, k & 1)
            sends.append(pltpu.make_async_remote_copy(
                src_ref=x_ref, dst_ref=out_ref.at[me], send_sem=send_sems.at[k - 1],
                recv_sem=recv_sems.at[k - 1], device_id=(px, py, pc), device_id_type=MESH))
            recvs.append(pltpu.make_async_remote_copy(
                src_ref=x_ref, dst_ref=out_ref.at[4 * px + 2 * py + pc], send_sem=send_sems.at[k - 1],
                recv_sem=recv_sems.at[k - 1], device_id=(px, py, pc), device_id_type=MESH))
        for cp in sends:
            cp.start()
        for cp in recvs:
            cp.wait_recv()
        for cp in sends:
            cp.wait_send()
        loc.wait()

    return pl.pallas_call(
        body, name=name, in_specs=[ANY], out_specs=ANY,
        out_shape=jax.ShapeDtypeStruct((N_DEV,) + x.shape, x.dtype),
        scratch_shapes=[pltpu.SemaphoreType.DMA((N_DEV - 1,)), pltpu.SemaphoreType.DMA((N_DEV - 1,)),
                        pltpu.SemaphoreType.DMA],
    )(x)


def _gather_chips(arrays, after, *, name):
    n = len(arrays)

    def body(*refs):
        ins, outs = refs[:n], refs[n + 1:2 * n + 1]
        ici_send, ici_recv, d2d_send, d2d_recv, own_send, own_recv = refs[2 * n + 1:]
        mx, my, mc = _me()
        me = 2 * mx + my
        sib = (mx, my, 1 - mc)
        locs, sends, lands, passes, gifts = [], [], [], [], []
        for a in range(n):
            h = arrays[a].shape[0] // 2
            mine, other = pl.ds(mc * h, h), pl.ds((1 - mc) * h, h)
            locs.append(pltpu.make_async_remote_copy(
                src_ref=ins[a], dst_ref=outs[a].at[me], send_sem=own_send.at[a], recv_sem=own_recv.at[a],
                device_id=sib, device_id_type=MESH))
            for k in range(1, N_CHIPS):
                px, py = _flip(mx, k & 2), _flip(my, k & 1)
                peer = 2 * px + py
                ici = dict(send_sem=ici_send.at[a, k - 1], recv_sem=ici_recv.at[a, k - 1],
                           device_id=(px, py, mc), device_id_type=MESH)
                d2d = dict(send_sem=d2d_send.at[a, k - 1], recv_sem=d2d_recv.at[a, k - 1],
                           device_id=sib, device_id_type=MESH)
                sends.append(pltpu.make_async_remote_copy(
                    src_ref=ins[a].at[mine], dst_ref=outs[a].at[me, mine], **ici))
                lands.append(pltpu.make_async_remote_copy(
                    src_ref=ins[a].at[mine], dst_ref=outs[a].at[peer, mine], **ici))
                passes.append(pltpu.make_async_remote_copy(
                    src_ref=outs[a].at[peer, mine], dst_ref=outs[a].at[peer, mine], **d2d))
                gifts.append(pltpu.make_async_remote_copy(
                    src_ref=outs[a].at[peer, other], dst_ref=outs[a].at[peer, other], **d2d))
        for cp in locs + sends:
            cp.start()
        for land, fwd in zip(lands, passes):
            land.wait_recv()
            fwd.start()
        for cp in gifts + locs:
            cp.wait_recv()
        for cp in sends + passes + locs:
            cp.wait_send()

    nsem = (n, N_CHIPS - 1)
    return pl.pallas_call(
        body, name=name, in_specs=[ANY] * (n + 1), out_specs=[ANY] * n,
        out_shape=[jax.ShapeDtypeStruct((N_CHIPS,) + a.shape, a.dtype) for a in arrays],
        scratch_shapes=[pltpu.SemaphoreType.DMA(nsem), pltpu.SemaphoreType.DMA(nsem), pltpu.SemaphoreType.DMA(nsem),
                        pltpu.SemaphoreType.DMA(nsem), pltpu.SemaphoreType.DMA((n,)), pltpu.SemaphoreType.DMA((n,))],
    )(*arrays, after)


def _swap_half(arrays, *, name):
    n = len(arrays)

    def body(*refs):
        ins, outs = refs[:n], refs[n:2 * n]
        send_sems, recv_sems = refs[2 * n:]
        mx, my, mc = _me()
        cps = []
        for a in range(n):
            P, R, _ = arrays[a].shape
            cps.append(pltpu.make_async_remote_copy(
                src_ref=ins[a].at[pl.ds(0, P), pl.ds((1 - mc) * (R // 2), R // 2)], dst_ref=outs[a],
                send_sem=send_sems.at[a], recv_sem=recv_sems.at[a],
                device_id=(mx, my, 1 - mc), device_id_type=MESH))
        for cp in cps:
            cp.start()
        for cp in cps:
            cp.wait_recv()
        for cp in cps:
            cp.wait_send()

    return pl.pallas_call(
        body, name=name, in_specs=[ANY] * n, out_specs=[ANY] * n,
        out_shape=[jax.ShapeDtypeStruct((a.shape[0], a.shape[1] // 2, a.shape[2]), a.dtype) for a in arrays],
        scratch_shapes=[pltpu.SemaphoreType.DMA((n,)), pltpu.SemaphoreType.DMA((n,))],
    )(*arrays)


def _swap_sibling(arrays, *, name):
    n = len(arrays)

    def body(*refs):
        ins, outs = refs[:n], refs[n:2 * n]
        send_sems, recv_sems = refs[2 * n:]
        mx, my, mc = _me()
        cps = [pltpu.make_async_remote_copy(
            src_ref=ins[a], dst_ref=outs[a], send_sem=send_sems.at[a], recv_sem=recv_sems.at[a],
            device_id=(mx, my, 1 - mc), device_id_type=MESH) for a in range(n)]
        for cp in cps:
            cp.start()
        for cp in cps:
            cp.wait_recv()
        for cp in cps:
            cp.wait_send()

    return pl.pallas_call(
        body, name=name, in_specs=[ANY] * n, out_specs=[ANY] * n,
        out_shape=[jax.ShapeDtypeStruct(a.shape, a.dtype) for a in arrays],
        scratch_shapes=[pltpu.SemaphoreType.DMA((n,)), pltpu.SemaphoreType.DMA((n,))],
    )(*arrays)


def _plan_scatter(srcs, lands):
    mx, my, mc = _me()
    copies = []
    for a in range(len(srcs)):
        for k in range(1, N_CHIPS):
            px, py = _flip(mx, k & 2), _flip(my, k & 1)
            copies.append((srcs[a].at[2 * px + py], lands[a].at[k - 1], lands[a].at[k - 1], (px, py, mc)))
    return copies


def _plan_gather(srcs, lands):
    mx, my, mc = _me()
    me = 2 * mx + my
    copies = []
    for a in range(len(srcs)):
        copies.append((srcs[a], lands[a].at[me], lands[a].at[me], (mx, my, 1 - mc)))
        for k in range(1, N_CHIPS):
            px, py = _flip(mx, k & 2), _flip(my, k & 1)
            copies.append((srcs[a], lands[a].at[me], lands[a].at[2 * px + py], (px, py, mc)))
    return copies


def _copy(c, k, send_sems, recv_sems, landing=False):
    src, dst, land, dev = c
    return pltpu.make_async_remote_copy(src_ref=src, dst_ref=land if landing else dst, send_sem=send_sems.at[k],
                                        recv_sem=recv_sems.at[k], device_id=dev, device_id_type=MESH)


def _exchange(srcs, land_shapes, plan, ncopies, *, name):
    ni, nl = len(srcs), len(land_shapes)

    def body(*refs):
        send_sems, recv_sems = refs[ni + nl:]
        copies = plan(refs[:ni], refs[ni:ni + nl])
        for k, c in enumerate(copies):
            _copy(c, k, send_sems, recv_sems).start()
        for k, c in enumerate(copies):
            _copy(c, k, send_sems, recv_sems, landing=True).wait_recv()
        for k, c in enumerate(copies):
            _copy(c, k, send_sems, recv_sems).wait_send()

    return pl.pallas_call(
        body, name=name, in_specs=[ANY] * ni, out_specs=[ANY] * nl, out_shape=list(land_shapes),
        scratch_shapes=[pltpu.SemaphoreType.DMA((ncopies,)), pltpu.SemaphoreType.DMA((ncopies,))],
    )(*srcs)


HBM_SPEC = pl.BlockSpec(memory_space=pltpu.HBM)
SEM_SPEC = pl.BlockSpec(memory_space=pltpu.SEMAPHORE)
SPLIT_EFFECT = pltpu.SideEffectType.DATAFLOW_SIDE_EFFECTING


def _exchange_start(srcs, land_shapes, plan, ncopies, after, *, name):
    ni, nl = len(srcs), len(land_shapes)

    def body(*refs):
        in_refs, land_refs = refs[:ni], refs[ni:ni + nl]
        send_sems, recv_sems = refs[ni + nl + 1], refs[ni + nl + 2]
        token = refs[-1]
        for k, c in enumerate(plan(in_refs, land_refs)):
            _copy(c, k, send_sems, recv_sems).start()
        token[...] = jnp.zeros_like(token)

    bufs = [pltpu.with_memory_space_constraint(a, pltpu.HBM) for a in srcs]
    bufs += [pltpu.with_memory_space_constraint(lax.empty(s.shape, s.dtype), pltpu.HBM) for s in land_shapes]
    outs = pl.pallas_call(
        body, name=name,
        in_specs=[HBM_SPEC] * (ni + nl) + [ANY],
        out_specs=(SEM_SPEC, SEM_SPEC, *[HBM_SPEC] * (ni + nl), pl.BlockSpec(memory_space=pltpu.VMEM)),
        out_shape=(pltpu.SemaphoreType.DMA((ncopies,)), pltpu.SemaphoreType.DMA((ncopies,)),
                   *[pltpu.HBM(b.shape, b.dtype) for b in bufs], jax.ShapeDtypeStruct((8, 128), F32)),
        input_output_aliases={i: 2 + i for i in range(ni + nl)},
        compiler_params=pltpu.CompilerParams(has_side_effects=SPLIT_EFFECT),
    )(*bufs, after)
    return outs[:-1], outs[-1]


def _exchange_wait(started, ni, plan, after, *, name):
    send_sems, recv_sems = started[0], started[1]
    bufs = list(started[2:])
    nb = len(bufs)

    def body(*refs):
        in_refs, land_refs = refs[:ni], refs[ni:nb]
        send, recv = refs[nb], refs[nb + 1]
        for k, c in enumerate(plan(in_refs, land_refs)):
            cp = _copy(c, k, send, recv, landing=True)
            cp.wait_send()
            cp.wait_recv()

    outs = pl.pallas_call(
        body, name=name, in_specs=[HBM_SPEC] * nb + [SEM_SPEC, SEM_SPEC, ANY], out_specs=[HBM_SPEC] * nb,
        out_shape=[pltpu.HBM(b.shape, b.dtype) for b in bufs],
        input_output_aliases={i: i for i in range(nb)},
        compiler_params=pltpu.CompilerParams(has_side_effects=SPLIT_EFFECT),
    )(*bufs, send_sems, recv_sems, after)
    return list(outs[:ni]), list(outs[ni:])


def _sum_own(own, recv, chip_idx, *, name):
    _, R, Cc = own.shape
    tr = _rows_tile(R, Cc, 5)

    def body(s_ref, o_ref, r_ref, t_ref):
        acc = o_ref[...].astype(F32)
        for s in range(N_CHIPS - 1):
            acc = acc + r_ref[s].astype(F32)
        t_ref[...] = acc

    return pl.pallas_call(
        body, name=name,
        grid_spec=pltpu.PrefetchScalarGridSpec(
            num_scalar_prefetch=1, grid=(R // tr,),
            in_specs=[pl.BlockSpec((None, tr, Cc), lambda i, s: (s[0], i, 0)),
                      pl.BlockSpec((N_CHIPS - 1, tr, Cc), lambda i, s: (0, i, 0))],
            out_specs=pl.BlockSpec((tr, Cc), lambda i, s: (i, 0))),
        out_shape=jax.ShapeDtypeStruct((R, Cc), F32), compiler_params=_cparams(("parallel",)),
    )(chip_idx, own, recv)


BIG = ("conv_w_pw1", "conv_w_pw2", "ret_w_in", "ret_w_out", "mlp_w1", "mlp_w2")
COLS = ("conv_w_pw1", "ret_w_in", "mlp_w1")
SMALL = ("ada_b", "norm_mix_g", "norm_mlp_g", "conv_b_pw1", "conv_w_dw", "conv_b_dw", "conv_ln_g", "conv_ln_b",
         "conv_b_pw2", "ret_gn_g", "ret_gn_b", "final_norm_g")
SMALL_SHARDED = ("conv_w_dw", "ret_gn_g", "ret_gn_b")
WEIGHTS = ("ada_w", "ada_b", "norm_mix_g", "norm_mlp_g", "conv_w_pw1", "conv_b_pw1", "conv_w_dw", "conv_b_dw",
           "conv_ln_g", "conv_ln_b", "conv_w_pw2", "conv_b_pw2", "ret_w_in", "ret_gn_g", "ret_gn_b", "ret_w_out",
           "mlp_w1", "mlp_w2", "final_norm_g")


def _vec8(rows, D):
    rows = [r.reshape(1, D).astype(F32) for r in rows]
    return jnp.concatenate(rows + [jnp.zeros((8 - len(rows), D), F32)], axis=0)


def _unshard_last(g):
    nd = g.ndim
    t = jnp.transpose(g, tuple(range(1, nd - 1)) + (0, nd - 1))
    return t.reshape(t.shape[:-2] + (t.shape[-2] * t.shape[-1],))


def _pack(parts):
    flat = jnp.concatenate([p.reshape(-1).astype(F32) for p in parts])
    pad = (-flat.shape[0]) % 1024
    return jnp.concatenate([flat, jnp.zeros((pad,), F32)]).reshape(-1, 128)


def _unpack(packed, shapes):
    flat = packed.reshape(-1)
    out, pos = [], 0
    for s in shapes:
        n = math.prod(s)
        out.append(flat[pos:pos + n].reshape(s))
        pos += n
    return out


def kernel(x, c, ada_w, ada_b, norm_mix_g, norm_mlp_g, conv_w_pw1, conv_b_pw1, conv_w_dw, conv_b_dw, conv_ln_g, conv_ln_b, conv_w_pw2, conv_b_pw2, ret_w_in, ret_gn_g, ret_gn_b, ret_w_out, mlp_w1, mlp_w2, final_norm_g, loss_target, m_ada_w, m_ada_b, m_norm_mix_g, m_norm_mlp_g, m_conv_w_pw1, m_conv_b_pw1, m_conv_w_dw, m_conv_b_dw, m_conv_ln_g, m_conv_ln_b, m_conv_w_pw2, m_conv_b_pw2, m_ret_w_in, m_ret_gn_g, m_ret_gn_b, m_ret_w_out, m_mlp_w1, m_mlp_w2, m_final_norm_g, v_ada_w, v_ada_b, v_norm_mix_g, v_norm_mlp_g, v_conv_w_pw1, v_conv_b_pw1, v_conv_w_dw, v_conv_b_dw, v_conv_ln_g, v_conv_ln_b, v_conv_w_pw2, v_conv_b_pw2, v_ret_w_in, v_ret_gn_g, v_ret_gn_b, v_ret_w_out, v_mlp_w1, v_mlp_w2, v_final_norm_g):
    W = dict(ada_w=ada_w, ada_b=ada_b, norm_mix_g=norm_mix_g, norm_mlp_g=norm_mlp_g, conv_w_pw1=conv_w_pw1,
             conv_b_pw1=conv_b_pw1, conv_w_dw=conv_w_dw, conv_b_dw=conv_b_dw, conv_ln_g=conv_ln_g,
             conv_ln_b=conv_ln_b, conv_w_pw2=conv_w_pw2, conv_b_pw2=conv_b_pw2, ret_w_in=ret_w_in,
             ret_gn_g=ret_gn_g, ret_gn_b=ret_gn_b, ret_w_out=ret_w_out, mlp_w1=mlp_w1, mlp_w2=mlp_w2,
             final_norm_g=final_norm_g)
    Mo = dict(ada_w=m_ada_w, ada_b=m_ada_b, norm_mix_g=m_norm_mix_g, norm_mlp_g=m_norm_mlp_g,
              conv_w_pw1=m_conv_w_pw1, conv_b_pw1=m_conv_b_pw1, conv_w_dw=m_conv_w_dw, conv_b_dw=m_conv_b_dw,
              conv_ln_g=m_conv_ln_g, conv_ln_b=m_conv_ln_b, conv_w_pw2=m_conv_w_pw2, conv_b_pw2=m_conv_b_pw2,
              ret_w_in=m_ret_w_in, ret_gn_g=m_ret_gn_g, ret_gn_b=m_ret_gn_b, ret_w_out=m_ret_w_out,
              mlp_w1=m_mlp_w1, mlp_w2=m_mlp_w2, final_norm_g=m_final_norm_g)
    Vo = dict(ada_w=v_ada_w, ada_b=v_ada_b, norm_mix_g=v_norm_mix_g, norm_mlp_g=v_norm_mlp_g,
              conv_w_pw1=v_conv_w_pw1, conv_b_pw1=v_conv_b_pw1, conv_w_dw=v_conv_w_dw, conv_b_dw=v_conv_b_dw,
              conv_ln_g=v_conv_ln_g, conv_ln_b=v_conv_ln_b, conv_w_pw2=v_conv_w_pw2, conv_b_pw2=v_conv_b_pw2,
              ret_w_in=v_ret_w_in, ret_gn_g=v_ret_gn_g, ret_gn_b=v_ret_gn_b, ret_w_out=v_ret_w_out,
              mlp_w1=v_mlp_w1, mlp_w2=v_mlp_w2, final_norm_g=v_final_norm_g)

    S, D = x.shape[1], x.shape[2]
    depth = ada_w.shape[0]
    H = RET_HEADS
    dv = 2 * D // H
    xs = x.reshape(S, D)
    target = loss_target.reshape(S, D)
    mx, my, mc = _me()
    chip = 2 * mx + my
    dev = 4 * mx + 2 * my + mc

    def layer_weights(l):
        mixer = ("conv_w_pw1", "conv_w_pw2") if l % 2 == 0 else ("ret_w_in", "ret_w_out")
        return [(nm, l // 2) for nm in mixer] + [("mlp_w1", l), ("mlp_w2", l)]

    c_all = _allgather8(c.reshape(8, D // 8), name="gather_c").reshape(N_DEV, D)
    cs_ada = ada_w.shape[2]
    bias_sh = lax.dynamic_slice_in_dim(ada_b.reshape(depth, N_CHIPS, cs_ada), chip, 1, axis=1)
    mod_sh = _mm(c_all, ada_w, mode="nn", name="ada_fwd", b3d=True, tn=cs_ada, a_silu=True,
                 bias=bias_sh.reshape(1, depth * cs_ada))
    mod_all = _allgather8(mod_sh, name="gather_mod")[0::2]
    mod_me = lax.dynamic_slice_in_dim(mod_all, dev, 1, axis=1).reshape(N_CHIPS, depth, cs_ada)
    mod = jnp.transpose(mod_me, (1, 0, 2)).reshape(depth, 6, D)

    keys0 = layer_weights(0)[:2]
    got0 = _gather_chips([W[nm][i].astype(BF16) for nm, i in keys0] + [W[nm] for nm in SMALL_SHARDED], mod,
                         name="gather_weights")
    Wg = dict(zip(keys0, got0))
    full_small = {nm: _unshard_last(got0[len(keys0) + i]) for i, nm in enumerate(SMALL_SHARDED)}
    pending, order = {}, got0[0]
    for tag, keys in [("0m", layer_weights(0)[2:])] + [(l, layer_weights(l)) for l in range(1, depth)]:
        srcs = [W[nm][i].astype(BF16) for nm, i in keys]
        shapes = [jax.ShapeDtypeStruct((N_CHIPS,) + s.shape, BF16) for s in srcs]
        started, order = _exchange_start(srcs, shapes, _plan_gather, 4 * len(srcs), order,
                                         name=f"gather_start_{tag}")
        pending[tag] = (keys, started)
    mod = mod + order[0, 0]

    def arrive(tag, after):
        keys, started = pending.pop(tag)
        _, lands = _exchange_wait(started, len(keys), _plan_gather, after, name=f"gather_wait_{tag}")
        Wg.update(zip(keys, lands))

    def wfull(nm, l):
        g = Wg[nm, l]
        return g.reshape(g.shape[0] * g.shape[1], g.shape[2])

    pos_ids = jnp.arange(S, dtype=F32)
    dk = D // H
    inv = ROPE_BASE ** (-jnp.arange(0, dk, 2, dtype=F32) / dk)
    ang = pos_ids[:, None] * inv[None, :]
    cos_t, sin_t = jnp.cos(ang), jnp.sin(ang)
    tables = _ret_tables(H)
    gn_g_full = full_small["ret_gn_g"].reshape(-1, H, 1, dv)
    gn_b_full = full_small["ret_gn_b"].reshape(-1, H, 1, dv)
    wdw_full = full_small["conv_w_dw"]

    def wdw_pad(j):
        return jnp.concatenate([wdw_full[j], jnp.zeros((CONV_HALO - CONV_WIDTH, D), F32)], axis=0)

    saved = []
    xa, y_prev, gate_prev = xs, None, None
    for l in range(depth):
        j = l // 2
        sv = {}
        if l in pending:
            arrive(l, y_prev)
        vec_a = _vec8([gate_prev if gate_prev is not None else jnp.zeros((D,), F32), norm_mix_g[l], mod[l, 0],
                       mod[l, 1]], D)
        xa, h = _row_fwd(xa, y_prev, vec_a, name="row_fwd" if y_prev is not None else "row_fwd_first")
        sv.update(xa=xa, h=h, vec_a=vec_a)
        if l % 2 == 0:
            u = _mm(h, Wg["conv_w_pw1", j], mode="nn", name="pw1_fwd", b3d=True,
                    tn=Wg["conv_w_pw1", j].shape[2], bias=conv_b_pw1[j].reshape(1, -1))
            cvec = _vec8([conv_b_dw[j], conv_ln_g[j], conv_ln_b[j]], D)
            v_glu, cv, z = _conv_fwd(u, wdw_pad(j), cvec, name="conv_fwd")
            ymix = _mm(z, wfull("conv_w_pw2", j), mode="nn", name="pw2_fwd", bias=conv_b_pw2[j].reshape(1, -1))
            sv.update(u=u, v_glu=v_glu, cv=cv, z=z, cvec=cvec)
        else:
            proj = _mm(h, Wg["ret_w_in", j], mode="nn", name="win_fwd", b3d=True, out_dtype=BF16,
                       tn=Wg["ret_w_in", j].shape[2])
            yr, y2, states = _ret_fwd(proj, cos_t, sin_t, tables, gn_g_full[j], gn_b_full[j], name="ret_fwd")
            ymix = _mm(y2, wfull("ret_w_out", j), mode="nn", name="wout_fwd")
            sv.update(proj=proj, yr=yr, y2=y2, states=states)
        if f"{l}m" in pending:
            arrive(f"{l}m", ymix)
        vec_b = _vec8([mod[l, 2], norm_mlp_g[l], mod[l, 3], mod[l, 4]], D)
        xb, h2 = _row_fwd(xa, ymix, vec_b, name="row_fwd")
        ra, p = _mm(h2, Wg["mlp_w1", l], mode="nn", name="w1_fwd", b3d=True, tn=Wg["mlp_w1", l].shape[2],
                    epi="relu2")
        mo = _mm(p, wfull("mlp_w2", l), mode="nn", name="w2_fwd")
        sv.update(ymix=ymix, xb=xb, h2=h2, ra=ra, p=p, mo=mo, vec_b=vec_b)
        saved.append(sv)
        xa, y_prev, gate_prev = xb, mo, mod[l, 5]

    fvec = _vec8([gate_prev, final_norm_g], D)
    dx, dyb, fpart = _final(xa, y_prev, target, fvec, name="final")
    loss = lax.psum(jnp.sum(fpart[2]), ("x", "y", "c"))
    G = {nm: [None] * W[nm].shape[0] for nm in BIG}
    dmod = [[None] * 6 for _ in range(depth)]
    dmod[depth - 1][5] = fpart[0]
    sg = dict(norm_mix_g=[None] * depth, norm_mlp_g=[None] * depth, final_norm_g=fpart[1])
    n_conv, n_ret = conv_w_pw1.shape[0], ret_w_in.shape[0]
    for nm in ("conv_b_pw1", "conv_w_dw", "conv_b_dw", "conv_ln_g", "conv_ln_b", "conv_b_pw2"):
        sg[nm] = [None] * n_conv
    for nm in ("ret_gn_g", "ret_gn_b"):
        sg[nm] = [None] * n_ret

    half_idx = mc.astype(jnp.int32).reshape(1)
    chip_idx = chip.astype(jnp.int32).reshape(1)

    def chip_sums(keys, tag):
        flat = [G[nm][i] for nm, i in keys]
        sib = _swap_half(flat, name="swap_grads_" + tag)
        sums = [_add_half(a, b, half_idx, name="add_grads") for a, b in zip(flat, sib)]
        shapes = [jax.ShapeDtypeStruct((N_CHIPS - 1,) + s.shape[1:], BF16) for s in sums]
        return sums, shapes

    launch = {depth // 2 - 1: list(range(depth // 2, depth))}
    launch.update({l - 1: [l] for l in range(1, depth // 2)})
    early_keys = layer_weights(0)
    in_flight = []

    for l in reversed(range(depth)):
        j = l // 2
        sv = saved[l]
        if l in launch:
            keys = [k for ll in launch[l] for k in layer_weights(ll)]
            sums, shapes = chip_sums(keys, f"from{launch[l][0]}")
            started, tok = _exchange_start(sums, shapes, _plan_scatter, 3 * len(sums), fpart,
                                           name=f"scatter_start_{launch[l][0]}")
            in_flight.append((keys, started, launch[l][0]))
            sv["vec_b"] = sv["vec_b"] + tok[:, :1]
        w1, w2 = Wg["mlp_w1", l], wfull("mlp_w2", l)
        cs1 = w1.shape[2]
        da = _mm(dyb, w2, mode="nt", name="w2_dx", out_dtype=BF16, epi="mul2", extra=sv["ra"])
        gw2 = _mm(sv["p"], dyb, mode="tn", name="w2_dw")
        G["mlp_w2"][l] = gw2.reshape(N_CHIPS, gw2.shape[0] // N_CHIPS, gw2.shape[1])
        G["mlp_w1"][l] = _mm(sv["h2"], da, mode="tn", name="w1_dw", out3d=(N_CHIPS, cs1), tn=cs1, tk=DW_TOKENS)
        dh2 = _mm(da, w1, mode="nt", name="w1_dx", b3d=True, tk=cs1, out_dtype=BF16)
        dx, dyb, part = _row_bwd(sv["xb"], dh2, dx, sv["ymix"], sv["vec_b"], name="row_bwd")
        dmod[l][2], sg["norm_mlp_g"][l], dmod[l][3], dmod[l][4] = part[0], part[1], part[2], part[3]
        if l % 2 == 0:
            sg["conv_b_pw2"][j] = part[4]
            wp1, wp2 = Wg["conv_w_pw1", j], wfull("conv_w_pw2", j)
            csp = wp1.shape[2]
            dz = _mm(dyb, wp2, mode="nt", name="pw2_dx")
            gp2 = _mm(sv["z"], dyb, mode="tn", name="pw2_dw", tk=DW_TOKENS)
            G["conv_w_pw2"][j] = gp2.reshape(N_CHIPS, gp2.shape[0] // N_CHIPS, gp2.shape[1])
            du, dwdw, cpart, dbu = _conv_bwd(dz, sv["cv"], sv["v_glu"], sv["u"], wdw_pad(j), sv["cvec"],
                                             name="conv_bwd")
            sg["conv_w_dw"][j] = dwdw[:CONV_WIDTH]
            sg["conv_b_dw"][j], sg["conv_ln_g"][j], sg["conv_ln_b"][j] = cpart[0], cpart[1], cpart[2]
            sg["conv_b_pw1"][j] = dbu[0]
            G["conv_w_pw1"][j] = _mm(sv["h"], du, mode="tn", name="pw1_dw", out3d=(N_CHIPS, csp), tn=csp,
                                     tk=DW_TOKENS)
            dh = _mm(du, wp1, mode="nt", name="pw1_dx", b3d=True, tk=csp, out_dtype=BF16)
        else:
            wi, wo = Wg["ret_w_in", j], wfull("ret_w_out", j)
            csi = wi.shape[2]
            dy2 = _mm(dyb, wo, mode="nt", name="wout_dx")
            gwo = _mm(sv["y2"], dyb, mode="tn", name="wout_dw")
            G["ret_w_out"][j] = gwo.reshape(N_CHIPS, gwo.shape[0] // N_CHIPS, gwo.shape[1])
            dq, dkk, dvv, dgt, dgg, dgb = _ret_bwd(sv["proj"], cos_t, sin_t, tables, gn_g_full[j], gn_b_full[j],
                                                   sv["yr"], dy2, sv["states"], name="ret_bwd")
            sg["ret_gn_g"][j], sg["ret_gn_b"][j] = dgg.reshape(H, dv), dgb.reshape(H, dv)
            dproj = jnp.concatenate([dq, dkk, dvv, dgt], axis=1)
            G["ret_w_in"][j] = _mm(sv["h"], dproj, mode="tn", name="win_dw", out3d=(N_CHIPS, csi), tn=csi,
                                   tk=DW_TOKENS)
            dh = _mm(dproj, wi, mode="nt", name="win_dx", b3d=True, tk=csi, out_dtype=BF16)
        yp = saved[l - 1]["mo"] if l > 0 else dh
        dx, dyb, part = _row_bwd(sv["xa"], dh, dx, yp, sv["vec_a"], name="row_bwd")
        sg["norm_mix_g"][l], dmod[l][0], dmod[l][1] = part[1], part[2], part[3]
        if l > 0:
            dmod[l - 1][5] = part[0]
    grad_x = dx.reshape(x.shape)

    dmod_me = jnp.stack([jnp.stack(r) for r in dmod]).reshape(depth, 6 * D)
    sgrads = dict(ada_b=dmod_me)
    for nm in SMALL[1:]:
        sgrads[nm] = sg[nm] if nm == "final_norm_g" else jnp.stack(sg[nm])
    full_shapes = [sgrads[nm].shape for nm in SMALL]
    packed_all = _allgather8(_pack([sgrads[nm] for nm in SMALL]), name="gather_small_grads")
    sums = _unpack(_sum_slots(packed_all[None], name="sum_small_grads"), full_shapes)
    gsm = {}
    for nm, g in zip(SMALL, sums):
        if nm in SMALL_SHARDED:
            n = g.shape[-1] // N_CHIPS
            g = lax.dynamic_slice_in_dim(g.reshape(g.shape[:-1] + (N_CHIPS, n)), chip, 1, axis=g.ndim - 1)
            g = g.reshape(g.shape[:-2] + (n,))
        gsm[nm] = g.reshape(W[nm].shape)
    pw, pm, pv, pg = (_pack([t[nm] for nm in SMALL]) for t in (W, Mo, Vo, gsm))
    e4 = lambda a: a.reshape((1, 1) + a.shape)
    e3 = lambda a: a.reshape((1,) + a.shape)
    sres = _adamw(e4(pg), e3(pw), e3(pm), e3(pv), name="adamw_small")
    shard_shapes = [W[nm].shape for nm in SMALL]
    small_out = [dict(zip(SMALL, _unpack(r, shard_shapes))) for r in sres]

    n_mod_rows = depth * 6 * D // 128
    dmod_all = packed_all[:, :n_mod_rows].reshape(N_DEV, depth, N_CHIPS, cs_ada)
    dmod_cols = lax.dynamic_slice_in_dim(dmod_all, chip, 1, axis=2).reshape(N_DEV, depth * cs_ada)
    kpad = 128 - N_DEV
    dmod_pad = jnp.concatenate([dmod_cols, jnp.zeros((kpad, depth * cs_ada), F32)], axis=0)
    ct_pad = jnp.concatenate([c_all.T, jnp.zeros((D, kpad), F32)], axis=1)
    g_ada = _mm(ct_pad, dmod_pad, mode="nn", name="ada_dw", a_silu=True, out3d=(depth, cs_ada), tn=cs_ada)
    ada_out = _adamw(g_ada.reshape(depth, 1, D, cs_ada), ada_w, m_ada_w, v_ada_w, name="adamw_ada")

    sums, shapes = chip_sums(early_keys, "from0")
    early_lands = _exchange(sums, shapes, _plan_scatter, 3 * len(sums), name="scatter_grads")
    done = [(early_keys, sums, early_lands)]
    for keys, started, first in in_flight:
        own, lands = _exchange_wait(started, len(keys), _plan_scatter, early_lands[-1], name=f"scatter_wait_{first}")
        done.append((keys, own, lands))
    total = {}
    for keys, own, lands in done:
        for k, o, r in zip(keys, own, lands):
            total[k] = _sum_own(o, r, chip_idx, name="sum_grads")
    halves = [jnp.stack([total[nm, i] for i in range(len(G[nm]))]) for nm in BIG]
    sib_halves = _swap_sibling(halves, name="swap_totals")
    big_out = {nm: _adamw_halves(hm, hs, half_idx, W[nm], Mo[nm], Vo[nm], name="adamw_big")
               for nm, hm, hs in zip(BIG, halves, sib_halves)}

    def res(nm, i):
        if nm == "ada_w":
            return ada_out[i]
        if nm in big_out:
            return big_out[nm][i]
        return small_out[i][nm]

    return (loss, grad_x, *[res(nm, 0) for nm in WEIGHTS], *[res(nm, 1) for nm in WEIGHTS],
            *[res(nm, 2) for nm in WEIGHTS], *[res(nm, 3) for nm in WEIGHTS])
```

```python
import functools
import math

import jax
import jax.numpy as jnp
from jax import lax
from jax.experimental import pallas as pl
from jax.experimental.pallas import tpu as pltpu

F32 = jnp.float32
BF16 = jnp.bfloat16
MESH = pl.DeviceIdType.MESH

EPS = 1e-6
CHUNK = 64
CONV_WIDTH = 31
CONV_HALO = 32
SUBLANES = 8
LANES = 128
CONV_BLOCK_ROWS = 128
RET_HEADS = 4
ROPE_BASE = 10000.0
ADAM_LR = 0.001
ADAM_B1 = 0.9
ADAM_B2 = 0.999
ADAM_EPS = 1e-08
ADAM_WD = 0.01
ADAM_STEP = 10
N_CHIPS = 4
N_DEV = 8
V7X_VMEM_LIMIT = 48 * 1024 * 1024
DW_TOKENS = 2048
ANY = pl.BlockSpec(memory_space=pl.ANY)


def _cparams(sem=None):
    return pltpu.CompilerParams(dimension_semantics=sem, vmem_limit_bytes=V7X_VMEM_LIMIT)


def _sigmoid(x):
    return jax.nn.sigmoid(x)


def _silu(x):
    return x * _sigmoid(x)


_DIMS = {
    "nn": (((1,), (0,)), ((), ())),
    "nt": (((1,), (1,)), ((), ())),
    "tn": (((0,), (0,)), ((), ())),
}


def _mm(a, b, *, mode, name, out_dtype=F32, tm=2048, tn=1024, tk=1024, b3d=False, out3d=None,
        bias=None, epi=None, extra=None, a_silu=False):
    if mode == "tn":
        K, M = a.shape
    else:
        M, K = a.shape
    if b3d:
        P, R, Cs = b.shape
        bshape = (R, P * Cs)
    else:
        bshape = b.shape
    N = bshape[0] if mode == "nt" else bshape[1]
    assert (bshape[1] if mode == "nt" else bshape[0]) == K, (name, a.shape, b.shape)
    tm, tn, tk = min(tm, M), min(tn, N), min(tk, K)
    assert M % tm == 0 and N % tn == 0 and K % tk == 0, (name, M, N, K, tm, tn, tk)
    nk = K // tk

    if mode == "tn":
        a_spec = pl.BlockSpec((tk, tm), lambda i, j, k: (k, i))
    else:
        a_spec = pl.BlockSpec((tm, tk), lambda i, j, k: (i, k))
    if mode == "nt":
        if b3d:
            nb = Cs // tk
            assert Cs % tk == 0
            b_spec = pl.BlockSpec((None, tn, tk), lambda i, j, k: (k // nb, j, k % nb))
        else:
            b_spec = pl.BlockSpec((tn, tk), lambda i, j, k: (j, k))
    else:
        if b3d:
            nb = Cs // tn
            assert Cs % tn == 0
            b_spec = pl.BlockSpec((None, tk, tn), lambda i, j, k: (j // nb, k, j % nb))
        else:
            b_spec = pl.BlockSpec((tk, tn), lambda i, j, k: (k, j))
    in_specs = [a_spec, b_spec]
    args = [a, b]
    if bias is not None:
        in_specs.append(pl.BlockSpec((1, tn), lambda i, j, k: (0, j)))
        args.append(bias)
    if extra is not None:
        in_specs.append(pl.BlockSpec((tm, tn), lambda i, j, k: (i, j)))
        args.append(extra)

    if out3d is not None:
        P_o, Cs_o = out3d
        assert P_o * Cs_o == N and Cs_o % tn == 0
        nbo = Cs_o // tn
        o_spec = pl.BlockSpec((None, tm, tn), lambda i, j, k: (j // nbo, i, j % nbo))
        o_shape = (P_o, M, Cs_o)
    else:
        o_spec = pl.BlockSpec((tm, tn), lambda i, j, k: (i, j))
        o_shape = (M, N)
    if epi == "relu2":
        out_shape = [jax.ShapeDtypeStruct(o_shape, BF16), jax.ShapeDtypeStruct(o_shape, BF16)]
        out_specs = [o_spec, o_spec]
    else:
        out_shape = jax.ShapeDtypeStruct(o_shape, out_dtype)
        out_specs = o_spec
    n_out = 2 if epi == "relu2" else 1
    dims = _DIMS[mode]
    has_bias, has_extra = bias is not None, extra is not None

    def body(*refs):
        a_ref, b_ref = refs[0], refs[1]
        pos = 2
        bias_ref = extra_ref = None
        if has_bias:
            bias_ref = refs[pos]
            pos += 1
        if has_extra:
            extra_ref = refs[pos]
            pos += 1
        outs = refs[pos:pos + n_out]
        acc_ref = refs[pos + n_out] if nk > 1 else None

        def partial():
            av = a_ref[...]
            if a_silu:
                av = _silu(av)
            return lax.dot_general(av, b_ref[...], dims, preferred_element_type=F32)

        def finish(r):
            if has_bias:
                r = r + bias_ref[...]
            if epi == "relu2":
                rr = jnp.maximum(r, 0.0)
                outs[0][...] = rr.astype(BF16)
                outs[1][...] = (rr * rr).astype(BF16)
            elif epi == "mul2":
                outs[0][...] = (r * 2.0 * extra_ref[...].astype(F32)).astype(outs[0].dtype)
            else:
                outs[0][...] = r.astype(outs[0].dtype)

        if nk == 1:
            finish(partial())
        else:
            k = pl.program_id(2)

            @pl.when(k == 0)
            def _():
                acc_ref[...] = jnp.zeros_like(acc_ref)

            acc_ref[...] += partial()

            @pl.when(k == nk - 1)
            def _():
                finish(acc_ref[...])

    return pl.pallas_call(
        body, name=name, grid=(M // tm, N // tn, nk), in_specs=in_specs, out_specs=out_specs,
        out_shape=out_shape,
        scratch_shapes=[pltpu.VMEM((tm, tn), F32)] if nk > 1 else [],
        compiler_params=_cparams(("parallel", "parallel", "arbitrary")),
    )(*args)


def _modnorm(x, gain, shift, scale):
    y = x * lax.rsqrt(jnp.mean(x * x, axis=-1, keepdims=True) + EPS)
    return (y * gain) * (1.0 + scale) + shift


def _row_fwd(xprev, y, vec, *, name, ts=512):
    S, D = xprev.shape
    ts = min(ts, S)
    has_res = y is not None
    row = pl.BlockSpec((ts, D), lambda i: (i, 0))
    vspec = pl.BlockSpec((8, D), lambda i: (0, 0))

    def body(*refs):
        if has_res:
            xp_ref, y_ref, v_ref, x_ref, h_ref = refs
            x = xp_ref[...] + v_ref[0:1, :] * y_ref[...]
            x_ref[...] = x
        else:
            xp_ref, v_ref, h_ref = refs
            x = xp_ref[...]
        h_ref[...] = _modnorm(x, v_ref[1:2, :], v_ref[2:3, :], v_ref[3:4, :]).astype(BF16)

    if has_res:
        return pl.pallas_call(
            body, name=name, grid=(S // ts,), in_specs=[row, row, vspec], out_specs=[row, row],
            out_shape=[jax.ShapeDtypeStruct((S, D), F32), jax.ShapeDtypeStruct((S, D), BF16)],
            compiler_params=_cparams(("parallel",)),
        )(xprev, y, vec)
    h = pl.pallas_call(
        body, name=name, grid=(S // ts,), in_specs=[row, vspec], out_specs=row,
        out_shape=jax.ShapeDtypeStruct((S, D), BF16),
        compiler_params=_cparams(("parallel",)),
    )(xprev, vec)
    return xprev, h


def _row_bwd(xin, dh, dxout, yprev, vec, *, name, ts=512):
    S, D = xin.shape
    ts = min(ts, S)
    row = pl.BlockSpec((ts, D), lambda i: (i, 0))
    vspec = pl.BlockSpec((8, D), lambda i: (0, 0))

    def body(x_ref, dh_ref, dx_ref, y_ref, v_ref, dxin_ref, dy_ref, part_ref):
        @pl.when(pl.program_id(0) == 0)
        def _():
            part_ref[...] = jnp.zeros_like(part_ref)

        gate = v_ref[0:1, :]
        _, vjp = jax.vjp(_modnorm, x_ref[...], v_ref[1:2, :], v_ref[2:3, :], v_ref[3:4, :])
        dxn, dgain, dshift, dscale = vjp(dh_ref[...].astype(F32))
        dxin = dx_ref[...] + dxn
        dxin_ref[...] = dxin
        dy = dxin * gate
        dy_ref[...] = dy.astype(BF16)
        part_ref[0:1, :] += jnp.sum(dxin * y_ref[...], axis=0, keepdims=True)
        part_ref[1:2, :] += dgain
        part_ref[2:3, :] += dshift
        part_ref[3:4, :] += dscale
        part_ref[4:5, :] += jnp.sum(dy, axis=0, keepdims=True)

    return pl.pallas_call(
        body, name=name, grid=(S // ts,), in_specs=[row, row, row, row, vspec],
        out_specs=[row, row, vspec],
        out_shape=[jax.ShapeDtypeStruct((S, D), F32), jax.ShapeDtypeStruct((S, D), BF16),
                   jax.ShapeDtypeStruct((8, D), F32)],
        compiler_params=_cparams(("arbitrary",)),
    )(xin, dh, dxout, yprev, vec)


def _final(xprev, y, target, vec, *, name, ts=512):
    S, D = xprev.shape
    ts = min(ts, S)
    row = pl.BlockSpec((ts, D), lambda i: (i, 0))
    vspec = pl.BlockSpec((8, D), lambda i: (0, 0))

    def norm(x, gain):
        return x * lax.rsqrt(jnp.mean(x * x, axis=-1, keepdims=True) + EPS) * gain

    def body(xp_ref, y_ref, t_ref, v_ref, dx_ref, dy_ref, part_ref):
        @pl.when(pl.program_id(0) == 0)
        def _():
            part_ref[...] = jnp.zeros_like(part_ref)

        gate = v_ref[0:1, :]
        yv = y_ref[...]
        x = xp_ref[...] + gate * yv
        out, vjp = jax.vjp(norm, x, v_ref[1:2, :])
        err = out - t_ref[...]
        dx, dgain = vjp(err * (1.0 / D))
        dx_ref[...] = dx
        dy_ref[...] = (dx * gate).astype(BF16)
        part_ref[0:1, :] += jnp.sum(dx * yv, axis=0, keepdims=True)
        part_ref[1:2, :] += dgain
        part_ref[2:3, :] += jnp.sum(err * err, axis=0, keepdims=True) * (0.5 / D)

    return pl.pallas_call(
        body, name=name, grid=(S // ts,), in_specs=[row, row, row, vspec], out_specs=[row, row, vspec],
        out_shape=[jax.ShapeDtypeStruct((S, D), F32), jax.ShapeDtypeStruct((S, D), BF16),
                   jax.ShapeDtypeStruct((8, D), F32)],
        compiler_params=_cparams(("arbitrary",)),
    )(xprev, y, target, vec)


def _ln_silu(cv, g, b):
    mu = jnp.mean(cv, axis=-1, keepdims=True)
    var = jnp.mean(jnp.square(cv - mu), axis=-1, keepdims=True)
    u = (cv - mu) * lax.rsqrt(var + EPS) * g + b
    return _silu(u)


def _shift_copies(ext, sh, n):
    for b in range(1, SUBLANES):
        sh[b - 1, 0:n, :] = ext[pl.ds(b, n), :]


def _shifted(ext, sh, off, r0, rows, cols):
    a, b = divmod(off, SUBLANES)
    if b == 0:
        return ext[pl.ds(SUBLANES * a + r0, rows), cols]
    return sh[b - 1, pl.ds(SUBLANES * a + r0, rows), cols]


def _conv_fwd(u, wdw, vec, *, name, ts=256):
    S, D2 = u.shape
    D = D2 // 2
    ts = min(ts, S)
    H = CONV_HALO
    row = pl.BlockSpec((ts, D), lambda i: (i, 0))

    rb_rows = min(CONV_BLOCK_ROWS, ts)

    def body(u_ref, w_ref, v_ref, vo_ref, cv_ref, z_ref, ext, sh):
        @pl.when(pl.program_id(0) == 0)
        def _():
            ext[0:H, :] = jnp.zeros((H, D), F32)

        uu = u_ref[...]
        v = uu[:, :D] * _sigmoid(uu[:, D:])
        vo_ref[...] = v
        ext[H:H + ts, :] = v
        _shift_copies(ext, sh, ts + H - 8)
        for r0 in range(0, ts, rb_rows):
            for c0 in range(0, D, LANES):
                cols = pl.ds(c0, LANES)
                acc = jnp.zeros((rb_rows, LANES), F32)
                for t in range(CONV_WIDTH):
                    src = _shifted(ext, sh, H - (CONV_WIDTH - 1) + t, r0, rb_rows, cols)
                    acc = acc + src * w_ref[pl.ds(t, 1), cols]
                cv_ref[pl.ds(r0, rb_rows), cols] = acc + v_ref[0:1, cols]
        z_ref[...] = _ln_silu(cv_ref[...], v_ref[1:2, :], v_ref[2:3, :]).astype(BF16)
        ext[0:H, :] = ext[ts:ts + H, :]

    return pl.pallas_call(
        body, name=name, grid=(S // ts,),
        in_specs=[pl.BlockSpec((ts, D2), lambda i: (i, 0)), pl.BlockSpec((H, D), lambda i: (0, 0)),
                  pl.BlockSpec((8, D), lambda i: (0, 0))],
        out_specs=[row, row, row],
        out_shape=[jax.ShapeDtypeStruct((S, D), F32), jax.ShapeDtypeStruct((S, D), F32),
                   jax.ShapeDtypeStruct((S, D), BF16)],
        scratch_shapes=[pltpu.VMEM((ts + H, D), F32), pltpu.VMEM((7, ts + H - 8, D), F32)],
        compiler_params=_cparams(("arbitrary",)),
    )(u, wdw, vec)


def _conv_bwd(dz, cv, v, u, wdw, vec, *, name, ts=256):
    S, D = cv.shape
    ts = min(ts, S)
    H = CONV_HALO
    nt = S // ts
    per = ts // H
    rev = lambda i: (nt - 1 - i, 0)
    row = pl.BlockSpec((ts, D), rev)

    rb_rows = min(CONV_BLOCK_ROWS, ts)
    nsh = ts + H - 8

    def body(dz_ref, cv_ref, v_ref, vh_ref, u_ref, w_ref, vec_ref, du_ref, dw_ref, part_ref, dbu_ref,
             dext, vext, dsh, vsh, dwacc, dvbuf):
        i = pl.program_id(0)

        @pl.when(i == 0)
        def _():
            dext[ts:ts + H, :] = jnp.zeros((H, D), F32)
            dwacc[...] = jnp.zeros_like(dwacc)
            part_ref[...] = jnp.zeros_like(part_ref)
            dbu_ref[...] = jnp.zeros_like(dbu_ref)

        _, vjp = jax.vjp(_ln_silu, cv_ref[...], vec_ref[1:2, :], vec_ref[2:3, :])
        dcv, dg, db = vjp(dz_ref[...])
        part_ref[0:1, :] += jnp.sum(dcv, axis=0, keepdims=True)
        part_ref[1:2, :] += dg
        part_ref[2:3, :] += db
        dext[0:ts, :] = dcv
        vext[0:H, :] = vh_ref[...] * jnp.where(i == nt - 1, 0.0, 1.0)
        vext[H:H + ts, :] = v_ref[...]
        _shift_copies(dext, dsh, nsh)
        _shift_copies(vext, vsh, nsh)
        for r0 in range(0, ts, rb_rows):
            for c0 in range(0, D, LANES):
                cols = pl.ds(c0, LANES)
                dblk = dext[pl.ds(r0, rb_rows), cols]
                dv = jnp.zeros((rb_rows, LANES), F32)
                for t in range(CONV_WIDTH):
                    prod = dblk * _shifted(vext, vsh, H - (CONV_WIDTH - 1) + t, r0, rb_rows, cols)
                    parts = [prod[s:s + SUBLANES, :] for s in range(0, rb_rows, SUBLANES)]
                    while len(parts) > 1:
                        parts = [parts[k] + parts[k + 1] for k in range(0, len(parts), 2)]
                    dwacc[pl.ds(t * SUBLANES, SUBLANES), cols] += parts[0]
                    dv = dv + _shifted(dext, dsh, CONV_WIDTH - 1 - t, r0, rb_rows, cols) * w_ref[pl.ds(t, 1), cols]
                dvbuf[pl.ds(r0, rb_rows), cols] = dv
        dv = dvbuf[...]
        uu = u_ref[...]
        a, g = uu[:, :D], uu[:, D:]
        sg = _sigmoid(g)
        da = dv * sg
        dg_ = dv * a * sg * (1.0 - sg)
        du = jnp.concatenate([da, dg_], axis=-1)
        du_ref[...] = du.astype(BF16)
        dbu_ref[0:1, :] += jnp.sum(du, axis=0, keepdims=True)
        dext[ts:ts + H, :] = dext[0:H, :]

        @pl.when(i == nt - 1)
        def _():
            dw_ref[...] = jnp.zeros_like(dw_ref)
            for t in range(CONV_WIDTH):
                dw_ref[pl.ds(t, 1), :] = jnp.sum(dwacc[pl.ds(t * SUBLANES, SUBLANES), :], axis=0, keepdims=True)

    return pl.pallas_call(
        body, name=name, grid=(nt,),
        in_specs=[row, row, row,
                  pl.BlockSpec((H, D), lambda i: (jnp.maximum((nt - 1 - i) * per - 1, 0), 0)),
                  pl.BlockSpec((ts, 2 * D), rev), pl.BlockSpec((H, D), lambda i: (0, 0)),
                  pl.BlockSpec((8, D), lambda i: (0, 0))],
        out_specs=[pl.BlockSpec((ts, 2 * D), rev), pl.BlockSpec((H, D), lambda i: (0, 0)),
                   pl.BlockSpec((8, D), lambda i: (0, 0)), pl.BlockSpec((8, 2 * D), lambda i: (0, 0))],
        out_shape=[jax.ShapeDtypeStruct((S, 2 * D), BF16), jax.ShapeDtypeStruct((H, D), F32),
                   jax.ShapeDtypeStruct((8, D), F32), jax.ShapeDtypeStruct((8, 2 * D), F32)],
        scratch_shapes=[pltpu.VMEM((ts + H, D), F32), pltpu.VMEM((ts + H, D), F32),
                        pltpu.VMEM((7, nsh, D), F32), pltpu.VMEM((7, nsh, D), F32),
                        pltpu.VMEM((CONV_WIDTH * SUBLANES, D), F32), pltpu.VMEM((ts, D), F32)],
        compiler_params=_cparams(("arbitrary",)),
    )(dz, cv, v, v, u, wdw, vec)


def _rope(x, c, s, half):
    x1, x2 = x[:, :half], x[:, half:]
    return jnp.concatenate([x1 * c - x2 * s, x2 * c + x1 * s], axis=-1)


def _rope_t(d, c, s, half):
    d1, d2 = d[:, :half], d[:, half:]
    return jnp.concatenate([d1 * c + d2 * s, d2 * c - d1 * s], axis=-1)


def _gn_gate(y, gate, g, b):
    mu = jnp.mean(y, axis=-1, keepdims=True)
    var = jnp.mean(jnp.square(y - mu), axis=-1, keepdims=True)
    return _silu(gate) * ((y - mu) * lax.rsqrt(var + EPS) * g + b)


def _dot(a, b, mode="nn"):
    return lax.dot_general(a, b, _DIMS[mode], preferred_element_type=F32)


def _ret_tables(H):
    lg = jnp.log(1.0 - 2.0 ** (-5.0 - jnp.arange(H, dtype=F32)))
    idx = jnp.arange(CHUNK, dtype=F32)
    dmat = jnp.exp(lg[:, None, None] * jnp.abs(idx[:, None] - idx[None, :]))
    xi = jnp.exp(lg[:, None] * (idx + 1.0))[..., None]
    zeta = jnp.exp(lg[:, None] * (CHUNK - 1.0 - idx))[..., None]
    dec = jnp.exp(lg * CHUNK)[:, None, None]
    return dmat, xi, zeta, dec


RET_HEADS_PER_STEP = 4


def _ret_specs(R, dk, dv, half, hps, order):
    C = CHUNK
    ng = RET_HEADS // hps
    return dict(
        q=pl.BlockSpec((R, hps * dk), lambda h, n: (order(n), h)),
        k=pl.BlockSpec((R, hps * dk), lambda h, n: (order(n), ng + h)),
        v=pl.BlockSpec((R, hps * dv), lambda h, n: (order(n), ng + h)),
        gate=pl.BlockSpec((R, hps * dv), lambda h, n: (order(n), 2 * ng + h)),
        rope=pl.BlockSpec((R, half), lambda h, n: (order(n), 0)),
        dmat=pl.BlockSpec((hps, C, C), lambda h, n: (h, 0, 0)),
        col=pl.BlockSpec((hps, C, 1), lambda h, n: (h, 0, 0)),
        one=pl.BlockSpec((hps, 1, 1), lambda h, n: (h, 0, 0)),
        gn=pl.BlockSpec((hps, 1, dv), lambda h, n: (h, 0, 0)),
        yv=pl.BlockSpec((R, hps * dv), lambda h, n: (order(n), h)),
        yk=pl.BlockSpec((R, hps * dk), lambda h, n: (order(n), h)),
    )


def _ret_fwd(proj, cos, sin, tables, gn_g, gn_b, *, name, cps=4):
    S = proj.shape[0]
    D = proj.shape[1] // 6
    H, C, hps = RET_HEADS, CHUNK, RET_HEADS_PER_STEP
    dk, dv, half = D // H, 2 * D // H, D // H // 2
    nc = S // C
    cps = min(cps, nc)
    R = cps * C
    scale = dk ** -0.5
    sp = _ret_specs(R, dk, dv, half, hps, lambda n: n)
    dmat, xi, zeta, dec = tables

    def body(q_ref, k_ref, v_ref, g_ref, cos_ref, sin_ref, dm_ref, xi_ref, ze_ref, dec_ref, gg_ref, gb_ref,
             y_ref, y2_ref, st_ref, state):
        @pl.when(pl.program_id(1) == 0)
        def _():
            state[...] = jnp.zeros_like(state)

        for j in range(cps):
            rows = pl.ds(j * C, C)
            cs, sn = cos_ref[rows, :], sin_ref[rows, :]
            for hh in range(hps):
                ck, cv = pl.ds(hh * dk, dk), pl.ds(hh * dv, dv)
                dm, xv, zv, dc = dm_ref[hh], xi_ref[hh], ze_ref[hh], dec_ref[hh]
                qr = _rope(q_ref[rows, ck].astype(F32), cs, sn, half)
                kr = _rope(k_ref[rows, ck].astype(F32), cs, sn, half) * scale
                vb = v_ref[rows, cv]
                p = (_dot(qr.astype(BF16), kr.astype(BF16), "nt") * dm).astype(BF16)
                st = state[hh]
                stb = st.astype(BF16)
                st_ref[hh, j] = stb
                y = _dot(p, vb) + _dot((qr * xv).astype(BF16), stb)
                state[hh] = st * dc + _dot((kr * zv).astype(BF16), vb, "tn")
                y_ref[rows, cv] = y
                y2_ref[rows, cv] = _gn_gate(y, g_ref[rows, cv].astype(F32), gg_ref[hh], gb_ref[hh]).astype(BF16)

    return pl.pallas_call(
        body, name=name, grid=(H // hps, nc // cps),
        in_specs=[sp["q"], sp["k"], sp["v"], sp["gate"], sp["rope"], sp["rope"], sp["dmat"], sp["col"],
                  sp["col"], sp["one"], sp["gn"], sp["gn"]],
        out_specs=[sp["yv"], sp["yv"], pl.BlockSpec((hps, cps, dk, dv), lambda h, n: (h, n, 0, 0))],
        out_shape=[jax.ShapeDtypeStruct((S, 2 * D), F32), jax.ShapeDtypeStruct((S, 2 * D), BF16),
                   jax.ShapeDtypeStruct((H, nc, dk, dv), BF16)],
        scratch_shapes=[pltpu.VMEM((hps, dk, dv), F32)],
        compiler_params=_cparams(("arbitrary", "arbitrary")),
    )(proj, proj, proj, proj, cos, sin, dmat, xi, zeta, dec, gn_g, gn_b)


def _ret_bwd(proj, cos, sin, tables, gn_g, gn_b, y, dy2, states, *, name, cps=4):
    S = proj.shape[0]
    D = proj.shape[1] // 6
    H, C, hps = RET_HEADS, CHUNK, RET_HEADS_PER_STEP
    dk, dv, half = D // H, 2 * D // H, D // H // 2
    nc = S // C
    cps = min(cps, nc)
    ns = nc // cps
    R = cps * C
    scale = dk ** -0.5
    order = lambda n: ns - 1 - n
    sp = _ret_specs(R, dk, dv, half, hps, order)
    dmat, xi, zeta, dec = tables

    def body(q_ref, k_ref, v_ref, g_ref, cos_ref, sin_ref, dm_ref, xi_ref, ze_ref, dec_ref, gg_ref, gb_ref,
             y_ref, dy2_ref, st_ref, dq_ref, dk_ref, dv_ref, dg_ref, dgg_ref, dgb_ref, gst):
        @pl.when(pl.program_id(1) == 0)
        def _():
            gst[...] = jnp.zeros_like(gst)
            dgg_ref[...] = jnp.zeros_like(dgg_ref)
            dgb_ref[...] = jnp.zeros_like(dgb_ref)

        for j in reversed(range(cps)):
            rows = pl.ds(j * C, C)
            cs, sn = cos_ref[rows, :], sin_ref[rows, :]
            for hh in range(hps):
                ck, cv = pl.ds(hh * dk, dk), pl.ds(hh * dv, dv)
                dm, xv, zv, dc = dm_ref[hh], xi_ref[hh], ze_ref[hh], dec_ref[hh]
                _, vjp = jax.vjp(_gn_gate, y_ref[rows, cv], g_ref[rows, cv].astype(F32), gg_ref[hh], gb_ref[hh])
                dy, dgate, dgg, dgb = vjp(dy2_ref[rows, cv])
                dgg_ref[hh] += dgg
                dgb_ref[hh] += dgb
                dg_ref[rows, cv] = dgate.astype(BF16)
                dyb = dy.astype(BF16)
                qr = _rope(q_ref[rows, ck].astype(F32), cs, sn, half)
                kr = _rope(k_ref[rows, ck].astype(F32), cs, sn, half) * scale
                qb, kb, vb = qr.astype(BF16), kr.astype(BF16), v_ref[rows, cv]
                p = (_dot(qb, kb, "nt") * dm).astype(BF16)
                g = gst[hh]
                gb16 = g.astype(BF16)
                sprev = st_ref[hh, j]
                dvv = _dot(p, dyb, "tn") + _dot((kr * zv).astype(BF16), gb16)
                dpb = (_dot(dyb, vb, "nt") * dm).astype(BF16)
                dqr = _dot(dpb, kb) + _dot(dyb, sprev, "nt") * xv
                dkr = _dot(dpb, qb, "tn") + _dot(vb, gb16, "nt") * zv
                gst[hh] = g * dc + _dot((qr * xv).astype(BF16), dyb, "tn")
                dq_ref[rows, ck] = _rope_t(dqr, cs, sn, half).astype(BF16)
                dk_ref[rows, ck] = _rope_t(dkr * scale, cs, sn, half).astype(BF16)
                dv_ref[rows, cv] = dvv.astype(BF16)

    return pl.pallas_call(
        body, name=name, grid=(H // hps, ns),
        in_specs=[sp["q"], sp["k"], sp["v"], sp["gate"], sp["rope"], sp["rope"], sp["dmat"], sp["col"],
                  sp["col"], sp["one"], sp["gn"], sp["gn"], sp["yv"], sp["yv"],
                  pl.BlockSpec((hps, cps, dk, dv), lambda h, n: (h, order(n), 0, 0))],
        out_specs=[sp["yk"], sp["yk"], sp["yv"], sp["yv"], sp["gn"], sp["gn"]],
        out_shape=[jax.ShapeDtypeStruct((S, D), BF16), jax.ShapeDtypeStruct((S, D), BF16),
                   jax.ShapeDtypeStruct((S, 2 * D), BF16), jax.ShapeDtypeStruct((S, 2 * D), BF16),
                   jax.ShapeDtypeStruct((H, 1, dv), F32), jax.ShapeDtypeStruct((H, 1, dv), F32)],
        scratch_shapes=[pltpu.VMEM((hps, dk, dv), F32)],
        compiler_params=_cparams(("arbitrary", "arbitrary")),
    )(proj, proj, proj, proj, cos, sin, dmat, xi, zeta, dec, gn_g, gn_b, y, dy2, states)


def _rows_tile(rows, cols, n_arrays):
    cap = max(8, V7X_VMEM_LIMIT // 3 // (n_arrays * 2 * 4 * cols))
    t = rows
    while t > cap and t % 2 == 0:
        t //= 2
    return t


def _add_half(g, r, half_idx, *, name):
    P, R, Cc = g.shape
    hR = R // 2
    tr = _rows_tile(hR, Cc, 3)
    nb = hR // tr

    def body(h_ref, g_ref, r_ref, o_ref):
        o_ref[...] = (g_ref[...] + r_ref[...]).astype(BF16)

    return pl.pallas_call(
        body, name=name,
        grid_spec=pltpu.PrefetchScalarGridSpec(
            num_scalar_prefetch=1, grid=(P, nb),
            in_specs=[pl.BlockSpec((None, tr, Cc), lambda s, i, h: (s, h[0] * nb + i, 0)),
                      pl.BlockSpec((None, tr, Cc), lambda s, i, h: (s, i, 0))],
            out_specs=pl.BlockSpec((None, tr, Cc), lambda s, i, h: (s, i, 0))),
        out_shape=jax.ShapeDtypeStruct((P, hR, Cc), BF16), compiler_params=_cparams(("parallel", "parallel")),
    )(half_idx, g, r)


def _sum_slots(x, *, name):
    L, NS, R, Cc = x.shape
    tr = _rows_tile(R, Cc, NS + 1)

    def body(x_ref, o_ref):
        acc = x_ref[0].astype(F32)
        for s in range(1, NS):
            acc = acc + x_ref[s].astype(F32)
        o_ref[...] = acc

    return pl.pallas_call(
        body, name=name, grid=(L, R // tr),
        in_specs=[pl.BlockSpec((None, NS, tr, Cc), lambda l, i: (l, 0, i, 0))],
        out_specs=pl.BlockSpec((None, tr, Cc), lambda l, i: (l, i, 0)),
        out_shape=jax.ShapeDtypeStruct((L, R, Cc), F32), compiler_params=_cparams(("parallel", "parallel")),
    )(x)


def _adam_store(g, w_ref, m_ref, v_ref, go_ref, d_ref, mo_ref, vo_ref):
    mn = ADAM_B1 * m_ref[...] + (1.0 - ADAM_B1) * g
    vn = ADAM_B2 * v_ref[...] + (1.0 - ADAM_B2) * jnp.square(g)
    m_hat = mn / (1.0 - ADAM_B1 ** ADAM_STEP)
    v_hat = vn / (1.0 - ADAM_B2 ** ADAM_STEP)
    go_ref[...] = g
    d_ref[...] = -ADAM_LR * (m_hat / (jnp.sqrt(v_hat) + ADAM_EPS) + ADAM_WD * w_ref[...])
    mo_ref[...] = mn
    vo_ref[...] = vn


def _adamw(gslots, w, m, v, *, name):
    L, NS, R, Cc = gslots.shape
    tr = _rows_tile(R, Cc, NS + 7)
    gspec = pl.BlockSpec((None, NS, tr, Cc), lambda l, i: (l, 0, i, 0))
    spec = pl.BlockSpec((None, tr, Cc), lambda l, i: (l, i, 0))

    def body(g_ref, *refs):
        g = g_ref[0]
        for s in range(1, NS):
            g = g + g_ref[s]
        _adam_store(g, *refs)

    sd = jax.ShapeDtypeStruct((L, R, Cc), F32)
    return pl.pallas_call(
        body, name=name, grid=(L, R // tr), in_specs=[gspec, spec, spec, spec],
        out_specs=[spec, spec, spec, spec], out_shape=[sd, sd, sd, sd],
        compiler_params=_cparams(("parallel", "parallel")),
    )(gslots, w, m, v)


def _adamw_halves(g_mine, g_sib, half_idx, w, m, v, *, name):
    L, hR, Cc = g_mine.shape
    tr = _rows_tile(hR, Cc, 9)
    nbh = hR // tr
    gspec = pl.BlockSpec((None, tr, Cc), lambda l, i, h: (l, i % nbh, 0))
    spec = pl.BlockSpec((None, tr, Cc), lambda l, i, h: (l, i, 0))

    def body(h_ref, gm_ref, gs_ref, *refs):
        mine = (pl.program_id(1) // nbh) == h_ref[0]
        _adam_store(jnp.where(mine, gm_ref[...], gs_ref[...]), *refs)

    sd = jax.ShapeDtypeStruct((L, 2 * hR, Cc), F32)
    return pl.pallas_call(
        body, name=name,
        grid_spec=pltpu.PrefetchScalarGridSpec(
            num_scalar_prefetch=1, grid=(L, 2 * nbh), in_specs=[gspec, gspec, spec, spec, spec],
            out_specs=[spec, spec, spec, spec]),
        out_shape=[sd, sd, sd, sd], compiler_params=_cparams(("parallel", "parallel")),
    )(half_idx, g_mine, g_sib, w, m, v)


def _me():
    return lax.axis_index("x"), lax.axis_index("y"), lax.axis_index("c")


def _flip(v, bit):
    return 1 - v if bit else v


def _allgather8(x, *, name):
    def body(x_ref, out_ref, send_sems, recv_sems, loc_sem):
        mx, my, mc = _me()
        me = 4 * mx + 2 * my + mc
        loc = pltpu.make_async_copy(x_ref, out_ref.at[me], loc_sem)
        loc.start()
        sends, recvs = [], []
        for k in range(1, N_DEV):
            px, py, pc = _flip(mx, k & 4), _flip(my, k & 2), _flip(mc, k & 1)
            sends.append(pltpu.make_async_remote_copy(
                src_ref=x_ref, dst_ref=out_ref.at[me], send_sem=send_sems.at[k - 1],
                recv_sem=recv_sems.at[k - 1], device_id=(px, py, pc), device_id_type=MESH))
            recvs.append(pltpu.make_async_remote_copy(
                src_ref=x_ref, dst_ref=out_ref.at[4 * px + 2 * py + pc], send_sem=send_sems.at[k - 1],
                recv_sem=recv_sems.at[k - 1], device_id=(px, py, pc), device_id_type=MESH))
        for cp in sends:
            cp.start()
        for cp in recvs:
            cp.wait_recv()
        for cp in sends:
            cp.wait_send()
        loc.wait()

    return pl.pallas_call(
        body, name=name, in_specs=[ANY], out_specs=ANY,
        out_shape=jax.ShapeDtypeStruct((N_DEV,) + x.shape, x.dtype),
        scratch_shapes=[pltpu.SemaphoreType.DMA((N_DEV - 1,)), pltpu.SemaphoreType.DMA((N_DEV - 1,)),
                        pltpu.SemaphoreType.DMA],
    )(x)


def _gather_chips(arrays, after, *, name):
    n = len(arrays)

    def body(*refs):
        ins, outs = refs[:n], refs[n + 1:2 * n + 1]
        ici_send, ici_recv, d2d_send, d2d_recv, own_send, own_recv = refs[2 * n + 1:]
        mx, my, mc = _me()
        me = 2 * mx + my
        sib = (mx, my, 1 - mc)
        locs, sends, lands, passes, gifts = [], [], [], [], []
        for a in range(n):
            h = arrays[a].shape[0] // 2
            mine, other = pl.ds(mc * h, h), pl.ds((1 - mc) * h, h)
            locs.append(pltpu.make_async_remote_copy(
                src_ref=ins[a], dst_ref=outs[a].at[me], send_sem=own_send.at[a], recv_sem=own_recv.at[a],
                device_id=sib, device_id_type=MESH))
            for k in range(1, N_CHIPS):
                px, py = _flip(mx, k & 2), _flip(my, k & 1)
                peer = 2 * px + py
                ici = dict(send_sem=ici_send.at[a, k - 1], recv_sem=ici_recv.at[a, k - 1],
                           device_id=(px, py, mc), device_id_type=MESH)
                d2d = dict(send_sem=d2d_send.at[a, k - 1], recv_sem=d2d_recv.at[a, k - 1],
                           device_id=sib, device_id_type=MESH)
                sends.append(pltpu.make_async_remote_copy(
                    src_ref=ins[a].at[mine], dst_ref=outs[a].at[me, mine], **ici))
                lands.append(pltpu.make_async_remote_copy(
                    src_ref=ins[a].at[mine], dst_ref=outs[a].at[peer, mine], **ici))
                passes.append(pltpu.make_async_remote_copy(
                    src_ref=outs[a].at[peer, mine], dst_ref=outs[a].at[peer, mine], **d2d))
                gifts.append(pltpu.make_async_remote_copy(
                    src_ref=outs[a].at[peer, other], dst_ref=outs[a].at[peer, other], **d2d))
        for cp in locs + sends:
            cp.start()
        for land, fwd in zip(lands, passes):
            land.wait_recv()
            fwd.start()
        for cp in gifts + locs:
            cp.wait_recv()
        for cp in sends + passes + locs:
            cp.wait_send()

    nsem = (n, N_CHIPS - 1)
    return pl.pallas_call(
        body, name=name, in_specs=[ANY] * (n + 1), out_specs=[ANY] * n,
        out_shape=[jax.ShapeDtypeStruct((N_CHIPS,) + a.shape, a.dtype) for a in arrays],
        scratch_shapes=[pltpu.SemaphoreType.DMA(nsem), pltpu.SemaphoreType.DMA(nsem), pltpu.SemaphoreType.DMA(nsem),
                        pltpu.SemaphoreType.DMA(nsem), pltpu.SemaphoreType.DMA((n,)), pltpu.SemaphoreType.DMA((n,))],
    )(*arrays, after)


def _swap_half(arrays, *, name):
    n = len(arrays)

    def body(*refs):
        ins, outs = refs[:n], refs[n:2 * n]
        send_sems, recv_sems = refs[2 * n:]
        mx, my, mc = _me()
        cps = []
        for a in range(n):
            P, R, _ = arrays[a].shape
            cps.append(pltpu.make_async_remote_copy(
                src_ref=ins[a].at[pl.ds(0, P), pl.ds((1 - mc) * (R // 2), R // 2)], dst_ref=outs[a],
                send_sem=send_sems.at[a], recv_sem=recv_sems.at[a],
                device_id=(mx, my, 1 - mc), device_id_type=MESH))
        for cp in cps:
            cp.start()
        for cp in cps:
            cp.wait_recv()
        for cp in cps:
            cp.wait_send()

    return pl.pallas_call(
        body, name=name, in_specs=[ANY] * n, out_specs=[ANY] * n,
        out_shape=[jax.ShapeDtypeStruct((a.shape[0], a.shape[1] // 2, a.shape[2]), a.dtype) for a in arrays],
        scratch_shapes=[pltpu.SemaphoreType.DMA((n,)), pltpu.SemaphoreType.DMA((n,))],
    )(*arrays)


def _swap_sibling(arrays, *, name):
    n = len(arrays)

    def body(*refs):
        ins, outs = refs[:n], refs[n:2 * n]
        send_sems, recv_sems = refs[2 * n:]
        mx, my, mc = _me()
        cps = [pltpu.make_async_remote_copy(
            src_ref=ins[a], dst_ref=outs[a], send_sem=send_sems.at[a], recv_sem=recv_sems.at[a],
            device_id=(mx, my, 1 - mc), device_id_type=MESH) for a in range(n)]
        for cp in cps:
            cp.start()
        for cp in cps:
            cp.wait_recv()
        for cp in cps:
            cp.wait_send()

    return pl.pallas_call(
        body, name=name, in_specs=[ANY] * n, out_specs=[ANY] * n,
        out_shape=[jax.ShapeDtypeStruct(a.shape, a.dtype) for a in arrays],
        scratch_shapes=[pltpu.SemaphoreType.DMA((n,)), pltpu.SemaphoreType.DMA((n,))],
    )(*arrays)


def _plan_scatter(srcs, lands):
    mx, my, mc = _me()
    copies = []
    for a in range(len(srcs)):
        for k in range(1, N_CHIPS):
            px, py = _flip(mx, k & 2), _flip(my, k & 1)
            copies.append((srcs[a].at[2 * px + py], lands[a].at[k - 1], lands[a].at[k - 1], (px, py, mc)))
    return copies


def _plan_gather(srcs, lands):
    mx, my, mc = _me()
    me = 2 * mx + my
    copies = []
    for a in range(len(srcs)):
        copies.append((srcs[a], lands[a].at[me], lands[a].at[me], (mx, my, 1 - mc)))
        for k in range(1, N_CHIPS):
            px, py = _flip(mx, k & 2), _flip(my, k & 1)
            copies.append((srcs[a], lands[a].at[me], lands[a].at[2 * px + py], (px, py, mc)))
    return copies


def _copy(c, k, send_sems, recv_sems, landing=False):
    src, dst, land, dev = c
    return pltpu.make_async_remote_copy(src_ref=src, dst_ref=land if landing else dst, send_sem=send_sems.at[k],
                                        recv_sem=recv_sems.at[k], device_id=dev, device_id_type=MESH)


def _exchange(srcs, land_shapes, plan, ncopies, *, name):
    ni, nl = len(srcs), len(land_shapes)

    def body(*refs):
        send_sems, recv_sems = refs[ni + nl:]
        copies = plan(refs[:ni], refs[ni:ni + nl])
        for k, c in enumerate(copies):
            _copy(c, k, send_sems, recv_sems).start()
        for k, c in enumerate(copies):
            _copy(c, k, send_sems, recv_sems, landing=True).wait_recv()
        for k, c in enumerate(copies):
            _copy(c, k, send_sems, recv_sems).wait_send()

    return pl.pallas_call(
        body, name=name, in_specs=[ANY] * ni, out_specs=[ANY] * nl, out_shape=list(land_shapes),
        scratch_shapes=[pltpu.SemaphoreType.DMA((ncopies,)), pltpu.SemaphoreType.DMA((ncopies,))],
    )(*srcs)


HBM_SPEC = pl.BlockSpec(memory_space=pltpu.HBM)
SEM_SPEC = pl.BlockSpec(memory_space=pltpu.SEMAPHORE)
SPLIT_EFFECT = pltpu.SideEffectType.DATAFLOW_SIDE_EFFECTING


def _exchange_start(srcs, land_shapes, plan, ncopies, after, *, name):
    ni, nl = len(srcs), len(land_shapes)

    def body(*refs):
        in_refs, land_refs = refs[:ni], refs[ni:ni + nl]
        send_sems, recv_sems = refs[ni + nl + 1], refs[ni + nl + 2]
        token = refs[-1]
        for k, c in enumerate(plan(in_refs, land_refs)):
            _copy(c, k, send_sems, recv_sems).start()
        token[...] = jnp.zeros_like(token)

    bufs = [pltpu.with_memory_space_constraint(a, pltpu.HBM) for a in srcs]
    bufs += [pltpu.with_memory_space_constraint(lax.empty(s.shape, s.dtype), pltpu.HBM) for s in land_shapes]
    outs = pl.pallas_call(
        body, name=name,
        in_specs=[HBM_SPEC] * (ni + nl) + [ANY],
        out_specs=(SEM_SPEC, SEM_SPEC, *[HBM_SPEC] * (ni + nl), pl.BlockSpec(memory_space=pltpu.VMEM)),
        out_shape=(pltpu.SemaphoreType.DMA((ncopies,)), pltpu.SemaphoreType.DMA((ncopies,)),
                   *[pltpu.HBM(b.shape, b.dtype) for b in bufs], jax.ShapeDtypeStruct((8, 128), F32)),
        input_output_aliases={i: 2 + i for i in range(ni + nl)},
        compiler_params=pltpu.CompilerParams(has_side_effects=SPLIT_EFFECT),
    )(*bufs, after)
    return outs[:-1], outs[-1]


def _exchange_wait(started, ni, plan, after, *, name):
    send_sems, recv_sems = started[0], started[1]
    bufs = list(started[2:])
    nb = len(bufs)

    def body(*refs):
        in_refs, land_refs = refs[:ni], refs[ni:nb]
        send, recv = refs[nb], refs[nb + 1]
        for k, c in enumerate(plan(in_refs, land_refs)):
            cp = _copy(c, k, send, recv, landing=True)
            cp.wait_send()
            cp.wait_recv()

    outs = pl.pallas_call(
        body, name=name, in_specs=[HBM_SPEC] * nb + [SEM_SPEC, SEM_SPEC, ANY], out_specs=[HBM_SPEC] * nb,
        out_shape=[pltpu.HBM(b.shape, b.dtype) for b in bufs],
        input_output_aliases={i: i for i in range(nb)},
        compiler_params=pltpu.CompilerParams(has_side_effects=SPLIT_EFFECT),
    )(*bufs, send_sems, recv_sems, after)
    return list(outs[:ni]), list(outs[ni:])


def _sum_own(own, recv, chip_idx, *, name):
    _, R, Cc = own.shape
    tr = _rows_tile(R, Cc, 5)

    def body(s_ref, o_ref, r_ref, t_ref):
        acc = o_ref[...].astype(F32)
        for s in range(N_CHIPS - 1):
            acc = acc + r_ref[s].astype(F32)
        t_ref[...] = acc

    return pl.pallas_call(
        body, name=name,
        grid_spec=pltpu.PrefetchScalarGridSpec(
            num_scalar_prefetch=1, grid=(R // tr,),
            in_specs=[pl.BlockSpec((None, tr, Cc), lambda i, s: (s[0], i, 0)),
                      pl.BlockSpec((N_CHIPS - 1, tr, Cc), lambda i, s: (0, i, 0))],
            out_specs=pl.BlockSpec((tr, Cc), lambda i, s: (i, 0))),
        out_shape=jax.ShapeDtypeStruct((R, Cc), F32), compiler_params=_cparams(("parallel",)),
    )(chip_idx, own, recv)


BIG = ("conv_w_pw1", "conv_w_pw2", "ret_w_in", "ret_w_out", "mlp_w1", "mlp_w2")
COLS = ("conv_w_pw1", "ret_w_in", "mlp_w1")
SMALL = ("ada_b", "norm_mix_g", "norm_mlp_g", "conv_b_pw1", "conv_w_dw", "conv_b_dw", "conv_ln_g", "conv_ln_b",
         "conv_b_pw2", "ret_gn_g", "ret_gn_b", "final_norm_g")
SMALL_SHARDED = ("conv_w_dw", "ret_gn_g", "ret_gn_b")
WEIGHTS = ("ada_w", "ada_b", "norm_mix_g", "norm_mlp_g", "conv_w_pw1", "conv_b_pw1", "conv_w_dw", "conv_b_dw",
           "conv_ln_g", "conv_ln_b", "conv_w_pw2", "conv_b_pw2", "ret_w_in", "ret_gn_g", "ret_gn_b", "ret_w_out",
           "mlp_w1", "mlp_w2", "final_norm_g")


def _vec8(rows, D):
    rows = [r.reshape(1, D).astype(F32) for r in rows]
    return jnp.concatenate(rows + [jnp.zeros((8 - len(rows), D), F32)], axis=0)


def _unshard_last(g):
    nd = g.ndim
    t = jnp.transpose(g, tuple(range(1, nd - 1)) + (0, nd - 1))
    return t.reshape(t.shape[:-2] + (t.shape[-2] * t.shape[-1],))


def _pack(parts):
    flat = jnp.concatenate([p.reshape(-1).astype(F32) for p in parts])
    pad = (-flat.shape[0]) % 1024
    return jnp.concatenate([flat, jnp.zeros((pad,), F32)]).reshape(-1, 128)


def _unpack(packed, shapes):
    flat = packed.reshape(-1)
    out, pos = [], 0
    for s in shapes:
        n = math.prod(s)
        out.append(flat[pos:pos + n].reshape(s))
        pos += n
    return out


def kernel(x, c, ada_w, ada_b, norm_mix_g, norm_mlp_g, conv_w_pw1, conv_b_pw1, conv_w_dw, conv_b_dw, conv_ln_g, conv_ln_b, conv_w_pw2, conv_b_pw2, ret_w_in, ret_gn_g, ret_gn_b, ret_w_out, mlp_w1, mlp_w2, final_norm_g, loss_target, m_ada_w, m_ada_b, m_norm_mix_g, m_norm_mlp_g, m_conv_w_pw1, m_conv_b_pw1, m_conv_w_dw, m_conv_b_dw, m_conv_ln_g, m_conv_ln_b, m_conv_w_pw2, m_conv_b_pw2, m_ret_w_in, m_ret_gn_g, m_ret_gn_b, m_ret_w_out, m_mlp_w1, m_mlp_w2, m_final_norm_g, v_ada_w, v_ada_b, v_norm_mix_g, v_norm_mlp_g, v_conv_w_pw1, v_conv_b_pw1, v_conv_w_dw, v_conv_b_dw, v_conv_ln_g, v_conv_ln_b, v_conv_w_pw2, v_conv_b_pw2, v_ret_w_in, v_ret_gn_g, v_ret_gn_b, v_ret_w_out, v_mlp_w1, v_mlp_w2, v_final_norm_g):
    W = dict(ada_w=ada_w, ada_b=ada_b, norm_mix_g=norm_mix_g, norm_mlp_g=norm_mlp_g, conv_w_pw1=conv_w_pw1,
             conv_b_pw1=conv_b_pw1, conv_w_dw=conv_w_dw, conv_b_dw=conv_b_dw, conv_ln_g=conv_ln_g,
             conv_ln_b=conv_ln_b, conv_w_pw2=conv_w_pw2, conv_b_pw2=conv_b_pw2, ret_w_in=ret_w_in,
             ret_gn_g=ret_gn_g, ret_gn_b=ret_gn_b, ret_w_out=ret_w_out, mlp_w1=mlp_w1, mlp_w2=mlp_w2,
             final_norm_g=final_norm_g)
    Mo = dict(ada_w=m_ada_w, ada_b=m_ada_b, norm_mix_g=m_norm_mix_g, norm_mlp_g=m_norm_mlp_g,
              conv_w_pw1=m_conv_w_pw1, conv_b_pw1=m_conv_b_pw1, conv_w_dw=m_conv_w_dw, conv_b_dw=m_conv_b_dw,
              conv_ln_g=m_conv_ln_g, conv_ln_b=m_conv_ln_b, conv_w_pw2=m_conv_w_pw2, conv_b_pw2=m_conv_b_pw2,
              ret_w_in=m_ret_w_in, ret_gn_g=m_ret_gn_g, ret_gn_b=m_ret_gn_b, ret_w_out=m_ret_w_out,
              mlp_w1=m_mlp_w1, mlp_w2=m_mlp_w2, final_norm_g=m_final_norm_g)
    Vo = dict(ada_w=v_ada_w, ada_b=v_ada_b, norm_mix_g=v_norm_mix_g, norm_mlp_g=v_norm_mlp_g,
              conv_w_pw1=v_conv_w_pw1, conv_b_pw1=v_conv_b_pw1, conv_w_dw=v_conv_w_dw, conv_b_dw=v_conv_b_dw,
              conv_ln_g=v_conv_ln_g, conv_ln_b=v_conv_ln_b, conv_w_pw2=v_conv_w_pw2, conv_b_pw2=v_conv_b_pw2,
              ret_w_in=v_ret_w_in, ret_gn_g=v_ret_gn_g, ret_gn_b=v_ret_gn_b, ret_w_out=v_ret_w_out,
              mlp_w1=v_mlp_w1, mlp_w2=v_mlp_w2, final_norm_g=v_final_norm_g)

    S, D = x.shape[1], x.shape[2]
    depth = ada_w.shape[0]
    H = RET_HEADS
    dv = 2 * D // H
    xs = x.reshape(S, D)
    target = loss_target.reshape(S, D)
    mx, my, mc = _me()
    chip = 2 * mx + my
    dev = 4 * mx + 2 * my + mc

    def layer_weights(l):
        mixer = ("conv_w_pw1", "conv_w_pw2") if l % 2 == 0 else ("ret_w_in", "ret_w_out")
        return [(nm, l // 2) for nm in mixer] + [("mlp_w1", l), ("mlp_w2", l)]

    c_all = _allgather8(c.reshape(8, D // 8), name="gather_c").reshape(N_DEV, D)
    cs_ada = ada_w.shape[2]
    bias_sh = lax.dynamic_slice_in_dim(ada_b.reshape(depth, N_CHIPS, cs_ada), chip, 1, axis=1)
    mod_sh = _mm(c_all, ada_w, mode="nn", name="ada_fwd", b3d=True, tn=cs_ada, a_silu=True,
                 bias=bias_sh.reshape(1, depth * cs_ada))
    mod_all = _allgather8(mod_sh, name="gather_mod")[0::2]
    mod_me = lax.dynamic_slice_in_dim(mod_all, dev, 1, axis=1).reshape(N_CHIPS, depth, cs_ada)
    mod = jnp.transpose(mod_me, (1, 0, 2)).reshape(depth, 6, D)

    keys0 = layer_weights(0)[:2]
    got0 = _gather_chips([W[nm][i].astype(BF16) for nm, i in keys0] + [W[nm] for nm in SMALL_SHARDED], mod,
                         name="gather_weights")
    Wg = dict(zip(keys0, got0))
    full_small = {nm: _unshard_last(got0[len(keys0) + i]) for i, nm in enumerate(SMALL_SHARDED)}
    pending, order = {}, got0[0]
    for tag, keys in [("0m", layer_weights(0)[2:])] + [(l, layer_weights(l)) for l in range(1, depth)]:
        srcs = [W[nm][i].astype(BF16) for nm, i in keys]
        shapes = [jax.ShapeDtypeStruct((N_CHIPS,) + s.shape, BF16) for s in srcs]
        started, order = _exchange_start(srcs, shapes, _plan_gather, 4 * len(srcs), order,
                                         name=f"gather_start_{tag}")
        pending[tag] = (keys, started)
    mod = mod + order[0, 0]

    def arrive(tag, after):
        keys, started = pending.pop(tag)
        _, lands = _exchange_wait(started, len(keys), _plan_gather, after, name=f"gather_wait_{tag}")
        Wg.update(zip(keys, lands))

    def wfull(nm, l):
        g = Wg[nm, l]
        return g.reshape(g.shape[0] * g.shape[1], g.shape[2])

    pos_ids = jnp.arange(S, dtype=F32)
    dk = D // H
    inv = ROPE_BASE ** (-jnp.arange(0, dk, 2, dtype=F32) / dk)
    ang = pos_ids[:, None] * inv[None, :]
    cos_t, sin_t = jnp.cos(ang), jnp.sin(ang)
    tables = _ret_tables(H)
    gn_g_full = full_small["ret_gn_g"].reshape(-1, H, 1, dv)
    gn_b_full = full_small["ret_gn_b"].reshape(-1, H, 1, dv)
    wdw_full = full_small["conv_w_dw"]

    def wdw_pad(j):
        return jnp.concatenate([wdw_full[j], jnp.zeros((CONV_HALO - CONV_WIDTH, D), F32)], axis=0)

    saved = []
    xa, y_prev, gate_prev = xs, None, None
    for l in range(depth):
        j = l // 2
        sv = {}
        if l in pending:
            arrive(l, y_prev)
        vec_a = _vec8([gate_prev if gate_prev is not None else jnp.zeros((D,), F32), norm_mix_g[l], mod[l, 0],
                       mod[l, 1]], D)
        xa, h = _row_fwd(xa, y_prev, vec_a, name="row_fwd" if y_prev is not None else "row_fwd_first")
        sv.update(xa=xa, h=h, vec_a=vec_a)
        if l % 2 == 0:
            u = _mm(h, Wg["conv_w_pw1", j], mode="nn", name="pw1_fwd", b3d=True,
                    tn=Wg["conv_w_pw1", j].shape[2], bias=conv_b_pw1[j].reshape(1, -1))
            cvec = _vec8([conv_b_dw[j], conv_ln_g[j], conv_ln_b[j]], D)
            v_glu, cv, z = _conv_fwd(u, wdw_pad(j), cvec, name="conv_fwd")
            ymix = _mm(z, wfull("conv_w_pw2", j), mode="nn", name="pw2_fwd", bias=conv_b_pw2[j].reshape(1, -1))
            sv.update(u=u, v_glu=v_glu, cv=cv, z=z, cvec=cvec)
        else:
            proj = _mm(h, Wg["ret_w_in", j], mode="nn", name="win_fwd", b3d=True, out_dtype=BF16,
                       tn=Wg["ret_w_in", j].shape[2])
            yr, y2, states = _ret_fwd(proj, cos_t, sin_t, tables, gn_g_full[j], gn_b_full[j], name="ret_fwd")
            ymix = _mm(y2, wfull("ret_w_out", j), mode="nn", name="wout_fwd")
            sv.update(proj=proj, yr=yr, y2=y2, states=states)
        if f"{l}m" in pending:
            arrive(f"{l}m", ymix)
        vec_b = _vec8([mod[l, 2], norm_mlp_g[l], mod[l, 3], mod[l, 4]], D)
        xb, h2 = _row_fwd(xa, ymix, vec_b, name="row_fwd")
        ra, p = _mm(h2, Wg["mlp_w1", l], mode="nn", name="w1_fwd", b3d=True, tn=Wg["mlp_w1", l].shape[2],
                    epi="relu2")
        mo = _mm(p, wfull("mlp_w2", l), mode="nn", name="w2_fwd")
        sv.update(ymix=ymix, xb=xb, h2=h2, ra=ra, p=p, mo=mo, vec_b=vec_b)
        saved.append(sv)
        xa, y_prev, gate_prev = xb, mo, mod[l, 5]

    fvec = _vec8([gate_prev, final_norm_g], D)
    dx, dyb, fpart = _final(xa, y_prev, target, fvec, name="final")
    loss = lax.psum(jnp.sum(fpart[2]), ("x", "y", "c"))
    G = {nm: [None] * W[nm].shape[0] for nm in BIG}
    dmod = [[None] * 6 for _ in range(depth)]
    dmod[depth - 1][5] = fpart[0]
    sg = dict(norm_mix_g=[None] * depth, norm_mlp_g=[None] * depth, final_norm_g=fpart[1])
    n_conv, n_ret = conv_w_pw1.shape[0], ret_w_in.shape[0]
    for nm in ("conv_b_pw1", "conv_w_dw", "conv_b_dw", "conv_ln_g", "conv_ln_b", "conv_b_pw2"):
        sg[nm] = [None] * n_conv
    for nm in ("ret_gn_g", "ret_gn_b"):
        sg[nm] = [None] * n_ret

    half_idx = mc.astype(jnp.int32).reshape(1)
    chip_idx = chip.astype(jnp.int32).reshape(1)

    def chip_sums(keys, tag):
        flat = [G[nm][i] for nm, i in keys]
        sib = _swap_half(flat, name="swap_grads_" + tag)
        sums = [_add_half(a, b, half_idx, name="add_grads") for a, b in zip(flat, sib)]
        shapes = [jax.ShapeDtypeStruct((N_CHIPS - 1,) + s.shape[1:], BF16) for s in sums]
        return sums, shapes

    launch = {depth // 2 - 1: list(range(depth // 2, depth))}
    launch.update({l - 1: [l] for l in range(1, depth // 2)})
    early_keys = layer_weights(0)
    in_flight = []

    for l in reversed(range(depth)):
        j = l // 2
        sv = saved[l]
        if l in launch:
            keys = [k for ll in launch[l] for k in layer_weights(ll)]
            sums, shapes = chip_sums(keys, f"from{launch[l][0]}")
            started, tok = _exchange_start(sums, shapes, _plan_scatter, 3 * len(sums), fpart,
                                           name=f"scatter_start_{launch[l][0]}")
            in_flight.append((keys, started, launch[l][0]))
            sv["vec_b"] = sv["vec_b"] + tok[:, :1]
        w1, w2 = Wg["mlp_w1", l], wfull("mlp_w2", l)
        cs1 = w1.shape[2]
        da = _mm(dyb, w2, mode="nt", name="w2_dx", out_dtype=BF16, epi="mul2", extra=sv["ra"])
        gw2 = _mm(sv["p"], dyb, mode="tn", name="w2_dw")
        G["mlp_w2"][l] = gw2.reshape(N_CHIPS, gw2.shape[0] // N_CHIPS, gw2.shape[1])
        G["mlp_w1"][l] = _mm(sv["h2"], da, mode="tn", name="w1_dw", out3d=(N_CHIPS, cs1), tn=cs1, tk=DW_TOKENS)
        dh2 = _mm(da, w1, mode="nt", name="w1_dx", b3d=True, tk=cs1, out_dtype=BF16)
        dx, dyb, part = _row_bwd(sv["xb"], dh2, dx, sv["ymix"], sv["vec_b"], name="row_bwd")
        dmod[l][2], sg["norm_mlp_g"][l], dmod[l][3], dmod[l][4] = part[0], part[1], part[2], part[3]
        if l % 2 == 0:
            sg["conv_b_pw2"][j] = part[4]
            wp1, wp2 = Wg["conv_w_pw1", j], wfull("conv_w_pw2", j)
            csp = wp1.shape[2]
            dz = _mm(dyb, wp2, mode="nt", name="pw2_dx")
            gp2 = _mm(sv["z"], dyb, mode="tn", name="pw2_dw", tk=DW_TOKENS)
            G["conv_w_pw2"][j] = gp2.reshape(N_CHIPS, gp2.shape[0] // N_CHIPS, gp2.shape[1])
            du, dwdw, cpart, dbu = _conv_bwd(dz, sv["cv"], sv["v_glu"], sv["u"], wdw_pad(j), sv["cvec"],
                                             name="conv_bwd")
            sg["conv_w_dw"][j] = dwdw[:CONV_WIDTH]
            sg["conv_b_dw"][j], sg["conv_ln_g"][j], sg["conv_ln_b"][j] = cpart[0], cpart[1], cpart[2]
            sg["conv_b_pw1"][j] = dbu[0]
            G["conv_w_pw1"][j] = _mm(sv["h"], du, mode="tn", name="pw1_dw", out3d=(N_CHIPS, csp), tn=csp,
                                     tk=DW_TOKENS)
            dh = _mm(du, wp1, mode="nt", name="pw1_dx", b3d=True, tk=csp, out_dtype=BF16)
        else:
            wi, wo = Wg["ret_w_in", j], wfull("ret_w_out", j)
            csi = wi.shape[2]
            dy2 = _mm(dyb, wo, mode="nt", name="wout_dx")
            gwo = _mm(sv["y2"], dyb, mode="tn", name="wout_dw")
            G["ret_w_out"][j] = gwo.reshape(N_CHIPS, gwo.shape[0] // N_CHIPS, gwo.shape[1])
            dq, dkk, dvv, dgt, dgg, dgb = _ret_bwd(sv["proj"], cos_t, sin_t, tables, gn_g_full[j], gn_b_full[j],
                                                   sv["yr"], dy2, sv["states"], name="ret_bwd")
            sg["ret_gn_g"][j], sg["ret_gn_b"][j] = dgg.reshape(H, dv), dgb.reshape(H, dv)
            dproj = jnp.concatenate([dq, dkk, dvv, dgt], axis=1)
            G["ret_w_in"][j] = _mm(sv["h"], dproj, mode="tn", name="win_dw", out3d=(N_CHIPS, csi), tn=csi,
                                   tk=DW_TOKENS)
            dh = _mm(dproj, wi, mode="nt", name="win_dx", b3d=True, tk=csi, out_dtype=BF16)
        yp = saved[l - 1]["mo"] if l > 0 else dh
        dx, dyb, part = _row_bwd(sv["xa"], dh, dx, yp, sv["vec_a"], name="row_bwd")
        sg["norm_mix_g"][l], dmod[l][0], dmod[l][1] = part[1], part[2], part[3]
        if l > 0:
            dmod[l - 1][5] = part[0]
    grad_x = dx.reshape(x.shape)

    dmod_me = jnp.stack([jnp.stack(r) for r in dmod]).reshape(depth, 6 * D)
    sgrads = dict(ada_b=dmod_me)
    for nm in SMALL[1:]:
        sgrads[nm] = sg[nm] if nm == "final_norm_g" else jnp.stack(sg[nm])
    full_shapes = [sgrads[nm].shape for nm in SMALL]
    packed_all = _allgather8(_pack([sgrads[nm] for nm in SMALL]), name="gather_small_grads")
    sums = _unpack(_sum_slots(packed_all[None], name="sum_small_grads"), full_shapes)
    gsm = {}
    for nm, g in zip(SMALL, sums):
        if nm in SMALL_SHARDED:
            n = g.shape[-1] // N_CHIPS
            g = lax.dynamic_slice_in_dim(g.reshape(g.shape[:-1] + (N_CHIPS, n)), chip, 1, axis=g.ndim - 1)
            g = g.reshape(g.shape[:-2] + (n,))
        gsm[nm] = g.reshape(W[nm].shape)
    small_out = [{}, {}, {}, {}]
    for nm in SMALL:
        rows = lambda a: a.reshape(1, -1, LANES)
        res4 = _adamw(gsm[nm].reshape(1, 1, -1, LANES), rows(W[nm]), rows(Mo[nm]), rows(Vo[nm]), name="adamw_small")
        for i in range(4):
            small_out[i][nm] = res4[i].reshape(W[nm].shape)

    n_mod_rows = depth * 6 * D // 128
    dmod_all = packed_all[:, :n_mod_rows].reshape(N_DEV, depth, N_CHIPS, cs_ada)
    dmod_cols = lax.dynamic_slice_in_dim(dmod_all, chip, 1, axis=2).reshape(N_DEV, depth * cs_ada)
    kpad = 128 - N_DEV
    dmod_pad = jnp.concatenate([dmod_cols, jnp.zeros((kpad, depth * cs_ada), F32)], axis=0)
    ct_pad = jnp.concatenate([c_all.T, jnp.zeros((D, kpad), F32)], axis=1)
    g_ada = _mm(ct_pad, dmod_pad, mode="nn", name="ada_dw", a_silu=True, out3d=(depth, cs_ada), tn=cs_ada)
    ada_out = _adamw(g_ada.reshape(depth, 1, D, cs_ada), ada_w, m_ada_w, v_ada_w, name="adamw_ada")

    sums, shapes = chip_sums(early_keys, "from0")
    early_lands = _exchange(sums, shapes, _plan_scatter, 3 * len(sums), name="scatter_grads")
    done = [(early_keys, sums, early_lands)]
    for keys, started, first in in_flight:
        own, lands = _exchange_wait(started, len(keys), _plan_scatter, early_lands[-1], name=f"scatter_wait_{first}")
        done.append((keys, own, lands))
    total = {}
    for keys, own, lands in done:
        for k, o, r in zip(keys, own, lands):
            total[k] = _sum_own(o, r, chip_idx, name="sum_grads")
    halves = [jnp.stack([total[nm, i] for i in range(len(G[nm]))]) for nm in BIG]
    sib_halves = _swap_sibling(halves, name="swap_totals")
    big_out = {nm: _adamw_halves(hm, hs, half_idx, W[nm], Mo[nm], Vo[nm], name="adamw_big")
               for nm, hm, hs in zip(BIG, halves, sib_halves)}

    def res(nm, i):
        if nm == "ada_w":
            return ada_out[i]
        if nm in big_out:
            return big_out[nm][i]
        return small_out[i][nm]

    return (loss, grad_x, *[res(nm, 0) for nm in WEIGHTS], *[res(nm, 1) for nm in WEIGHTS],
            *[res(nm, 2) for nm in WEIGHTS], *[res(nm, 3) for nm in WEIGHTS])
```

```python
import functools
import math

import jax
import jax.numpy as jnp
from jax import lax
from jax.experimental import pallas as pl
from jax.experimental.pallas import tpu as pltpu

F32 = jnp.float32
BF16 = jnp.bfloat16
MESH = pl.DeviceIdType.MESH

EPS = 1e-6
CHUNK = 64
CONV_WIDTH = 31
CONV_HALO = 32
SUBLANES = 8
LANES = 128
CONV_BLOCK_ROWS = 128
RET_HEADS = 4
ROPE_BASE = 10000.0
ADAM_LR = 0.001
ADAM_B1 = 0.9
ADAM_B2 = 0.999
ADAM_EPS = 1e-08
ADAM_WD = 0.01
ADAM_STEP = 10
N_CHIPS = 4
N_DEV = 8
V7X_VMEM_LIMIT = 48 * 1024 * 1024
DW_TOKENS = 2048
ANY = pl.BlockSpec(memory_space=pl.ANY)


def _cparams(sem=None):
    return pltpu.CompilerParams(dimension_semantics=sem, vmem_limit_bytes=V7X_VMEM_LIMIT)


def _sigmoid(x):
    return jax.nn.sigmoid(x)


def _silu(x):
    return x * _sigmoid(x)


_DIMS = {
    "nn": (((1,), (0,)), ((), ())),
    "nt": (((1,), (1,)), ((), ())),
    "tn": (((0,), (0,)), ((), ())),
}


def _mm(a, b, *, mode, name, out_dtype=F32, tm=2048, tn=1024, tk=1024, b3d=False, out3d=None,
        bias=None, epi=None, extra=None, a_silu=False):
    if mode == "tn":
        K, M = a.shape
    else:
        M, K = a.shape
    if b3d:
        P, R, Cs = b.shape
        bshape = (R, P * Cs)
    else:
        bshape = b.shape
    N = bshape[0] if mode == "nt" else bshape[1]
    assert (bshape[1] if mode == "nt" else bshape[0]) == K, (name, a.shape, b.shape)
    tm, tn, tk = min(tm, M), min(tn, N), min(tk, K)
    assert M % tm == 0 and N % tn == 0 and K % tk == 0, (name, M, N, K, tm, tn, tk)
    nk = K // tk

    if mode == "tn":
        a_spec = pl.BlockSpec((tk, tm), lambda i, j, k: (k, i))
    else:
        a_spec = pl.BlockSpec((tm, tk), lambda i, j, k: (i, k))
    if mode == "nt":
        if b3d:
            nb = Cs // tk
            assert Cs % tk == 0
            b_spec = pl.BlockSpec((None, tn, tk), lambda i, j, k: (k // nb, j, k % nb))
        else:
            b_spec = pl.BlockSpec((tn, tk), lambda i, j, k: (j, k))
    else:
        if b3d:
            nb = Cs // tn
            assert Cs % tn == 0
            b_spec = pl.BlockSpec((None, tk, tn), lambda i, j, k: (j // nb, k, j % nb))
        else:
            b_spec = pl.BlockSpec((tk, tn), lambda i, j, k: (k, j))
    in_specs = [a_spec, b_spec]
    args = [a, b]
    if bias is not None:
        in_specs.append(pl.BlockSpec((1, tn), lambda i, j, k: (0, j)))
        args.append(bias)
    if extra is not None:
        in_specs.append(pl.BlockSpec((tm, tn), lambda i, j, k: (i, j)))
        args.append(extra)

    if out3d is not None:
        P_o, Cs_o = out3d
        assert P_o * Cs_o == N and Cs_o % tn == 0
        nbo = Cs_o // tn
        o_spec = pl.BlockSpec((None, tm, tn), lambda i, j, k: (j // nbo, i, j % nbo))
        o_shape = (P_o, M, Cs_o)
    else:
        o_spec = pl.BlockSpec((tm, tn), lambda i, j, k: (i, j))
        o_shape = (M, N)
    if epi == "relu2":
        out_shape = [jax.ShapeDtypeStruct(o_shape, BF16), jax.ShapeDtypeStruct(o_shape, BF16)]
        out_specs = [o_spec, o_spec]
    else:
        out_shape = jax.ShapeDtypeStruct(o_shape, out_dtype)
        out_specs = o_spec
    n_out = 2 if epi == "relu2" else 1
    dims = _DIMS[mode]
    has_bias, has_extra = bias is not None, extra is not None

    def body(*refs):
        a_ref, b_ref = refs[0], refs[1]
        pos = 2
        bias_ref = extra_ref = None
        if has_bias:
            bias_ref = refs[pos]
            pos += 1
        if has_extra:
            extra_ref = refs[pos]
            pos += 1
        outs = refs[pos:pos + n_out]
        acc_ref = refs[pos + n_out] if nk > 1 else None

        def partial():
            av = a_ref[...]
            if a_silu:
                av = _silu(av)
            return lax.dot_general(av, b_ref[...], dims, preferred_element_type=F32)

        def finish(r):
            if has_bias:
                r = r + bias_ref[...]
            if epi == "relu2":
                rr = jnp.maximum(r, 0.0)
                outs[0][...] = rr.astype(BF16)
                outs[1][...] = (rr * rr).astype(BF16)
            elif epi == "mul2":
                outs[0][...] = (r * 2.0 * extra_ref[...].astype(F32)).astype(outs[0].dtype)
            else:
                outs[0][...] = r.astype(outs[0].dtype)

        if nk == 1:
            finish(partial())
        else:
            k = pl.program_id(2)

            @pl.when(k == 0)
            def _():
                acc_ref[...] = jnp.zeros_like(acc_ref)

            acc_ref[...] += partial()

            @pl.when(k == nk - 1)
            def _():
                finish(acc_ref[...])

    return pl.pallas_call(
        body, name=name, grid=(M // tm, N // tn, nk), in_specs=in_specs, out_specs=out_specs,
        out_shape=out_shape,
        scratch_shapes=[pltpu.VMEM((tm, tn), F32)] if nk > 1 else [],
        compiler_params=_cparams(("parallel", "parallel", "arbitrary")),
    )(*args)


def _modnorm(x, gain, shift, scale):
    y = x * lax.rsqrt(jnp.mean(x * x, axis=-1, keepdims=True) + EPS)
    return (y * gain) * (1.0 + scale) + shift


def _row_fwd(xprev, y, vec, *, name, ts=512):
    S, D = xprev.shape
    ts = min(ts, S)
    has_res = y is not None
    row = pl.BlockSpec((ts, D), lambda i: (i, 0))
    vspec = pl.BlockSpec((8, D), lambda i: (0, 0))

    def body(*refs):
        if has_res:
            xp_ref, y_ref, v_ref, x_ref, h_ref = refs
            x = xp_ref[...] + v_ref[0:1, :] * y_ref[...]
            x_ref[...] = x
        else:
            xp_ref, v_ref, h_ref = refs
            x = xp_ref[...]
        h_ref[...] = _modnorm(x, v_ref[1:2, :], v_ref[2:3, :], v_ref[3:4, :]).astype(BF16)

    if has_res:
        return pl.pallas_call(
            body, name=name, grid=(S // ts,), in_specs=[row, row, vspec], out_specs=[row, row],
            out_shape=[jax.ShapeDtypeStruct((S, D), F32), jax.ShapeDtypeStruct((S, D), BF16)],
            compiler_params=_cparams(("parallel",)),
        )(xprev, y, vec)
    h = pl.pallas_call(
        body, name=name, grid=(S // ts,), in_specs=[row, vspec], out_specs=row,
        out_shape=jax.ShapeDtypeStruct((S, D), BF16),
        compiler_params=_cparams(("parallel",)),
    )(xprev, vec)
    return xprev, h


def _row_bwd(xin, dh, dxout, yprev, vec, *, name, ts=512):
    S, D = xin.shape
    ts = min(ts, S)
    row = pl.BlockSpec((ts, D), lambda i: (i, 0))
    vspec = pl.BlockSpec((8, D), lambda i: (0, 0))

    def body(x_ref, dh_ref, dx_ref, y_ref, v_ref, dxin_ref, dy_ref, part_ref):
        @pl.when(pl.program_id(0) == 0)
        def _():
            part_ref[...] = jnp.zeros_like(part_ref)

        gate = v_ref[0:1, :]
        _, vjp = jax.vjp(_modnorm, x_ref[...], v_ref[1:2, :], v_ref[2:3, :], v_ref[3:4, :])
        dxn, dgain, dshift, dscale = vjp(dh_ref[...].astype(F32))
        dxin = dx_ref[...] + dxn
        dxin_ref[...] = dxin
        dy = dxin * gate
        dy_ref[...] = dy.astype(BF16)
        part_ref[0:1, :] += jnp.sum(dxin * y_ref[...], axis=0, keepdims=True)
        part_ref[1:2, :] += dgain
        part_ref[2:3, :] += dshift
        part_ref[3:4, :] += dscale
        part_ref[4:5, :] += jnp.sum(dy, axis=0, keepdims=True)

    return pl.pallas_call(
        body, name=name, grid=(S // ts,), in_specs=[row, row, row, row, vspec],
        out_specs=[row, row, vspec],
        out_shape=[jax.ShapeDtypeStruct((S, D), F32), jax.ShapeDtypeStruct((S, D), BF16),
                   jax.ShapeDtypeStruct((8, D), F32)],
        compiler_params=_cparams(("arbitrary",)),
    )(xin, dh, dxout, yprev, vec)


def _final(xprev, y, target, vec, *, name, ts=512):
    S, D = xprev.shape
    ts = min(ts, S)
    row = pl.BlockSpec((ts, D), lambda i: (i, 0))
    vspec = pl.BlockSpec((8, D), lambda i: (0, 0))

    def norm(x, gain):
        return x * lax.rsqrt(jnp.mean(x * x, axis=-1, keepdims=True) + EPS) * gain

    def body(xp_ref, y_ref, t_ref, v_ref, dx_ref, dy_ref, part_ref):
        @pl.when(pl.program_id(0) == 0)
        def _():
            part_ref[...] = jnp.zeros_like(part_ref)

        gate = v_ref[0:1, :]
        yv = y_ref[...]
        x = xp_ref[...] + gate * yv
        out, vjp = jax.vjp(norm, x, v_ref[1:2, :])
        err = out - t_ref[...]
        dx, dgain = vjp(err * (1.0 / D))
        dx_ref[...] = dx
        dy_ref[...] = (dx * gate).astype(BF16)
        part_ref[0:1, :] += jnp.sum(dx * yv, axis=0, keepdims=True)
        part_ref[1:2, :] += dgain
        part_ref[2:3, :] += jnp.sum(err * err, axis=0, keepdims=True) * (0.5 / D)

    return pl.pallas_call(
        body, name=name, grid=(S // ts,), in_specs=[row, row, row, vspec], out_specs=[row, row, vspec],
        out_shape=[jax.ShapeDtypeStruct((S, D), F32), jax.ShapeDtypeStruct((S, D), BF16),
                   jax.ShapeDtypeStruct((8, D), F32)],
        compiler_params=_cparams(("arbitrary",)),
    )(xprev, y, target, vec)


def _ln_silu(cv, g, b):
    mu = jnp.mean(cv, axis=-1, keepdims=True)
    var = jnp.mean(jnp.square(cv - mu), axis=-1, keepdims=True)
    u = (cv - mu) * lax.rsqrt(var + EPS) * g + b
    return _silu(u)


def _shift_copies(ext, sh, n):
    for b in range(1, SUBLANES):
        sh[b - 1, 0:n, :] = ext[pl.ds(b, n), :]


def _shifted(ext, sh, off, r0, rows, cols):
    a, b = divmod(off, SUBLANES)
    if b == 0:
        return ext[pl.ds(SUBLANES * a + r0, rows), cols]
    return sh[b - 1, pl.ds(SUBLANES * a + r0, rows), cols]


def _conv_fwd(u, wdw, vec, *, name, ts=256):
    S, D2 = u.shape
    D = D2 // 2
    ts = min(ts, S)
    H = CONV_HALO
    row = pl.BlockSpec((ts, D), lambda i: (i, 0))

    rb_rows = min(CONV_BLOCK_ROWS, ts)

    def body(u_ref, w_ref, v_ref, vo_ref, cv_ref, z_ref, ext, sh):
        @pl.when(pl.program_id(0) == 0)
        def _():
            ext[0:H, :] = jnp.zeros((H, D), F32)

        uu = u_ref[...]
        v = uu[:, :D] * _sigmoid(uu[:, D:])
        vo_ref[...] = v
        ext[H:H + ts, :] = v
        _shift_copies(ext, sh, ts + H - 8)
        for r0 in range(0, ts, rb_rows):
            for c0 in range(0, D, LANES):
                cols = pl.ds(c0, LANES)
                acc = jnp.zeros((rb_rows, LANES), F32)
                for t in range(CONV_WIDTH):
                    src = _shifted(ext, sh, H - (CONV_WIDTH - 1) + t, r0, rb_rows, cols)
                    acc = acc + src * w_ref[pl.ds(t, 1), cols]
                cv_ref[pl.ds(r0, rb_rows), cols] = acc + v_ref[0:1, cols]
        z_ref[...] = _ln_silu(cv_ref[...], v_ref[1:2, :], v_ref[2:3, :]).astype(BF16)
        ext[0:H, :] = ext[ts:ts + H, :]

    return pl.pallas_call(
        body, name=name, grid=(S // ts,),
        in_specs=[pl.BlockSpec((ts, D2), lambda i: (i, 0)), pl.BlockSpec((H, D), lambda i: (0, 0)),
                  pl.BlockSpec((8, D), lambda i: (0, 0))],
        out_specs=[row, row, row],
        out_shape=[jax.ShapeDtypeStruct((S, D), F32), jax.ShapeDtypeStruct((S, D), F32),
                   jax.ShapeDtypeStruct((S, D), BF16)],
        scratch_shapes=[pltpu.VMEM((ts + H, D), F32), pltpu.VMEM((7, ts + H - 8, D), F32)],
        compiler_params=_cparams(("arbitrary",)),
    )(u, wdw, vec)


def _conv_bwd(dz, cv, v, u, wdw, vec, *, name, ts=256):
    S, D = cv.shape
    ts = min(ts, S)
    H = CONV_HALO
    nt = S // ts
    per = ts // H
    rev = lambda i: (nt - 1 - i, 0)
    row = pl.BlockSpec((ts, D), rev)

    rb_rows = min(CONV_BLOCK_ROWS, ts)
    nsh = ts + H - 8

    def body(dz_ref, cv_ref, v_ref, vh_ref, u_ref, w_ref, vec_ref, du_ref, dw_ref, part_ref, dbu_ref,
             dext, vext, dsh, vsh, dwacc, dvbuf):
        i = pl.program_id(0)

        @pl.when(i == 0)
        def _():
            dext[ts:ts + H, :] = jnp.zeros((H, D), F32)
            dwacc[...] = jnp.zeros_like(dwacc)
            part_ref[...] = jnp.zeros_like(part_ref)
            dbu_ref[...] = jnp.zeros_like(dbu_ref)

        _, vjp = jax.vjp(_ln_silu, cv_ref[...], vec_ref[1:2, :], vec_ref[2:3, :])
        dcv, dg, db = vjp(dz_ref[...])
        part_ref[0:1, :] += jnp.sum(dcv, axis=0, keepdims=True)
        part_ref[1:2, :] += dg
        part_ref[2:3, :] += db
        dext[0:ts, :] = dcv
        vext[0:H, :] = vh_ref[...] * jnp.where(i == nt - 1, 0.0, 1.0)
        vext[H:H + ts, :] = v_ref[...]
        _shift_copies(dext, dsh, nsh)
        _shift_copies(vext, vsh, nsh)
        for r0 in range(0, ts, rb_rows):
            for c0 in range(0, D, LANES):
                cols = pl.ds(c0, LANES)
                dblk = dext[pl.ds(r0, rb_rows), cols]
                dv = jnp.zeros((rb_rows, LANES), F32)
                for t in range(CONV_WIDTH):
                    prod = dblk * _shifted(vext, vsh, H - (CONV_WIDTH - 1) + t, r0, rb_rows, cols)
                    parts = [prod[s:s + SUBLANES, :] for s in range(0, rb_rows, SUBLANES)]
                    while len(parts) > 1:
                        parts = [parts[k] + parts[k + 1] for k in range(0, len(parts), 2)]
                    dwacc[pl.ds(t * SUBLANES, SUBLANES), cols] += parts[0]
                    dv = dv + _shifted(dext, dsh, CONV_WIDTH - 1 - t, r0, rb_rows, cols) * w_ref[pl.ds(t, 1), cols]
                dvbuf[pl.ds(r0, rb_rows), cols] = dv
        dv = dvbuf[...]
        uu = u_ref[...]
        a, g = uu[:, :D], uu[:, D:]
        sg = _sigmoid(g)
        da = dv * sg
        dg_ = dv * a * sg * (1.0 - sg)
        du = jnp.concatenate([da, dg_], axis=-1)
        du_ref[...] = du.astype(BF16)
        dbu_ref[0:1, :] += jnp.sum(du, axis=0, keepdims=True)
        dext[ts:ts + H, :] = dext[0:H, :]

        @pl.when(i == nt - 1)
        def _():
            dw_ref[...] = jnp.zeros_like(dw_ref)
            for t in range(CONV_WIDTH):
                dw_ref[pl.ds(t, 1), :] = jnp.sum(dwacc[pl.ds(t * SUBLANES, SUBLANES), :], axis=0, keepdims=True)

    return pl.pallas_call(
        body, name=name, grid=(nt,),
        in_specs=[row, row, row,
                  pl.BlockSpec((H, D), lambda i: (jnp.maximum((nt - 1 - i) * per - 1, 0), 0)),
                  pl.BlockSpec((ts, 2 * D), rev), pl.BlockSpec((H, D), lambda i: (0, 0)),
                  pl.BlockSpec((8, D), lambda i: (0, 0))],
        out_specs=[pl.BlockSpec((ts, 2 * D), rev), pl.BlockSpec((H, D), lambda i: (0, 0)),
                   pl.BlockSpec((8, D), lambda i: (0, 0)), pl.BlockSpec((8, 2 * D), lambda i: (0, 0))],
        out_shape=[jax.ShapeDtypeStruct((S, 2 * D), BF16), jax.ShapeDtypeStruct((H, D), F32),
                   jax.ShapeDtypeStruct((8, D), F32), jax.ShapeDtypeStruct((8, 2 * D), F32)],
        scratch_shapes=[pltpu.VMEM((ts + H, D), F32), pltpu.VMEM((ts + H, D), F32),
                        pltpu.VMEM((7, nsh, D), F32), pltpu.VMEM((7, nsh, D), F32),
                        pltpu.VMEM((CONV_WIDTH * SUBLANES, D), F32), pltpu.VMEM((ts, D), F32)],
        compiler_params=_cparams(("arbitrary",)),
    )(dz, cv, v, v, u, wdw, vec)


def _rope(x, c, s, half):
    x1, x2 = x[:, :half], x[:, half:]
    return jnp.concatenate([x1 * c - x2 * s, x2 * c + x1 * s], axis=-1)


def _rope_t(d, c, s, half):
    d1, d2 = d[:, :half], d[:, half:]
    return jnp.concatenate([d1 * c + d2 * s, d2 * c - d1 * s], axis=-1)


def _gn_gate(y, gate, g, b):
    mu = jnp.mean(y, axis=-1, keepdims=True)
    var = jnp.mean(jnp.square(y - mu), axis=-1, keepdims=True)
    return _silu(gate) * ((y - mu) * lax.rsqrt(var + EPS) * g + b)


def _dot(a, b, mode="nn"):
    return lax.dot_general(a, b, _DIMS[mode], preferred_element_type=F32)


def _ret_tables(H):
    lg = jnp.log(1.0 - 2.0 ** (-5.0 - jnp.arange(H, dtype=F32)))
    idx = jnp.arange(CHUNK, dtype=F32)
    dmat = jnp.exp(lg[:, None, None] * jnp.abs(idx[:, None] - idx[None, :]))
    xi = jnp.exp(lg[:, None] * (idx + 1.0))[..., None]
    zeta = jnp.exp(lg[:, None] * (CHUNK - 1.0 - idx))[..., None]
    dec = jnp.exp(lg * CHUNK)[:, None, None]
    return dmat, xi, zeta, dec


RET_HEADS_PER_STEP = 4


def _ret_specs(R, dk, dv, half, hps, order):
    C = CHUNK
    ng = RET_HEADS // hps
    return dict(
        q=pl.BlockSpec((R, hps * dk), lambda h, n: (order(n), h)),
        k=pl.BlockSpec((R, hps * dk), lambda h, n: (order(n), ng + h)),
        v=pl.BlockSpec((R, hps * dv), lambda h, n: (order(n), ng + h)),
        gate=pl.BlockSpec((R, hps * dv), lambda h, n: (order(n), 2 * ng + h)),
        rope=pl.BlockSpec((R, half), lambda h, n: (order(n), 0)),
        dmat=pl.BlockSpec((hps, C, C), lambda h, n: (h, 0, 0)),
        col=pl.BlockSpec((hps, C, 1), lambda h, n: (h, 0, 0)),
        one=pl.BlockSpec((hps, 1, 1), lambda h, n: (h, 0, 0)),
        gn=pl.BlockSpec((hps, 1, dv), lambda h, n: (h, 0, 0)),
        yv=pl.BlockSpec((R, hps * dv), lambda h, n: (order(n), h)),
        yk=pl.BlockSpec((R, hps * dk), lambda h, n: (order(n), h)),
    )


def _ret_fwd(proj, cos, sin, tables, gn_g, gn_b, *, name, cps=4):
    S = proj.shape[0]
    D = proj.shape[1] // 6
    H, C, hps = RET_HEADS, CHUNK, RET_HEADS_PER_STEP
    dk, dv, half = D // H, 2 * D // H, D // H // 2
    nc = S // C
    cps = min(cps, nc)
    R = cps * C
    scale = dk ** -0.5
    sp = _ret_specs(R, dk, dv, half, hps, lambda n: n)
    dmat, xi, zeta, dec = tables

    def body(q_ref, k_ref, v_ref, g_ref, cos_ref, sin_ref, dm_ref, xi_ref, ze_ref, dec_ref, gg_ref, gb_ref,
             y_ref, y2_ref, st_ref, state):
        @pl.when(pl.program_id(1) == 0)
        def _():
            state[...] = jnp.zeros_like(state)

        for j in range(cps):
            rows = pl.ds(j * C, C)
            cs, sn = cos_ref[rows, :], sin_ref[rows, :]
            for hh in range(hps):
                ck, cv = pl.ds(hh * dk, dk), pl.ds(hh * dv, dv)
                dm, xv, zv, dc = dm_ref[hh], xi_ref[hh], ze_ref[hh], dec_ref[hh]
                qr = _rope(q_ref[rows, ck].astype(F32), cs, sn, half)
                kr = _rope(k_ref[rows, ck].astype(F32), cs, sn, half) * scale
                vb = v_ref[rows, cv]
                p = (_dot(qr.astype(BF16), kr.astype(BF16), "nt") * dm).astype(BF16)
                st = state[hh]
                stb = st.astype(BF16)
                st_ref[hh, j] = stb
                y = _dot(p, vb) + _dot((qr * xv).astype(BF16), stb)
                state[hh] = st * dc + _dot((kr * zv).astype(BF16), vb, "tn")
                y_ref[rows, cv] = y
                y2_ref[rows, cv] = _gn_gate(y, g_ref[rows, cv].astype(F32), gg_ref[hh], gb_ref[hh]).astype(BF16)

    return pl.pallas_call(
        body, name=name, grid=(H // hps, nc // cps),
        in_specs=[sp["q"], sp["k"], sp["v"], sp["gate"], sp["rope"], sp["rope"], sp["dmat"], sp["col"],
                  sp["col"], sp["one"], sp["gn"], sp["gn"]],
        out_specs=[sp["yv"], sp["yv"], pl.BlockSpec((hps, cps, dk, dv), lambda h, n: (h, n, 0, 0))],
        out_shape=[jax.ShapeDtypeStruct((S, 2 * D), F32), jax.ShapeDtypeStruct((S, 2 * D), BF16),
                   jax.ShapeDtypeStruct((H, nc, dk, dv), BF16)],
        scratch_shapes=[pltpu.VMEM((hps, dk, dv), F32)],
        compiler_params=_cparams(("arbitrary", "arbitrary")),
    )(proj, proj, proj, proj, cos, sin, dmat, xi, zeta, dec, gn_g, gn_b)


def _ret_bwd(proj, cos, sin, tables, gn_g, gn_b, y, dy2, states, *, name, cps=4):
    S = proj.shape[0]
    D = proj.shape[1] // 6
    H, C, hps = RET_HEADS, CHUNK, RET_HEADS_PER_STEP
    dk, dv, half = D // H, 2 * D // H, D // H // 2
    nc = S // C
    cps = min(cps, nc)
    ns = nc // cps
    R = cps * C
    scale = dk ** -0.5
    order = lambda n: ns - 1 - n
    sp = _ret_specs(R, dk, dv, half, hps, order)
    dmat, xi, zeta, dec = tables

    def body(q_ref, k_ref, v_ref, g_ref, cos_ref, sin_ref, dm_ref, xi_ref, ze_ref, dec_ref, gg_ref, gb_ref,
             y_ref, dy2_ref, st_ref, dq_ref, dk_ref, dv_ref, dg_ref, dgg_ref, dgb_ref, gst):
        @pl.when(pl.program_id(1) == 0)
        def _():
            gst[...] = jnp.zeros_like(gst)
            dgg_ref[...] = jnp.zeros_like(dgg_ref)
            dgb_ref[...] = jnp.zeros_like(dgb_ref)

        for j in reversed(range(cps)):
            rows = pl.ds(j * C, C)
            cs, sn = cos_ref[rows, :], sin_ref[rows, :]
            for hh in range(hps):
                ck, cv = pl.ds(hh * dk, dk), pl.ds(hh * dv, dv)
                dm, xv, zv, dc = dm_ref[hh], xi_ref[hh], ze_ref[hh], dec_ref[hh]
                _, vjp = jax.vjp(_gn_gate, y_ref[rows, cv], g_ref[rows, cv].astype(F32), gg_ref[hh], gb_ref[hh])
                dy, dgate, dgg, dgb = vjp(dy2_ref[rows, cv])
                dgg_ref[hh] += dgg
                dgb_ref[hh] += dgb
                dg_ref[rows, cv] = dgate.astype(BF16)
                dyb = dy.astype(BF16)
                qr = _rope(q_ref[rows, ck].astype(F32), cs, sn, half)
                kr = _rope(k_ref[rows, ck].astype(F32), cs, sn, half) * scale
                qb, kb, vb = qr.astype(BF16), kr.astype(BF16), v_ref[rows, cv]
                p = (_dot(qb, kb, "nt") * dm).astype(BF16)
                g = gst[hh]
                gb16 = g.astype(BF16)
                sprev = st_ref[hh, j]
                dvv = _dot(p, dyb, "tn") + _dot((kr * zv).astype(BF16), gb16)
                dpb = (_dot(dyb, vb, "nt") * dm).astype(BF16)
                dqr = _dot(dpb, kb) + _dot(dyb, sprev, "nt") * xv
                dkr = _dot(dpb, qb, "tn") + _dot(vb, gb16, "nt") * zv
                gst[hh] = g * dc + _dot((qr * xv).astype(BF16), dyb, "tn")
                dq_ref[rows, ck] = _rope_t(dqr, cs, sn, half).astype(BF16)
                dk_ref[rows, ck] = _rope_t(dkr * scale, cs, sn, half).astype(BF16)
                dv_ref[rows, cv] = dvv.astype(BF16)

    return pl.pallas_call(
        body, name=name, grid=(H // hps, ns),
        in_specs=[sp["q"], sp["k"], sp["v"], sp["gate"], sp["rope"], sp["rope"], sp["dmat"], sp["col"],
                  sp["col"], sp["one"], sp["gn"], sp["gn"], sp["yv"], sp["yv"],
                  pl.BlockSpec((hps, cps, dk, dv), lambda h, n: (h, order(n), 0, 0))],
        out_specs=[sp["yk"], sp["yk"], sp["yv"], sp["yv"], sp["gn"], sp["gn"]],
        out_shape=[jax.ShapeDtypeStruct((S, D), BF16), jax.ShapeDtypeStruct((S, D), BF16),
                   jax.ShapeDtypeStruct((S, 2 * D), BF16), jax.ShapeDtypeStruct((S, 2 * D), BF16),
                   jax.ShapeDtypeStruct((H, 1, dv), F32), jax.ShapeDtypeStruct((H, 1, dv), F32)],
        scratch_shapes=[pltpu.VMEM((hps, dk, dv), F32)],
        compiler_params=_cparams(("arbitrary", "arbitrary")),
    )(proj, proj, proj, proj, cos, sin, dmat, xi, zeta, dec, gn_g, gn_b, y, dy2, states)


def _rows_tile(rows, cols, n_arrays):
    cap = max(8, V7X_VMEM_LIMIT // 3 // (n_arrays * 2 * 4 * cols))
    t = rows
    while t > cap and t % 2 == 0:
        t //= 2
    return t


def _add_half(g, r, half_idx, *, name):
    P, R, Cc = g.shape
    hR = R // 2
    tr = _rows_tile(hR, Cc, 3)
    nb = hR // tr

    def body(h_ref, g_ref, r_ref, o_ref):
        o_ref[...] = (g_ref[...] + r_ref[...]).astype(BF16)

    return pl.pallas_call(
        body, name=name,
        grid_spec=pltpu.PrefetchScalarGridSpec(
            num_scalar_prefetch=1, grid=(P, nb),
            in_specs=[pl.BlockSpec((None, tr, Cc), lambda s, i, h: (s, h[0] * nb + i, 0)),
                      pl.BlockSpec((None, tr, Cc), lambda s, i, h: (s, i, 0))],
            out_specs=pl.BlockSpec((None, tr, Cc), lambda s, i, h: (s, i, 0))),
        out_shape=jax.ShapeDtypeStruct((P, hR, Cc), BF16), compiler_params=_cparams(("parallel", "parallel")),
    )(half_idx, g, r)


def _sum_slots(x, *, name):
    L, NS, R, Cc = x.shape
    tr = _rows_tile(R, Cc, NS + 1)

    def body(x_ref, o_ref):
        acc = x_ref[0].astype(F32)
        for s in range(1, NS):
            acc = acc + x_ref[s].astype(F32)
        o_ref[...] = acc

    return pl.pallas_call(
        body, name=name, grid=(L, R // tr),
        in_specs=[pl.BlockSpec((None, NS, tr, Cc), lambda l, i: (l, 0, i, 0))],
        out_specs=pl.BlockSpec((None, tr, Cc), lambda l, i: (l, i, 0)),
        out_shape=jax.ShapeDtypeStruct((L, R, Cc), F32), compiler_params=_cparams(("parallel", "parallel")),
    )(x)


def _adam_store(g, w_ref, m_ref, v_ref, go_ref, d_ref, mo_ref, vo_ref):
    mn = ADAM_B1 * m_ref[...] + (1.0 - ADAM_B1) * g
    vn = ADAM_B2 * v_ref[...] + (1.0 - ADAM_B2) * jnp.square(g)
    m_hat = mn / (1.0 - ADAM_B1 ** ADAM_STEP)
    v_hat = vn / (1.0 - ADAM_B2 ** ADAM_STEP)
    go_ref[...] = g
    d_ref[...] = -ADAM_LR * (m_hat / (jnp.sqrt(v_hat) + ADAM_EPS) + ADAM_WD * w_ref[...])
    mo_ref[...] = mn
    vo_ref[...] = vn


def _adamw(gslots, w, m, v, *, name):
    L, NS, R, Cc = gslots.shape
    tr = _rows_tile(R, Cc, NS + 7)
    gspec = pl.BlockSpec((None, NS, tr, Cc), lambda l, i: (l, 0, i, 0))
    spec = pl.BlockSpec((None, tr, Cc), lambda l, i: (l, i, 0))

    def body(g_ref, *refs):
        g = g_ref[0]
        for s in range(1, NS):
            g = g + g_ref[s]
        _adam_store(g, *refs)

    sd = jax.ShapeDtypeStruct((L, R, Cc), F32)
    return pl.pallas_call(
        body, name=name, grid=(L, R // tr), in_specs=[gspec, spec, spec, spec],
        out_specs=[spec, spec, spec, spec], out_shape=[sd, sd, sd, sd],
        compiler_params=_cparams(("parallel", "parallel")),
    )(gslots, w, m, v)


def _adamw_halves(g_mine, g_sib, half_idx, w, m, v, *, name):
    L, hR, Cc = g_mine.shape
    tr = _rows_tile(hR, Cc, 9)
    nbh = hR // tr
    gspec = pl.BlockSpec((None, tr, Cc), lambda l, i, h: (l, i % nbh, 0))
    spec = pl.BlockSpec((None, tr, Cc), lambda l, i, h: (l, i, 0))

    def body(h_ref, gm_ref, gs_ref, *refs):
        mine = (pl.program_id(1) // nbh) == h_ref[0]
        _adam_store(jnp.where(mine, gm_ref[...], gs_ref[...]), *refs)

    sd = jax.ShapeDtypeStruct((L, 2 * hR, Cc), F32)
    return pl.pallas_call(
        body, name=name,
        grid_spec=pltpu.PrefetchScalarGridSpec(
            num_scalar_prefetch=1, grid=(L, 2 * nbh), in_specs=[gspec, gspec, spec, spec, spec],
            out_specs=[spec, spec, spec, spec]),
        out_shape=[sd, sd, sd, sd], compiler_params=_cparams(("parallel", "parallel")),
    )(half_idx, g_mine, g_sib, w, m, v)


def _me():
    return lax.axis_index("x"), lax.axis_index("y"), lax.axis_index("c")


def _flip(v, bit):
    return 1 - v if bit else v


def _allgather8(x, *, name):
    def body(x_ref, out_ref, send_sems, recv_sems, loc_sem):
        mx, my, mc = _me()
        me = 4 * mx + 2 * my + mc
        loc = pltpu.make_async_copy(x_ref, out_ref.at[me], loc_sem)
        loc.start()
        sends, recvs = [], []
        for k in range(1, N_DEV):
            px, py, pc = _flip(mx, k & 4), _flip(my, k & 2), _flip(mc, k & 1)
            sends.append(pltpu.make_async_remote_copy(
                src_ref=x_ref, dst_ref=out_ref.at[me], send_sem=send_sems.at[k - 1],
                recv_sem=recv_sems.at[k - 1], device_id=(px, py, pc), device_id_type=MESH))
            recvs.append(pltpu.make_async_remote_copy(
                src_ref=x_ref, dst_ref=out_ref.at[4 * px + 2 * py + pc], send_sem=send_sems.at[k - 1],
                recv_sem=recv_sems.at[k - 1], device_id=(px, py, pc), device_id_type=MESH))
        for cp in sends:
            cp.start()
        for cp in recvs:
            cp.wait_recv()
        for cp in sends:
            cp.wait_send()
        loc.wait()

    return pl.pallas_call(
        body, name=name, in_specs=[ANY], out_specs=ANY,
        out_shape=jax.ShapeDtypeStruct((N_DEV,) + x.shape, x.dtype),
        scratch_shapes=[pltpu.SemaphoreType.DMA((N_DEV - 1,)), pltpu.SemaphoreType.DMA((N_DEV - 1,)),
                        pltpu.SemaphoreType.DMA],
    )(x)


def _gather_chips(arrays, after, *, name):
    n = len(arrays)

    def body(*refs):
        ins, outs = refs[:n], refs[n + 1:2 * n + 1]
        ici_send, ici_recv, d2d_send, d2d_recv, own_send, own_recv = refs[2 * n + 1:]
        mx, my, mc = _me()
        me = 2 * mx + my
        sib = (mx, my, 1 - mc)
        locs, sends, lands, passes, gifts = [], [], [], [], []
        for a in range(n):
            h = arrays[a].shape[0] // 2
            mine, other = pl.ds(mc * h, h), pl.ds((1 - mc) * h, h)
            locs.append(pltpu.make_async_remote_copy(
                src_ref=ins[a], dst_ref=outs[a].at[me], send_sem=own_send.at[a], recv_sem=own_recv.at[a],
                device_id=sib, device_id_type=MESH))
            for k in range(1, N_CHIPS):
                px, py = _flip(mx, k & 2), _flip(my, k & 1)
                peer = 2 * px + py
                ici = dict(send_sem=ici_send.at[a, k - 1], recv_sem=ici_recv.at[a, k - 1],
                           device_id=(px, py, mc), device_id_type=MESH)
                d2d = dict(send_sem=d2d_send.at[a, k - 1], recv_sem=d2d_recv.at[a, k - 1],
                           device_id=sib, device_id_type=MESH)
                sends.append(pltpu.make_async_remote_copy(
                    src_ref=ins[a].at[mine], dst_ref=outs[a].at[me, mine], **ici))
                lands.append(pltpu.make_async_remote_copy(
                    src_ref=ins[a].at[mine], dst_ref=outs[a].at[peer, mine], **ici))
                passes.append(pltpu.make_async_remote_copy(
                    src_ref=outs[a].at[peer, mine], dst_ref=outs[a].at[peer, mine], **d2d))
                gifts.append(pltpu.make_async_remote_copy(
                    src_ref=outs[a].at[peer, other], dst_ref=outs[a].at[peer, other], **d2d))
        for cp in locs + sends:
            cp.start()
        for land, fwd in zip(lands, passes):
            land.wait_recv()
            fwd.start()
        for cp in gifts + locs:
            cp.wait_recv()
        for cp in sends + passes + locs:
            cp.wait_send()

    nsem = (n, N_CHIPS - 1)
    return pl.pallas_call(
        body, name=name, in_specs=[ANY] * (n + 1), out_specs=[ANY] * n,
        out_shape=[jax.ShapeDtypeStruct((N_CHIPS,) + a.shape, a.dtype) for a in arrays],
        scratch_shapes=[pltpu.SemaphoreType.DMA(nsem), pltpu.SemaphoreType.DMA(nsem), pltpu.SemaphoreType.DMA(nsem),
                        pltpu.SemaphoreType.DMA(nsem), pltpu.SemaphoreType.DMA((n,)), pltpu.SemaphoreType.DMA((n,))],
    )(*arrays, after)


def _swap_half(arrays, *, name):
    n = len(arrays)

    def body(*refs):
        ins, outs = refs[:n], refs[n:2 * n]
        send_sems, recv_sems = refs[2 * n:]
        mx, my, mc = _me()
        cps = []
        for a in range(n):
            P, R, _ = arrays[a].shape
            cps.append(pltpu.make_async_remote_copy(
                src_ref=ins[a].at[pl.ds(0, P), pl.ds((1 - mc) * (R // 2), R // 2)], dst_ref=outs[a],
                send_sem=send_sems.at[a], recv_sem=recv_sems.at[a],
                device_id=(mx, my, 1 - mc), device_id_type=MESH))
        for cp in cps:
            cp.start()
        for cp in cps:
            cp.wait_recv()
        for cp in cps:
            cp.wait_send()

    return pl.pallas_call(
        body, name=name, in_specs=[ANY] * n, out_specs=[ANY] * n,
        out_shape=[jax.ShapeDtypeStruct((a.shape[0], a.shape[1] // 2, a.shape[2]), a.dtype) for a in arrays],
        scratch_shapes=[pltpu.SemaphoreType.DMA((n,)), pltpu.SemaphoreType.DMA((n,))],
    )(*arrays)


def _swap_sibling(arrays, *, name):
    n = len(arrays)

    def body(*refs):
        ins, outs = refs[:n], refs[n:2 * n]
        send_sems, recv_sems = refs[2 * n:]
        mx, my, mc = _me()
        cps = [pltpu.make_async_remote_copy(
            src_ref=ins[a], dst_ref=outs[a], send_sem=send_sems.at[a], recv_sem=recv_sems.at[a],
            device_id=(mx, my, 1 - mc), device_id_type=MESH) for a in range(n)]
        for cp in cps:
            cp.start()
        for cp in cps:
            cp.wait_recv()
        for cp in cps:
            cp.wait_send()

    return pl.pallas_call(
        body, name=name, in_specs=[ANY] * n, out_specs=[ANY] * n,
        out_shape=[jax.ShapeDtypeStruct(a.shape, a.dtype) for a in arrays],
        scratch_shapes=[pltpu.SemaphoreType.DMA((n,)), pltpu.SemaphoreType.DMA((n,))],
    )(*arrays)


def _plan_scatter(srcs, lands):
    mx, my, mc = _me()
    copies = []
    for a in range(len(srcs)):
        for k in range(1, N_CHIPS):
            px, py = _flip(mx, k & 2), _flip(my, k & 1)
            copies.append((srcs[a].at[2 * px + py], lands[a].at[k - 1], lands[a].at[k - 1], (px, py, mc)))
    return copies


def _plan_gather(srcs, lands):
    mx, my, mc = _me()
    me = 2 * mx + my
    copies = []
    for a in range(len(srcs)):
        copies.append((srcs[a], lands[a].at[me], lands[a].at[me], (mx, my, 1 - mc)))
        for k in range(1, N_CHIPS):
            px, py = _flip(mx, k & 2), _flip(my, k & 1)
            copies.append((srcs[a], lands[a].at[me], lands[a].at[2 * px + py], (px, py, mc)))
    return copies


def _copy(c, k, send_sems, recv_sems, landing=False):
    src, dst, land, dev = c
    return pltpu.make_async_remote_copy(src_ref=src, dst_ref=land if landing else dst, send_sem=send_sems.at[k],
                                        recv_sem=recv_sems.at[k], device_id=dev, device_id_type=MESH)


def _exchange(srcs, land_shapes, plan, ncopies, *, name):
    ni, nl = len(srcs), len(land_shapes)

    def body(*refs):
        send_sems, recv_sems = refs[ni + nl:]
        copies = plan(refs[:ni], refs[ni:ni + nl])
        for k, c in enumerate(copies):
            _copy(c, k, send_sems, recv_sems).start()
        for k, c in enumerate(copies):
            _copy(c, k, send_sems, recv_sems, landing=True).wait_recv()
        for k, c in enumerate(copies):
            _copy(c, k, send_sems, recv_sems).wait_send()

    return pl.pallas_call(
        body, name=name, in_specs=[ANY] * ni, out_specs=[ANY] * nl, out_shape=list(land_shapes),
        scratch_shapes=[pltpu.SemaphoreType.DMA((ncopies,)), pltpu.SemaphoreType.DMA((ncopies,))],
    )(*srcs)


HBM_SPEC = pl.BlockSpec(memory_space=pltpu.HBM)
SEM_SPEC = pl.BlockSpec(memory_space=pltpu.SEMAPHORE)
SPLIT_EFFECT = pltpu.SideEffectType.DATAFLOW_SIDE_EFFECTING


def _exchange_start(srcs, land_shapes, plan, ncopies, after, *, name):
    ni, nl = len(srcs), len(land_shapes)

    def body(*refs):
        in_refs, land_refs = refs[:ni], refs[ni:ni + nl]
        send_sems, recv_sems = refs[ni + nl + 1], refs[ni + nl + 2]
        token = refs[-1]
        for k, c in enumerate(plan(in_refs, land_refs)):
            _copy(c, k, send_sems, recv_sems).start()
        token[...] = jnp.zeros_like(token)

    bufs = [pltpu.with_memory_space_constraint(a, pltpu.HBM) for a in srcs]
    bufs += [pltpu.with_memory_space_constraint(lax.empty(s.shape, s.dtype), pltpu.HBM) for s in land_shapes]
    outs = pl.pallas_call(
        body, name=name,
        in_specs=[HBM_SPEC] * (ni + nl) + [ANY],
        out_specs=(SEM_SPEC, SEM_SPEC, *[HBM_SPEC] * (ni + nl), pl.BlockSpec(memory_space=pltpu.VMEM)),
        out_shape=(pltpu.SemaphoreType.DMA((ncopies,)), pltpu.SemaphoreType.DMA((ncopies,)),
                   *[pltpu.HBM(b.shape, b.dtype) for b in bufs], jax.ShapeDtypeStruct((8, 128), F32)),
        input_output_aliases={i: 2 + i for i in range(ni + nl)},
        compiler_params=pltpu.CompilerParams(has_side_effects=SPLIT_EFFECT),
    )(*bufs, after)
    return outs[:-1], outs[-1]


def _exchange_wait(started, ni, plan, after, *, name):
    send_sems, recv_sems = started[0], started[1]
    bufs = list(started[2:])
    nb = len(bufs)

    def body(*refs):
        in_refs, land_refs = refs[:ni], refs[ni:nb]
        send, recv = refs[nb], refs[nb + 1]
        for k, c in enumerate(plan(in_refs, land_refs)):
            cp = _copy(c, k, send, recv, landing=True)
            cp.wait_send()
            cp.wait_recv()

    outs = pl.pallas_call(
        body, name=name, in_specs=[HBM_SPEC] * nb + [SEM_SPEC, SEM_SPEC, ANY], out_specs=[HBM_SPEC] * nb,
        out_shape=[pltpu.HBM(b.shape, b.dtype) for b in bufs],
        input_output_aliases={i: i for i in range(nb)},
        compiler_params=pltpu.CompilerParams(has_side_effects=SPLIT_EFFECT),
    )(*bufs, send_sems, recv_sems, after)
    return list(outs[:ni]), list(outs[ni:])


def _sum_own(own, recv, chip_idx, *, name):
    _, R, Cc = own.shape
    tr = _rows_tile(R, Cc, 5)

    def body(s_ref, o_ref, r_ref, t_ref):
        acc = o_ref[...].astype(F32)
        for s in range(N_CHIPS - 1):
            acc = acc + r_ref[s].astype(F32)
        t_ref[...] = acc

    return pl.pallas_call(
        body, name=name,
        grid_spec=pltpu.PrefetchScalarGridSpec(
            num_scalar_prefetch=1, grid=(R // tr,),
            in_specs=[pl.BlockSpec((None, tr, Cc), lambda i, s: (s[0], i, 0)),
                      pl.BlockSpec((N_CHIPS - 1, tr, Cc), lambda i, s: (0, i, 0))],
            out_specs=pl.BlockSpec((tr, Cc), lambda i, s: (i, 0))),
        out_shape=jax.ShapeDtypeStruct((R, Cc), F32), compiler_params=_cparams(("parallel",)),
    )(chip_idx, own, recv)


BIG = ("conv_w_pw1", "conv_w_pw2", "ret_w_in", "ret_w_out", "mlp_w1", "mlp_w2")
COLS = ("conv_w_pw1", "ret_w_in", "mlp_w1")
SMALL = ("ada_b", "norm_mix_g", "norm_mlp_g", "conv_b_pw1", "conv_w_dw", "conv_b_dw", "conv_ln_g", "conv_ln_b",
         "conv_b_pw2", "ret_gn_g", "ret_gn_b", "final_norm_g")
SMALL_SHARDED = ("conv_w_dw", "ret_gn_g", "ret_gn_b")
WEIGHTS = ("ada_w", "ada_b", "norm_mix_g", "norm_mlp_g", "conv_w_pw1", "conv_b_pw1", "conv_w_dw", "conv_b_dw",
           "conv_ln_g", "conv_ln_b", "conv_w_pw2", "conv_b_pw2", "ret_w_in", "ret_gn_g", "ret_gn_b", "ret_w_out",
           "mlp_w1", "mlp_w2", "final_norm_g")


def _vec8(rows, D):
    rows = [r.reshape(1, D).astype(F32) for r in rows]
    return jnp.concatenate(rows + [jnp.zeros((8 - len(rows), D), F32)], axis=0)


def _unshard_last(g):
    nd = g.ndim
    t = jnp.transpose(g, tuple(range(1, nd - 1)) + (0, nd - 1))
    return t.reshape(t.shape[:-2] + (t.shape[-2] * t.shape[-1],))


def _pack(parts):
    flat = jnp.concatenate([p.reshape(-1).astype(F32) for p in parts])
    pad = (-flat.shape[0]) % 1024
    return jnp.concatenate([flat, jnp.zeros((pad,), F32)]).reshape(-1, 128)


def _unpack(packed, shapes):
    flat = packed.reshape(-1)
    out, pos = [], 0
    for s in shapes:
        n = math.prod(s)
        out.append(flat[pos:pos + n].reshape(s))
        pos += n
    return out


def kernel(x, c, ada_w, ada_b, norm_mix_g, norm_mlp_g, conv_w_pw1, conv_b_pw1, conv_w_dw, conv_b_dw, conv_ln_g, conv_ln_b, conv_w_pw2, conv_b_pw2, ret_w_in, ret_gn_g, ret_gn_b, ret_w_out, mlp_w1, mlp_w2, final_norm_g, loss_target, m_ada_w, m_ada_b, m_norm_mix_g, m_norm_mlp_g, m_conv_w_pw1, m_conv_b_pw1, m_conv_w_dw, m_conv_b_dw, m_conv_ln_g, m_conv_ln_b, m_conv_w_pw2, m_conv_b_pw2, m_ret_w_in, m_ret_gn_g, m_ret_gn_b, m_ret_w_out, m_mlp_w1, m_mlp_w2, m_final_norm_g, v_ada_w, v_ada_b, v_norm_mix_g, v_norm_mlp_g, v_conv_w_pw1, v_conv_b_pw1, v_conv_w_dw, v_conv_b_dw, v_conv_ln_g, v_conv_ln_b, v_conv_w_pw2, v_conv_b_pw2, v_ret_w_in, v_ret_gn_g, v_ret_gn_b, v_ret_w_out, v_mlp_w1, v_mlp_w2, v_final_norm_g):
    W = dict(ada_w=ada_w, ada_b=ada_b, norm_mix_g=norm_mix_g, norm_mlp_g=norm_mlp_g, conv_w_pw1=conv_w_pw1,
             conv_b_pw1=conv_b_pw1, conv_w_dw=conv_w_dw, conv_b_dw=conv_b_dw, conv_ln_g=conv_ln_g,
             conv_ln_b=conv_ln_b, conv_w_pw2=conv_w_pw2, conv_b_pw2=conv_b_pw2, ret_w_in=ret_w_in,
             ret_gn_g=ret_gn_g, ret_gn_b=ret_gn_b, ret_w_out=ret_w_out, mlp_w1=mlp_w1, mlp_w2=mlp_w2,
             final_norm_g=final_norm_g)
    Mo = dict(ada_w=m_ada_w, ada_b=m_ada_b, norm_mix_g=m_norm_mix_g, norm_mlp_g=m_norm_mlp_g,
              conv_w_pw1=m_conv_w_pw1, conv_b_pw1=m_conv_b_pw1, conv_w_dw=m_conv_w_dw, conv_b_dw=m_conv_b_dw,
              conv_ln_g=m_conv_ln_g, conv_ln_b=m_conv_ln_b, conv_w_pw2=m_conv_w_pw2, conv_b_pw2=m_conv_b_pw2,
              ret_w_in=m_ret_w_in, ret_gn_g=m_ret_gn_g, ret_gn_b=m_ret_gn_b, ret_w_out=m_ret_w_out,
              mlp_w1=m_mlp_w1, mlp_w2=m_mlp_w2, final_norm_g=m_final_norm_g)
    Vo = dict(ada_w=v_ada_w, ada_b=v_ada_b, norm_mix_g=v_norm_mix_g, norm_mlp_g=v_norm_mlp_g,
              conv_w_pw1=v_conv_w_pw1, conv_b_pw1=v_conv_b_pw1, conv_w_dw=v_conv_w_dw, conv_b_dw=v_conv_b_dw,
              conv_ln_g=v_conv_ln_g, conv_ln_b=v_conv_ln_b, conv_w_pw2=v_conv_w_pw2, conv_b_pw2=v_conv_b_pw2,
              ret_w_in=v_ret_w_in, ret_gn_g=v_ret_gn_g, ret_gn_b=v_ret_gn_b, ret_w_out=v_ret_w_out,
              mlp_w1=v_mlp_w1, mlp_w2=v_mlp_w2, final_norm_g=v_final_norm_g)

    S, D = x.shape[1], x.shape[2]
    depth = ada_w.shape[0]
    H = RET_HEADS
    dv = 2 * D // H
    xs = x.reshape(S, D)
    target = loss_target.reshape(S, D)
    mx, my, mc = _me()
    chip = 2 * mx + my
    dev = 4 * mx + 2 * my + mc

    def layer_weights(l):
        mixer = ("conv_w_pw1", "conv_w_pw2") if l % 2 == 0 else ("ret_w_in", "ret_w_out")
        return [(nm, l // 2) for nm in mixer] + [("mlp_w1", l), ("mlp_w2", l)]

    c_all = _allgather8(c.reshape(8, D // 8), name="gather_c").reshape(N_DEV, D)
    cs_ada = ada_w.shape[2]
    bias_sh = lax.dynamic_slice_in_dim(ada_b.reshape(depth, N_CHIPS, cs_ada), chip, 1, axis=1)
    mod_sh = _mm(c_all, ada_w, mode="nn", name="ada_fwd", b3d=True, tn=cs_ada, a_silu=True,
                 bias=bias_sh.reshape(1, depth * cs_ada))
    mod_all = _allgather8(mod_sh, name="gather_mod")[0::2]
    mod_me = lax.dynamic_slice_in_dim(mod_all, dev, 1, axis=1).reshape(N_CHIPS, depth, cs_ada)
    mod = jnp.transpose(mod_me, (1, 0, 2)).reshape(depth, 6, D)

    keys0 = layer_weights(0)[:2]
    got0 = _gather_chips([W[nm][i].astype(BF16) for nm, i in keys0] + [W[nm] for nm in SMALL_SHARDED], mod,
                         name="gather_weights")
    Wg = dict(zip(keys0, got0))
    full_small = {nm: _unshard_last(got0[len(keys0) + i]) for i, nm in enumerate(SMALL_SHARDED)}
    pending, order = {}, got0[0]
    for tag, keys in [("0m", layer_weights(0)[2:])] + [(l, layer_weights(l)) for l in range(1, depth)]:
        srcs = [W[nm][i].astype(BF16) for nm, i in keys]
        shapes = [jax.ShapeDtypeStruct((N_CHIPS,) + s.shape, BF16) for s in srcs]
        started, order = _exchange_start(srcs, shapes, _plan_gather, 4 * len(srcs), order,
                                         name=f"gather_start_{tag}")
        pending[tag] = (keys, started)
    mod = mod + order[0, 0]

    def arrive(tag, after):
        keys, started = pending.pop(tag)
        _, lands = _exchange_wait(started, len(keys), _plan_gather, after, name=f"gather_wait_{tag}")
        Wg.update(zip(keys, lands))

    def wfull(nm, l):
        g = Wg[nm, l]
        return g.reshape(g.shape[0] * g.shape[1], g.shape[2])

    pos_ids = jnp.arange(S, dtype=F32)
    dk = D // H
    inv = ROPE_BASE ** (-jnp.arange(0, dk, 2, dtype=F32) / dk)
    ang = pos_ids[:, None] * inv[None, :]
    cos_t, sin_t = jnp.cos(ang), jnp.sin(ang)
    tables = _ret_tables(H)
    gn_g_full = full_small["ret_gn_g"].reshape(-1, H, 1, dv)
    gn_b_full = full_small["ret_gn_b"].reshape(-1, H, 1, dv)
    wdw_full = full_small["conv_w_dw"]

    def wdw_pad(j):
        return jnp.concatenate([wdw_full[j], jnp.zeros((CONV_HALO - CONV_WIDTH, D), F32)], axis=0)

    saved = []
    xa, y_prev, gate_prev = xs, None, None
    for l in range(depth):
        j = l // 2
        sv = {}
        if l in pending:
            arrive(l, y_prev)
        vec_a = _vec8([gate_prev if gate_prev is not None else jnp.zeros((D,), F32), norm_mix_g[l], mod[l, 0],
                       mod[l, 1]], D)
        xa, h = _row_fwd(xa, y_prev, vec_a, name="row_fwd" if y_prev is not None else "row_fwd_first")
        sv.update(xa=xa, h=h, vec_a=vec_a)
        if l % 2 == 0:
            u = _mm(h, Wg["conv_w_pw1", j], mode="nn", name="pw1_fwd", b3d=True,
                    tn=Wg["conv_w_pw1", j].shape[2], bias=conv_b_pw1[j].reshape(1, -1))
            cvec = _vec8([conv_b_dw[j], conv_ln_g[j], conv_ln_b[j]], D)
            v_glu, cv, z = _conv_fwd(u, wdw_pad(j), cvec, name="conv_fwd")
            ymix = _mm(z, wfull("conv_w_pw2", j), mode="nn", name="pw2_fwd", bias=conv_b_pw2[j].reshape(1, -1))
            sv.update(u=u, v_glu=v_glu, cv=cv, z=z, cvec=cvec)
        else:
            proj = _mm(h, Wg["ret_w_in", j], mode="nn", name="win_fwd", b3d=True, out_dtype=BF16,
                       tn=Wg["ret_w_in", j].shape[2])
            yr, y2, states = _ret_fwd(proj, cos_t, sin_t, tables, gn_g_full[j], gn_b_full[j], name="ret_fwd")
            ymix = _mm(y2, wfull("ret_w_out", j), mode="nn", name="wout_fwd")
            sv.update(proj=proj, yr=yr, y2=y2, states=states)
        if f"{l}m" in pending:
            arrive(f"{l}m", ymix)
        vec_b = _vec8([mod[l, 2], norm_mlp_g[l], mod[l, 3], mod[l, 4]], D)
        xb, h2 = _row_fwd(xa, ymix, vec_b, name="row_fwd")
        ra, p = _mm(h2, Wg["mlp_w1", l], mode="nn", name="w1_fwd", b3d=True, tn=Wg["mlp_w1", l].shape[2],
                    epi="relu2")
        mo = _mm(p, wfull("mlp_w2", l), mode="nn", name="w2_fwd")
        sv.update(ymix=ymix, xb=xb, h2=h2, ra=ra, p=p, mo=mo, vec_b=vec_b)
        saved.append(sv)
        xa, y_prev, gate_prev = xb, mo, mod[l, 5]

    fvec = _vec8([gate_prev, final_norm_g], D)
    dx, dyb, fpart = _final(xa, y_prev, target, fvec, name="final")
    loss = lax.psum(jnp.sum(fpart[2]), ("x", "y", "c"))
    G = {nm: [None] * W[nm].shape[0] for nm in BIG}
    dmod = [[None] * 6 for _ in range(depth)]
    dmod[depth - 1][5] = fpart[0]
    sg = dict(norm_mix_g=[None] * depth, norm_mlp_g=[None] * depth, final_norm_g=fpart[1])
    n_conv, n_ret = conv_w_pw1.shape[0], ret_w_in.shape[0]
    for nm in ("conv_b_pw1", "conv_w_dw", "conv_b_dw", "conv_ln_g", "conv_ln_b", "conv_b_pw2"):
        sg[nm] = [None] * n_conv
    for nm in ("ret_gn_g", "ret_gn_b"):
        sg[nm] = [None] * n_ret

    half_idx = mc.astype(jnp.int32).reshape(1)
    chip_idx = chip.astype(jnp.int32).reshape(1)

    def chip_sums(keys, tag):
        flat = [G[nm][i] for nm, i in keys]
        sib = _swap_half(flat, name="swap_grads_" + tag)
        sums = [_add_half(a, b, half_idx, name="add_grads") for a, b in zip(flat, sib)]
        shapes = [jax.ShapeDtypeStruct((N_CHIPS - 1,) + s.shape[1:], BF16) for s in sums]
        return sums, shapes

    launch = {depth // 2 - 1: list(range(depth // 2, depth))}
    launch.update({l - 1: [l] for l in range(1, depth // 2)})
    early_keys = layer_weights(0)[:2]
    in_flight = []

    for l in reversed(range(depth)):
        j = l // 2
        sv = saved[l]
        if l in launch:
            keys = [k for ll in launch[l] for k in layer_weights(ll)]
            sums, shapes = chip_sums(keys, f"from{launch[l][0]}")
            started, tok = _exchange_start(sums, shapes, _plan_scatter, 3 * len(sums), fpart,
                                           name=f"scatter_start_{launch[l][0]}")
            in_flight.append((keys, started, launch[l][0]))
            sv["vec_b"] = sv["vec_b"] + tok[:, :1]
        w1, w2 = Wg["mlp_w1", l], wfull("mlp_w2", l)
        cs1 = w1.shape[2]
        da = _mm(dyb, w2, mode="nt", name="w2_dx", out_dtype=BF16, epi="mul2", extra=sv["ra"])
        gw2 = _mm(sv["p"], dyb, mode="tn", name="w2_dw")
        G["mlp_w2"][l] = gw2.reshape(N_CHIPS, gw2.shape[0] // N_CHIPS, gw2.shape[1])
        G["mlp_w1"][l] = _mm(sv["h2"], da, mode="tn", name="w1_dw", out3d=(N_CHIPS, cs1), tn=cs1, tk=DW_TOKENS)
        dh2 = _mm(da, w1, mode="nt", name="w1_dx", b3d=True, tk=cs1, out_dtype=BF16)
        dx, dyb, part = _row_bwd(sv["xb"], dh2, dx, sv["ymix"], sv["vec_b"], name="row_bwd")
        dmod[l][2], sg["norm_mlp_g"][l], dmod[l][3], dmod[l][4] = part[0], part[1], part[2], part[3]
        if l == 0:
            keys = layer_weights(0)[2:]
            sums, shapes = chip_sums(keys, "from0m")
            started, tok = _exchange_start(sums, shapes, _plan_scatter, 3 * len(sums), fpart, name="scatter_start_0m")
            in_flight.append((keys, started, "0m"))
            sv["cvec"] = sv["cvec"] + tok[:, :1]
        if l % 2 == 0:
            sg["conv_b_pw2"][j] = part[4]
            wp1, wp2 = Wg["conv_w_pw1", j], wfull("conv_w_pw2", j)
            csp = wp1.shape[2]
            dz = _mm(dyb, wp2, mode="nt", name="pw2_dx")
            gp2 = _mm(sv["z"], dyb, mode="tn", name="pw2_dw", tk=DW_TOKENS)
            G["conv_w_pw2"][j] = gp2.reshape(N_CHIPS, gp2.shape[0] // N_CHIPS, gp2.shape[1])
            du, dwdw, cpart, dbu = _conv_bwd(dz, sv["cv"], sv["v_glu"], sv["u"], wdw_pad(j), sv["cvec"],
                                             name="conv_bwd")
            sg["conv_w_dw"][j] = dwdw[:CONV_WIDTH]
            sg["conv_b_dw"][j], sg["conv_ln_g"][j], sg["conv_ln_b"][j] = cpart[0], cpart[1], cpart[2]
            sg["conv_b_pw1"][j] = dbu[0]
            G["conv_w_pw1"][j] = _mm(sv["h"], du, mode="tn", name="pw1_dw", out3d=(N_CHIPS, csp), tn=csp,
                                     tk=DW_TOKENS)
            dh = _mm(du, wp1, mode="nt", name="pw1_dx", b3d=True, tk=csp, out_dtype=BF16)
        else:
            wi, wo = Wg["ret_w_in", j], wfull("ret_w_out", j)
            csi = wi.shape[2]
            dy2 = _mm(dyb, wo, mode="nt", name="wout_dx")
            gwo = _mm(sv["y2"], dyb, mode="tn", name="wout_dw")
            G["ret_w_out"][j] = gwo.reshape(N_CHIPS, gwo.shape[0] // N_CHIPS, gwo.shape[1])
            dq, dkk, dvv, dgt, dgg, dgb = _ret_bwd(sv["proj"], cos_t, sin_t, tables, gn_g_full[j], gn_b_full[j],
                                                   sv["yr"], dy2, sv["states"], name="ret_bwd")
            sg["ret_gn_g"][j], sg["ret_gn_b"][j] = dgg.reshape(H, dv), dgb.reshape(H, dv)
            dproj = jnp.concatenate([dq, dkk, dvv, dgt], axis=1)
            G["ret_w_in"][j] = _mm(sv["h"], dproj, mode="tn", name="win_dw", out3d=(N_CHIPS, csi), tn=csi,
                                   tk=DW_TOKENS)
            dh = _mm(dproj, wi, mode="nt", name="win_dx", b3d=True, tk=csi, out_dtype=BF16)
        yp = saved[l - 1]["mo"] if l > 0 else dh
        dx, dyb, part = _row_bwd(sv["xa"], dh, dx, yp, sv["vec_a"], name="row_bwd")
        sg["norm_mix_g"][l], dmod[l][0], dmod[l][1] = part[1], part[2], part[3]
        if l > 0:
            dmod[l - 1][5] = part[0]
    grad_x = dx.reshape(x.shape)

    dmod_me = jnp.stack([jnp.stack(r) for r in dmod]).reshape(depth, 6 * D)
    sgrads = dict(ada_b=dmod_me)
    for nm in SMALL[1:]:
        sgrads[nm] = sg[nm] if nm == "final_norm_g" else jnp.stack(sg[nm])
    full_shapes = [sgrads[nm].shape for nm in SMALL]
    packed_all = _allgather8(_pack([sgrads[nm] for nm in SMALL]), name="gather_small_grads")
    sums = _unpack(_sum_slots(packed_all[None], name="sum_small_grads"), full_shapes)
    gsm = {}
    for nm, g in zip(SMALL, sums):
        if nm in SMALL_SHARDED:
            n = g.shape[-1] // N_CHIPS
            g = lax.dynamic_slice_in_dim(g.reshape(g.shape[:-1] + (N_CHIPS, n)), chip, 1, axis=g.ndim - 1)
            g = g.reshape(g.shape[:-2] + (n,))
        gsm[nm] = g.reshape(W[nm].shape)
    small_out = [{}, {}, {}, {}]
    for nm in SMALL:
        rows = lambda a: a.reshape(1, -1, LANES)
        res4 = _adamw(gsm[nm].reshape(1, 1, -1, LANES), rows(W[nm]), rows(Mo[nm]), rows(Vo[nm]), name="adamw_small")
        for i in range(4):
            small_out[i][nm] = res4[i].reshape(W[nm].shape)

    n_mod_rows = depth * 6 * D // 128
    dmod_all = packed_all[:, :n_mod_rows].reshape(N_DEV, depth, N_CHIPS, cs_ada)
    dmod_cols = lax.dynamic_slice_in_dim(dmod_all, chip, 1, axis=2).reshape(N_DEV, depth * cs_ada)
    kpad = 128 - N_DEV
    dmod_pad = jnp.concatenate([dmod_cols, jnp.zeros((kpad, depth * cs_ada), F32)], axis=0)
    ct_pad = jnp.concatenate([c_all.T, jnp.zeros((D, kpad), F32)], axis=1)
    g_ada = _mm(ct_pad, dmod_pad, mode="nn", name="ada_dw", a_silu=True, out3d=(depth, cs_ada), tn=cs_ada)
    ada_out = _adamw(g_ada.reshape(depth, 1, D, cs_ada), ada_w, m_ada_w, v_ada_w, name="adamw_ada")

    sums, shapes = chip_sums(early_keys, "from0")
    early_lands = _exchange(sums, shapes, _plan_scatter, 3 * len(sums), name="scatter_grads")
    done = [(early_keys, sums, early_lands)]
    for keys, started, first in in_flight:
        own, lands = _exchange_wait(started, len(keys), _plan_scatter, early_lands[-1], name=f"scatter_wait_{first}")
        done.append((keys, own, lands))
    total = {}
    for keys, own, lands in done:
        for k, o, r in zip(keys, own, lands):
            total[k] = _sum_own(o, r, chip_idx, name="sum_grads")
    halves = [jnp.stack([total[nm, i] for i in range(len(G[nm]))]) for nm in BIG]
    sib_halves = _swap_sibling(halves, name="swap_totals")
    big_out = {nm: _adamw_halves(hm, hs, half_idx, W[nm], Mo[nm], Vo[nm], name="adamw_big")
               for nm, hm, hs in zip(BIG, halves, sib_halves)}

    def res(nm, i):
        if nm == "ada_w":
            return ada_out[i]
        if nm in big_out:
            return big_out[nm][i]
        return small_out[i][nm]

    return (loss, grad_x, *[res(nm, 0) for nm in WEIGHTS], *[res(nm, 1) for nm in WEIGHTS],
            *[res(nm, 2) for nm in WEIGHTS], *[res(nm, 3) for nm in WEIGHTS])
```
